```python
import jax, jax.numpy as jnp
from jax import lax
import numpy as np

D_MODEL = 1024
BATCH = 4
SEQ = 4096
DEPTH = 1

GRID_W = 64
CTX_LEN = 256
N_FOURIER_GROUPS = 4
FOURIER_GROUP_W = 128
FOURIER_W = N_FOURIER_GROUPS * FOURIER_GROUP_W
N_HEADS = 8
QK_NOPE = 64
QK_ROPE = 32
HEAD_QK = QK_NOPE + QK_ROPE
V_DIM = 64
Q_LORA = 256
KV_LORA = 128
MLA_W = N_HEADS * V_DIM
N_BRANCH = 2
D_FF = 4 * D_MODEL
D_IN = FOURIER_W + Q_LORA + KV_LORA + QK_ROPE + N_BRANCH * D_MODEL
ROPE_THETA = 10000.0
Q_BLOCK = 128
EPS = 1e-6

kernel_name = "hybrid_fourier_mla_dit_block"


def rmsnorm(x, g):
    xf = x.astype(jnp.float32)
    y = xf * lax.rsqrt(jnp.mean(xf * xf, axis=-1, keepdims=True) + EPS)
    return (y * g.astype(jnp.float32)).astype(x.dtype)


def modulate(h, shift, scale):
    return h * (1 + scale) + shift


def axial_rope_tables(n):
    rows = n // GRID_W
    row = jnp.repeat(jnp.arange(rows, dtype=jnp.float32), GRID_W)
    col = jnp.tile(jnp.arange(GRID_W, dtype=jnp.float32), rows)
    n_freq = QK_ROPE // 4
    freqs = ROPE_THETA ** (-jnp.arange(n_freq, dtype=jnp.float32) / n_freq)
    ang_r = row[:, None] * freqs[None, :]
    ang_c = col[:, None] * freqs[None, :]
    ang = jnp.concatenate([ang_r, ang_r, ang_c, ang_c], axis=-1)
    return jnp.cos(ang), jnp.sin(ang)


def apply_rope(x, cos, sin):
    a1, a2, b1, b2 = jnp.split(x, 4, axis=-1)
    rot = jnp.concatenate([-a2, a1, -b2, b1], axis=-1)
    return x * cos[None, :, None, :].astype(x.dtype) + rot * sin[None, :, None, :].astype(x.dtype)


def split_proj(p):
    o1 = FOURIER_W
    o2 = o1 + Q_LORA
    o3 = o2 + KV_LORA
    o4 = o3 + QK_ROPE
    return p[..., :o1], p[..., o1:o2], p[..., o2:o3], p[..., o3:o4], p[..., o4:]


def fourier_mix(f):
    B, N, _ = f.shape
    fg = f.reshape(B, N, N_FOURIER_GROUPS, FOURIER_GROUP_W).astype(jnp.float32)
    y = jnp.fft.fft2(fg, axes=(1, 3), norm="ortho").real
    return y.reshape(B, N, FOURIER_W).astype(f.dtype)


def mla_kv(kv_lat, k_rope, kv_norm_g, w_kvb, k_gain, rope):
    B, N, _ = kv_lat.shape
    kv = (rmsnorm(kv_lat, kv_norm_g) @ w_kvb).reshape(B, N, N_HEADS, QK_NOPE + V_DIM)
    k_nope, v = kv[..., :QK_NOPE], kv[..., QK_NOPE:]
    k_r = jnp.broadcast_to(k_rope[:, :, None, :], (B, N, N_HEADS, QK_ROPE))
    k = rmsnorm(jnp.concatenate([k_nope, k_r], axis=-1), k_gain)
    if rope is not None:
        k = jnp.concatenate([k[..., :QK_NOPE], apply_rope(k[..., QK_NOPE:], *rope)], axis=-1)
    return k, v


def mla_q(q_lat, q_norm_g, w_qb, q_gain, rope):
    B, N, _ = q_lat.shape
    q = (rmsnorm(q_lat, q_norm_g) @ w_qb).reshape(B, N, N_HEADS, HEAD_QK)
    q = rmsnorm(q, q_gain)
    if rope is not None:
        q = jnp.concatenate([q[..., :QK_NOPE], apply_rope(q[..., QK_NOPE:], *rope)], axis=-1)
    return q


def latent_attention(q, k_all, v_all):
    B, S, H, Dh = q.shape
    nb = S // Q_BLOCK
    qb = q.reshape(B, nb, Q_BLOCK, H, Dh).transpose(1, 0, 2, 3, 4)
    scale = HEAD_QK ** -0.5

    def one_block(qblk):
        s = jnp.einsum('bqhd,bkhd->bhqk', qblk, k_all, preferred_element_type=jnp.float32) * scale
        p = jax.nn.softmax(s, axis=-1).astype(v_all.dtype)
        return jnp.einsum('bhqk,bkhd->bqhd', p, v_all)

    o = lax.map(one_block, qb)
    return o.transpose(1, 0, 2, 3, 4).reshape(B, S, H * V_DIM)


def context_attention(q, k, v):
    B, N = q.shape[0], q.shape[1]
    s = jnp.einsum('bqhd,bkhd->bhqk', q, k, preferred_element_type=jnp.float32) * (HEAD_QK ** -0.5)
    p = jax.nn.softmax(s, axis=-1).astype(v.dtype)
    return jnp.einsum('bhqk,bkhd->bqhd', p, v).reshape(B, N, N_HEADS * V_DIM)


def merge_branches(f_mix, att, gates, w_fourier, w_mla_o, b_gate, w_out):
    g = jax.nn.sigmoid(gates + b_gate)
    g_f, g_a = g[..., :D_MODEL], g[..., D_MODEL:]
    y = g_f * (f_mix @ w_fourier) + g_a * (att @ w_mla_o)
    return y @ w_out


def sq_relu_mlp(h, w_up, w_down):
    return jnp.square(jax.nn.relu(h @ w_up)) @ w_down


def setup_inputs(seed: int = 0) -> dict:
    key = jax.random.key(seed)
    ks = jax.random.split(key, 24)
    D, L = D_MODEL, DEPTH

    def w(k, shape, fan_in, mult=1.0):
        return jax.random.normal(k, shape, jnp.float32) * (mult * fan_in ** -0.5)

    def gain(k, shape):
        return 1.0 + 0.05 * jax.random.normal(k, shape, jnp.float32)

    return {
        "x": jax.random.normal(ks[0], (BATCH, SEQ, D), jnp.float32),
        "c": jax.random.normal(ks[1], (BATCH, D), jnp.float32),
        "ctx": jax.random.normal(ks[2], (BATCH, CTX_LEN, D), jnp.float32),
        "c_ctx": jax.random.normal(ks[3], (D,), jnp.float32),
        "ada_w": w(ks[4], (L, D, 6 * D), D, 0.5),
        "ada_b": 0.02 * jax.random.normal(ks[5], (L, 6 * D), jnp.float32),
        "norm1_g": gain(ks[6], (L, D)),
        "norm2_g": gain(ks[7], (L, D)),
        "w_in": w(ks[8], (L, D, D_IN), D),
        "b_gate": 0.02 * jax.random.normal(ks[9], (L, N_BRANCH * D), jnp.float32),
        "w_fourier": w(ks[10], (L, FOURIER_W, D), FOURIER_W),
        "q_norm_g": gain(ks[11], (L, Q_LORA)),
        "w_qb": w(ks[12], (L, Q_LORA, N_HEADS * HEAD_QK), Q_LORA),
        "kv_norm_g": gain(ks[13], (L, KV_LORA)),
        "w_kvb": w(ks[14], (L, KV_LORA, N_HEADS * (QK_NOPE + V_DIM)), KV_LORA),
        "q_gain": gain(ks[15], (L, HEAD_QK)),
        "k_gain": gain(ks[16], (L, HEAD_QK)),
        "w_mla_o": w(ks[17], (L, MLA_W, D), MLA_W),
        "w_out": w(ks[18], (L, D, D), D),
        "w_up": w(ks[19], (L, D, D_FF), D),
        "w_down": w(ks[20], (L, D_FF, D), D_FF),
    }


def reference(x, c, ctx, c_ctx, ada_w, ada_b, norm1_g, norm2_g, w_in, b_gate, w_fourier, q_norm_g, w_qb,
              kv_norm_g, w_kvb, q_gain, k_gain, w_mla_o, w_out, w_up, w_down):
    rope = axial_rope_tables(x.shape[1])
    for l in range(DEPTH):
        last = l == DEPTH - 1
        mod_x = jax.nn.silu(c) @ ada_w[l] + ada_b[l]
        sh1, sc1, g1, sh2, sc2, g2 = jnp.split(mod_x[:, None, :], 6, axis=-1)
        mod_c = jax.nn.silu(c_ctx) @ ada_w[l] + ada_b[l]
        csh1, csc1, cg1, csh2, csc2, cg2 = jnp.split(mod_c, 6, axis=-1)

        hc = modulate(rmsnorm(ctx, norm1_g[l]), csh1, csc1)
        pc = hc @ w_in[l]
        fc, qlc, kvlc, krc, gc = split_proj(pc)
        kc, vc = mla_kv(kvlc, krc, kv_norm_g[l], w_kvb[l], k_gain[l], None)

        hx = modulate(rmsnorm(x, norm1_g[l]), sh1, sc1)
        px = hx @ w_in[l]
        fx, qlx, kvlx, krx, gx = split_proj(px)
        kx, vx = mla_kv(kvlx, krx, kv_norm_g[l], w_kvb[l], k_gain[l], rope)
        qx = mla_q(qlx, q_norm_g[l], w_qb[l], q_gain[l], rope)
        att_x = latent_attention(qx, jnp.concatenate([kx, kc], axis=1), jnp.concatenate([vx, vc], axis=1))
        y_x = merge_branches(fourier_mix(fx), att_x, gx, w_fourier[l], w_mla_o[l], b_gate[l], w_out[l])

        if not last:
            qc = mla_q(qlc, q_norm_g[l], w_qb[l], q_gain[l], None)
            att_c = context_attention(qc, kc, vc)
            y_c = merge_branches(fourier_mix(fc), att_c, gc, w_fourier[l], w_mla_o[l], b_gate[l], w_out[l])
            ctx = ctx + cg1 * y_c
            hc2 = modulate(rmsnorm(ctx, norm2_g[l]), csh2, csc2)
            ctx = ctx + cg2 * sq_relu_mlp(hc2, w_up[l], w_down[l])

        x = x + g1 * y_x
        h2 = modulate(rmsnorm(x, norm2_g[l]), sh2, sc2)
        x = x + g2 * sq_relu_mlp(h2, w_up[l], w_down[l])
    return x
```

```python
import functools
import math

import numpy as np
import jax
import jax.numpy as jnp
from jax import lax
from jax.experimental import pallas as pl
from jax.experimental.pallas import tpu as pltpu

F32 = jnp.float32
BF16 = jnp.bfloat16

D_MODEL = 1024
GRID_W = 64
N_GROUPS = 4
GROUP_W = 128
FOURIER_W = N_GROUPS * GROUP_W
N_HEADS = 8
QK_NOPE = 64
QK_ROPE = 32
HEAD_QK = QK_NOPE + QK_ROPE
HEAD_PAD = 128
V_DIM = 64
V_ROWS = V_DIM + 16
Q_LORA = 256
KV_LORA = 128
ROPE_THETA = 10000.0
EPS = 1e-6
RADIX = 64
DFT_BLK = 8

V7X_VMEM_LIMIT = 56 * 1024 * 1024


def _params(*sem):
    return pltpu.CompilerParams(dimension_semantics=sem, vmem_limit_bytes=V7X_VMEM_LIMIT)


def _dot(a, b):
    return jnp.dot(a, b, preferred_element_type=F32)


def _rms_lanes(x, n):
    return x * lax.rsqrt(jnp.sum(x * x, axis=-1, keepdims=True) * (1.0 / n) + EPS)


def _rot_half_rows(x):
    return jnp.concatenate([-x[8:16], x[0:8], -x[24:32], x[16:24]], axis=0)


def _adaln_body(c_ref, w_ref, b_ref, o_ref):
    c = c_ref[...]
    s = c * jax.nn.sigmoid(c)
    o_ref[...] = _dot(s, w_ref[...]) + b_ref[...]


def _adaln(cc, ada_w, ada_b):
    rows, d = cc.shape
    n = ada_w.shape[1]
    tn = 1024
    return pl.pallas_call(
        _adaln_body,
        grid=(n // tn,),
        in_specs=[pl.BlockSpec((rows, d), lambda j: (0, 0)),
                  pl.BlockSpec((d, tn), lambda j: (0, j)),
                  pl.BlockSpec((1, tn), lambda j: (0, j))],
        out_specs=pl.BlockSpec((rows, tn), lambda j: (0, j)),
        out_shape=jax.ShapeDtypeStruct((rows, n), F32),
        compiler_params=_params("arbitrary"),
        name="adaln",
    )(cc, ada_w, ada_b)


def _norm_mod(x, m, g):
    ms = jnp.sum(x * x, axis=-1, keepdims=True) * (1.0 / x.shape[-1])
    return (x * lax.rsqrt(ms + EPS)) * (g * (1.0 + m[1:2])) + m[0:1]


def _kv_path(hb, wkv_ref, kvg_ref, wkbT_ref, wvT_ref, kgn_ref, kgr_ref, cos_ref, sin_ref,
             k_ref, vT_ref):
    t = hb.shape[0]
    kvp = _dot(hb, wkv_ref[...])
    kvn = _rms_lanes(kvp[:, :KV_LORA], KV_LORA) * kvg_ref[...]
    kvnT = kvn.T.astype(BF16)
    krT = kvp[:, KV_LORA:].T[0:QK_ROPE]
    knT = _dot(wkbT_ref[...], kvnT)
    vT = _dot(wvT_ref[...], kvnT)
    sr = jnp.sum(krT * krT, axis=0, keepdims=True)
    krg = krT * kgr_ref[...]
    kr_rot = krg * cos_ref[...] + _rot_half_rows(krg) * sin_ref[...]
    ones = jnp.ones((V_ROWS - V_DIM, t), F32)
    zpad = jnp.zeros((HEAD_PAD - HEAD_QK, t), F32)
    for h in range(N_HEADS):
        kn = knT[h * QK_NOPE:(h + 1) * QK_NOPE]
        ss = jnp.sum(kn * kn, axis=0, keepdims=True) + sr
        r = lax.rsqrt(ss * (1.0 / HEAD_QK) + EPS)
        kT = jnp.concatenate([kn * kgn_ref[...] * r, kr_rot * r, zpad], axis=0)
        k_ref[0, h] = kT.T.astype(k_ref.dtype)
        vh = vT[h * V_DIM:(h + 1) * V_DIM]
        vT_ref[0, h] = jnp.concatenate([vh, ones], axis=0).astype(vT_ref.dtype)


def _ctx_body(x_ref, mod_ref, g1_ref, wkv_ref, kvg_ref, wkbT_ref, wvT_ref, kgn_ref, kgr_ref,
              cos_ref, sin_ref, k_ref, vT_ref):
    h = _norm_mod(x_ref[0], mod_ref[0], g1_ref[...])
    _kv_path(h.astype(BF16), wkv_ref, kvg_ref, wkbT_ref, wvT_ref, kgn_ref, kgr_ref,
             cos_ref, sin_ref, k_ref, vT_ref)


def _proj_body(x_ref, mod_ref, g1_ref, wf_ref, cs_ref, wg_ref, bg_ref,
               wq_ref, qg_ref, wqbT_ref, qgain_ref,
               wkv_ref, kvg_ref, wkbT_ref, wvT_ref, kgn_ref, kgr_ref, cos_ref, sin_ref,
               z_ref, g_ref, qT_ref, k_ref, vT_ref):
    hb = _norm_mod(x_ref[0], mod_ref[0], g1_ref[...]).astype(BF16)

    f = _dot(hb, wf_ref[...]).astype(BF16)
    for g in range(N_GROUPS):
        z = _dot(f[:, g * GROUP_W:(g + 1) * GROUP_W], cs_ref[...])
        z_ref[0, 0, :, g * GROUP_W:(g + 1) * GROUP_W] = z[:, :GROUP_W].astype(z_ref.dtype)
        z_ref[0, 1, :, g * GROUP_W:(g + 1) * GROUP_W] = z[:, GROUP_W:].astype(z_ref.dtype)

    g_ref[0] = jax.nn.sigmoid(_dot(hb, wg_ref[...]) + bg_ref[...]).astype(g_ref.dtype)

    qn = _rms_lanes(_dot(hb, wq_ref[...]), Q_LORA) * qg_ref[...]
    qT = _dot(wqbT_ref[...], qn.T.astype(BF16))
    for h in range(N_HEADS):
        qh = qT[h * HEAD_PAD:(h + 1) * HEAD_PAD]
        r = lax.rsqrt(jnp.sum(qh * qh, axis=0, keepdims=True) * (1.0 / HEAD_QK) + EPS)
        qh = qh * qgain_ref[...] * r
        qr = qh[QK_NOPE:HEAD_QK]
        qr = qr * cos_ref[...] + _rot_half_rows(qr) * sin_ref[...]
        qT_ref[0, h] = jnp.concatenate([qh[:QK_NOPE], qr, qh[HEAD_QK:]], axis=0).astype(qT_ref.dtype)

    _kv_path(hb, wkv_ref, kvg_ref, wkbT_ref, wvT_ref, kgn_ref, kgr_ref, cos_ref, sin_ref,
             k_ref, vT_ref)


def _full(shape):
    nd = len(shape)
    return pl.BlockSpec(shape, lambda *_: (0,) * nd)


def _ctx_kv(ctx, mod3, g1, wkv, kvg, wkbT, wvT, kgn, kgr, cosT, sinT):
    b, t, d = ctx.shape
    return pl.pallas_call(
        _ctx_body,
        grid=(b,),
        in_specs=[pl.BlockSpec((1, t, d), lambda i: (i, 0, 0)),
                  pl.BlockSpec((1, 6, d), lambda i: (b, 0, 0)),
                  _full(g1.shape), _full(wkv.shape), _full(kvg.shape), _full(wkbT.shape),
                  _full(wvT.shape), _full(kgn.shape), _full(kgr.shape),
                  _full(cosT.shape), _full(sinT.shape)],
        out_specs=[pl.BlockSpec((1, N_HEADS, t, HEAD_PAD), lambda i: (i, 0, 0, 0)),
                   pl.BlockSpec((1, N_HEADS, V_ROWS, t), lambda i: (i, 0, 0, 0))],
        out_shape=[jax.ShapeDtypeStruct((b, N_HEADS, t, HEAD_PAD), BF16),
                   jax.ShapeDtypeStruct((b, N_HEADS, V_ROWS, t), BF16)],
        compiler_params=_params("arbitrary"),
        name="ctx_kv",
    )(ctx, mod3, g1, wkv, kvg, wkbT, wvT, kgn, kgr, cosT, sinT)


def _proj(x, mod3, g1, wf, cs, wg, bg, wq, qg, wqbT, qgain, wkv, kvg, wkbT, wvT, kgn, kgr,
          cosT, sinT, ts):
    b, s, d = x.shape
    tok = lambda shape: pl.BlockSpec(shape, lambda i, j: (0, j))
    return pl.pallas_call(
        _proj_body,
        grid=(b, s // ts),
        in_specs=[pl.BlockSpec((1, ts, d), lambda i, j: (i, j, 0)),
                  pl.BlockSpec((1, 6, d), lambda i, j: (i, 0, 0)),
                  _full(g1.shape), _full(wf.shape), _full(cs.shape), _full(wg.shape),
                  _full(bg.shape), _full(wq.shape), _full(qg.shape), _full(wqbT.shape),
                  tok((HEAD_PAD, ts)),
                  _full(wkv.shape), _full(kvg.shape), _full(wkbT.shape), _full(wvT.shape),
                  tok((QK_NOPE, ts)), tok((QK_ROPE, ts)), tok((QK_ROPE, ts)), tok((QK_ROPE, ts))],
        out_specs=[pl.BlockSpec((1, 2, ts, FOURIER_W), lambda i, j: (i, 0, j, 0)),
                   pl.BlockSpec((1, ts, 2 * d), lambda i, j: (i, j, 0)),
                   pl.BlockSpec((1, N_HEADS, HEAD_PAD, ts), lambda i, j: (i, 0, 0, j)),
                   pl.BlockSpec((1, N_HEADS, ts, HEAD_PAD), lambda i, j: (i, 0, j, 0)),
                   pl.BlockSpec((1, N_HEADS, V_ROWS, ts), lambda i, j: (i, 0, 0, j))],
        out_shape=[jax.ShapeDtypeStruct((b, 2, s, FOURIER_W), BF16),
                   jax.ShapeDtypeStruct((b, s, 2 * d), BF16),
                   jax.ShapeDtypeStruct((b, N_HEADS, HEAD_PAD, s), BF16),
                   jax.ShapeDtypeStruct((b, N_HEADS, s, HEAD_PAD), BF16),
                   jax.ShapeDtypeStruct((b, N_HEADS, V_ROWS, s), BF16)],
        compiler_params=_params("arbitrary", "arbitrary"),
        name="proj",
    )(x, mod3, g1, wf, cs, wg, bg, wq, qg, wqbT, qgain, wkv, kvg, wkbT, wvT, kgn, kgr,
      cosT, sinT)


def _dft_a_body(m_ref, z_ref, o_ref):
    o_ref[0] = _dot(m_ref[...], z_ref[0]).astype(o_ref.dtype)


def _dft_a(m1, zt, tc):
    b, r, c = zt.shape
    return pl.pallas_call(
        _dft_a_body,
        grid=(b, c // tc),
        in_specs=[_full(m1.shape), pl.BlockSpec((1, r, tc), lambda i, j: (i, 0, j))],
        out_specs=pl.BlockSpec((1, r, tc), lambda i, j: (i, 0, j)),
        out_shape=jax.ShapeDtypeStruct((b, r, c), BF16),
        compiler_params=_params("arbitrary", "arbitrary"),
        name="dft_a",
    )(m1, zt)


def _dft_b_body(g_ref, a_ref, o_ref):
    a = a_ref[0].reshape(2 * DFT_BLK * RADIX, FOURIER_W)
    y = _dot(g_ref[0], a)
    o_ref[0] = y.reshape(RADIX, DFT_BLK, FOURIER_W)


def _dft_b(gtab, a5):
    b = a5.shape[0]
    nblk = RADIX // DFT_BLK
    return pl.pallas_call(
        _dft_b_body,
        grid=(nblk, b),
        in_specs=[pl.BlockSpec((1,) + gtab.shape[1:], lambda j, i: (j, 0, 0)),
                  pl.BlockSpec((1, 2, DFT_BLK, RADIX, FOURIER_W), lambda j, i: (i, 0, j, 0, 0))],
        out_specs=pl.BlockSpec((1, RADIX, DFT_BLK, FOURIER_W), lambda j, i: (i, 0, j, 0)),
        out_shape=jax.ShapeDtypeStruct((b, RADIX, RADIX, FOURIER_W), F32),
        compiler_params=_params("arbitrary", "arbitrary"),
        name="dft_b",
    )(gtab, a5)


def _attn_body(qT_ref, k_ref, vT_ref, kc_ref, vcT_ref, o_ref, *, tkb):
    qT = qT_ref[0, 0]
    nkb = k_ref.shape[2] // tkb

    def scores(k_blk):
        return _dot(k_blk, qT)

    def update(carry, k_blk, vT_blk):
        m, acc = carry
        s = scores(k_blk)
        m_new = jnp.maximum(m, jnp.max(s, axis=0, keepdims=True))
        p = jnp.exp2(s - m_new).astype(BF16)
        acc = jnp.exp2(m - m_new) * acc + _dot(vT_blk, p)
        return m_new, acc

    def blk(i):
        off = pl.multiple_of(i * tkb, tkb)
        return k_ref[0, 0, pl.ds(off, tkb), :], vT_ref[0, 0, :, pl.ds(off, tkb)]

    k0, v0 = blk(0)
    s0 = scores(k0)
    m0 = jnp.max(s0, axis=0, keepdims=True)
    acc0 = _dot(v0, jnp.exp2(s0 - m0).astype(BF16))
    carry = lax.fori_loop(1, nkb, lambda i, c: update(c, *blk(i)), (m0, acc0))
    _, acc = update(carry, kc_ref[0, 0], vcT_ref[0, 0])
    o_ref[0] = (acc[:V_DIM] / acc[V_DIM:V_DIM + 1]).astype(o_ref.dtype)


def _attention(qT, k, vT, kc, vcT, tq, tkb):
    b, h, _, s = qT.shape
    sc = kc.shape[2]
    return pl.pallas_call(
        functools.partial(_attn_body, tkb=tkb),
        grid=(b, h, s // tq),
        in_specs=[pl.BlockSpec((1, 1, HEAD_PAD, tq), lambda i, j, q: (i, j, 0, q)),
                  pl.BlockSpec((1, 1, s, HEAD_PAD), lambda i, j, q: (i, j, 0, 0)),
                  pl.BlockSpec((1, 1, V_ROWS, s), lambda i, j, q: (i, j, 0, 0)),
                  pl.BlockSpec((1, 1, sc, HEAD_PAD), lambda i, j, q: (i, j, 0, 0)),
                  pl.BlockSpec((1, 1, V_ROWS, sc), lambda i, j, q: (i, j, 0, 0))],
        out_specs=pl.BlockSpec((1, V_DIM, tq), lambda i, j, q: (i, j, q)),
        out_shape=jax.ShapeDtypeStruct((b, h * V_DIM, s), BF16),
        compiler_params=_params("arbitrary", "arbitrary", "arbitrary"),
        name="attention",
    )(qT, k, vT, kc, vcT)


def _merge_body(fm_ref, aT_ref, g_ref, x_ref, mod_ref, wfo_ref, wao_ref, wout_ref, o_ref):
    d = x_ref.shape[-1]
    yf = _dot(fm_ref[0].astype(BF16), wfo_ref[...])
    ya = lax.dot_general(aT_ref[0], wao_ref[...], (((0,), (0,)), ((), ())),
                         preferred_element_type=F32)
    g = g_ref[0]
    y = g[:, :d].astype(F32) * yf + g[:, d:].astype(F32) * ya
    y2 = _dot(y.astype(BF16), wout_ref[...])
    o_ref[0] = x_ref[0] + mod_ref[0][2:3] * y2


def _merge(fm, aT, gates, x, mod3, wfo, wao, wout, ts):
    b, s, d = x.shape
    return pl.pallas_call(
        _merge_body,
        grid=(b, s // ts),
        in_specs=[pl.BlockSpec((1, ts, FOURIER_W), lambda i, j: (i, j, 0)),
                  pl.BlockSpec((1, aT.shape[1], ts), lambda i, j: (i, 0, j)),
                  pl.BlockSpec((1, ts, 2 * d), lambda i, j: (i, j, 0)),
                  pl.BlockSpec((1, ts, d), lambda i, j: (i, j, 0)),
                  pl.BlockSpec((1, 6, d), lambda i, j: (i, 0, 0)),
                  _full(wfo.shape), _full(wao.shape), _full(wout.shape)],
        out_specs=pl.BlockSpec((1, ts, d), lambda i, j: (i, j, 0)),
        out_shape=jax.ShapeDtypeStruct((b, s, d), F32),
        compiler_params=_params("arbitrary", "arbitrary"),
        name="merge",
    )(fm, aT, gates, x, mod3, wfo, wao, wout)


def _mlp_body(x_ref, mod_ref, g2_ref, wup_ref, wdn_ref, o_ref, *, fc):
    x = x_ref[0]
    m = mod_ref[0]
    hb = _norm_mod(x, m[3:5], g2_ref[...]).astype(BF16)
    acc = jnp.zeros(x.shape, F32)
    for c in range(wup_ref.shape[1] // fc):
        u = jnp.maximum(_dot(hb, wup_ref[:, c * fc:(c + 1) * fc]), 0.0)
        acc = acc + _dot((u * u).astype(BF16), wdn_ref[c * fc:(c + 1) * fc, :])
    o_ref[0] = x + m[5:6] * acc


def _mlp(x1, mod3, g2, wup, wdn, ts, fc):
    b, s, d = x1.shape
    return pl.pallas_call(
        functools.partial(_mlp_body, fc=fc),
        grid=(b, s // ts),
        in_specs=[pl.BlockSpec((1, ts, d), lambda i, j: (i, j, 0)),
                  pl.BlockSpec((1, 6, d), lambda i, j: (i, 0, 0)),
                  _full(g2.shape), _full(wup.shape), _full(wdn.shape)],
        out_specs=pl.BlockSpec((1, ts, d), lambda i, j: (i, j, 0)),
        out_shape=jax.ShapeDtypeStruct((b, s, d), F32),
        compiler_params=_params("arbitrary", "arbitrary"),
        name="mlp",
    )(x1, mod3, g2, wup, wdn)


@functools.lru_cache(maxsize=None)
def _dft_tables():
    j = np.arange(GROUP_W)
    ang = 2.0 * np.pi * np.outer(j, j) / GROUP_W
    cs = np.concatenate([np.cos(ang), -np.sin(ang)], axis=1) / math.sqrt(GROUP_W)

    a = np.arange(RADIX)
    ang = 2.0 * np.pi * np.outer(a, a) / RADIX
    c64, s64 = np.cos(ang), np.sin(ang)
    m1 = np.block([[c64, s64], [-s64, c64]]) / math.sqrt(RADIX)

    n = RADIX * RADIX
    k1 = np.arange(RADIX)[:, None, None]
    k2 = np.arange(RADIX)[None, :, None]
    bb = np.arange(RADIX)[None, None, :]
    th = 2.0 * np.pi * ((bb * (RADIX * k1 + k2)) % n) / n
    gc = np.cos(th) / math.sqrt(RADIX)
    gs = np.sin(th) / math.sqrt(RADIX)
    nblk = RADIX // DFT_BLK
    gtab = np.zeros((nblk, RADIX, DFT_BLK, 2, DFT_BLK, RADIX), np.float32)
    for jb in range(nblk):
        for q in range(DFT_BLK):
            gtab[jb, :, q, 0, q, :] = gc[:, jb * DFT_BLK + q, :]
            gtab[jb, :, q, 1, q, :] = gs[:, jb * DFT_BLK + q, :]
    gtab = gtab.reshape(nblk, RADIX * DFT_BLK, 2 * DFT_BLK * RADIX)
    return cs.astype(np.float32), m1.astype(np.float32), gtab


def _rope_tables_t(n):
    rows = n // GRID_W
    row = jnp.repeat(jnp.arange(rows, dtype=F32), GRID_W)
    col = jnp.tile(jnp.arange(GRID_W, dtype=F32), rows)
    n_freq = QK_ROPE // 4
    freqs = ROPE_THETA ** (-jnp.arange(n_freq, dtype=F32) / n_freq)
    ang_r = row[:, None] * freqs[None, :]
    ang_c = col[:, None] * freqs[None, :]
    ang = jnp.concatenate([ang_r, ang_r, ang_c, ang_c], axis=-1)
    return jnp.cos(ang).T, jnp.sin(ang).T


def kernel(x, c, ctx, c_ctx, ada_w, ada_b, norm1_g, norm2_g, w_in, b_gate, w_fourier, q_norm_g, w_qb,
           kv_norm_g, w_kvb, q_gain, k_gain, w_mla_o, w_out, w_up, w_down):
    bsz, s, d = x.shape
    sc = ctx.shape[1]
    assert ada_w.shape[0] == 1 and s == RADIX * RADIX and d == D_MODEL
    ts = 512

    cc = jnp.zeros((8, d), F32).at[:bsz].set(c).at[bsz].set(c_ctx)
    mod3 = _adaln(cc, ada_w[0], ada_b[0][None, :]).reshape(8, 6, d)

    o1, o2, o3, o4 = FOURIER_W, FOURIER_W + Q_LORA, FOURIER_W + Q_LORA + KV_LORA, \
        FOURIER_W + Q_LORA + KV_LORA + QK_ROPE
    w = w_in[0]
    wf = w[:, :o1].astype(BF16)
    wq = w[:, o1:o2].astype(BF16)
    wkv = jnp.pad(w[:, o2:o4], ((0, 0), (0, 2 * KV_LORA - (o4 - o2)))).astype(BF16)
    wg = w[:, o4:].astype(BF16)
    wqbT = jnp.pad(w_qb[0].reshape(Q_LORA, N_HEADS, HEAD_QK),
                   ((0, 0), (0, 0), (0, HEAD_PAD - HEAD_QK))).reshape(Q_LORA, -1).T.astype(BF16)
    wkvb = w_kvb[0].reshape(KV_LORA, N_HEADS, QK_NOPE + V_DIM)
    wkbT = wkvb[:, :, :QK_NOPE].reshape(KV_LORA, -1).T.astype(BF16)
    wvT = wkvb[:, :, QK_NOPE:].reshape(KV_LORA, -1).T.astype(BF16)

    g1 = norm1_g[0][None, :]
    g2 = norm2_g[0][None, :]
    qg = q_norm_g[0][None, :]
    kvg = kv_norm_g[0][None, :]
    bg = b_gate[0][None, :]
    qscale = (HEAD_QK ** -0.5) * math.log2(math.e)
    qgain = jnp.broadcast_to(jnp.pad(q_gain[0] * qscale, (0, HEAD_PAD - HEAD_QK))[:, None], (HEAD_PAD, s))
    kgn = jnp.broadcast_to(k_gain[0][:QK_NOPE, None], (QK_NOPE, s))
    kgr = jnp.broadcast_to(k_gain[0][QK_NOPE:, None], (QK_ROPE, s))
    cosT, sinT = _rope_tables_t(s)
    cos1 = jnp.ones((QK_ROPE, sc), F32)
    sin0 = jnp.zeros((QK_ROPE, sc), F32)

    cs_np, m1_np, gtab_np = _dft_tables()
    cs = jnp.asarray(cs_np).astype(BF16)
    m1 = jnp.asarray(m1_np).astype(BF16)
    gtab = jnp.asarray(gtab_np).astype(BF16)

    kc, vcT = _ctx_kv(ctx, mod3, g1, wkv, kvg, wkbT, wvT, kgn[:, :sc], kgr[:, :sc], cos1, sin0)

    z, gates, qT, k, vT = _proj(x, mod3, g1, wf, cs, wg, bg, wq, qg, wqbT, qgain, wkv, kvg, wkbT, wvT,
                                kgn, kgr, cosT, sinT, ts)

    zt = z.reshape(bsz, 2 * RADIX, RADIX * FOURIER_W)
    a = _dft_a(m1, zt, 4096)
    a5 = a.reshape(bsz, 2, RADIX, RADIX, FOURIER_W)
    fm = _dft_b(gtab, a5).reshape(bsz, s, FOURIER_W)

    aT = _attention(qT, k, vT, kc, vcT, 512, 512)

    x1 = _merge(fm, aT, gates, x, mod3, w_fourier[0].astype(BF16), w_mla_o[0].astype(BF16),
                w_out[0].astype(BF16), ts)
    return _mlp(x1, mod3, g2, w_up[0].astype(BF16), w_down[0].astype(BF16), ts, 1024)
```

```python
import functools
import math

import numpy as np
import jax
import jax.numpy as jnp
from jax import lax
from jax.experimental import pallas as pl
from jax.experimental.pallas import tpu as pltpu

F32 = jnp.float32
BF16 = jnp.bfloat16

D_MODEL = 1024
GRID_W = 64
N_GROUPS = 4
GROUP_W = 128
FOURIER_W = N_GROUPS * GROUP_W
N_HEADS = 8
QK_NOPE = 64
QK_ROPE = 32
HEAD_QK = QK_NOPE + QK_ROPE
HEAD_PAD = 128
V_DIM = 64
V_ROWS = V_DIM + 16
Q_LORA = 256
KV_LORA = 128
ROPE_THETA = 10000.0
EPS = 1e-6
RADIX = 64
DFT_BLK = 8

V7X_VMEM_LIMIT = 56 * 1024 * 1024


def _params(*sem):
    return pltpu.CompilerParams(dimension_semantics=sem, vmem_limit_bytes=V7X_VMEM_LIMIT)


def _dot(a, b):
    return jnp.dot(a, b, preferred_element_type=F32)


def _rms_lanes(x, n):
    return x * lax.rsqrt(jnp.sum(x * x, axis=-1, keepdims=True) * (1.0 / n) + EPS)


def _rot_half_rows(x):
    return jnp.concatenate([-x[8:16], x[0:8], -x[24:32], x[16:24]], axis=0)


def _adaln_body(c_ref, w_ref, b_ref, o_ref):
    c = c_ref[...]
    s = c * jax.nn.sigmoid(c)
    o_ref[...] = _dot(s, w_ref[...]) + b_ref[...]


def _adaln(cc, ada_w, ada_b):
    rows, d = cc.shape
    n = ada_w.shape[1]
    tn = 1024
    return pl.pallas_call(
        _adaln_body,
        grid=(n // tn,),
        in_specs=[pl.BlockSpec((rows, d), lambda j: (0, 0)),
                  pl.BlockSpec((d, tn), lambda j: (0, j)),
                  pl.BlockSpec((1, tn), lambda j: (0, j))],
        out_specs=pl.BlockSpec((rows, tn), lambda j: (0, j)),
        out_shape=jax.ShapeDtypeStruct((rows, n), F32),
        compiler_params=_params("arbitrary"),
        name="adaln",
    )(cc, ada_w, ada_b)


def _norm_mod(x, m, g):
    ms = jnp.sum(x * x, axis=-1, keepdims=True) * (1.0 / x.shape[-1])
    return (x * lax.rsqrt(ms + EPS)) * (g * (1.0 + m[1:2])) + m[0:1]


def _kv_path(hb, wkv_ref, kvg_ref, wkbT_ref, wvT_ref, kgn_ref, kgr_ref, cos_ref, sin_ref,
             k_ref, vT_ref):
    t = hb.shape[0]
    kvp = _dot(hb, wkv_ref[...])
    kvn = _rms_lanes(kvp[:, :KV_LORA], KV_LORA) * kvg_ref[...]
    kvnT = kvn.T.astype(BF16)
    krT = kvp[:, KV_LORA:].T[0:QK_ROPE]
    knT = _dot(wkbT_ref[...], kvnT)
    vT = _dot(wvT_ref[...], kvnT)
    sr = jnp.sum(krT * krT, axis=0, keepdims=True)
    krg = krT * kgr_ref[...]
    kr_rot = krg * cos_ref[...] + _rot_half_rows(krg) * sin_ref[...]
    ones = jnp.ones((V_ROWS - V_DIM, t), F32)
    zpad = jnp.zeros((HEAD_PAD - HEAD_QK, t), F32)
    for h in range(N_HEADS):
        kn = knT[h * QK_NOPE:(h + 1) * QK_NOPE]
        ss = jnp.sum(kn * kn, axis=0, keepdims=True) + sr
        r = lax.rsqrt(ss * (1.0 / HEAD_QK) + EPS)
        kT = jnp.concatenate([kn * kgn_ref[...] * r, kr_rot * r, zpad], axis=0)
        k_ref[0, h] = kT.T.astype(k_ref.dtype)
        vh = vT[h * V_DIM:(h + 1) * V_DIM]
        vT_ref[0, h] = jnp.concatenate([vh, ones], axis=0).astype(vT_ref.dtype)


def _ctx_body(x_ref, mod_ref, g1_ref, wkv_ref, kvg_ref, wkbT_ref, wvT_ref, kgn_ref, kgr_ref,
              cos_ref, sin_ref, k_ref, vT_ref):
    h = _norm_mod(x_ref[0], mod_ref[0], g1_ref[...])
    _kv_path(h.astype(BF16), wkv_ref, kvg_ref, wkbT_ref, wvT_ref, kgn_ref, kgr_ref,
             cos_ref, sin_ref, k_ref, vT_ref)


def _proj_body(x_ref, mod_ref, g1_ref, wf_ref, cs_ref, wg_ref, bg_ref,
               wq_ref, qg_ref, wqbT_ref, qgain_ref,
               wkv_ref, kvg_ref, wkbT_ref, wvT_ref, kgn_ref, kgr_ref, cos_ref, sin_ref,
               z_ref, g_ref, qT_ref, k_ref, vT_ref):
    hb = _norm_mod(x_ref[0], mod_ref[0], g1_ref[...]).astype(BF16)

    f = _dot(hb, wf_ref[...]).astype(BF16)
    for g in range(N_GROUPS):
        z = _dot(f[:, g * GROUP_W:(g + 1) * GROUP_W], cs_ref[...])
        z_ref[0, 0, :, g * GROUP_W:(g + 1) * GROUP_W] = z[:, :GROUP_W].astype(z_ref.dtype)
        z_ref[0, 1, :, g * GROUP_W:(g + 1) * GROUP_W] = z[:, GROUP_W:].astype(z_ref.dtype)

    g_ref[0] = jax.nn.sigmoid(_dot(hb, wg_ref[...]) + bg_ref[...]).astype(g_ref.dtype)

    qn = _rms_lanes(_dot(hb, wq_ref[...]), Q_LORA) * qg_ref[...]
    qT = _dot(wqbT_ref[...], qn.T.astype(BF16))
    for h in range(N_HEADS):
        qh = qT[h * HEAD_PAD:(h + 1) * HEAD_PAD]
        r = lax.rsqrt(jnp.sum(qh * qh, axis=0, keepdims=True) * (1.0 / HEAD_QK) + EPS)
        qh = qh * qgain_ref[...] * r
        qr = qh[QK_NOPE:HEAD_QK]
        qr = qr * cos_ref[...] + _rot_half_rows(qr) * sin_ref[...]
        qT_ref[0, h] = jnp.concatenate([qh[:QK_NOPE], qr, qh[HEAD_QK:]], axis=0).astype(qT_ref.dtype)

    _kv_path(hb, wkv_ref, kvg_ref, wkbT_ref, wvT_ref, kgn_ref, kgr_ref, cos_ref, sin_ref,
             k_ref, vT_ref)


def _full(shape):
    nd = len(shape)
    return pl.BlockSpec(shape, lambda *_: (0,) * nd)


def _ctx_kv(ctx, mod3, g1, wkv, kvg, wkbT, wvT, kgn, kgr, cosT, sinT):
    b, t, d = ctx.shape
    return pl.pallas_call(
        _ctx_body,
        grid=(b,),
        in_specs=[pl.BlockSpec((1, t, d), lambda i: (i, 0, 0)),
                  pl.BlockSpec((1, 6, d), lambda i: (b, 0, 0)),
                  _full(g1.shape), _full(wkv.shape), _full(kvg.shape), _full(wkbT.shape),
                  _full(wvT.shape), _full(kgn.shape), _full(kgr.shape),
                  _full(cosT.shape), _full(sinT.shape)],
        out_specs=[pl.BlockSpec((1, N_HEADS, t, HEAD_PAD), lambda i: (i, 0, 0, 0)),
                   pl.BlockSpec((1, N_HEADS, V_ROWS, t), lambda i: (i, 0, 0, 0))],
        out_shape=[jax.ShapeDtypeStruct((b, N_HEADS, t, HEAD_PAD), BF16),
                   jax.ShapeDtypeStruct((b, N_HEADS, V_ROWS, t), BF16)],
        compiler_params=_params("arbitrary"),
        name="ctx_kv",
    )(ctx, mod3, g1, wkv, kvg, wkbT, wvT, kgn, kgr, cosT, sinT)


def _proj(x, mod3, g1, wf, cs, wg, bg, wq, qg, wqbT, qgain, wkv, kvg, wkbT, wvT, kgn, kgr,
          cosT, sinT, ts):
    b, s, d = x.shape
    tok = lambda shape: pl.BlockSpec(shape, lambda i, j: (0, j))
    return pl.pallas_call(
        _proj_body,
        grid=(b, s // ts),
        in_specs=[pl.BlockSpec((1, ts, d), lambda i, j: (i, j, 0)),
                  pl.BlockSpec((1, 6, d), lambda i, j: (i, 0, 0)),
                  _full(g1.shape), _full(wf.shape), _full(cs.shape), _full(wg.shape),
                  _full(bg.shape), _full(wq.shape), _full(qg.shape), _full(wqbT.shape),
                  tok((HEAD_PAD, ts)),
                  _full(wkv.shape), _full(kvg.shape), _full(wkbT.shape), _full(wvT.shape),
                  tok((QK_NOPE, ts)), tok((QK_ROPE, ts)), tok((QK_ROPE, ts)), tok((QK_ROPE, ts))],
        out_specs=[pl.BlockSpec((1, 2, ts, FOURIER_W), lambda i, j: (i, 0, j, 0)),
                   pl.BlockSpec((1, ts, 2 * d), lambda i, j: (i, j, 0)),
                   pl.BlockSpec((1, N_HEADS, HEAD_PAD, ts), lambda i, j: (i, 0, 0, j)),
                   pl.BlockSpec((1, N_HEADS, ts, HEAD_PAD), lambda i, j: (i, 0, j, 0)),
                   pl.BlockSpec((1, N_HEADS, V_ROWS, ts), lambda i, j: (i, 0, 0, j))],
        out_shape=[jax.ShapeDtypeStruct((b, 2, s, FOURIER_W), BF16),
                   jax.ShapeDtypeStruct((b, s, 2 * d), BF16),
                   jax.ShapeDtypeStruct((b, N_HEADS, HEAD_PAD, s), BF16),
                   jax.ShapeDtypeStruct((b, N_HEADS, s, HEAD_PAD), BF16),
                   jax.ShapeDtypeStruct((b, N_HEADS, V_ROWS, s), BF16)],
        compiler_params=_params("arbitrary", "arbitrary"),
        name="proj",
    )(x, mod3, g1, wf, cs, wg, bg, wq, qg, wqbT, qgain, wkv, kvg, wkbT, wvT, kgn, kgr,
      cosT, sinT)


def _dft_a_body(m_ref, z_ref, o_ref):
    o_ref[0] = _dot(m_ref[...], z_ref[0]).astype(o_ref.dtype)


def _dft_a(m1, zt, tc):
    b, r, c = zt.shape
    return pl.pallas_call(
        _dft_a_body,
        grid=(b, c // tc),
        in_specs=[_full(m1.shape), pl.BlockSpec((1, r, tc), lambda i, j: (i, 0, j))],
        out_specs=pl.BlockSpec((1, r, tc), lambda i, j: (i, 0, j)),
        out_shape=jax.ShapeDtypeStruct((b, r, c), BF16),
        compiler_params=_params("arbitrary", "arbitrary"),
        name="dft_a",
    )(m1, zt)


def _dft_b_body(g_ref, a_ref, o_ref):
    a = a_ref[0].reshape(2 * DFT_BLK * RADIX, FOURIER_W)
    y = _dot(g_ref[0], a)
    o_ref[0] = y.reshape(RADIX, DFT_BLK, FOURIER_W)


def _dft_b(gtab, a5):
    b = a5.shape[0]
    nblk = RADIX // DFT_BLK
    return pl.pallas_call(
        _dft_b_body,
        grid=(nblk, b),
        in_specs=[pl.BlockSpec((1,) + gtab.shape[1:], lambda j, i: (j, 0, 0)),
                  pl.BlockSpec((1, 2, DFT_BLK, RADIX, FOURIER_W), lambda j, i: (i, 0, j, 0, 0))],
        out_specs=pl.BlockSpec((1, RADIX, DFT_BLK, FOURIER_W), lambda j, i: (i, 0, j, 0)),
        out_shape=jax.ShapeDtypeStruct((b, RADIX, RADIX, FOURIER_W), F32),
        compiler_params=_params("arbitrary", "arbitrary"),
        name="dft_b",
    )(gtab, a5)


def _attn_body(qT_ref, k_ref, vT_ref, kc_ref, vcT_ref, o_ref, *, tkb):
    qT = qT_ref[0, 0]
    nkb = k_ref.shape[2] // tkb

    def scores(k_blk):
        return _dot(k_blk, qT)

    def colmax(s):
        r = s.shape[0] // 8
        return jnp.max(jnp.max(s.reshape(8, r, s.shape[1]), axis=0), axis=0, keepdims=True)

    def update(carry, s, vT_blk):
        m, acc = carry
        m_new = jnp.maximum(m, colmax(s))
        p = jnp.exp2(s - m_new).astype(BF16)
        acc = jnp.exp2(m - m_new) * acc + _dot(vT_blk, p)
        return m_new, acc

    tq = qT.shape[1]
    carry = (jnp.full((1, tq), -jnp.inf, F32), jnp.zeros((V_ROWS, tq), F32))
    s_cur = scores(k_ref[0, 0, 0:tkb, :])
    for i in range(nkb):
        s_next = scores(k_ref[0, 0, (i + 1) * tkb:(i + 2) * tkb, :] if i + 1 < nkb else kc_ref[0, 0])
        carry = update(carry, s_cur, vT_ref[0, 0, :, i * tkb:(i + 1) * tkb])
        s_cur = s_next
    _, acc = update(carry, s_cur, vcT_ref[0, 0])
    o_ref[0] = (acc[:V_DIM] / acc[V_DIM:V_DIM + 1]).astype(o_ref.dtype)


def _attention(qT, k, vT, kc, vcT, tq, tkb):
    b, h, _, s = qT.shape
    sc = kc.shape[2]
    return pl.pallas_call(
        functools.partial(_attn_body, tkb=tkb),
        grid=(b, h, s // tq),
        in_specs=[pl.BlockSpec((1, 1, HEAD_PAD, tq), lambda i, j, q: (i, j, 0, q)),
                  pl.BlockSpec((1, 1, s, HEAD_PAD), lambda i, j, q: (i, j, 0, 0)),
                  pl.BlockSpec((1, 1, V_ROWS, s), lambda i, j, q: (i, j, 0, 0)),
                  pl.BlockSpec((1, 1, sc, HEAD_PAD), lambda i, j, q: (i, j, 0, 0)),
                  pl.BlockSpec((1, 1, V_ROWS, sc), lambda i, j, q: (i, j, 0, 0))],
        out_specs=pl.BlockSpec((1, V_DIM, tq), lambda i, j, q: (i, j, q)),
        out_shape=jax.ShapeDtypeStruct((b, h * V_DIM, s), BF16),
        compiler_params=_params("arbitrary", "arbitrary", "arbitrary"),
        name="attention",
    )(qT, k, vT, kc, vcT)


def _merge_body(fm_ref, aT_ref, g_ref, x_ref, mod_ref, wfo_ref, wao_ref, wout_ref, o_ref):
    d = x_ref.shape[-1]
    yf = _dot(fm_ref[0].astype(BF16), wfo_ref[...])
    ya = lax.dot_general(aT_ref[0], wao_ref[...], (((0,), (0,)), ((), ())),
                         preferred_element_type=F32)
    g = g_ref[0]
    y = g[:, :d].astype(F32) * yf + g[:, d:].astype(F32) * ya
    y2 = _dot(y.astype(BF16), wout_ref[...])
    o_ref[0] = x_ref[0] + mod_ref[0][2:3] * y2


def _merge(fm, aT, gates, x, mod3, wfo, wao, wout, ts):
    b, s, d = x.shape
    return pl.pallas_call(
        _merge_body,
        grid=(b, s // ts),
        in_specs=[pl.BlockSpec((1, ts, FOURIER_W), lambda i, j: (i, j, 0)),
                  pl.BlockSpec((1, aT.shape[1], ts), lambda i, j: (i, 0, j)),
                  pl.BlockSpec((1, ts, 2 * d), lambda i, j: (i, j, 0)),
                  pl.BlockSpec((1, ts, d), lambda i, j: (i, j, 0)),
                  pl.BlockSpec((1, 6, d), lambda i, j: (i, 0, 0)),
                  _full(wfo.shape), _full(wao.shape), _full(wout.shape)],
        out_specs=pl.BlockSpec((1, ts, d), lambda i, j: (i, j, 0)),
        out_shape=jax.ShapeDtypeStruct((b, s, d), F32),
        compiler_params=_params("arbitrary", "arbitrary"),
        name="merge",
    )(fm, aT, gates, x, mod3, wfo, wao, wout)


def _mlp_body(x_ref, mod_ref, g2_ref, wup_ref, wdn_ref, o_ref, *, fc):
    x = x_ref[0]
    m = mod_ref[0]
    hb = _norm_mod(x, m[3:5], g2_ref[...]).astype(BF16)
    acc = jnp.zeros(x.shape, F32)
    for c in range(wup_ref.shape[1] // fc):
        u = jnp.maximum(_dot(hb, wup_ref[:, c * fc:(c + 1) * fc]), 0.0)
        acc = acc + _dot((u * u).astype(BF16), wdn_ref[c * fc:(c + 1) * fc, :])
    o_ref[0] = x + m[5:6] * acc


def _mlp(x1, mod3, g2, wup, wdn, ts, fc):
    b, s, d = x1.shape
    return pl.pallas_call(
        functools.partial(_mlp_body, fc=fc),
        grid=(b, s // ts),
        in_specs=[pl.BlockSpec((1, ts, d), lambda i, j: (i, j, 0)),
                  pl.BlockSpec((1, 6, d), lambda i, j: (i, 0, 0)),
                  _full(g2.shape), _full(wup.shape), _full(wdn.shape)],
        out_specs=pl.BlockSpec((1, ts, d), lambda i, j: (i, j, 0)),
        out_shape=jax.ShapeDtypeStruct((b, s, d), F32),
        compiler_params=_params("arbitrary", "arbitrary"),
        name="mlp",
    )(x1, mod3, g2, wup, wdn)


@functools.lru_cache(maxsize=None)
def _dft_tables():
    j = np.arange(GROUP_W)
    ang = 2.0 * np.pi * np.outer(j, j) / GROUP_W
    cs = np.concatenate([np.cos(ang), -np.sin(ang)], axis=1) / math.sqrt(GROUP_W)

    a = np.arange(RADIX)
    ang = 2.0 * np.pi * np.outer(a, a) / RADIX
    c64, s64 = np.cos(ang), np.sin(ang)
    m1 = np.block([[c64, s64], [-s64, c64]]) / math.sqrt(RADIX)

    n = RADIX * RADIX
    k1 = np.arange(RADIX)[:, None, None]
    k2 = np.arange(RADIX)[None, :, None]
    bb = np.arange(RADIX)[None, None, :]
    th = 2.0 * np.pi * ((bb * (RADIX * k1 + k2)) % n) / n
    gc = np.cos(th) / math.sqrt(RADIX)
    gs = np.sin(th) / math.sqrt(RADIX)
    nblk = RADIX // DFT_BLK
    gtab = np.zeros((nblk, RADIX, DFT_BLK, 2, DFT_BLK, RADIX), np.float32)
    for jb in range(nblk):
        for q in range(DFT_BLK):
            gtab[jb, :, q, 0, q, :] = gc[:, jb * DFT_BLK + q, :]
            gtab[jb, :, q, 1, q, :] = gs[:, jb * DFT_BLK + q, :]
    gtab = gtab.reshape(nblk, RADIX * DFT_BLK, 2 * DFT_BLK * RADIX)
    return cs.astype(np.float32), m1.astype(np.float32), gtab


def _rope_tables_t(n):
    rows = n // GRID_W
    row = jnp.repeat(jnp.arange(rows, dtype=F32), GRID_W)
    col = jnp.tile(jnp.arange(GRID_W, dtype=F32), rows)
    n_freq = QK_ROPE // 4
    freqs = ROPE_THETA ** (-jnp.arange(n_freq, dtype=F32) / n_freq)
    ang_r = row[:, None] * freqs[None, :]
    ang_c = col[:, None] * freqs[None, :]
    ang = jnp.concatenate([ang_r, ang_r, ang_c, ang_c], axis=-1)
    return jnp.cos(ang).T, jnp.sin(ang).T


def kernel(x, c, ctx, c_ctx, ada_w, ada_b, norm1_g, norm2_g, w_in, b_gate, w_fourier, q_norm_g, w_qb,
           kv_norm_g, w_kvb, q_gain, k_gain, w_mla_o, w_out, w_up, w_down):
    bsz, s, d = x.shape
    sc = ctx.shape[1]
    assert ada_w.shape[0] == 1 and s == RADIX * RADIX and d == D_MODEL
    ts = 512

    cc = jnp.zeros((8, d), F32).at[:bsz].set(c).at[bsz].set(c_ctx)
    mod3 = _adaln(cc, ada_w[0], ada_b[0][None, :]).reshape(8, 6, d)

    o1, o2, o3, o4 = FOURIER_W, FOURIER_W + Q_LORA, FOURIER_W + Q_LORA + KV_LORA, \
        FOURIER_W + Q_LORA + KV_LORA + QK_ROPE
    w = w_in[0]
    wf = w[:, :o1].astype(BF16)
    wq = w[:, o1:o2].astype(BF16)
    wkv = jnp.pad(w[:, o2:o4], ((0, 0), (0, 2 * KV_LORA - (o4 - o2)))).astype(BF16)
    wg = w[:, o4:].astype(BF16)
    wqbT = jnp.pad(w_qb[0].reshape(Q_LORA, N_HEADS, HEAD_QK),
                   ((0, 0), (0, 0), (0, HEAD_PAD - HEAD_QK))).reshape(Q_LORA, -1).T.astype(BF16)
    wkvb = w_kvb[0].reshape(KV_LORA, N_HEADS, QK_NOPE + V_DIM)
    wkbT = wkvb[:, :, :QK_NOPE].reshape(KV_LORA, -1).T.astype(BF16)
    wvT = wkvb[:, :, QK_NOPE:].reshape(KV_LORA, -1).T.astype(BF16)

    g1 = norm1_g[0][None, :]
    g2 = norm2_g[0][None, :]
    qg = q_norm_g[0][None, :]
    kvg = kv_norm_g[0][None, :]
    bg = b_gate[0][None, :]
    qscale = (HEAD_QK ** -0.5) * math.log2(math.e)
    qgain = jnp.broadcast_to(jnp.pad(q_gain[0] * qscale, (0, HEAD_PAD - HEAD_QK))[:, None], (HEAD_PAD, s))
    kgn = jnp.broadcast_to(k_gain[0][:QK_NOPE, None], (QK_NOPE, s))
    kgr = jnp.broadcast_to(k_gain[0][QK_NOPE:, None], (QK_ROPE, s))
    cosT, sinT = _rope_tables_t(s)
    cos1 = jnp.ones((QK_ROPE, sc), F32)
    sin0 = jnp.zeros((QK_ROPE, sc), F32)

    cs_np, m1_np, gtab_np = _dft_tables()
    cs = jnp.asarray(cs_np).astype(BF16)
    m1 = jnp.asarray(m1_np).astype(BF16)
    gtab = jnp.asarray(gtab_np).astype(BF16)

    kc, vcT = _ctx_kv(ctx, mod3, g1, wkv, kvg, wkbT, wvT, kgn[:, :sc], kgr[:, :sc], cos1, sin0)

    z, gates, qT, k, vT = _proj(x, mod3, g1, wf, cs, wg, bg, wq, qg, wqbT, qgain, wkv, kvg, wkbT, wvT,
                                kgn, kgr, cosT, sinT, ts)

    zt = z.reshape(bsz, 2 * RADIX, RADIX * FOURIER_W)
    a = _dft_a(m1, zt, 4096)
    a5 = a.reshape(bsz, 2, RADIX, RADIX, FOURIER_W)
    fm = _dft_b(gtab, a5).reshape(bsz, s, FOURIER_W)

    aT = _attention(qT, k, vT, kc, vcT, 512, 512)

    x1 = _merge(fm, aT, gates, x, mod3, w_fourier[0].astype(BF16), w_mla_o[0].astype(BF16),
                w_out[0].astype(BF16), ts)
    return _mlp(x1, mod3, g2, w_up[0].astype(BF16), w_down[0].astype(BF16), ts, 1024)
```

```python
import functools
import math

import numpy as np
import jax
import jax.numpy as jnp
from jax import lax
from jax.experimental import pallas as pl
from jax.experimental.pallas import tpu as pltpu

F32 = jnp.float32
BF16 = jnp.bfloat16

D_MODEL = 1024
GRID_W = 64
N_GROUPS = 4
GROUP_W = 128
FOURIER_W = N_GROUPS * GROUP_W
N_HEADS = 8
QK_NOPE = 64
QK_ROPE = 32
HEAD_QK = QK_NOPE + QK_ROPE
HEAD_PAD = 128
V_DIM = 64
V_ROWS = V_DIM + 16
Q_LORA = 256
KV_LORA = 128
ROPE_THETA = 10000.0
EPS = 1e-6
RADIX = 64
DFT_BLK = 8

V7X_VMEM_LIMIT = 56 * 1024 * 1024


def _params(*sem):
    return pltpu.CompilerParams(dimension_semantics=sem, vmem_limit_bytes=V7X_VMEM_LIMIT)


def _dot(a, b):
    return jnp.dot(a, b, preferred_element_type=F32)


def _rms_lanes(x, n):
    return x * lax.rsqrt(jnp.sum(x * x, axis=-1, keepdims=True) * (1.0 / n) + EPS)


def _rot_half_rows(x):
    return jnp.concatenate([-x[8:16], x[0:8], -x[24:32], x[16:24]], axis=0)


def _adaln_body(c_ref, w_ref, b_ref, o_ref):
    c = c_ref[...]
    s = c * jax.nn.sigmoid(c)
    o_ref[...] = _dot(s, w_ref[...]) + b_ref[...]


def _adaln(cc, ada_w, ada_b):
    rows, d = cc.shape
    n = ada_w.shape[1]
    tn = 1024
    return pl.pallas_call(
        _adaln_body,
        grid=(n // tn,),
        in_specs=[pl.BlockSpec((rows, d), lambda j: (0, 0)),
                  pl.BlockSpec((d, tn), lambda j: (0, j)),
                  pl.BlockSpec((1, tn), lambda j: (0, j))],
        out_specs=pl.BlockSpec((rows, tn), lambda j: (0, j)),
        out_shape=jax.ShapeDtypeStruct((rows, n), F32),
        compiler_params=_params("arbitrary"),
        name="adaln",
    )(cc, ada_w, ada_b)


def _norm_mod(x, m, g):
    ms = jnp.sum(x * x, axis=-1, keepdims=True) * (1.0 / x.shape[-1])
    return (x * lax.rsqrt(ms + EPS)) * (g * (1.0 + m[1:2])) + m[0:1]


def _kv_path(hb, wkv_ref, kvg_ref, wkbT_ref, wvT_ref, kgn_ref, kgr_ref, cos_ref, sin_ref,
             k_ref, vT_ref):
    t = hb.shape[0]
    kvp = _dot(hb, wkv_ref[...])
    kvn = _rms_lanes(kvp[:, :KV_LORA], KV_LORA) * kvg_ref[...]
    kvnT = kvn.T.astype(BF16)
    krT = kvp[:, KV_LORA:].T[0:QK_ROPE]
    knT = _dot(wkbT_ref[...], kvnT)
    vT = _dot(wvT_ref[...], kvnT)
    sr = jnp.sum(krT * krT, axis=0, keepdims=True)
    krg = krT * kgr_ref[...]
    kr_rot = krg * cos_ref[...] + _rot_half_rows(krg) * sin_ref[...]
    ones = jnp.ones((V_ROWS - V_DIM, t), F32)
    zpad = jnp.zeros((HEAD_PAD - HEAD_QK, t), F32)
    for h in range(N_HEADS):
        kn = knT[h * QK_NOPE:(h + 1) * QK_NOPE]
        ss = jnp.sum(kn * kn, axis=0, keepdims=True) + sr
        r = lax.rsqrt(ss * (1.0 / HEAD_QK) + EPS)
        kT = jnp.concatenate([kn * kgn_ref[...] * r, kr_rot * r, zpad], axis=0)
        k_ref[0, h] = kT.T.astype(k_ref.dtype)
        vh = vT[h * V_DIM:(h + 1) * V_DIM]
        vT_ref[0, h] = jnp.concatenate([vh, ones], axis=0).astype(vT_ref.dtype)


def _ctx_body(x_ref, mod_ref, g1_ref, wkv_ref, kvg_ref, wkbT_ref, wvT_ref, kgn_ref, kgr_ref,
              cos_ref, sin_ref, k_ref, vT_ref):
    h = _norm_mod(x_ref[0], mod_ref[0], g1_ref[...])
    _kv_path(h.astype(BF16), wkv_ref, kvg_ref, wkbT_ref, wvT_ref, kgn_ref, kgr_ref,
             cos_ref, sin_ref, k_ref, vT_ref)


def _proj_body(x_ref, mod_ref, g1_ref, wf_ref, cs_ref, wg_ref, bg_ref,
               wq_ref, qg_ref, wqbT_ref, qgain_ref,
               wkv_ref, kvg_ref, wkbT_ref, wvT_ref, kgn_ref, kgr_ref, cos_ref, sin_ref,
               z_ref, g_ref, qT_ref, k_ref, vT_ref):
    hb = _norm_mod(x_ref[0], mod_ref[0], g1_ref[...]).astype(BF16)

    f = _dot(hb, wf_ref[...]).astype(BF16)
    for g in range(N_GROUPS):
        z = _dot(f[:, g * GROUP_W:(g + 1) * GROUP_W], cs_ref[...])
        z_ref[0, 0, :, g * GROUP_W:(g + 1) * GROUP_W] = z[:, :GROUP_W].astype(z_ref.dtype)
        z_ref[0, 1, :, g * GROUP_W:(g + 1) * GROUP_W] = z[:, GROUP_W:].astype(z_ref.dtype)

    g_ref[0] = jax.nn.sigmoid(_dot(hb, wg_ref[...]) + bg_ref[...]).astype(g_ref.dtype)

    qn = _rms_lanes(_dot(hb, wq_ref[...]), Q_LORA) * qg_ref[...]
    qT = _dot(wqbT_ref[...], qn.T.astype(BF16))
    for h in range(N_HEADS):
        qh = qT[h * HEAD_PAD:(h + 1) * HEAD_PAD]
        r = lax.rsqrt(jnp.sum(qh * qh, axis=0, keepdims=True) * (1.0 / HEAD_QK) + EPS)
        qh = qh * qgain_ref[...] * r
        qr = qh[QK_NOPE:HEAD_QK]
        qr = qr * cos_ref[...] + _rot_half_rows(qr) * sin_ref[...]
        qT_ref[0, h] = jnp.concatenate([qh[:QK_NOPE], qr, qh[HEAD_QK:]], axis=0).astype(qT_ref.dtype)

    _kv_path(hb, wkv_ref, kvg_ref, wkbT_ref, wvT_ref, kgn_ref, kgr_ref, cos_ref, sin_ref,
             k_ref, vT_ref)


def _full(shape):
    nd = len(shape)
    return pl.BlockSpec(shape, lambda *_: (0,) * nd)


def _ctx_kv(ctx, mod3, g1, wkv, kvg, wkbT, wvT, kgn, kgr, cosT, sinT):
    b, t, d = ctx.shape
    return pl.pallas_call(
        _ctx_body,
        grid=(b,),
        in_specs=[pl.BlockSpec((1, t, d), lambda i: (i, 0, 0)),
                  pl.BlockSpec((1, 6, d), lambda i: (b, 0, 0)),
                  _full(g1.shape), _full(wkv.shape), _full(kvg.shape), _full(wkbT.shape),
                  _full(wvT.shape), _full(kgn.shape), _full(kgr.shape),
                  _full(cosT.shape), _full(sinT.shape)],
        out_specs=[pl.BlockSpec((1, N_HEADS, t, HEAD_PAD), lambda i: (i, 0, 0, 0)),
                   pl.BlockSpec((1, N_HEADS, V_ROWS, t), lambda i: (i, 0, 0, 0))],
        out_shape=[jax.ShapeDtypeStruct((b, N_HEADS, t, HEAD_PAD), BF16),
                   jax.ShapeDtypeStruct((b, N_HEADS, V_ROWS, t), BF16)],
        compiler_params=_params("arbitrary"),
        name="ctx_kv",
    )(ctx, mod3, g1, wkv, kvg, wkbT, wvT, kgn, kgr, cosT, sinT)


def _proj(x, mod3, g1, wf, cs, wg, bg, wq, qg, wqbT, qgain, wkv, kvg, wkbT, wvT, kgn, kgr,
          cosT, sinT, ts):
    b, s, d = x.shape
    tok = lambda shape: pl.BlockSpec(shape, lambda i, j: (0, j))
    return pl.pallas_call(
        _proj_body,
        grid=(b, s // ts),
        in_specs=[pl.BlockSpec((1, ts, d), lambda i, j: (i, j, 0)),
                  pl.BlockSpec((1, 6, d), lambda i, j: (i, 0, 0)),
                  _full(g1.shape), _full(wf.shape), _full(cs.shape), _full(wg.shape),
                  _full(bg.shape), _full(wq.shape), _full(qg.shape), _full(wqbT.shape),
                  tok((HEAD_PAD, ts)),
                  _full(wkv.shape), _full(kvg.shape), _full(wkbT.shape), _full(wvT.shape),
                  tok((QK_NOPE, ts)), tok((QK_ROPE, ts)), tok((QK_ROPE, ts)), tok((QK_ROPE, ts))],
        out_specs=[pl.BlockSpec((1, 2, ts, FOURIER_W), lambda i, j: (i, 0, j, 0)),
                   pl.BlockSpec((1, ts, 2 * d), lambda i, j: (i, j, 0)),
                   pl.BlockSpec((1, N_HEADS, HEAD_PAD, ts), lambda i, j: (i, 0, 0, j)),
                   pl.BlockSpec((1, N_HEADS, ts, HEAD_PAD), lambda i, j: (i, 0, j, 0)),
                   pl.BlockSpec((1, N_HEADS, V_ROWS, ts), lambda i, j: (i, 0, 0, j))],
        out_shape=[jax.ShapeDtypeStruct((b, 2, s, FOURIER_W), F32),
                   jax.ShapeDtypeStruct((b, s, 2 * d), BF16),
                   jax.ShapeDtypeStruct((b, N_HEADS, HEAD_PAD, s), BF16),
                   jax.ShapeDtypeStruct((b, N_HEADS, s, HEAD_PAD), BF16),
                   jax.ShapeDtypeStruct((b, N_HEADS, V_ROWS, s), BF16)],
        compiler_params=_params("arbitrary", "arbitrary"),
        name="proj",
    )(x, mod3, g1, wf, cs, wg, bg, wq, qg, wqbT, qgain, wkv, kvg, wkbT, wvT, kgn, kgr,
      cosT, sinT)


def _dft_body(ga_ref, gb_ref, z_ref, o_ref, a_scr):
    nblk = RADIX // DFT_BLK
    j = pl.program_id(1)

    @pl.when(j < nblk)
    def _():
        z = z_ref[0].reshape(2 * RADIX * DFT_BLK, FOURIER_W).astype(BF16)
        a = _dot(ga_ref[...], z)
        off = pl.multiple_of(j * DFT_BLK, DFT_BLK)
        a_scr[:, :, pl.ds(off, DFT_BLK), :] = a.reshape(2, RADIX, DFT_BLK, FOURIER_W)

    @pl.when(j >= nblk)
    def _():
        off = pl.multiple_of((j - nblk) * DFT_BLK, DFT_BLK)
        a = a_scr[:, pl.ds(off, DFT_BLK), :, :]
        a = a.reshape(2 * DFT_BLK * RADIX, FOURIER_W).astype(BF16)
        y = _dot(gb_ref[0], a)
        o_ref[0] = y.reshape(RADIX, DFT_BLK, FOURIER_W)


def _dft(ga, gb, z5):
    b = z5.shape[0]
    nblk = RADIX // DFT_BLK
    return pl.pallas_call(
        _dft_body,
        grid=(b, 2 * nblk),
        in_specs=[_full(ga.shape),
                  pl.BlockSpec((1,) + gb.shape[1:], lambda i, j: (jnp.maximum(j - nblk, 0), 0, 0)),
                  pl.BlockSpec((1, 2, RADIX, DFT_BLK, FOURIER_W),
                               lambda i, j: (i, 0, 0, jnp.minimum(j, nblk - 1), 0))],
        out_specs=pl.BlockSpec((1, RADIX, DFT_BLK, FOURIER_W),
                               lambda i, j: (i, 0, jnp.maximum(j - nblk, 0), 0)),
        out_shape=jax.ShapeDtypeStruct((b, RADIX, RADIX, FOURIER_W), F32),
        scratch_shapes=[pltpu.VMEM((2, RADIX, RADIX, FOURIER_W), F32)],
        compiler_params=_params("arbitrary", "arbitrary"),
        name="dft",
    )(ga, gb, z5)


def _attn_body(qT_ref, k_ref, vT_ref, kc_ref, vcT_ref, o_ref, *, tq, tkb):
    nkb = k_ref.shape[2] // tkb

    def colmax(s):
        r = s.shape[0] // 8
        return jnp.max(jnp.max(s.reshape(8, r, s.shape[1]), axis=0), axis=0, keepdims=True)

    def update(carry, s, vT_blk):
        m, acc = carry
        m_new = jnp.maximum(m, colmax(s))
        p = jnp.exp2(s - m_new).astype(BF16)
        acc = jnp.exp2(m - m_new) * acc + _dot(vT_blk, p)
        return m_new, acc

    def q_tile(qi, _):
        qoff = pl.multiple_of(qi * tq, tq)
        qT = qT_ref[0, 0, :, pl.ds(qoff, tq)]

        def scores(k_blk):
            return _dot(k_blk, qT)

        carry = (jnp.full((1, tq), -jnp.inf, F32), jnp.zeros((V_ROWS, tq), F32))
        s_cur = scores(k_ref[0, 0, 0:tkb, :])
        for i in range(nkb):
            s_next = scores(k_ref[0, 0, (i + 1) * tkb:(i + 2) * tkb, :] if i + 1 < nkb else kc_ref[0, 0])
            carry = update(carry, s_cur, vT_ref[0, 0, :, i * tkb:(i + 1) * tkb])
            s_cur = s_next
        _, acc = update(carry, s_cur, vcT_ref[0, 0])
        o_ref[0, :, pl.ds(qoff, tq)] = (acc[:V_DIM] / acc[V_DIM:V_DIM + 1]).astype(o_ref.dtype)
        return 0

    lax.fori_loop(0, qT_ref.shape[3] // tq, q_tile, 0)


def _attention(qT, k, vT, kc, vcT, tq, tkb):
    b, h, _, s = qT.shape
    sc = kc.shape[2]
    return pl.pallas_call(
        functools.partial(_attn_body, tq=tq, tkb=tkb),
        grid=(b, h),
        in_specs=[pl.BlockSpec((1, 1, HEAD_PAD, s), lambda i, j: (i, j, 0, 0)),
                  pl.BlockSpec((1, 1, s, HEAD_PAD), lambda i, j: (i, j, 0, 0)),
                  pl.BlockSpec((1, 1, V_ROWS, s), lambda i, j: (i, j, 0, 0)),
                  pl.BlockSpec((1, 1, sc, HEAD_PAD), lambda i, j: (i, j, 0, 0)),
                  pl.BlockSpec((1, 1, V_ROWS, sc), lambda i, j: (i, j, 0, 0))],
        out_specs=pl.BlockSpec((1, V_DIM, s), lambda i, j: (i, j, 0)),
        out_shape=jax.ShapeDtypeStruct((b, h * V_DIM, s), BF16),
        compiler_params=_params("arbitrary", "arbitrary"),
        name="attention",
    )(qT, k, vT, kc, vcT)


def _merge_body(fm_ref, aT_ref, g_ref, x_ref, mod_ref, wfo_ref, wao_ref, wout_ref, o_ref):
    d = x_ref.shape[-1]
    yf = _dot(fm_ref[0].astype(BF16), wfo_ref[...])
    ya = lax.dot_general(aT_ref[0], wao_ref[...], (((0,), (0,)), ((), ())),
                         preferred_element_type=F32)
    g = g_ref[0]
    y = g[:, :d].astype(F32) * yf + g[:, d:].astype(F32) * ya
    y2 = _dot(y.astype(BF16), wout_ref[...])
    o_ref[0] = x_ref[0] + mod_ref[0][2:3] * y2


def _merge(fm, aT, gates, x, mod3, wfo, wao, wout, ts):
    b, s, d = x.shape
    return pl.pallas_call(
        _merge_body,
        grid=(b, s // ts),
        in_specs=[pl.BlockSpec((1, ts, FOURIER_W), lambda i, j: (i, j, 0)),
                  pl.BlockSpec((1, aT.shape[1], ts), lambda i, j: (i, 0, j)),
                  pl.BlockSpec((1, ts, 2 * d), lambda i, j: (i, j, 0)),
                  pl.BlockSpec((1, ts, d), lambda i, j: (i, j, 0)),
                  pl.BlockSpec((1, 6, d), lambda i, j: (i, 0, 0)),
                  _full(wfo.shape), _full(wao.shape), _full(wout.shape)],
        out_specs=pl.BlockSpec((1, ts, d), lambda i, j: (i, j, 0)),
        out_shape=jax.ShapeDtypeStruct((b, s, d), F32),
        compiler_params=_params("arbitrary", "arbitrary"),
        name="merge",
    )(fm, aT, gates, x, mod3, wfo, wao, wout)


def _mlp_body(x_ref, mod_ref, g2_ref, wup_ref, wdn_ref, o_ref, *, fc):
    x = x_ref[0]
    m = mod_ref[0]
    hb = _norm_mod(x, m[3:5], g2_ref[...]).astype(BF16)
    acc = jnp.zeros(x.shape, F32)
    for c in range(wup_ref.shape[1] // fc):
        u = jnp.maximum(_dot(hb, wup_ref[:, c * fc:(c + 1) * fc]), 0.0)
        acc = acc + _dot((u * u).astype(BF16), wdn_ref[c * fc:(c + 1) * fc, :])
    o_ref[0] = x + m[5:6] * acc


def _mlp(x1, mod3, g2, wup, wdn, ts, fc):
    b, s, d = x1.shape
    return pl.pallas_call(
        functools.partial(_mlp_body, fc=fc),
        grid=(b, s // ts),
        in_specs=[pl.BlockSpec((1, ts, d), lambda i, j: (i, j, 0)),
                  pl.BlockSpec((1, 6, d), lambda i, j: (i, 0, 0)),
                  _full(g2.shape), _full(wup.shape), _full(wdn.shape)],
        out_specs=pl.BlockSpec((1, ts, d), lambda i, j: (i, j, 0)),
        out_shape=jax.ShapeDtypeStruct((b, s, d), F32),
        compiler_params=_params("arbitrary", "arbitrary"),
        name="mlp",
    )(x1, mod3, g2, wup, wdn)


@functools.lru_cache(maxsize=None)
def _dft_tables():
    j = np.arange(GROUP_W)
    ang = 2.0 * np.pi * np.outer(j, j) / GROUP_W
    cs = np.concatenate([np.cos(ang), -np.sin(ang)], axis=1) / math.sqrt(GROUP_W)

    a = np.arange(RADIX)
    ang = 2.0 * np.pi * np.outer(a, a) / RADIX
    c64, s64 = np.cos(ang), np.sin(ang)
    m1 = np.block([[c64, s64], [-s64, c64]]) / math.sqrt(RADIX)

    n = RADIX * RADIX
    k1 = np.arange(RADIX)[:, None, None]
    k2 = np.arange(RADIX)[None, :, None]
    bb = np.arange(RADIX)[None, None, :]
    th = 2.0 * np.pi * ((bb * (RADIX * k1 + k2)) % n) / n
    gc = np.cos(th) / math.sqrt(RADIX)
    gs = np.sin(th) / math.sqrt(RADIX)
    nblk = RADIX // DFT_BLK
    gtab = np.zeros((nblk, RADIX, DFT_BLK, 2, DFT_BLK, RADIX), np.float32)
    for jb in range(nblk):
        for q in range(DFT_BLK):
            gtab[jb, :, q, 0, q, :] = gc[:, jb * DFT_BLK + q, :]
            gtab[jb, :, q, 1, q, :] = gs[:, jb * DFT_BLK + q, :]
    gtab = gtab.reshape(nblk, RADIX * DFT_BLK, 2 * DFT_BLK * RADIX)
    ga = np.kron(m1, np.eye(DFT_BLK))
    return cs.astype(np.float32), ga.astype(np.float32), gtab


def _rope_tables_t(n):
    rows = n // GRID_W
    row = jnp.repeat(jnp.arange(rows, dtype=F32), GRID_W)
    col = jnp.tile(jnp.arange(GRID_W, dtype=F32), rows)
    n_freq = QK_ROPE // 4
    freqs = ROPE_THETA ** (-jnp.arange(n_freq, dtype=F32) / n_freq)
    ang_r = row[:, None] * freqs[None, :]
    ang_c = col[:, None] * freqs[None, :]
    ang = jnp.concatenate([ang_r, ang_r, ang_c, ang_c], axis=-1)
    return jnp.cos(ang).T, jnp.sin(ang).T


def kernel(x, c, ctx, c_ctx, ada_w, ada_b, norm1_g, norm2_g, w_in, b_gate, w_fourier, q_norm_g, w_qb,
           kv_norm_g, w_kvb, q_gain, k_gain, w_mla_o, w_out, w_up, w_down):
    bsz, s, d = x.shape
    sc = ctx.shape[1]
    assert ada_w.shape[0] == 1 and s == RADIX * RADIX and d == D_MODEL
    ts = 512

    cc = jnp.zeros((8, d), F32).at[:bsz].set(c).at[bsz].set(c_ctx)
    mod3 = _adaln(cc, ada_w[0], ada_b[0][None, :]).reshape(8, 6, d)

    o1, o2, o3, o4 = FOURIER_W, FOURIER_W + Q_LORA, FOURIER_W + Q_LORA + KV_LORA, \
        FOURIER_W + Q_LORA + KV_LORA + QK_ROPE
    w = w_in[0]
    wf = w[:, :o1].astype(BF16)
    wq = w[:, o1:o2].astype(BF16)
    wkv = jnp.pad(w[:, o2:o4], ((0, 0), (0, 2 * KV_LORA - (o4 - o2)))).astype(BF16)
    wg = w[:, o4:].astype(BF16)
    wqbT = jnp.pad(w_qb[0].reshape(Q_LORA, N_HEADS, HEAD_QK),
                   ((0, 0), (0, 0), (0, HEAD_PAD - HEAD_QK))).reshape(Q_LORA, -1).T.astype(BF16)
    wkvb = w_kvb[0].reshape(KV_LORA, N_HEADS, QK_NOPE + V_DIM)
    wkbT = wkvb[:, :, :QK_NOPE].reshape(KV_LORA, -1).T.astype(BF16)
    wvT = wkvb[:, :, QK_NOPE:].reshape(KV_LORA, -1).T.astype(BF16)

    g1 = norm1_g[0][None, :]
    g2 = norm2_g[0][None, :]
    qg = q_norm_g[0][None, :]
    kvg = kv_norm_g[0][None, :]
    bg = b_gate[0][None, :]
    qscale = (HEAD_QK ** -0.5) * math.log2(math.e)
    qgain = jnp.broadcast_to(jnp.pad(q_gain[0] * qscale, (0, HEAD_PAD - HEAD_QK))[:, None], (HEAD_PAD, s))
    kgn = jnp.broadcast_to(k_gain[0][:QK_NOPE, None], (QK_NOPE, s))
    kgr = jnp.broadcast_to(k_gain[0][QK_NOPE:, None], (QK_ROPE, s))
    cosT, sinT = _rope_tables_t(s)
    cos1 = jnp.ones((QK_ROPE, sc), F32)
    sin0 = jnp.zeros((QK_ROPE, sc), F32)

    cs_np, ga_np, gtab_np = _dft_tables()
    cs = jnp.asarray(cs_np).astype(BF16)
    ga = jnp.asarray(ga_np).astype(BF16)
    gtab = jnp.asarray(gtab_np).astype(BF16)

    kc, vcT = _ctx_kv(ctx, mod3, g1, wkv, kvg, wkbT, wvT, kgn[:, :sc], kgr[:, :sc], cos1, sin0)

    z, gates, qT, k, vT = _proj(x, mod3, g1, wf, cs, wg, bg, wq, qg, wqbT, qgain, wkv, kvg, wkbT, wvT,
                                kgn, kgr, cosT, sinT, ts)

    fm = _dft(ga, gtab, z.reshape(bsz, 2, RADIX, RADIX, FOURIER_W)).reshape(bsz, s, FOURIER_W)

    aT = _attention(qT, k, vT, kc, vcT, 512, 512)

    x1 = _merge(fm, aT, gates, x, mod3, w_fourier[0].astype(BF16), w_mla_o[0].astype(BF16),
                w_out[0].astype(BF16), ts)
    return _mlp(x1, mod3, g2, w_up[0].astype(BF16), w_down[0].astype(BF16), ts, 1024)
```

```python
import functools
import math

import numpy as np
import jax
import jax.numpy as jnp
from jax import lax
from jax.experimental import pallas as pl
from jax.experimental.pallas import tpu as pltpu

F32 = jnp.float32
BF16 = jnp.bfloat16

D_MODEL = 1024
GRID_W = 64
N_GROUPS = 4
GROUP_W = 128
FOURIER_W = N_GROUPS * GROUP_W
N_HEADS = 8
QK_NOPE = 64
QK_ROPE = 32
HEAD_QK = QK_NOPE + QK_ROPE
HEAD_PAD = 128
V_DIM = 64
V_ROWS = V_DIM + 16
Q_LORA = 256
KV_LORA = 128
ROPE_THETA = 10000.0
EPS = 1e-6
RADIX = 64
DFT_BLK = 8
ATTN_AHEAD = 2

V7X_VMEM_LIMIT = 56 * 1024 * 1024


def _params(*sem):
    return pltpu.CompilerParams(dimension_semantics=sem, vmem_limit_bytes=V7X_VMEM_LIMIT)


def _dot(a, b):
    return jnp.dot(a, b, preferred_element_type=F32)


def _rms_lanes(x, n):
    return x * lax.rsqrt(jnp.sum(x * x, axis=-1, keepdims=True) * (1.0 / n) + EPS)


def _rot_half_rows(x):
    return jnp.concatenate([-x[8:16], x[0:8], -x[24:32], x[16:24]], axis=0)


def _adaln_body(c_ref, w_ref, b_ref, o_ref):
    c = c_ref[...]
    s = c * jax.nn.sigmoid(c)
    o_ref[...] = _dot(s, w_ref[...]) + b_ref[...]


def _adaln(cc, ada_w, ada_b):
    rows, d = cc.shape
    n = ada_w.shape[1]
    tn = 1024
    return pl.pallas_call(
        _adaln_body,
        grid=(n // tn,),
        in_specs=[pl.BlockSpec((rows, d), lambda j: (0, 0)),
                  pl.BlockSpec((d, tn), lambda j: (0, j)),
                  pl.BlockSpec((1, tn), lambda j: (0, j))],
        out_specs=pl.BlockSpec((rows, tn), lambda j: (0, j)),
        out_shape=jax.ShapeDtypeStruct((rows, n), F32),
        compiler_params=_params("arbitrary"),
        name="adaln",
    )(cc, ada_w, ada_b)


def _norm_mod(x, m, g):
    ms = jnp.sum(x * x, axis=-1, keepdims=True) * (1.0 / x.shape[-1])
    return (x * lax.rsqrt(ms + EPS)) * (g * (1.0 + m[1:2])) + m[0:1]


def _kv_path(hb, wkv_ref, kvg_ref, wkbT_ref, wvT_ref, kgn_ref, kgr_ref, cos_ref, sin_ref,
             k_ref, vT_ref):
    t = hb.shape[0]
    kvp = _dot(hb, wkv_ref[...])
    kvn = _rms_lanes(kvp[:, :KV_LORA], KV_LORA) * kvg_ref[...]
    kvnT = kvn.T.astype(BF16)
    krT = kvp[:, KV_LORA:].T[0:QK_ROPE]
    knT = _dot(wkbT_ref[...], kvnT)
    vT = _dot(wvT_ref[...], kvnT)
    sr = jnp.sum(krT * krT, axis=0, keepdims=True)
    krg = krT * kgr_ref[...]
    kr_rot = krg * cos_ref[...] + _rot_half_rows(krg) * sin_ref[...]
    ones = jnp.ones((V_ROWS - V_DIM, t), F32)
    zpad = jnp.zeros((HEAD_PAD - HEAD_QK, t), F32)
    for h in range(N_HEADS):
        kn = knT[h * QK_NOPE:(h + 1) * QK_NOPE]
        ss = jnp.sum(kn * kn, axis=0, keepdims=True) + sr
        r = lax.rsqrt(ss * (1.0 / HEAD_QK) + EPS)
        kT = jnp.concatenate([kn * kgn_ref[...] * r, kr_rot * r, zpad], axis=0)
        k_ref[0, h] = kT.T.astype(k_ref.dtype)
        vh = vT[h * V_DIM:(h + 1) * V_DIM]
        vT_ref[0, h] = jnp.concatenate([vh, ones], axis=0).astype(vT_ref.dtype)


def _ctx_body(x_ref, mod_ref, g1_ref, wkv_ref, kvg_ref, wkbT_ref, wvT_ref, kgn_ref, kgr_ref,
              cos_ref, sin_ref, k_ref, vT_ref):
    h = _norm_mod(x_ref[0], mod_ref[0], g1_ref[...])
    _kv_path(h.astype(BF16), wkv_ref, kvg_ref, wkbT_ref, wvT_ref, kgn_ref, kgr_ref,
             cos_ref, sin_ref, k_ref, vT_ref)


def _proj_body(x_ref, mod_ref, g1_ref, wf_ref, cs_ref, wg_ref, bg_ref,
               wq_ref, qg_ref, wqbT_ref, qgain_ref,
               wkv_ref, kvg_ref, wkbT_ref, wvT_ref, kgn_ref, kgr_ref, cos_ref, sin_ref,
               z_ref, g_ref, qT_ref, k_ref, vT_ref):
    hb = _norm_mod(x_ref[0], mod_ref[0], g1_ref[...]).astype(BF16)

    f = _dot(hb, wf_ref[...]).astype(BF16)
    for g in range(N_GROUPS):
        z = _dot(f[:, g * GROUP_W:(g + 1) * GROUP_W], cs_ref[...])
        z_ref[0, 0, :, g * GROUP_W:(g + 1) * GROUP_W] = z[:, :GROUP_W].astype(z_ref.dtype)
        z_ref[0, 1, :, g * GROUP_W:(g + 1) * GROUP_W] = z[:, GROUP_W:].astype(z_ref.dtype)

    g_ref[0] = jax.nn.sigmoid(_dot(hb, wg_ref[...]) + bg_ref[...]).astype(g_ref.dtype)

    qn = _rms_lanes(_dot(hb, wq_ref[...]), Q_LORA) * qg_ref[...]
    qT = _dot(wqbT_ref[...], qn.T.astype(BF16))
    for h in range(N_HEADS):
        qh = qT[h * HEAD_PAD:(h + 1) * HEAD_PAD]
        r = lax.rsqrt(jnp.sum(qh * qh, axis=0, keepdims=True) * (1.0 / HEAD_QK) + EPS)
        qh = qh * qgain_ref[...] * r
        qr = qh[QK_NOPE:HEAD_QK]
        qr = qr * cos_ref[...] + _rot_half_rows(qr) * sin_ref[...]
        qT_ref[0, h] = jnp.concatenate([qh[:QK_NOPE], qr, qh[HEAD_QK:]], axis=0).astype(qT_ref.dtype)

    _kv_path(hb, wkv_ref, kvg_ref, wkbT_ref, wvT_ref, kgn_ref, kgr_ref, cos_ref, sin_ref,
             k_ref, vT_ref)


def _full(shape):
    nd = len(shape)
    return pl.BlockSpec(shape, lambda *_: (0,) * nd)


def _ctx_kv(ctx, mod3, g1, wkv, kvg, wkbT, wvT, kgn, kgr, cosT, sinT):
    b, t, d = ctx.shape
    return pl.pallas_call(
        _ctx_body,
        grid=(b,),
        in_specs=[pl.BlockSpec((1, t, d), lambda i: (i, 0, 0)),
                  pl.BlockSpec((1, 6, d), lambda i: (b, 0, 0)),
                  _full(g1.shape), _full(wkv.shape), _full(kvg.shape), _full(wkbT.shape),
                  _full(wvT.shape), _full(kgn.shape), _full(kgr.shape),
                  _full(cosT.shape), _full(sinT.shape)],
        out_specs=[pl.BlockSpec((1, N_HEADS, t, HEAD_PAD), lambda i: (i, 0, 0, 0)),
                   pl.BlockSpec((1, N_HEADS, V_ROWS, t), lambda i: (i, 0, 0, 0))],
        out_shape=[jax.ShapeDtypeStruct((b, N_HEADS, t, HEAD_PAD), BF16),
                   jax.ShapeDtypeStruct((b, N_HEADS, V_ROWS, t), BF16)],
        compiler_params=_params("arbitrary"),
        name="ctx_kv",
    )(ctx, mod3, g1, wkv, kvg, wkbT, wvT, kgn, kgr, cosT, sinT)


def _proj(x, mod3, g1, wf, cs, wg, bg, wq, qg, wqbT, qgain, wkv, kvg, wkbT, wvT, kgn, kgr,
          cosT, sinT, ts):
    b, s, d = x.shape
    tok = lambda shape: pl.BlockSpec(shape, lambda i, j: (0, j))
    return pl.pallas_call(
        _proj_body,
        grid=(b, s // ts),
        in_specs=[pl.BlockSpec((1, ts, d), lambda i, j: (i, j, 0)),
                  pl.BlockSpec((1, 6, d), lambda i, j: (i, 0, 0)),
                  _full(g1.shape), _full(wf.shape), _full(cs.shape), _full(wg.shape),
                  _full(bg.shape), _full(wq.shape), _full(qg.shape), _full(wqbT.shape),
                  tok((HEAD_PAD, ts)),
                  _full(wkv.shape), _full(kvg.shape), _full(wkbT.shape), _full(wvT.shape),
                  tok((QK_NOPE, ts)), tok((QK_ROPE, ts)), tok((QK_ROPE, ts)), tok((QK_ROPE, ts))],
        out_specs=[pl.BlockSpec((1, 2, ts, FOURIER_W), lambda i, j: (i, 0, j, 0)),
                   pl.BlockSpec((1, ts, 2 * d), lambda i, j: (i, j, 0)),
                   pl.BlockSpec((1, N_HEADS, HEAD_PAD, ts), lambda i, j: (i, 0, 0, j)),
                   pl.BlockSpec((1, N_HEADS, ts, HEAD_PAD), lambda i, j: (i, 0, j, 0)),
                   pl.BlockSpec((1, N_HEADS, V_ROWS, ts), lambda i, j: (i, 0, 0, j))],
        out_shape=[jax.ShapeDtypeStruct((b, 2, s, FOURIER_W), F32),
                   jax.ShapeDtypeStruct((b, s, 2 * d), BF16),
                   jax.ShapeDtypeStruct((b, N_HEADS, HEAD_PAD, s), BF16),
                   jax.ShapeDtypeStruct((b, N_HEADS, s, HEAD_PAD), BF16),
                   jax.ShapeDtypeStruct((b, N_HEADS, V_ROWS, s), BF16)],
        compiler_params=_params("arbitrary", "arbitrary"),
        name="proj",
    )(x, mod3, g1, wf, cs, wg, bg, wq, qg, wqbT, qgain, wkv, kvg, wkbT, wvT, kgn, kgr,
      cosT, sinT)


def _dft_body(ga_ref, gb_ref, z_ref, o_ref, a_scr):
    nblk = RADIX // DFT_BLK
    j = pl.program_id(1)

    @pl.when(j < nblk)
    def _():
        z = z_ref[0].reshape(2 * RADIX * DFT_BLK, FOURIER_W).astype(BF16)
        a = _dot(ga_ref[...], z)
        off = pl.multiple_of(j * DFT_BLK, DFT_BLK)
        a_scr[:, :, pl.ds(off, DFT_BLK), :] = a.reshape(2, RADIX, DFT_BLK, FOURIER_W)

    @pl.when(j >= nblk)
    def _():
        off = pl.multiple_of((j - nblk) * DFT_BLK, DFT_BLK)
        a = a_scr[:, pl.ds(off, DFT_BLK), :, :]
        a = a.reshape(2 * DFT_BLK * RADIX, FOURIER_W).astype(BF16)
        y = _dot(gb_ref[0], a)
        o_ref[0] = y.reshape(RADIX, DFT_BLK, FOURIER_W)


def _dft(ga, gb, z5):
    b = z5.shape[0]
    nblk = RADIX // DFT_BLK
    return pl.pallas_call(
        _dft_body,
        grid=(b, 2 * nblk),
        in_specs=[_full(ga.shape),
                  pl.BlockSpec((1,) + gb.shape[1:], lambda i, j: (jnp.maximum(j - nblk, 0), 0, 0)),
                  pl.BlockSpec((1, 2, RADIX, DFT_BLK, FOURIER_W),
                               lambda i, j: (i, 0, 0, jnp.minimum(j, nblk - 1), 0))],
        out_specs=pl.BlockSpec((1, RADIX, DFT_BLK, FOURIER_W),
                               lambda i, j: (i, 0, jnp.maximum(j - nblk, 0), 0)),
        out_shape=jax.ShapeDtypeStruct((b, RADIX, RADIX, FOURIER_W), F32),
        scratch_shapes=[pltpu.VMEM((2, RADIX, RADIX, FOURIER_W), F32)],
        compiler_params=_params("arbitrary", "arbitrary"),
        name="dft",
    )(ga, gb, z5)


def _attn_body(qT_ref, k_ref, vT_ref, kc_ref, vcT_ref, o_ref, *, tq, tkb):
    nkb = k_ref.shape[2] // tkb

    def colmax(s):
        r = s.shape[0] // 8
        return jnp.max(jnp.max(s.reshape(8, r, s.shape[1]), axis=0), axis=0, keepdims=True)

    def update(carry, s, vT_blk):
        m, acc = carry
        m_new = jnp.maximum(m, colmax(s))
        p = jnp.exp2(s - m_new).astype(BF16)
        acc = jnp.exp2(m - m_new) * acc + _dot(vT_blk, p)
        return m_new, acc

    nq = qT_ref.shape[3] // tq

    def k_blk(i):
        return k_ref[0, 0, i * tkb:(i + 1) * tkb, :] if i < nkb else kc_ref[0, 0]

    def vT_blk(i):
        return vT_ref[0, 0, :, i * tkb:(i + 1) * tkb] if i < nkb else vcT_ref[0, 0]

    def q_cols(qi):
        return qT_ref[0, 0, :, pl.ds(pl.multiple_of(qi * tq, tq), tq)]

    def q_tile(qi, pending):
        qT = q_cols(qi)
        qT_next = q_cols(jnp.minimum(qi + 1, nq - 1))
        pending = list(pending)
        carry = (jnp.full((1, tq), -jnp.inf, F32), jnp.zeros((V_ROWS, tq), F32))
        for i in range(nkb + 1):
            j = i + ATTN_AHEAD
            pending.append(_dot(k_blk(j), qT) if j <= nkb else _dot(k_blk(j - nkb - 1), qT_next))
            carry = update(carry, pending.pop(0), vT_blk(i))
        _, acc = carry
        o_ref[0, :, pl.ds(pl.multiple_of(qi * tq, tq), tq)] = (
            acc[:V_DIM] / acc[V_DIM:V_DIM + 1]).astype(o_ref.dtype)
        return tuple(pending)

    lax.fori_loop(0, nq, q_tile, tuple(_dot(k_blk(i), q_cols(0)) for i in range(ATTN_AHEAD)))


def _attention(qT, k, vT, kc, vcT, tq, tkb):
    b, h, _, s = qT.shape
    sc = kc.shape[2]
    return pl.pallas_call(
        functools.partial(_attn_body, tq=tq, tkb=tkb),
        grid=(b, h),
        in_specs=[pl.BlockSpec((1, 1, HEAD_PAD, s), lambda i, j: (i, j, 0, 0)),
                  pl.BlockSpec((1, 1, s, HEAD_PAD), lambda i, j: (i, j, 0, 0)),
                  pl.BlockSpec((1, 1, V_ROWS, s), lambda i, j: (i, j, 0, 0)),
                  pl.BlockSpec((1, 1, sc, HEAD_PAD), lambda i, j: (i, j, 0, 0)),
                  pl.BlockSpec((1, 1, V_ROWS, sc), lambda i, j: (i, j, 0, 0))],
        out_specs=pl.BlockSpec((1, V_DIM, s), lambda i, j: (i, j, 0)),
        out_shape=jax.ShapeDtypeStruct((b, h * V_DIM, s), BF16),
        compiler_params=_params("arbitrary", "arbitrary"),
        name="attention",
    )(qT, k, vT, kc, vcT)


def _merge_body(fm_ref, aT_ref, g_ref, x_ref, mod_ref, wfo_ref, wao_ref, wout_ref, o_ref):
    d = x_ref.shape[-1]
    yf = _dot(fm_ref[0].astype(BF16), wfo_ref[...])
    ya = lax.dot_general(aT_ref[0], wao_ref[...], (((0,), (0,)), ((), ())),
                         preferred_element_type=F32)
    g = g_ref[0]
    y = g[:, :d].astype(F32) * yf + g[:, d:].astype(F32) * ya
    y2 = _dot(y.astype(BF16), wout_ref[...])
    o_ref[0] = x_ref[0] + mod_ref[0][2:3] * y2


def _merge(fm, aT, gates, x, mod3, wfo, wao, wout, ts):
    b, s, d = x.shape
    return pl.pallas_call(
        _merge_body,
        grid=(b, s // ts),
        in_specs=[pl.BlockSpec((1, ts, FOURIER_W), lambda i, j: (i, j, 0)),
                  pl.BlockSpec((1, aT.shape[1], ts), lambda i, j: (i, 0, j)),
                  pl.BlockSpec((1, ts, 2 * d), lambda i, j: (i, j, 0)),
                  pl.BlockSpec((1, ts, d), lambda i, j: (i, j, 0)),
                  pl.BlockSpec((1, 6, d), lambda i, j: (i, 0, 0)),
                  _full(wfo.shape), _full(wao.shape), _full(wout.shape)],
        out_specs=pl.BlockSpec((1, ts, d), lambda i, j: (i, j, 0)),
        out_shape=jax.ShapeDtypeStruct((b, s, d), F32),
        compiler_params=_params("arbitrary", "arbitrary"),
        name="merge",
    )(fm, aT, gates, x, mod3, wfo, wao, wout)


def _mlp_body(x_ref, mod_ref, g2_ref, wup_ref, wdn_ref, o_ref, *, fc):
    x = x_ref[0]
    m = mod_ref[0]
    hb = _norm_mod(x, m[3:5], g2_ref[...]).astype(BF16)
    acc = jnp.zeros(x.shape, F32)
    for c in range(wup_ref.shape[1] // fc):
        u = jnp.maximum(_dot(hb, wup_ref[:, c * fc:(c + 1) * fc]), 0.0)
        acc = acc + _dot((u * u).astype(BF16), wdn_ref[c * fc:(c + 1) * fc, :])
    o_ref[0] = x + m[5:6] * acc


def _mlp(x1, mod3, g2, wup, wdn, ts, fc):
    b, s, d = x1.shape
    return pl.pallas_call(
        functools.partial(_mlp_body, fc=fc),
        grid=(b, s // ts),
        in_specs=[pl.BlockSpec((1, ts, d), lambda i, j: (i, j, 0)),
                  pl.BlockSpec((1, 6, d), lambda i, j: (i, 0, 0)),
                  _full(g2.shape), _full(wup.shape), _full(wdn.shape)],
        out_specs=pl.BlockSpec((1, ts, d), lambda i, j: (i, j, 0)),
        out_shape=jax.ShapeDtypeStruct((b, s, d), F32),
        compiler_params=_params("arbitrary", "arbitrary"),
        name="mlp",
    )(x1, mod3, g2, wup, wdn)


@functools.lru_cache(maxsize=None)
def _dft_tables():
    j = np.arange(GROUP_W)
    ang = 2.0 * np.pi * np.outer(j, j) / GROUP_W
    cs = np.concatenate([np.cos(ang), -np.sin(ang)], axis=1) / math.sqrt(GROUP_W)

    a = np.arange(RADIX)
    ang = 2.0 * np.pi * np.outer(a, a) / RADIX
    c64, s64 = np.cos(ang), np.sin(ang)
    m1 = np.block([[c64, s64], [-s64, c64]]) / math.sqrt(RADIX)

    n = RADIX * RADIX
    k1 = np.arange(RADIX)[:, None, None]
    k2 = np.arange(RADIX)[None, :, None]
    bb = np.arange(RADIX)[None, None, :]
    th = 2.0 * np.pi * ((bb * (RADIX * k1 + k2)) % n) / n
    gc = np.cos(th) / math.sqrt(RADIX)
    gs = np.sin(th) / math.sqrt(RADIX)
    nblk = RADIX // DFT_BLK
    gtab = np.zeros((nblk, RADIX, DFT_BLK, 2, DFT_BLK, RADIX), np.float32)
    for jb in range(nblk):
        for q in range(DFT_BLK):
            gtab[jb, :, q, 0, q, :] = gc[:, jb * DFT_BLK + q, :]
            gtab[jb, :, q, 1, q, :] = gs[:, jb * DFT_BLK + q, :]
    gtab = gtab.reshape(nblk, RADIX * DFT_BLK, 2 * DFT_BLK * RADIX)
    ga = np.kron(m1, np.eye(DFT_BLK))
    return cs.astype(np.float32), ga.astype(np.float32), gtab


def _rope_tables_t(n):
    rows = n // GRID_W
    row = jnp.repeat(jnp.arange(rows, dtype=F32), GRID_W)
    col = jnp.tile(jnp.arange(GRID_W, dtype=F32), rows)
    n_freq = QK_ROPE // 4
    freqs = ROPE_THETA ** (-jnp.arange(n_freq, dtype=F32) / n_freq)
    ang_r = row[:, None] * freqs[None, :]
    ang_c = col[:, None] * freqs[None, :]
    ang = jnp.concatenate([ang_r, ang_r, ang_c, ang_c], axis=-1)
    return jnp.cos(ang).T, jnp.sin(ang).T


def kernel(x, c, ctx, c_ctx, ada_w, ada_b, norm1_g, norm2_g, w_in, b_gate, w_fourier, q_norm_g, w_qb,
           kv_norm_g, w_kvb, q_gain, k_gain, w_mla_o, w_out, w_up, w_down):
    bsz, s, d = x.shape
    sc = ctx.shape[1]
    assert ada_w.shape[0] == 1 and s == RADIX * RADIX and d == D_MODEL
    ts = 512

    cc = jnp.zeros((8, d), F32).at[:bsz].set(c).at[bsz].set(c_ctx)
    mod3 = _adaln(cc, ada_w[0], ada_b[0][None, :]).reshape(8, 6, d)

    o1, o2, o3, o4 = FOURIER_W, FOURIER_W + Q_LORA, FOURIER_W + Q_LORA + KV_LORA, \
        FOURIER_W + Q_LORA + KV_LORA + QK_ROPE
    w = w_in[0]
    wf = w[:, :o1].astype(BF16)
    wq = w[:, o1:o2].astype(BF16)
    wkv = jnp.pad(w[:, o2:o4], ((0, 0), (0, 2 * KV_LORA - (o4 - o2)))).astype(BF16)
    wg = w[:, o4:].astype(BF16)
    wqbT = jnp.pad(w_qb[0].reshape(Q_LORA, N_HEADS, HEAD_QK),
                   ((0, 0), (0, 0), (0, HEAD_PAD - HEAD_QK))).reshape(Q_LORA, -1).T.astype(BF16)
    wkvb = w_kvb[0].reshape(KV_LORA, N_HEADS, QK_NOPE + V_DIM)
    wkbT = wkvb[:, :, :QK_NOPE].reshape(KV_LORA, -1).T.astype(BF16)
    wvT = wkvb[:, :, QK_NOPE:].reshape(KV_LORA, -1).T.astype(BF16)

    g1 = norm1_g[0][None, :]
    g2 = norm2_g[0][None, :]
    qg = q_norm_g[0][None, :]
    kvg = kv_norm_g[0][None, :]
    bg = b_gate[0][None, :]
    qscale = (HEAD_QK ** -0.5) * math.log2(math.e)
    qgain = jnp.broadcast_to(jnp.pad(q_gain[0] * qscale, (0, HEAD_PAD - HEAD_QK))[:, None], (HEAD_PAD, s))
    kgn = jnp.broadcast_to(k_gain[0][:QK_NOPE, None], (QK_NOPE, s))
    kgr = jnp.broadcast_to(k_gain[0][QK_NOPE:, None], (QK_ROPE, s))
    cosT, sinT = _rope_tables_t(s)
    cos1 = jnp.ones((QK_ROPE, sc), F32)
    sin0 = jnp.zeros((QK_ROPE, sc), F32)

    cs_np, ga_np, gtab_np = _dft_tables()
    cs = jnp.asarray(cs_np).astype(BF16)
    ga = jnp.asarray(ga_np).astype(BF16)
    gtab = jnp.asarray(gtab_np).astype(BF16)

    kc, vcT = _ctx_kv(ctx, mod3, g1, wkv, kvg, wkbT, wvT, kgn[:, :sc], kgr[:, :sc], cos1, sin0)

    z, gates, qT, k, vT = _proj(x, mod3, g1, wf, cs, wg, bg, wq, qg, wqbT, qgain, wkv, kvg, wkbT, wvT,
                                kgn, kgr, cosT, sinT, ts)

    fm = _dft(ga, gtab, z.reshape(bsz, 2, RADIX, RADIX, FOURIER_W)).reshape(bsz, s, FOURIER_W)

    aT = _attention(qT, k, vT, kc, vcT, 512, 256)

    x1 = _merge(fm, aT, gates, x, mod3, w_fourier[0].astype(BF16), w_mla_o[0].astype(BF16),
                w_out[0].astype(BF16), ts)
    return _mlp(x1, mod3, g2, w_up[0].astype(BF16), w_down[0].astype(BF16), ts, 1024)
```

```python
import functools
import math

import numpy as np
import jax
import jax.numpy as jnp
from jax import lax
from jax.experimental import pallas as pl
from jax.experimental.pallas import tpu as pltpu

F32 = jnp.float32
BF16 = jnp.bfloat16

D_MODEL = 1024
GRID_W = 64
N_GROUPS = 4
GROUP_W = 128
FOURIER_W = N_GROUPS * GROUP_W
N_HEADS = 8
QK_NOPE = 64
QK_ROPE = 32
HEAD_QK = QK_NOPE + QK_ROPE
HEAD_PAD = 128
V_DIM = 64
V_ROWS = V_DIM + 16
Q_LORA = 256
KV_LORA = 128
ROPE_THETA = 10000.0
EPS = 1e-6
RADIX = 64
DFT_BLK = 8
GATE_CHUNK = 512
ATTN_AHEAD = 2

V7X_VMEM_LIMIT = 56 * 1024 * 1024


def _params(*sem):
    return pltpu.CompilerParams(dimension_semantics=sem, vmem_limit_bytes=V7X_VMEM_LIMIT)


def _dot(a, b):
    return jnp.dot(a, b, preferred_element_type=F32)


def _rms_lanes(x, n):
    return x * lax.rsqrt(jnp.sum(x * x, axis=-1, keepdims=True) * (1.0 / n) + EPS)


def _rot_half_rows(x):
    return jnp.concatenate([-x[8:16], x[0:8], -x[24:32], x[16:24]], axis=0)


def _adaln_body(c_ref, w_ref, b_ref, o_ref):
    c = c_ref[...]
    s = c * jax.nn.sigmoid(c)
    o_ref[...] = _dot(s, w_ref[...]) + b_ref[...]


def _adaln(cc, ada_w, ada_b):
    rows, d = cc.shape
    n = ada_w.shape[1]
    tn = 1024
    return pl.pallas_call(
        _adaln_body,
        grid=(n // tn,),
        in_specs=[pl.BlockSpec((rows, d), lambda j: (0, 0)),
                  pl.BlockSpec((d, tn), lambda j: (0, j)),
                  pl.BlockSpec((1, tn), lambda j: (0, j))],
        out_specs=pl.BlockSpec((rows, tn), lambda j: (0, j)),
        out_shape=jax.ShapeDtypeStruct((rows, n), F32),
        compiler_params=_params("arbitrary"),
        name="adaln",
    )(cc, ada_w, ada_b)


def _norm_mod(x, m, g):
    ms = jnp.sum(x * x, axis=-1, keepdims=True) * (1.0 / x.shape[-1])
    return (x * lax.rsqrt(ms + EPS)) * (g * (1.0 + m[1:2])) + m[0:1]


def _kv_latents(kvp, kvg_ref):
    kvn = _rms_lanes(kvp[:, :KV_LORA], KV_LORA) * kvg_ref[...]
    return kvn.T.astype(BF16), kvp[:, KV_LORA:].T[0:QK_ROPE]


def _kv_heads(knT, vT, krT, kgn_ref, kgr_ref, cos_ref, sin_ref, k_ref, vT_ref):
    t = knT.shape[1]
    sr = jnp.sum(krT * krT, axis=0, keepdims=True)
    krg = krT * kgr_ref[...]
    kr_rot = krg * cos_ref[...] + _rot_half_rows(krg) * sin_ref[...]
    ones = jnp.ones((V_ROWS - V_DIM, t), F32)
    zpad = jnp.zeros((HEAD_PAD - HEAD_QK, t), F32)
    for h in range(N_HEADS):
        kn = knT[h * QK_NOPE:(h + 1) * QK_NOPE]
        ss = jnp.sum(kn * kn, axis=0, keepdims=True) + sr
        r = lax.rsqrt(ss * (1.0 / HEAD_QK) + EPS)
        kT = jnp.concatenate([kn * kgn_ref[...] * r, kr_rot * r, zpad], axis=0)
        k_ref[0, h] = kT.T.astype(k_ref.dtype)
        vh = vT[h * V_DIM:(h + 1) * V_DIM]
        vT_ref[0, h] = jnp.concatenate([vh, ones], axis=0).astype(vT_ref.dtype)


def _q_heads(qT, qgain_ref, cos_ref, sin_ref, qT_ref):
    for h in range(N_HEADS):
        qh = qT[h * HEAD_PAD:(h + 1) * HEAD_PAD]
        r = lax.rsqrt(jnp.sum(qh * qh, axis=0, keepdims=True) * (1.0 / HEAD_QK) + EPS)
        qh = qh * qgain_ref[...] * r
        qr = qh[QK_NOPE:HEAD_QK]
        qr = qr * cos_ref[...] + _rot_half_rows(qr) * sin_ref[...]
        qT_ref[0, h] = jnp.concatenate([qh[:QK_NOPE], qr, qh[HEAD_QK:]], axis=0).astype(qT_ref.dtype)


def _ctx_body(x_ref, mod_ref, g1_ref, wkv_ref, kvg_ref, wkbT_ref, wvT_ref, kgn_ref, kgr_ref,
              cos_ref, sin_ref, k_ref, vT_ref):
    hb = _norm_mod(x_ref[0], mod_ref[0], g1_ref[...]).astype(BF16)
    kvnT, krT = _kv_latents(_dot(hb, wkv_ref[...]), kvg_ref)
    _kv_heads(_dot(wkbT_ref[...], kvnT), _dot(wvT_ref[...], kvnT), krT,
              kgn_ref, kgr_ref, cos_ref, sin_ref, k_ref, vT_ref)


def _proj_body(x_ref, mod_ref, g1_ref, wf_ref, cs_ref, wg_ref, bg_ref,
               wq_ref, qg_ref, wqbT_ref, qgain_ref,
               wkv_ref, kvg_ref, wkbT_ref, wvT_ref, kgn_ref, kgr_ref, cos_ref, sin_ref,
               z_ref, g_ref, qT_ref, k_ref, vT_ref):
    hb = _norm_mod(x_ref[0], mod_ref[0], g1_ref[...]).astype(BF16)
    qp = _dot(hb, wq_ref[...])
    kvp = _dot(hb, wkv_ref[...])
    f = _dot(hb, wf_ref[...]).astype(BF16)

    qnT = (_rms_lanes(qp, Q_LORA) * qg_ref[...]).T.astype(BF16)
    kvnT, krT = _kv_latents(kvp, kvg_ref)

    for g in range(N_GROUPS):
        z = _dot(f[:, g * GROUP_W:(g + 1) * GROUP_W], cs_ref[...])
        z_ref[0, 0, :, g * GROUP_W:(g + 1) * GROUP_W] = z[:, :GROUP_W].astype(z_ref.dtype)
        z_ref[0, 1, :, g * GROUP_W:(g + 1) * GROUP_W] = z[:, GROUP_W:].astype(z_ref.dtype)

    qT = _dot(wqbT_ref[...], qnT)
    knT = _dot(wkbT_ref[...], kvnT)
    vT = _dot(wvT_ref[...], kvnT)

    def gate_chunk(c):
        cols = slice(c * GATE_CHUNK, (c + 1) * GATE_CHUNK)
        g_ref[0, :, cols] = jax.nn.sigmoid(_dot(hb, wg_ref[:, cols]) + bg_ref[:, cols]).astype(g_ref.dtype)

    gate_chunk(0)
    _q_heads(qT, qgain_ref, cos_ref, sin_ref, qT_ref)
    gate_chunk(1)
    _kv_heads(knT, vT, krT, kgn_ref, kgr_ref, cos_ref, sin_ref, k_ref, vT_ref)
    for c in range(2, g_ref.shape[2] // GATE_CHUNK):
        gate_chunk(c)


def _full(shape):
    nd = len(shape)
    return pl.BlockSpec(shape, lambda *_: (0,) * nd, pipeline_mode=pl.Buffered(1))


def _ctx_kv(ctx, mod3, g1, wkv, kvg, wkbT, wvT, kgn, kgr, cosT, sinT):
    b, t, d = ctx.shape
    return pl.pallas_call(
        _ctx_body,
        grid=(b,),
        in_specs=[pl.BlockSpec((1, t, d), lambda i: (i, 0, 0)),
                  pl.BlockSpec((1, 6, d), lambda i: (b, 0, 0)),
                  _full(g1.shape), _full(wkv.shape), _full(kvg.shape), _full(wkbT.shape),
                  _full(wvT.shape), _full(kgn.shape), _full(kgr.shape),
                  _full(cosT.shape), _full(sinT.shape)],
        out_specs=[pl.BlockSpec((1, N_HEADS, t, HEAD_PAD), lambda i: (i, 0, 0, 0)),
                   pl.BlockSpec((1, N_HEADS, V_ROWS, t), lambda i: (i, 0, 0, 0))],
        out_shape=[jax.ShapeDtypeStruct((b, N_HEADS, t, HEAD_PAD), BF16),
                   jax.ShapeDtypeStruct((b, N_HEADS, V_ROWS, t), BF16)],
        compiler_params=_params("arbitrary"),
        name="ctx_kv",
    )(ctx, mod3, g1, wkv, kvg, wkbT, wvT, kgn, kgr, cosT, sinT)


def _proj(x, mod3, g1, wf, cs, wg, bg, wq, qg, wqbT, qgain, wkv, kvg, wkbT, wvT, kgn, kgr,
          cosT, sinT, ts):
    b, s, d = x.shape
    tok = lambda shape: pl.BlockSpec(shape, lambda i, j: (0, j))
    return pl.pallas_call(
        _proj_body,
        grid=(b, s // ts),
        in_specs=[pl.BlockSpec((1, ts, d), lambda i, j: (i, j, 0)),
                  pl.BlockSpec((1, 6, d), lambda i, j: (i, 0, 0)),
                  _full(g1.shape), _full(wf.shape), _full(cs.shape), _full(wg.shape),
                  _full(bg.shape), _full(wq.shape), _full(qg.shape), _full(wqbT.shape),
                  tok((HEAD_PAD, ts)),
                  _full(wkv.shape), _full(kvg.shape), _full(wkbT.shape), _full(wvT.shape),
                  tok((QK_NOPE, ts)), tok((QK_ROPE, ts)), tok((QK_ROPE, ts)), tok((QK_ROPE, ts))],
        out_specs=[pl.BlockSpec((1, 2, ts, FOURIER_W), lambda i, j: (i, 0, j, 0)),
                   pl.BlockSpec((1, ts, 2 * d), lambda i, j: (i, j, 0)),
                   pl.BlockSpec((1, N_HEADS, HEAD_PAD, ts), lambda i, j: (i, 0, 0, j)),
                   pl.BlockSpec((1, N_HEADS, ts, HEAD_PAD), lambda i, j: (i, 0, j, 0)),
                   pl.BlockSpec((1, N_HEADS, V_ROWS, ts), lambda i, j: (i, 0, 0, j))],
        out_shape=[jax.ShapeDtypeStruct((b, 2, s, FOURIER_W), F32),
                   jax.ShapeDtypeStruct((b, s, 2 * d), BF16),
                   jax.ShapeDtypeStruct((b, N_HEADS, HEAD_PAD, s), BF16),
                   jax.ShapeDtypeStruct((b, N_HEADS, s, HEAD_PAD), BF16),
                   jax.ShapeDtypeStruct((b, N_HEADS, V_ROWS, s), BF16)],
        compiler_params=_params("arbitrary", "arbitrary"),
        name="proj",
    )(x, mod3, g1, wf, cs, wg, bg, wq, qg, wqbT, qgain, wkv, kvg, wkbT, wvT, kgn, kgr,
      cosT, sinT)


def _dft_body(ga_ref, gb_ref, z_ref, o_ref, a_scr):
    nblk = RADIX // DFT_BLK
    j = pl.program_id(1)

    @pl.when(j < nblk)
    def _():
        z = z_ref[0].reshape(2 * RADIX * DFT_BLK, FOURIER_W).astype(BF16)
        a = _dot(ga_ref[...], z)
        off = pl.multiple_of(j * DFT_BLK, DFT_BLK)
        a_scr[:, :, pl.ds(off, DFT_BLK), :] = a.reshape(2, RADIX, DFT_BLK, FOURIER_W)

    @pl.when(j >= nblk)
    def _():
        off = pl.multiple_of((j - nblk) * DFT_BLK, DFT_BLK)
        a = a_scr[:, pl.ds(off, DFT_BLK), :, :]
        a = a.reshape(2 * DFT_BLK * RADIX, FOURIER_W).astype(BF16)
        y = _dot(gb_ref[j - nblk], a)
        o_ref[0] = y.reshape(RADIX, DFT_BLK, FOURIER_W)


def _dft(ga, gb, z5):
    b = z5.shape[0]
    nblk = RADIX // DFT_BLK
    return pl.pallas_call(
        _dft_body,
        grid=(b, 2 * nblk),
        in_specs=[_full(ga.shape), _full(gb.shape),
                  pl.BlockSpec((1, 2, RADIX, DFT_BLK, FOURIER_W),
                               lambda i, j: (i, 0, 0, jnp.minimum(j, nblk - 1), 0))],
        out_specs=pl.BlockSpec((1, RADIX, DFT_BLK, FOURIER_W),
                               lambda i, j: (i, 0, jnp.maximum(j - nblk, 0), 0)),
        out_shape=jax.ShapeDtypeStruct((b, RADIX, RADIX, FOURIER_W), F32),
        scratch_shapes=[pltpu.VMEM((2, RADIX, RADIX, FOURIER_W), F32)],
        compiler_params=_params("arbitrary", "arbitrary"),
        name="dft",
    )(ga, gb, z5)


def _attn_body(qT_ref, k_ref, vT_ref, kc_ref, vcT_ref, o_ref, *, tq, tkb):
    nkb = k_ref.shape[2] // tkb

    def colmax(s):
        r = s.shape[0] // 8
        return jnp.max(jnp.max(s.reshape(8, r, s.shape[1]), axis=0), axis=0, keepdims=True)

    def update(carry, s, vT_blk):
        m, acc = carry
        m_new = jnp.maximum(m, colmax(s))
        p = jnp.exp2(s - m_new).astype(BF16)
        acc = jnp.exp2(m - m_new) * acc + _dot(vT_blk, p)
        return m_new, acc

    nq = qT_ref.shape[3] // tq

    def k_blk(i):
        return k_ref[0, 0, i * tkb:(i + 1) * tkb, :] if i < nkb else kc_ref[0, 0]

    def vT_blk(i):
        return vT_ref[0, 0, :, i * tkb:(i + 1) * tkb] if i < nkb else vcT_ref[0, 0]

    def q_cols(qi):
        return qT_ref[0, 0, :, pl.ds(pl.multiple_of(qi * tq, tq), tq)]

    def q_tile(qi, pending):
        qT = q_cols(qi)
        qT_next = q_cols(jnp.minimum(qi + 1, nq - 1))
        pending = list(pending)
        carry = (jnp.full((1, tq), -jnp.inf, F32), jnp.zeros((V_ROWS, tq), F32))
        for i in range(nkb + 1):
            j = i + ATTN_AHEAD
            pending.append(_dot(k_blk(j), qT) if j <= nkb else _dot(k_blk(j - nkb - 1), qT_next))
            carry = update(carry, pending.pop(0), vT_blk(i))
        _, acc = carry
        o_ref[0, :, pl.ds(pl.multiple_of(qi * tq, tq), tq)] = (
            acc[:V_DIM] / acc[V_DIM:V_DIM + 1]).astype(o_ref.dtype)
        return tuple(pending)

    lax.fori_loop(0, nq, q_tile, tuple(_dot(k_blk(i), q_cols(0)) for i in range(ATTN_AHEAD)))


def _attention(qT, k, vT, kc, vcT, tq, tkb):
    b, h, _, s = qT.shape
    sc = kc.shape[2]
    return pl.pallas_call(
        functools.partial(_attn_body, tq=tq, tkb=tkb),
        grid=(b, h),
        in_specs=[pl.BlockSpec((1, 1, HEAD_PAD, s), lambda i, j: (i, j, 0, 0)),
                  pl.BlockSpec((1, 1, s, HEAD_PAD), lambda i, j: (i, j, 0, 0)),
                  pl.BlockSpec((1, 1, V_ROWS, s), lambda i, j: (i, j, 0, 0)),
                  pl.BlockSpec((1, 1, sc, HEAD_PAD), lambda i, j: (i, j, 0, 0)),
                  pl.BlockSpec((1, 1, V_ROWS, sc), lambda i, j: (i, j, 0, 0))],
        out_specs=pl.BlockSpec((1, V_DIM, s), lambda i, j: (i, j, 0)),
        out_shape=jax.ShapeDtypeStruct((b, h * V_DIM, s), BF16),
        compiler_params=_params("arbitrary", "arbitrary"),
        name="attention",
    )(qT, k, vT, kc, vcT)


def _post_body(fm_ref, aT_ref, g_ref, x_ref, mod_ref, wfo_ref, wao_ref, wout_ref,
               g2_ref, wup_ref, wdn_ref, o_ref, *, fc):
    d = x_ref.shape[-1]
    m = mod_ref[0]
    yf = _dot(fm_ref[0].astype(BF16), wfo_ref[...])
    ya = lax.dot_general(aT_ref[0], wao_ref[...], (((0,), (0,)), ((), ())),
                         preferred_element_type=F32)
    g = g_ref[0]
    y = g[:, :d].astype(F32) * yf + g[:, d:].astype(F32) * ya
    x1 = x_ref[0] + m[2:3] * _dot(y.astype(BF16), wout_ref[...])
    hb = _norm_mod(x1, m[3:5], g2_ref[...]).astype(BF16)
    acc = jnp.zeros(x1.shape, F32)
    for c in range(wup_ref.shape[1] // fc):
        u = jnp.maximum(_dot(hb, wup_ref[:, c * fc:(c + 1) * fc]), 0.0)
        acc = acc + _dot((u * u).astype(BF16), wdn_ref[c * fc:(c + 1) * fc, :])
    o_ref[0] = x1 + m[5:6] * acc


def _post(fm, aT, gates, x, mod3, wfo, wao, wout, g2, wup, wdn, ts, fc):
    b, s, d = x.shape
    return pl.pallas_call(
        functools.partial(_post_body, fc=fc),
        grid=(b, s // ts),
        in_specs=[pl.BlockSpec((1, ts, FOURIER_W), lambda i, j: (i, j, 0)),
                  pl.BlockSpec((1, aT.shape[1], ts), lambda i, j: (i, 0, j)),
                  pl.BlockSpec((1, ts, 2 * d), lambda i, j: (i, j, 0)),
                  pl.BlockSpec((1, ts, d), lambda i, j: (i, j, 0)),
                  pl.BlockSpec((1, 6, d), lambda i, j: (i, 0, 0)),
                  _full(wfo.shape), _full(wao.shape), _full(wout.shape),
                  _full(g2.shape), _full(wup.shape), _full(wdn.shape)],
        out_specs=pl.BlockSpec((1, ts, d), lambda i, j: (i, j, 0)),
        out_shape=jax.ShapeDtypeStruct((b, s, d), F32),
        compiler_params=_params("arbitrary", "arbitrary"),
        name="post",
    )(fm, aT, gates, x, mod3, wfo, wao, wout, g2, wup, wdn)


@functools.lru_cache(maxsize=None)
def _dft_tables():
    j = np.arange(GROUP_W)
    ang = 2.0 * np.pi * np.outer(j, j) / GROUP_W
    cs = np.concatenate([np.cos(ang), -np.sin(ang)], axis=1) / math.sqrt(GROUP_W)

    a = np.arange(RADIX)
    ang = 2.0 * np.pi * np.outer(a, a) / RADIX
    c64, s64 = np.cos(ang), np.sin(ang)
    m1 = np.block([[c64, s64], [-s64, c64]]) / math.sqrt(RADIX)

    n = RADIX * RADIX
    k1 = np.arange(RADIX)[:, None, None]
    k2 = np.arange(RADIX)[None, :, None]
    bb = np.arange(RADIX)[None, None, :]
    th = 2.0 * np.pi * ((bb * (RADIX * k1 + k2)) % n) / n
    gc = np.cos(th) / math.sqrt(RADIX)
    gs = np.sin(th) / math.sqrt(RADIX)
    nblk = RADIX // DFT_BLK
    gtab = np.zeros((nblk, RADIX, DFT_BLK, 2, DFT_BLK, RADIX), np.float32)
    for jb in range(nblk):
        for q in range(DFT_BLK):
            gtab[jb, :, q, 0, q, :] = gc[:, jb * DFT_BLK + q, :]
            gtab[jb, :, q, 1, q, :] = gs[:, jb * DFT_BLK + q, :]
    gtab = gtab.reshape(nblk, RADIX * DFT_BLK, 2 * DFT_BLK * RADIX)
    ga = np.kron(m1, np.eye(DFT_BLK))
    return cs.astype(np.float32), ga.astype(np.float32), gtab


def _rope_tables_t(n):
    rows = n // GRID_W
    row = jnp.repeat(jnp.arange(rows, dtype=F32), GRID_W)
    col = jnp.tile(jnp.arange(GRID_W, dtype=F32), rows)
    n_freq = QK_ROPE // 4
    freqs = ROPE_THETA ** (-jnp.arange(n_freq, dtype=F32) / n_freq)
    ang_r = row[:, None] * freqs[None, :]
    ang_c = col[:, None] * freqs[None, :]
    ang = jnp.concatenate([ang_r, ang_r, ang_c, ang_c], axis=-1)
    return jnp.cos(ang).T, jnp.sin(ang).T


def kernel(x, c, ctx, c_ctx, ada_w, ada_b, norm1_g, norm2_g, w_in, b_gate, w_fourier, q_norm_g, w_qb,
           kv_norm_g, w_kvb, q_gain, k_gain, w_mla_o, w_out, w_up, w_down):
    bsz, s, d = x.shape
    sc = ctx.shape[1]
    assert ada_w.shape[0] == 1 and s == RADIX * RADIX and d == D_MODEL
    ts = 512

    cc = jnp.zeros((8, d), F32).at[:bsz].set(c).at[bsz].set(c_ctx)
    mod3 = _adaln(cc, ada_w[0], ada_b[0][None, :]).reshape(8, 6, d)

    o1, o2, o3, o4 = FOURIER_W, FOURIER_W + Q_LORA, FOURIER_W + Q_LORA + KV_LORA, \
        FOURIER_W + Q_LORA + KV_LORA + QK_ROPE
    w = w_in[0]
    wf = w[:, :o1].astype(BF16)
    wq = w[:, o1:o2].astype(BF16)
    wkv = jnp.pad(w[:, o2:o4], ((0, 0), (0, 2 * KV_LORA - (o4 - o2)))).astype(BF16)
    wg = w[:, o4:].astype(BF16)
    wqbT = jnp.pad(w_qb[0].reshape(Q_LORA, N_HEADS, HEAD_QK),
                   ((0, 0), (0, 0), (0, HEAD_PAD - HEAD_QK))).reshape(Q_LORA, -1).T.astype(BF16)
    wkvb = w_kvb[0].reshape(KV_LORA, N_HEADS, QK_NOPE + V_DIM)
    wkbT = wkvb[:, :, :QK_NOPE].reshape(KV_LORA, -1).T.astype(BF16)
    wvT = wkvb[:, :, QK_NOPE:].reshape(KV_LORA, -1).T.astype(BF16)

    g1 = norm1_g[0][None, :]
    g2 = norm2_g[0][None, :]
    qg = q_norm_g[0][None, :]
    kvg = kv_norm_g[0][None, :]
    bg = b_gate[0][None, :]
    qscale = (HEAD_QK ** -0.5) * math.log2(math.e)
    qgain = jnp.broadcast_to(jnp.pad(q_gain[0] * qscale, (0, HEAD_PAD - HEAD_QK))[:, None], (HEAD_PAD, s))
    kgn = jnp.broadcast_to(k_gain[0][:QK_NOPE, None], (QK_NOPE, s))
    kgr = jnp.broadcast_to(k_gain[0][QK_NOPE:, None], (QK_ROPE, s))
    cosT, sinT = _rope_tables_t(s)
    cos1 = jnp.ones((QK_ROPE, sc), F32)
    sin0 = jnp.zeros((QK_ROPE, sc), F32)

    cs_np, ga_np, gtab_np = _dft_tables()
    cs = jnp.asarray(cs_np).astype(BF16)
    ga = jnp.asarray(ga_np).astype(BF16)
    gtab = jnp.asarray(gtab_np).astype(BF16)

    kc, vcT = _ctx_kv(ctx, mod3, g1, wkv, kvg, wkbT, wvT, kgn[:, :sc], kgr[:, :sc], cos1, sin0)

    z, gates, qT, k, vT = _proj(x, mod3, g1, wf, cs, wg, bg, wq, qg, wqbT, qgain, wkv, kvg, wkbT, wvT,
                                kgn, kgr, cosT, sinT, ts)

    fm = _dft(ga, gtab, z.reshape(bsz, 2, RADIX, RADIX, FOURIER_W)).reshape(bsz, s, FOURIER_W)

    aT = _attention(qT, k, vT, kc, vcT, 512, 256)

    return _post(fm, aT, gates, x, mod3, w_fourier[0].astype(BF16), w_mla_o[0].astype(BF16),
                 w_out[0].astype(BF16), g2, w_up[0].astype(BF16), w_down[0].astype(BF16), ts, 1024)
```

```python
import functools
import math

import numpy as np
import jax
import jax.numpy as jnp
from jax import lax
from jax.experimental import pallas as pl
from jax.experimental.pallas import tpu as pltpu

F32 = jnp.float32
BF16 = jnp.bfloat16

D_MODEL = 1024
GRID_W = 64
N_GROUPS = 4
GROUP_W = 128
FOURIER_W = N_GROUPS * GROUP_W
N_HEADS = 8
QK_NOPE = 64
QK_ROPE = 32
HEAD_QK = QK_NOPE + QK_ROPE
HEAD_PAD = 128
V_DIM = 64
V_ROWS = V_DIM + 16
Q_LORA = 256
KV_LORA = 128
ROPE_THETA = 10000.0
EPS = 1e-6
RADIX = 64
DFT_BLK = 8
DFT_STEP = 2
GATE_CHUNK = 512
ATTN_AHEAD = 2

V7X_VMEM_LIMIT = 56 * 1024 * 1024


def _params(*sem):
    return pltpu.CompilerParams(dimension_semantics=sem, vmem_limit_bytes=V7X_VMEM_LIMIT)


def _dot(a, b):
    return jnp.dot(a, b, preferred_element_type=F32)


def _rms_lanes(x, n):
    return x * lax.rsqrt(jnp.sum(x * x, axis=-1, keepdims=True) * (1.0 / n) + EPS)


def _rot_half_rows(x):
    return jnp.concatenate([-x[8:16], x[0:8], -x[24:32], x[16:24]], axis=0)


def _adaln_body(c_ref, w_ref, b_ref, o_ref):
    c = c_ref[...]
    s = c * jax.nn.sigmoid(c)
    o_ref[...] = _dot(s, w_ref[...]) + b_ref[...]


def _adaln(cc, ada_w, ada_b):
    rows, d = cc.shape
    n = ada_w.shape[1]
    tn = 1024
    return pl.pallas_call(
        _adaln_body,
        grid=(n // tn,),
        in_specs=[pl.BlockSpec((rows, d), lambda j: (0, 0)),
                  pl.BlockSpec((d, tn), lambda j: (0, j)),
                  pl.BlockSpec((1, tn), lambda j: (0, j))],
        out_specs=pl.BlockSpec((rows, tn), lambda j: (0, j)),
        out_shape=jax.ShapeDtypeStruct((rows, n), F32),
        compiler_params=_params("arbitrary"),
        name="adaln",
    )(cc, ada_w, ada_b)


def _norm_mod(x, m, g):
    ms = jnp.sum(x * x, axis=-1, keepdims=True) * (1.0 / x.shape[-1])
    return (x * lax.rsqrt(ms + EPS)) * (g * (1.0 + m[1:2])) + m[0:1]


def _kv_latents(kvp, kvg_ref):
    kvn = _rms_lanes(kvp[:, :KV_LORA], KV_LORA) * kvg_ref[...]
    return kvn.T.astype(BF16), kvp[:, KV_LORA:].T[0:QK_ROPE]


def _kv_heads(knT, vT, krT, kgn_ref, kgr_ref, cos_ref, sin_ref, k_ref, vT_ref):
    t = knT.shape[1]
    sr = jnp.sum(krT * krT, axis=0, keepdims=True)
    krg = krT * kgr_ref[...]
    kr_rot = krg * cos_ref[...] + _rot_half_rows(krg) * sin_ref[...]
    ones = jnp.ones((V_ROWS - V_DIM, t), F32)
    zpad = jnp.zeros((HEAD_PAD - HEAD_QK, t), F32)
    for h in range(N_HEADS):
        kn = knT[h * QK_NOPE:(h + 1) * QK_NOPE]
        ss = jnp.sum(kn * kn, axis=0, keepdims=True) + sr
        r = lax.rsqrt(ss * (1.0 / HEAD_QK) + EPS)
        kT = jnp.concatenate([kn * kgn_ref[...] * r, kr_rot * r, zpad], axis=0)
        k_ref[0, h] = kT.T.astype(k_ref.dtype)
        vh = vT[h * V_DIM:(h + 1) * V_DIM]
        vT_ref[0, h] = jnp.concatenate([vh, ones], axis=0).astype(vT_ref.dtype)


def _q_heads(qT, qgain_ref, cos_ref, sin_ref, qT_ref):
    for h in range(N_HEADS):
        qh = qT[h * HEAD_PAD:(h + 1) * HEAD_PAD]
        r = lax.rsqrt(jnp.sum(qh * qh, axis=0, keepdims=True) * (1.0 / HEAD_QK) + EPS)
        qh = qh * qgain_ref[...] * r
        qr = qh[QK_NOPE:HEAD_QK]
        qr = qr * cos_ref[...] + _rot_half_rows(qr) * sin_ref[...]
        qT_ref[0, h] = jnp.concatenate([qh[:QK_NOPE], qr, qh[HEAD_QK:]], axis=0).astype(qT_ref.dtype)


def _ctx_body(x_ref, mod_ref, g1_ref, wkv_ref, kvg_ref, wkbT_ref, wvT_ref, kgn_ref, kgr_ref,
              cos_ref, sin_ref, k_ref, vT_ref):
    hb = _norm_mod(x_ref[0], mod_ref[0], g1_ref[...]).astype(BF16)
    kvnT, krT = _kv_latents(_dot(hb, wkv_ref[...]), kvg_ref)
    _kv_heads(_dot(wkbT_ref[...], kvnT), _dot(wvT_ref[...], kvnT), krT,
              kgn_ref, kgr_ref, cos_ref, sin_ref, k_ref, vT_ref)


def _proj_body(x_ref, mod_ref, g1_ref, wf_ref, cs_ref, wg_ref, bg_ref,
               wq_ref, qg_ref, wqbT_ref, qgain_ref,
               wkv_ref, kvg_ref, wkbT_ref, wvT_ref, kgn_ref, kgr_ref, cos_ref, sin_ref,
               z_ref, g_ref, qT_ref, k_ref, vT_ref):
    hb = _norm_mod(x_ref[0], mod_ref[0], g1_ref[...]).astype(BF16)
    qp = _dot(hb, wq_ref[...])
    kvp = _dot(hb, wkv_ref[...])
    f = _dot(hb, wf_ref[...]).astype(BF16)

    qnT = (_rms_lanes(qp, Q_LORA) * qg_ref[...]).T.astype(BF16)
    kvnT, krT = _kv_latents(kvp, kvg_ref)

    for g in range(N_GROUPS):
        z = _dot(f[:, g * GROUP_W:(g + 1) * GROUP_W], cs_ref[...])
        z_ref[0, 0, :, g * GROUP_W:(g + 1) * GROUP_W] = z[:, :GROUP_W].astype(z_ref.dtype)
        z_ref[0, 1, :, g * GROUP_W:(g + 1) * GROUP_W] = z[:, GROUP_W:].astype(z_ref.dtype)

    qT = _dot(wqbT_ref[...], qnT)
    knT = _dot(wkbT_ref[...], kvnT)
    vT = _dot(wvT_ref[...], kvnT)

    def gate_chunk(c):
        cols = slice(c * GATE_CHUNK, (c + 1) * GATE_CHUNK)
        g_ref[0, :, cols] = jax.nn.sigmoid(_dot(hb, wg_ref[:, cols]) + bg_ref[:, cols]).astype(g_ref.dtype)

    gate_chunk(0)
    _q_heads(qT, qgain_ref, cos_ref, sin_ref, qT_ref)
    gate_chunk(1)
    _kv_heads(knT, vT, krT, kgn_ref, kgr_ref, cos_ref, sin_ref, k_ref, vT_ref)
    for c in range(2, g_ref.shape[2] // GATE_CHUNK):
        gate_chunk(c)


def _full(shape):
    nd = len(shape)
    return pl.BlockSpec(shape, lambda *_: (0,) * nd, pipeline_mode=pl.Buffered(1))


def _ctx_kv(ctx, mod3, g1, wkv, kvg, wkbT, wvT, kgn, kgr, cosT, sinT):
    b, t, d = ctx.shape
    return pl.pallas_call(
        _ctx_body,
        grid=(b,),
        in_specs=[pl.BlockSpec((1, t, d), lambda i: (i, 0, 0)),
                  pl.BlockSpec((1, 6, d), lambda i: (b, 0, 0)),
                  _full(g1.shape), _full(wkv.shape), _full(kvg.shape), _full(wkbT.shape),
                  _full(wvT.shape), _full(kgn.shape), _full(kgr.shape),
                  _full(cosT.shape), _full(sinT.shape)],
        out_specs=[pl.BlockSpec((1, N_HEADS, t, HEAD_PAD), lambda i: (i, 0, 0, 0)),
                   pl.BlockSpec((1, N_HEADS, V_ROWS, t), lambda i: (i, 0, 0, 0))],
        out_shape=[jax.ShapeDtypeStruct((b, N_HEADS, t, HEAD_PAD), BF16),
                   jax.ShapeDtypeStruct((b, N_HEADS, V_ROWS, t), BF16)],
        compiler_params=_params("arbitrary"),
        name="ctx_kv",
    )(ctx, mod3, g1, wkv, kvg, wkbT, wvT, kgn, kgr, cosT, sinT)


def _proj(x, mod3, g1, wf, cs, wg, bg, wq, qg, wqbT, qgain, wkv, kvg, wkbT, wvT, kgn, kgr,
          cosT, sinT, ts):
    b, s, d = x.shape
    tok = lambda shape: pl.BlockSpec(shape, lambda i, j: (0, j))
    return pl.pallas_call(
        _proj_body,
        grid=(b, s // ts),
        in_specs=[pl.BlockSpec((1, ts, d), lambda i, j: (i, j, 0)),
                  pl.BlockSpec((1, 6, d), lambda i, j: (i, 0, 0)),
                  _full(g1.shape), _full(wf.shape), _full(cs.shape), _full(wg.shape),
                  _full(bg.shape), _full(wq.shape), _full(qg.shape), _full(wqbT.shape),
                  tok((HEAD_PAD, ts)),
                  _full(wkv.shape), _full(kvg.shape), _full(wkbT.shape), _full(wvT.shape),
                  tok((QK_NOPE, ts)), tok((QK_ROPE, ts)), tok((QK_ROPE, ts)), tok((QK_ROPE, ts))],
        out_specs=[pl.BlockSpec((1, 2, ts, FOURIER_W), lambda i, j: (i, 0, j, 0)),
                   pl.BlockSpec((1, ts, 2 * d), lambda i, j: (i, j, 0)),
                   pl.BlockSpec((1, N_HEADS, HEAD_PAD, ts), lambda i, j: (i, 0, 0, j)),
                   pl.BlockSpec((1, N_HEADS, ts, HEAD_PAD), lambda i, j: (i, 0, j, 0)),
                   pl.BlockSpec((1, N_HEADS, V_ROWS, ts), lambda i, j: (i, 0, 0, j))],
        out_shape=[jax.ShapeDtypeStruct((b, 2, s, FOURIER_W), F32),
                   jax.ShapeDtypeStruct((b, s, 2 * d), BF16),
                   jax.ShapeDtypeStruct((b, N_HEADS, HEAD_PAD, s), BF16),
                   jax.ShapeDtypeStruct((b, N_HEADS, s, HEAD_PAD), BF16),
                   jax.ShapeDtypeStruct((b, N_HEADS, V_ROWS, s), BF16)],
        compiler_params=_params("arbitrary", "arbitrary"),
        name="proj",
    )(x, mod3, g1, wf, cs, wg, bg, wq, qg, wqbT, qgain, wkv, kvg, wkbT, wvT, kgn, kgr,
      cosT, sinT)


def _dft_body(ga_ref, gb_ref, z_ref, o_ref, a_scr):
    nstep = RADIX // (DFT_BLK * DFT_STEP)
    j = pl.program_id(1)

    @pl.when(j < nstep)
    def _():
        for t in range(DFT_STEP):
            rows = slice(t * DFT_BLK, (t + 1) * DFT_BLK)
            z = z_ref[0, :, :, rows, :].reshape(2 * RADIX * DFT_BLK, FOURIER_W)
            a = _dot(ga_ref[...], z.astype(BF16))
            off = pl.multiple_of((j * DFT_STEP + t) * DFT_BLK, DFT_BLK)
            a_scr[:, :, pl.ds(off, DFT_BLK), :] = a.reshape(2, RADIX, DFT_BLK, FOURIER_W)

    @pl.when(j >= nstep)
    def _():
        for t in range(DFT_STEP):
            blk = (j - nstep) * DFT_STEP + t
            off = pl.multiple_of(blk * DFT_BLK, DFT_BLK)
            a = a_scr[:, pl.ds(off, DFT_BLK), :, :]
            a = a.reshape(2 * DFT_BLK * RADIX, FOURIER_W).astype(BF16)
            y = _dot(gb_ref[blk], a)
            o_ref[0, :, t * DFT_BLK:(t + 1) * DFT_BLK, :] = y.reshape(RADIX, DFT_BLK, FOURIER_W)


def _dft(ga, gb, z5):
    b = z5.shape[0]
    nstep = RADIX // (DFT_BLK * DFT_STEP)
    rows = DFT_BLK * DFT_STEP
    return pl.pallas_call(
        _dft_body,
        grid=(b, 2 * nstep),
        in_specs=[_full(ga.shape), _full(gb.shape),
                  pl.BlockSpec((1, 2, RADIX, rows, FOURIER_W),
                               lambda i, j: (i, 0, 0, jnp.minimum(j, nstep - 1), 0))],
        out_specs=pl.BlockSpec((1, RADIX, rows, FOURIER_W),
                               lambda i, j: (i, 0, jnp.maximum(j - nstep, 0), 0)),
        out_shape=jax.ShapeDtypeStruct((b, RADIX, RADIX, FOURIER_W), F32),
        scratch_shapes=[pltpu.VMEM((2, RADIX, RADIX, FOURIER_W), F32)],
        compiler_params=_params("arbitrary", "arbitrary"),
        name="dft",
    )(ga, gb, z5)


def _attn_body(qT_ref, k_ref, vT_ref, kc_ref, vcT_ref, o_ref, *, tq, tkb):
    nkb = k_ref.shape[2] // tkb

    def colmax(s):
        r = s.shape[0] // 8
        return jnp.max(jnp.max(s.reshape(8, r, s.shape[1]), axis=0), axis=0, keepdims=True)

    def update(carry, s, vT_blk):
        m, acc = carry
        m_new = jnp.maximum(m, colmax(s))
        p = jnp.exp2(s - m_new).astype(BF16)
        acc = jnp.exp2(m - m_new) * acc + _dot(vT_blk, p)
        return m_new, acc

    nq = qT_ref.shape[3] // tq

    def k_blk(i):
        return k_ref[0, 0, i * tkb:(i + 1) * tkb, :] if i < nkb else kc_ref[0, 0]

    def vT_blk(i):
        return vT_ref[0, 0, :, i * tkb:(i + 1) * tkb] if i < nkb else vcT_ref[0, 0]

    def q_cols(qi):
        return qT_ref[0, 0, :, pl.ds(pl.multiple_of(qi * tq, tq), tq)]

    def q_tile(qi, pending):
        qT = q_cols(qi)
        qT_next = q_cols(jnp.minimum(qi + 1, nq - 1))
        pending = list(pending)
        carry = (jnp.full((1, tq), -jnp.inf, F32), jnp.zeros((V_ROWS, tq), F32))
        for i in range(nkb + 1):
            j = i + ATTN_AHEAD
            pending.append(_dot(k_blk(j), qT) if j <= nkb else _dot(k_blk(j - nkb - 1), qT_next))
            carry = update(carry, pending.pop(0), vT_blk(i))
        _, acc = carry
        o_ref[0, :, pl.ds(pl.multiple_of(qi * tq, tq), tq)] = (
            acc[:V_DIM] / acc[V_DIM:V_DIM + 1]).astype(o_ref.dtype)
        return tuple(pending)

    lax.fori_loop(0, nq, q_tile, tuple(_dot(k_blk(i), q_cols(0)) for i in range(ATTN_AHEAD)))


def _attention(qT, k, vT, kc, vcT, tq, tkb):
    b, h, _, s = qT.shape
    sc = kc.shape[2]
    return pl.pallas_call(
        functools.partial(_attn_body, tq=tq, tkb=tkb),
        grid=(b, h),
        in_specs=[pl.BlockSpec((1, 1, HEAD_PAD, s), lambda i, j: (i, j, 0, 0)),
                  pl.BlockSpec((1, 1, s, HEAD_PAD), lambda i, j: (i, j, 0, 0)),
                  pl.BlockSpec((1, 1, V_ROWS, s), lambda i, j: (i, j, 0, 0)),
                  pl.BlockSpec((1, 1, sc, HEAD_PAD), lambda i, j: (i, j, 0, 0)),
                  pl.BlockSpec((1, 1, V_ROWS, sc), lambda i, j: (i, j, 0, 0))],
        out_specs=pl.BlockSpec((1, V_DIM, s), lambda i, j: (i, j, 0)),
        out_shape=jax.ShapeDtypeStruct((b, h * V_DIM, s), BF16),
        compiler_params=_params("arbitrary", "arbitrary"),
        name="attention",
    )(qT, k, vT, kc, vcT)


def _post_body(fm_ref, aT_ref, g_ref, x_ref, mod_ref, wfo_ref, wao_ref, wout_ref,
               g2_ref, wup_ref, wdn_ref, o_ref, *, fc):
    d = x_ref.shape[-1]
    m = mod_ref[0]
    yf = _dot(fm_ref[0].astype(BF16), wfo_ref[...])
    ya = lax.dot_general(aT_ref[0], wao_ref[...], (((0,), (0,)), ((), ())),
                         preferred_element_type=F32)
    g = g_ref[0]
    y = g[:, :d].astype(F32) * yf + g[:, d:].astype(F32) * ya
    x1 = x_ref[0] + m[2:3] * _dot(y.astype(BF16), wout_ref[...])
    hb = _norm_mod(x1, m[3:5], g2_ref[...]).astype(BF16)
    acc = jnp.zeros(x1.shape, F32)
    for c in range(wup_ref.shape[1] // fc):
        u = jnp.maximum(_dot(hb, wup_ref[:, c * fc:(c + 1) * fc]), 0.0)
        acc = acc + _dot((u * u).astype(BF16), wdn_ref[c * fc:(c + 1) * fc, :])
    o_ref[0] = x1 + m[5:6] * acc


def _post(fm, aT, gates, x, mod3, wfo, wao, wout, g2, wup, wdn, ts, fc):
    b, s, d = x.shape
    return pl.pallas_call(
        functools.partial(_post_body, fc=fc),
        grid=(b, s // ts),
        in_specs=[pl.BlockSpec((1, ts, FOURIER_W), lambda i, j: (i, j, 0)),
                  pl.BlockSpec((1, aT.shape[1], ts), lambda i, j: (i, 0, j)),
                  pl.BlockSpec((1, ts, 2 * d), lambda i, j: (i, j, 0)),
                  pl.BlockSpec((1, ts, d), lambda i, j: (i, j, 0)),
                  pl.BlockSpec((1, 6, d), lambda i, j: (i, 0, 0)),
                  _full(wfo.shape), _full(wao.shape), _full(wout.shape),
                  _full(g2.shape), _full(wup.shape), _full(wdn.shape)],
        out_specs=pl.BlockSpec((1, ts, d), lambda i, j: (i, j, 0)),
        out_shape=jax.ShapeDtypeStruct((b, s, d), F32),
        compiler_params=_params("arbitrary", "arbitrary"),
        name="post",
    )(fm, aT, gates, x, mod3, wfo, wao, wout, g2, wup, wdn)


@functools.lru_cache(maxsize=None)
def _dft_tables():
    j = np.arange(GROUP_W)
    ang = 2.0 * np.pi * np.outer(j, j) / GROUP_W
    cs = np.concatenate([np.cos(ang), -np.sin(ang)], axis=1) / math.sqrt(GROUP_W)

    a = np.arange(RADIX)
    ang = 2.0 * np.pi * np.outer(a, a) / RADIX
    c64, s64 = np.cos(ang), np.sin(ang)
    m1 = np.block([[c64, s64], [-s64, c64]]) / math.sqrt(RADIX)

    n = RADIX * RADIX
    k1 = np.arange(RADIX)[:, None, None]
    k2 = np.arange(RADIX)[None, :, None]
    bb = np.arange(RADIX)[None, None, :]
    th = 2.0 * np.pi * ((bb * (RADIX * k1 + k2)) % n) / n
    gc = np.cos(th) / math.sqrt(RADIX)
    gs = np.sin(th) / math.sqrt(RADIX)
    nblk = RADIX // DFT_BLK
    gtab = np.zeros((nblk, RADIX, DFT_BLK, 2, DFT_BLK, RADIX), np.float32)
    for jb in range(nblk):
        for q in range(DFT_BLK):
            gtab[jb, :, q, 0, q, :] = gc[:, jb * DFT_BLK + q, :]
            gtab[jb, :, q, 1, q, :] = gs[:, jb * DFT_BLK + q, :]
    gtab = gtab.reshape(nblk, RADIX * DFT_BLK, 2 * DFT_BLK * RADIX)
    ga = np.kron(m1, np.eye(DFT_BLK))
    return cs.astype(np.float32), ga.astype(np.float32), gtab


def _rope_tables_t(n):
    rows = n // GRID_W
    row = jnp.repeat(jnp.arange(rows, dtype=F32), GRID_W)
    col = jnp.tile(jnp.arange(GRID_W, dtype=F32), rows)
    n_freq = QK_ROPE // 4
    freqs = ROPE_THETA ** (-jnp.arange(n_freq, dtype=F32) / n_freq)
    ang_r = row[:, None] * freqs[None, :]
    ang_c = col[:, None] * freqs[None, :]
    ang = jnp.concatenate([ang_r, ang_r, ang_c, ang_c], axis=-1)
    return jnp.cos(ang).T, jnp.sin(ang).T


def kernel(x, c, ctx, c_ctx, ada_w, ada_b, norm1_g, norm2_g, w_in, b_gate, w_fourier, q_norm_g, w_qb,
           kv_norm_g, w_kvb, q_gain, k_gain, w_mla_o, w_out, w_up, w_down):
    bsz, s, d = x.shape
    sc = ctx.shape[1]
    assert ada_w.shape[0] == 1 and s == RADIX * RADIX and d == D_MODEL
    ts = 512

    cc = jnp.zeros((8, d), F32).at[:bsz].set(c).at[bsz].set(c_ctx)
    mod3 = _adaln(cc, ada_w[0], ada_b[0][None, :]).reshape(8, 6, d)

    o1, o2, o3, o4 = FOURIER_W, FOURIER_W + Q_LORA, FOURIER_W + Q_LORA + KV_LORA, \
        FOURIER_W + Q_LORA + KV_LORA + QK_ROPE
    w = w_in[0]
    wf = w[:, :o1].astype(BF16)
    wq = w[:, o1:o2].astype(BF16)
    wkv = jnp.pad(w[:, o2:o4], ((0, 0), (0, 2 * KV_LORA - (o4 - o2)))).astype(BF16)
    wg = w[:, o4:].astype(BF16)
    wqbT = jnp.pad(w_qb[0].reshape(Q_LORA, N_HEADS, HEAD_QK),
                   ((0, 0), (0, 0), (0, HEAD_PAD - HEAD_QK))).reshape(Q_LORA, -1).T.astype(BF16)
    wkvb = w_kvb[0].reshape(KV_LORA, N_HEADS, QK_NOPE + V_DIM)
    wkbT = wkvb[:, :, :QK_NOPE].reshape(KV_LORA, -1).T.astype(BF16)
    wvT = wkvb[:, :, QK_NOPE:].reshape(KV_LORA, -1).T.astype(BF16)

    g1 = norm1_g[0][None, :]
    g2 = norm2_g[0][None, :]
    qg = q_norm_g[0][None, :]
    kvg = kv_norm_g[0][None, :]
    bg = b_gate[0][None, :]
    qscale = (HEAD_QK ** -0.5) * math.log2(math.e)
    qgain = jnp.broadcast_to(jnp.pad(q_gain[0] * qscale, (0, HEAD_PAD - HEAD_QK))[:, None], (HEAD_PAD, s))
    kgn = jnp.broadcast_to(k_gain[0][:QK_NOPE, None], (QK_NOPE, s))
    kgr = jnp.broadcast_to(k_gain[0][QK_NOPE:, None], (QK_ROPE, s))
    cosT, sinT = _rope_tables_t(s)
    cos1 = jnp.ones((QK_ROPE, sc), F32)
    sin0 = jnp.zeros((QK_ROPE, sc), F32)

    cs_np, ga_np, gtab_np = _dft_tables()
    cs = jnp.asarray(cs_np).astype(BF16)
    ga = jnp.asarray(ga_np).astype(BF16)
    gtab = jnp.asarray(gtab_np).astype(BF16)

    kc, vcT = _ctx_kv(ctx, mod3, g1, wkv, kvg, wkbT, wvT, kgn[:, :sc], kgr[:, :sc], cos1, sin0)

    z, gates, qT, k, vT = _proj(x, mod3, g1, wf, cs, wg, bg, wq, qg, wqbT, qgain, wkv, kvg, wkbT, wvT,
                                kgn, kgr, cosT, sinT, ts)

    fm = _dft(ga, gtab, z.reshape(bsz, 2, RADIX, RADIX, FOURIER_W)).reshape(bsz, s, FOURIER_W)

    aT = _attention(qT, k, vT, kc, vcT, 512, 256)

    return _post(fm, aT, gates, x, mod3, w_fourier[0].astype(BF16), w_mla_o[0].astype(BF16),
                 w_out[0].astype(BF16), g2, w_up[0].astype(BF16), w_down[0].astype(BF16), ts, 1024)
```

```python
import functools
import math

import numpy as np
import jax
import jax.numpy as jnp
from jax import lax
from jax.experimental import pallas as pl
from jax.experimental.pallas import tpu as pltpu

F32 = jnp.float32
BF16 = jnp.bfloat16

D_MODEL = 1024
GRID_W = 64
N_GROUPS = 4
GROUP_W = 128
FOURIER_W = N_GROUPS * GROUP_W
N_HEADS = 8
QK_NOPE = 64
QK_ROPE = 32
HEAD_QK = QK_NOPE + QK_ROPE
HEAD_PAD = 128
V_DIM = 64
V_ROWS = V_DIM + 16
Q_LORA = 256
KV_LORA = 128
W_IN_GATES = FOURIER_W + Q_LORA + KV_LORA + QK_ROPE
ROPE_THETA = 10000.0
EPS = 1e-6
RADIX = 64
DFT_BLK = 8
DFT_STEP = 4
GATE_CHUNK = 512
ATTN_AHEAD = 2

V7X_VMEM_LIMIT = 56 * 1024 * 1024


def _params(*sem):
    return pltpu.CompilerParams(dimension_semantics=sem, vmem_limit_bytes=V7X_VMEM_LIMIT)


def _dot(a, b):
    return jnp.dot(a, b, preferred_element_type=F32)


def _rms_lanes(x, n):
    return x * lax.rsqrt(jnp.sum(x * x, axis=-1, keepdims=True) * (1.0 / n) + EPS)


def _rot_half_rows(x):
    return jnp.concatenate([-x[8:16], x[0:8], -x[24:32], x[16:24]], axis=0)


def _adaln_body(c_ref, w_ref, b_ref, win_ref, o_ref, wf_ref, wq_ref, wkv_ref, wg_ref):
    c = c_ref[...]
    s = c * jax.nn.sigmoid(c)
    o_ref[...] = _dot(s, w_ref[...]) + b_ref[...]
    w = win_ref[...]
    o1, o2, o3 = FOURIER_W, FOURIER_W + Q_LORA, FOURIER_W + Q_LORA + 2 * KV_LORA
    wf_ref[...] = w[:, :o1].astype(BF16)
    wq_ref[...] = w[:, o1:o2].astype(BF16)
    wkv_ref[...] = w[:, o2:o3].astype(BF16)
    wg_ref[...] = w[:, W_IN_GATES:].astype(BF16)


def _adaln(cc, ada_w, ada_b, w_in):
    rows, d = cc.shape
    n = ada_w.shape[1]
    steps = 8
    tn, tr = n // steps, d // steps
    n_gates = w_in.shape[1] - W_IN_GATES
    col = lambda width: pl.BlockSpec((tr, width), lambda j: (j, 0))
    return pl.pallas_call(
        _adaln_body,
        grid=(steps,),
        in_specs=[pl.BlockSpec((rows, d), lambda j: (0, 0)),
                  pl.BlockSpec((d, tn), lambda j: (0, j)),
                  pl.BlockSpec((1, tn), lambda j: (0, j)),
                  col(w_in.shape[1])],
        out_specs=[pl.BlockSpec((rows, tn), lambda j: (0, j)),
                   col(FOURIER_W), col(Q_LORA), col(2 * KV_LORA), col(n_gates)],
        out_shape=[jax.ShapeDtypeStruct((rows, n), F32),
                   jax.ShapeDtypeStruct((d, FOURIER_W), BF16),
                   jax.ShapeDtypeStruct((d, Q_LORA), BF16),
                   jax.ShapeDtypeStruct((d, 2 * KV_LORA), BF16),
                   jax.ShapeDtypeStruct((d, n_gates), BF16)],
        compiler_params=_params("arbitrary"),
        name="adaln",
    )(cc, ada_w, ada_b, w_in)


def _norm_mod(x, m, g):
    ms = jnp.sum(x * x, axis=-1, keepdims=True) * (1.0 / x.shape[-1])
    return (x * lax.rsqrt(ms + EPS)) * (g * (1.0 + m[1:2])) + m[0:1]


def _kv_latents(kvp, kvg_ref):
    kvn = _rms_lanes(kvp[:, :KV_LORA], KV_LORA) * kvg_ref[...]
    return kvn.T.astype(BF16), kvp[:, KV_LORA:].T[0:QK_ROPE]


def _kv_heads(knT, vT, krT, kgn_ref, kgr_ref, cos_ref, sin_ref, k_ref, vT_ref):
    t = knT.shape[1]
    sr = jnp.sum(krT * krT, axis=0, keepdims=True)
    krg = krT * kgr_ref[...]
    kr_rot = krg * cos_ref[...] + _rot_half_rows(krg) * sin_ref[...]
    ones = jnp.ones((V_ROWS - V_DIM, t), F32)
    zpad = jnp.zeros((HEAD_PAD - HEAD_QK, t), F32)
    for h in range(N_HEADS):
        kn = knT[h * QK_NOPE:(h + 1) * QK_NOPE]
        ss = jnp.sum(kn * kn, axis=0, keepdims=True) + sr
        r = lax.rsqrt(ss * (1.0 / HEAD_QK) + EPS)
        kT = jnp.concatenate([kn * kgn_ref[...] * r, kr_rot * r, zpad], axis=0)
        k_ref[0, h] = kT.T.astype(k_ref.dtype)
        vh = vT[h * V_DIM:(h + 1) * V_DIM]
        vT_ref[0, h] = jnp.concatenate([vh, ones], axis=0).astype(vT_ref.dtype)


def _q_heads(qT, qgain_ref, cos_ref, sin_ref, qT_ref):
    for h in range(N_HEADS):
        qh = qT[h * HEAD_PAD:(h + 1) * HEAD_PAD]
        r = lax.rsqrt(jnp.sum(qh * qh, axis=0, keepdims=True) * (1.0 / HEAD_QK) + EPS)
        qh = qh * qgain_ref[...] * r
        qr = qh[QK_NOPE:HEAD_QK]
        qr = qr * cos_ref[...] + _rot_half_rows(qr) * sin_ref[...]
        qT_ref[0, h] = jnp.concatenate([qh[:QK_NOPE], qr, qh[HEAD_QK:]], axis=0).astype(qT_ref.dtype)


def _ctx_body(x_ref, mod_ref, g1_ref, wkv_ref, kvg_ref, wkbT_ref, wvT_ref, kgn_ref, kgr_ref,
              cos_ref, sin_ref, k_ref, vT_ref):
    hb = _norm_mod(x_ref[0], mod_ref[0], g1_ref[...]).astype(BF16)
    kvnT, krT = _kv_latents(_dot(hb, wkv_ref[...]), kvg_ref)
    _kv_heads(_dot(wkbT_ref[...], kvnT), _dot(wvT_ref[...], kvnT), krT,
              kgn_ref, kgr_ref, cos_ref, sin_ref, k_ref, vT_ref)


def _proj_body(x_ref, mod_ref, g1_ref, wf_ref, cs_ref, wg_ref, bg_ref,
               wq_ref, qg_ref, wqbT_ref, qgain_ref,
               wkv_ref, kvg_ref, wkbT_ref, wvT_ref, kgn_ref, kgr_ref, cos_ref, sin_ref,
               z_ref, g_ref, qT_ref, k_ref, vT_ref):
    hb = _norm_mod(x_ref[0], mod_ref[0], g1_ref[...]).astype(BF16)
    qp = _dot(hb, wq_ref[...])
    kvp = _dot(hb, wkv_ref[...])
    f = _dot(hb, wf_ref[...]).astype(BF16)

    qnT = (_rms_lanes(qp, Q_LORA) * qg_ref[...]).T.astype(BF16)
    kvnT, krT = _kv_latents(kvp, kvg_ref)

    for g in range(N_GROUPS):
        z = _dot(f[:, g * GROUP_W:(g + 1) * GROUP_W], cs_ref[...])
        z_ref[0, 0, :, g * GROUP_W:(g + 1) * GROUP_W] = z[:, :GROUP_W].astype(z_ref.dtype)
        z_ref[0, 1, :, g * GROUP_W:(g + 1) * GROUP_W] = z[:, GROUP_W:].astype(z_ref.dtype)

    qT = _dot(wqbT_ref[...], qnT)
    knT = _dot(wkbT_ref[...], kvnT)
    vT = _dot(wvT_ref[...], kvnT)

    def gate_chunk(c):
        cols = slice(c * GATE_CHUNK, (c + 1) * GATE_CHUNK)
        g_ref[0, :, cols] = jax.nn.sigmoid(_dot(hb, wg_ref[:, cols]) + bg_ref[:, cols]).astype(g_ref.dtype)

    gate_chunk(0)
    _q_heads(qT, qgain_ref, cos_ref, sin_ref, qT_ref)
    gate_chunk(1)
    _kv_heads(knT, vT, krT, kgn_ref, kgr_ref, cos_ref, sin_ref, k_ref, vT_ref)
    for c in range(2, g_ref.shape[2] // GATE_CHUNK):
        gate_chunk(c)


def _full(shape):
    nd = len(shape)
    return pl.BlockSpec(shape, lambda *_: (0,) * nd, pipeline_mode=pl.Buffered(1))


def _ctx_kv(ctx, mod3, g1, wkv, kvg, wkbT, wvT, kgn, kgr, cosT, sinT):
    b, t, d = ctx.shape
    return pl.pallas_call(
        _ctx_body,
        grid=(b,),
        in_specs=[pl.BlockSpec((1, t, d), lambda i: (i, 0, 0)),
                  pl.BlockSpec((1, 6, d), lambda i: (b, 0, 0)),
                  _full(g1.shape), _full(wkv.shape), _full(kvg.shape), _full(wkbT.shape),
                  _full(wvT.shape), _full(kgn.shape), _full(kgr.shape),
                  _full(cosT.shape), _full(sinT.shape)],
        out_specs=[pl.BlockSpec((1, N_HEADS, t, HEAD_PAD), lambda i: (i, 0, 0, 0)),
                   pl.BlockSpec((1, N_HEADS, V_ROWS, t), lambda i: (i, 0, 0, 0))],
        out_shape=[jax.ShapeDtypeStruct((b, N_HEADS, t, HEAD_PAD), BF16),
                   jax.ShapeDtypeStruct((b, N_HEADS, V_ROWS, t), BF16)],
        compiler_params=_params("arbitrary"),
        name="ctx_kv",
    )(ctx, mod3, g1, wkv, kvg, wkbT, wvT, kgn, kgr, cosT, sinT)


def _proj(x, mod3, g1, wf, cs, wg, bg, wq, qg, wqbT, qgain, wkv, kvg, wkbT, wvT, kgn, kgr,
          cosT, sinT, ts):
    b, s, d = x.shape
    tok = lambda shape: pl.BlockSpec(shape, lambda i, j: (0, j))
    return pl.pallas_call(
        _proj_body,
        grid=(b, s // ts),
        in_specs=[pl.BlockSpec((1, ts, d), lambda i, j: (i, j, 0)),
                  pl.BlockSpec((1, 6, d), lambda i, j: (i, 0, 0)),
                  _full(g1.shape), _full(wf.shape), _full(cs.shape), _full(wg.shape),
                  _full(bg.shape), _full(wq.shape), _full(qg.shape), _full(wqbT.shape),
                  tok((HEAD_PAD, ts)),
                  _full(wkv.shape), _full(kvg.shape), _full(wkbT.shape), _full(wvT.shape),
                  tok((QK_NOPE, ts)), tok((QK_ROPE, ts)), tok((QK_ROPE, ts)), tok((QK_ROPE, ts))],
        out_specs=[pl.BlockSpec((1, 2, ts, FOURIER_W), lambda i, j: (i, 0, j, 0)),
                   pl.BlockSpec((1, ts, 2 * d), lambda i, j: (i, j, 0)),
                   pl.BlockSpec((1, N_HEADS, HEAD_PAD, ts), lambda i, j: (i, 0, 0, j)),
                   pl.BlockSpec((1, N_HEADS, ts, HEAD_PAD), lambda i, j: (i, 0, j, 0)),
                   pl.BlockSpec((1, N_HEADS, V_ROWS, ts), lambda i, j: (i, 0, 0, j))],
        out_shape=[jax.ShapeDtypeStruct((b, 2, s, FOURIER_W), F32),
                   jax.ShapeDtypeStruct((b, s, 2 * d), BF16),
                   jax.ShapeDtypeStruct((b, N_HEADS, HEAD_PAD, s), BF16),
                   jax.ShapeDtypeStruct((b, N_HEADS, s, HEAD_PAD), BF16),
                   jax.ShapeDtypeStruct((b, N_HEADS, V_ROWS, s), BF16)],
        compiler_params=_params("arbitrary", "arbitrary"),
        name="proj",
    )(x, mod3, g1, wf, cs, wg, bg, wq, qg, wqbT, qgain, wkv, kvg, wkbT, wvT, kgn, kgr,
      cosT, sinT)


def _dft_body(ga_ref, gb_ref, z_ref, o_ref, a_scr):
    nstep = RADIX // (DFT_BLK * DFT_STEP)
    j = pl.program_id(1)

    @pl.when(j < nstep)
    def _():
        for t in range(DFT_STEP):
            rows = slice(t * DFT_BLK, (t + 1) * DFT_BLK)
            z = z_ref[0, :, :, rows, :].reshape(2 * RADIX * DFT_BLK, FOURIER_W)
            a = _dot(ga_ref[...], z.astype(BF16))
            off = pl.multiple_of((j * DFT_STEP + t) * DFT_BLK, DFT_BLK)
            a_scr[:, :, pl.ds(off, DFT_BLK), :] = a.reshape(2, RADIX, DFT_BLK, FOURIER_W)

    @pl.when(j >= nstep)
    def _():
        for t in range(DFT_STEP):
            blk = (j - nstep) * DFT_STEP + t
            off = pl.multiple_of(blk * DFT_BLK, DFT_BLK)
            a = a_scr[:, pl.ds(off, DFT_BLK), :, :]
            a = a.reshape(2 * DFT_BLK * RADIX, FOURIER_W).astype(BF16)
            y = _dot(gb_ref[blk], a)
            o_ref[0, :, t * DFT_BLK:(t + 1) * DFT_BLK, :] = y.reshape(RADIX, DFT_BLK, FOURIER_W)


def _dft(ga, gb, z5):
    b = z5.shape[0]
    nstep = RADIX // (DFT_BLK * DFT_STEP)
    rows = DFT_BLK * DFT_STEP
    return pl.pallas_call(
        _dft_body,
        grid=(b, 2 * nstep),
        in_specs=[_full(ga.shape), _full(gb.shape),
                  pl.BlockSpec((1, 2, RADIX, rows, FOURIER_W),
                               lambda i, j: (i, 0, 0, jnp.minimum(j, nstep - 1), 0))],
        out_specs=pl.BlockSpec((1, RADIX, rows, FOURIER_W),
                               lambda i, j: (i, 0, jnp.maximum(j - nstep, 0), 0)),
        out_shape=jax.ShapeDtypeStruct((b, RADIX, RADIX, FOURIER_W), F32),
        scratch_shapes=[pltpu.VMEM((2, RADIX, RADIX, FOURIER_W), F32)],
        compiler_params=_params("arbitrary", "arbitrary"),
        name="dft",
    )(ga, gb, z5)


def _attn_body(qT_ref, k_ref, vT_ref, kc_ref, vcT_ref, *rest, tq, tkb, n_cast):
    for src, dst in zip(rest[:n_cast], rest[n_cast + 1:]):
        dst[...] = src[...].astype(dst.dtype)
    o_ref = rest[n_cast]
    nkb = k_ref.shape[2] // tkb

    def colmax(s):
        r = s.shape[0] // 8
        return jnp.max(jnp.max(s.reshape(8, r, s.shape[1]), axis=0), axis=0, keepdims=True)

    def update(carry, s, vT_blk):
        m, acc = carry
        m_new = jnp.maximum(m, colmax(s))
        p = jnp.exp2(s - m_new).astype(BF16)
        acc = jnp.exp2(m - m_new) * acc + _dot(vT_blk, p)
        return m_new, acc

    nq = qT_ref.shape[3] // tq

    def k_blk(i):
        return k_ref[0, 0, i * tkb:(i + 1) * tkb, :] if i < nkb else kc_ref[0, 0]

    def vT_blk(i):
        return vT_ref[0, 0, :, i * tkb:(i + 1) * tkb] if i < nkb else vcT_ref[0, 0]

    def q_cols(qi):
        return qT_ref[0, 0, :, pl.ds(pl.multiple_of(qi * tq, tq), tq)]

    def q_tile(qi, pending):
        qT = q_cols(qi)
        qT_next = q_cols(jnp.minimum(qi + 1, nq - 1))
        pending = list(pending)
        carry = (jnp.full((1, tq), -jnp.inf, F32), jnp.zeros((V_ROWS, tq), F32))
        for i in range(nkb + 1):
            j = i + ATTN_AHEAD
            pending.append(_dot(k_blk(j), qT) if j <= nkb else _dot(k_blk(j - nkb - 1), qT_next))
            carry = update(carry, pending.pop(0), vT_blk(i))
        _, acc = carry
        o_ref[0, :, pl.ds(pl.multiple_of(qi * tq, tq), tq)] = (
            acc[:V_DIM] / acc[V_DIM:V_DIM + 1]).astype(o_ref.dtype)
        return tuple(pending)

    lax.fori_loop(0, nq, q_tile, tuple(_dot(k_blk(i), q_cols(0)) for i in range(ATTN_AHEAD)))


def _attention(qT, k, vT, kc, vcT, weights, tq, tkb):
    b, h, _, s = qT.shape
    sc = kc.shape[2]
    steps = b * h
    w_specs = []
    for w, axis in weights:
        blk = tuple(n // steps if a == axis else n for a, n in enumerate(w.shape))
        w_specs.append(pl.BlockSpec(blk, (lambda i, j: (i * h + j, 0)) if axis == 0
                                    else (lambda i, j: (0, i * h + j))))
    outs = pl.pallas_call(
        functools.partial(_attn_body, tq=tq, tkb=tkb, n_cast=len(weights)),
        grid=(b, h),
        in_specs=[pl.BlockSpec((1, 1, HEAD_PAD, s), lambda i, j: (i, j, 0, 0)),
                  pl.BlockSpec((1, 1, s, HEAD_PAD), lambda i, j: (i, j, 0, 0)),
                  pl.BlockSpec((1, 1, V_ROWS, s), lambda i, j: (i, j, 0, 0)),
                  pl.BlockSpec((1, 1, sc, HEAD_PAD), lambda i, j: (i, j, 0, 0)),
                  pl.BlockSpec((1, 1, V_ROWS, sc), lambda i, j: (i, j, 0, 0))] + w_specs,
        out_specs=[pl.BlockSpec((1, V_DIM, s), lambda i, j: (i, j, 0))] + w_specs,
        out_shape=[jax.ShapeDtypeStruct((b, h * V_DIM, s), BF16)]
        + [jax.ShapeDtypeStruct(w.shape, BF16) for w, _ in weights],
        compiler_params=_params("arbitrary", "arbitrary"),
        name="attention",
    )(qT, k, vT, kc, vcT, *[w for w, _ in weights])
    return outs[0], outs[1:]


def _post_body(fm_ref, aT_ref, g_ref, x_ref, mod_ref, wfo_ref, wao_ref, wout_ref,
               g2_ref, wup_ref, wdn_ref, o_ref, *, fc):
    d = x_ref.shape[-1]
    m = mod_ref[0]
    yf = _dot(fm_ref[0].astype(BF16), wfo_ref[...])
    ya = lax.dot_general(aT_ref[0], wao_ref[...], (((0,), (0,)), ((), ())),
                         preferred_element_type=F32)
    g = g_ref[0]
    y = g[:, :d].astype(F32) * yf + g[:, d:].astype(F32) * ya
    x1 = x_ref[0] + m[2:3] * _dot(y.astype(BF16), wout_ref[...])
    hb = _norm_mod(x1, m[3:5], g2_ref[...]).astype(BF16)
    acc = jnp.zeros(x1.shape, F32)
    for c in range(wup_ref.shape[1] // fc):
        u = jnp.maximum(_dot(hb, wup_ref[:, c * fc:(c + 1) * fc]), 0.0)
        acc = acc + _dot((u * u).astype(BF16), wdn_ref[c * fc:(c + 1) * fc, :])
    o_ref[0] = x1 + m[5:6] * acc


def _post(fm, aT, gates, x, mod3, wfo, wao, wout, g2, wup, wdn, ts, fc):
    b, s, d = x.shape
    return pl.pallas_call(
        functools.partial(_post_body, fc=fc),
        grid=(b, s // ts),
        in_specs=[pl.BlockSpec((1, ts, FOURIER_W), lambda i, j: (i, j, 0)),
                  pl.BlockSpec((1, aT.shape[1], ts), lambda i, j: (i, 0, j)),
                  pl.BlockSpec((1, ts, 2 * d), lambda i, j: (i, j, 0)),
                  pl.BlockSpec((1, ts, d), lambda i, j: (i, j, 0)),
                  pl.BlockSpec((1, 6, d), lambda i, j: (i, 0, 0)),
                  _full(wfo.shape), _full(wao.shape), _full(wout.shape),
                  _full(g2.shape), _full(wup.shape), _full(wdn.shape)],
        out_specs=pl.BlockSpec((1, ts, d), lambda i, j: (i, j, 0)),
        out_shape=jax.ShapeDtypeStruct((b, s, d), F32),
        compiler_params=_params("arbitrary", "arbitrary"),
        name="post",
    )(fm, aT, gates, x, mod3, wfo, wao, wout, g2, wup, wdn)


@functools.lru_cache(maxsize=None)
def _dft_tables():
    j = np.arange(GROUP_W)
    ang = 2.0 * np.pi * np.outer(j, j) / GROUP_W
    cs = np.concatenate([np.cos(ang), -np.sin(ang)], axis=1) / math.sqrt(GROUP_W)

    a = np.arange(RADIX)
    ang = 2.0 * np.pi * np.outer(a, a) / RADIX
    c64, s64 = np.cos(ang), np.sin(ang)
    m1 = np.block([[c64, s64], [-s64, c64]]) / math.sqrt(RADIX)

    n = RADIX * RADIX
    k1 = np.arange(RADIX)[:, None, None]
    k2 = np.arange(RADIX)[None, :, None]
    bb = np.arange(RADIX)[None, None, :]
    th = 2.0 * np.pi * ((bb * (RADIX * k1 + k2)) % n) / n
    gc = np.cos(th) / math.sqrt(RADIX)
    gs = np.sin(th) / math.sqrt(RADIX)
    nblk = RADIX // DFT_BLK
    gtab = np.zeros((nblk, RADIX, DFT_BLK, 2, DFT_BLK, RADIX), np.float32)
    for jb in range(nblk):
        for q in range(DFT_BLK):
            gtab[jb, :, q, 0, q, :] = gc[:, jb * DFT_BLK + q, :]
            gtab[jb, :, q, 1, q, :] = gs[:, jb * DFT_BLK + q, :]
    gtab = gtab.reshape(nblk, RADIX * DFT_BLK, 2 * DFT_BLK * RADIX)
    ga = np.kron(m1, np.eye(DFT_BLK))
    return cs.astype(np.float32), ga.astype(np.float32), gtab


def _rope_tables_t(n):
    rows = n // GRID_W
    row = jnp.repeat(jnp.arange(rows, dtype=F32), GRID_W)
    col = jnp.tile(jnp.arange(GRID_W, dtype=F32), rows)
    n_freq = QK_ROPE // 4
    freqs = ROPE_THETA ** (-jnp.arange(n_freq, dtype=F32) / n_freq)
    ang_r = row[:, None] * freqs[None, :]
    ang_c = col[:, None] * freqs[None, :]
    ang = jnp.concatenate([ang_r, ang_r, ang_c, ang_c], axis=-1)
    return jnp.cos(ang).T, jnp.sin(ang).T


def kernel(x, c, ctx, c_ctx, ada_w, ada_b, norm1_g, norm2_g, w_in, b_gate, w_fourier, q_norm_g, w_qb,
           kv_norm_g, w_kvb, q_gain, k_gain, w_mla_o, w_out, w_up, w_down):
    bsz, s, d = x.shape
    sc = ctx.shape[1]
    assert ada_w.shape[0] == 1 and s == RADIX * RADIX and d == D_MODEL
    ts = 512

    cc = jnp.zeros((8, d), F32).at[:bsz].set(c).at[bsz].set(c_ctx)
    mod, wf, wq, wkv, wg = _adaln(cc, ada_w[0], ada_b[0][None, :], w_in[0])
    mod3 = mod.reshape(8, 6, d)

    wqbT = jnp.pad(w_qb[0].reshape(Q_LORA, N_HEADS, HEAD_QK),
                   ((0, 0), (0, 0), (0, HEAD_PAD - HEAD_QK))).reshape(Q_LORA, -1).T.astype(BF16)
    wkvb = w_kvb[0].reshape(KV_LORA, N_HEADS, QK_NOPE + V_DIM)
    wkbT = wkvb[:, :, :QK_NOPE].reshape(KV_LORA, -1).T.astype(BF16)
    wvT = wkvb[:, :, QK_NOPE:].reshape(KV_LORA, -1).T.astype(BF16)

    g1 = norm1_g[0][None, :]
    g2 = norm2_g[0][None, :]
    qg = q_norm_g[0][None, :]
    kvg = kv_norm_g[0][None, :]
    bg = b_gate[0][None, :]
    qscale = (HEAD_QK ** -0.5) * math.log2(math.e)
    qgain = jnp.broadcast_to(jnp.pad(q_gain[0] * qscale, (0, HEAD_PAD - HEAD_QK))[:, None], (HEAD_PAD, s))
    kgn = jnp.broadcast_to(k_gain[0][:QK_NOPE, None], (QK_NOPE, s))
    kgr = jnp.broadcast_to(k_gain[0][QK_NOPE:, None], (QK_ROPE, s))
    cosT, sinT = _rope_tables_t(s)
    cos1 = jnp.ones((QK_ROPE, sc), F32)
    sin0 = jnp.zeros((QK_ROPE, sc), F32)

    cs_np, ga_np, gtab_np = _dft_tables()
    cs = jnp.asarray(cs_np).astype(BF16)
    ga = jnp.asarray(ga_np).astype(BF16)
    gtab = jnp.asarray(gtab_np).astype(BF16)

    kc, vcT = _ctx_kv(ctx, mod3, g1, wkv, kvg, wkbT, wvT, kgn[:, :sc], kgr[:, :sc], cos1, sin0)

    z, gates, qT, k, vT = _proj(x, mod3, g1, wf, cs, wg, bg, wq, qg, wqbT, qgain, wkv, kvg, wkbT, wvT,
                                kgn, kgr, cosT, sinT, ts)

    fm = _dft(ga, gtab, z.reshape(bsz, 2, RADIX, RADIX, FOURIER_W)).reshape(bsz, s, FOURIER_W)

    aT, (wfo, wao, wout, wup, wdn) = _attention(
        qT, k, vT, kc, vcT,
        [(w_fourier[0], 0), (w_mla_o[0], 0), (w_out[0], 0), (w_up[0], 1), (w_down[0], 0)], 512, 256)

    return _post(fm, aT, gates, x, mod3, wfo, wao, wout, g2, wup, wdn, ts, 1024)
```

```python
import functools
import math

import numpy as np
import jax
import jax.numpy as jnp
from jax import lax
from jax.experimental import pallas as pl
from jax.experimental.pallas import tpu as pltpu

F32 = jnp.float32
BF16 = jnp.bfloat16

D_MODEL = 1024
GRID_W = 64
N_GROUPS = 4
GROUP_W = 128
FOURIER_W = N_GROUPS * GROUP_W
N_HEADS = 8
QK_NOPE = 64
QK_ROPE = 32
HEAD_QK = QK_NOPE + QK_ROPE
HEAD_PAD = 128
V_DIM = 64
V_ROWS = V_DIM + 16
Q_LORA = 256
KV_LORA = 128
W_IN_GATES = FOURIER_W + Q_LORA + KV_LORA + QK_ROPE
ROPE_THETA = 10000.0
EPS = 1e-6
RADIX = 64
DFT_BLK = 8
DFT_STEP = 2
GATE_CHUNK = 512
ATTN_AHEAD = 2

V7X_VMEM_LIMIT = 56 * 1024 * 1024


def _params(*sem):
    return pltpu.CompilerParams(dimension_semantics=sem, vmem_limit_bytes=V7X_VMEM_LIMIT)


def _dot(a, b):
    return jnp.dot(a, b, preferred_element_type=F32)


def _dot_nt(a, bt):
    return lax.dot_general(a, bt, (((1,), (1,)), ((), ())), preferred_element_type=F32)


def _rms_lanes(x, n):
    return x * lax.rsqrt(jnp.sum(x * x, axis=-1, keepdims=True) * (1.0 / n) + EPS)


def _rot_half_rows(x):
    return jnp.concatenate([-x[8:16], x[0:8], -x[24:32], x[16:24]], axis=0)


W_IN_BLK = 256


def _adaln_body(c_ref, w_ref, b_ref, tf_ref, tq_ref, tkv_ref, tga_ref, tgb_ref,
                o_ref, wf_ref, wq_ref, wkv_ref, wg_ref):
    c = c_ref[...]
    s = c * jax.nn.sigmoid(c)
    o_ref[...] = _dot(s, w_ref[...]) + b_ref[...]

    @pl.when(pl.program_id(0) == 0)
    def _():
        wf_ref[...] = tf_ref[...].astype(BF16)
        wq_ref[...] = tq_ref[...].astype(BF16)
        wkv_ref[...] = tkv_ref[...].astype(BF16)

    off = W_IN_GATES % W_IN_BLK
    both = jnp.concatenate([tga_ref[...], tgb_ref[...]], axis=0)
    wg_ref[...] = both[off:off + W_IN_BLK].astype(BF16)


def _adaln(cc, ada_w, ada_b, w_inT):
    rows, d = cc.shape
    n = ada_w.shape[1]
    n_gates = w_inT.shape[0] - W_IN_GATES
    steps = n_gates // W_IN_BLK
    tn = n // steps
    g0 = W_IN_GATES // W_IN_BLK
    blk = lambda r, f: pl.BlockSpec((r, d), f)
    return pl.pallas_call(
        _adaln_body,
        grid=(steps,),
        in_specs=[pl.BlockSpec((rows, d), lambda j: (0, 0)),
                  pl.BlockSpec((d, tn), lambda j: (0, j)),
                  pl.BlockSpec((1, tn), lambda j: (0, j)),
                  blk(FOURIER_W, lambda j: (0, 0)),
                  blk(Q_LORA, lambda j: (FOURIER_W // Q_LORA, 0)),
                  blk(2 * KV_LORA, lambda j: ((FOURIER_W + Q_LORA) // (2 * KV_LORA), 0)),
                  blk(W_IN_BLK, lambda j: (g0 + j, 0)),
                  blk(W_IN_BLK, lambda j: (g0 + 1 + j, 0))],
        out_specs=[pl.BlockSpec((rows, tn), lambda j: (0, j)),
                   blk(FOURIER_W, lambda j: (0, 0)), blk(Q_LORA, lambda j: (0, 0)),
                   blk(2 * KV_LORA, lambda j: (0, 0)), blk(W_IN_BLK, lambda j: (j, 0))],
        out_shape=[jax.ShapeDtypeStruct((rows, n), F32),
                   jax.ShapeDtypeStruct((FOURIER_W, d), BF16),
                   jax.ShapeDtypeStruct((Q_LORA, d), BF16),
                   jax.ShapeDtypeStruct((2 * KV_LORA, d), BF16),
                   jax.ShapeDtypeStruct((n_gates, d), BF16)],
        compiler_params=_params("arbitrary"),
        name="adaln",
    )(cc, ada_w, ada_b, w_inT, w_inT, w_inT, w_inT, w_inT)


def _norm_mod(x, m, g):
    ms = jnp.sum(x * x, axis=-1, keepdims=True) * (1.0 / x.shape[-1])
    return (x * lax.rsqrt(ms + EPS)) * (g * (1.0 + m[1:2])) + m[0:1]


def _kv_latents(kvp, kvg_ref):
    kvn = _rms_lanes(kvp[:, :KV_LORA], KV_LORA) * kvg_ref[...]
    return kvn.T.astype(BF16), kvp[:, KV_LORA:].T[0:QK_ROPE]


def _kv_heads(knT, vT, krT, kgn_ref, kgr_ref, cos_ref, sin_ref, k_ref, vT_ref):
    t = knT.shape[1]
    sr = jnp.sum(krT * krT, axis=0, keepdims=True)
    krg = krT * kgr_ref[...]
    kr_rot = krg * cos_ref[...] + _rot_half_rows(krg) * sin_ref[...]
    ones = jnp.ones((V_ROWS - V_DIM, t), F32)
    zpad = jnp.zeros((HEAD_PAD - HEAD_QK, t), F32)
    for h in range(N_HEADS):
        kn = knT[h * QK_NOPE:(h + 1) * QK_NOPE]
        ss = jnp.sum(kn * kn, axis=0, keepdims=True) + sr
        r = lax.rsqrt(ss * (1.0 / HEAD_QK) + EPS)
        kT = jnp.concatenate([kn * kgn_ref[...] * r, kr_rot * r, zpad], axis=0)
        k_ref[0, h] = kT.T.astype(k_ref.dtype)
        vh = vT[h * V_DIM:(h + 1) * V_DIM]
        vT_ref[0, h] = jnp.concatenate([vh, ones], axis=0).astype(vT_ref.dtype)


def _q_heads(qT, qgain_ref, cos_ref, sin_ref, qT_ref):
    for h in range(N_HEADS):
        qh = qT[h * HEAD_PAD:(h + 1) * HEAD_PAD]
        r = lax.rsqrt(jnp.sum(qh * qh, axis=0, keepdims=True) * (1.0 / HEAD_QK) + EPS)
        qh = qh * qgain_ref[...] * r
        qr = qh[QK_NOPE:HEAD_QK]
        qr = qr * cos_ref[...] + _rot_half_rows(qr) * sin_ref[...]
        qT_ref[0, h] = jnp.concatenate([qh[:QK_NOPE], qr, qh[HEAD_QK:]], axis=0).astype(qT_ref.dtype)


def _ctx_body(x_ref, mod_ref, g1_ref, wkv_ref, kvg_ref, wkbT_ref, wvT_ref, kgn_ref, kgr_ref,
              cos_ref, sin_ref, k_ref, vT_ref):
    hb = _norm_mod(x_ref[0], mod_ref[0], g1_ref[...]).astype(BF16)
    kvnT, krT = _kv_latents(_dot_nt(hb, wkv_ref[...]), kvg_ref)
    _kv_heads(_dot(wkbT_ref[...], kvnT), _dot(wvT_ref[...], kvnT), krT,
              kgn_ref, kgr_ref, cos_ref, sin_ref, k_ref, vT_ref)


def _proj_body(x_ref, mod_ref, g1_ref, wf_ref, cs_ref, wg_ref, bg_ref,
               wq_ref, qg_ref, wqbT_ref, qgain_ref,
               wkv_ref, kvg_ref, wkbT_ref, wvT_ref, kgn_ref, kgr_ref, cos_ref, sin_ref,
               z_ref, g_ref, qT_ref, k_ref, vT_ref):
    hb = _norm_mod(x_ref[0], mod_ref[0], g1_ref[...]).astype(BF16)
    qp = _dot_nt(hb, wq_ref[...])
    kvp = _dot_nt(hb, wkv_ref[...])
    f = _dot_nt(hb, wf_ref[...]).astype(BF16)

    qnT = (_rms_lanes(qp, Q_LORA) * qg_ref[...]).T.astype(BF16)
    kvnT, krT = _kv_latents(kvp, kvg_ref)

    for g in range(N_GROUPS):
        z = _dot(f[:, g * GROUP_W:(g + 1) * GROUP_W], cs_ref[...])
        z_ref[0, 0, :, g * GROUP_W:(g + 1) * GROUP_W] = z[:, :GROUP_W].astype(z_ref.dtype)
        z_ref[0, 1, :, g * GROUP_W:(g + 1) * GROUP_W] = z[:, GROUP_W:].astype(z_ref.dtype)

    qT = _dot(wqbT_ref[...], qnT)
    knT = _dot(wkbT_ref[...], kvnT)
    vT = _dot(wvT_ref[...], kvnT)

    def gate_chunk(c):
        cols = slice(c * GATE_CHUNK, (c + 1) * GATE_CHUNK)
        g_ref[0, :, cols] = jax.nn.sigmoid(_dot_nt(hb, wg_ref[cols, :]) + bg_ref[:, cols]).astype(g_ref.dtype)

    gate_chunk(0)
    _q_heads(qT, qgain_ref, cos_ref, sin_ref, qT_ref)
    gate_chunk(1)
    _kv_heads(knT, vT, krT, kgn_ref, kgr_ref, cos_ref, sin_ref, k_ref, vT_ref)
    for c in range(2, g_ref.shape[2] // GATE_CHUNK):
        gate_chunk(c)


def _full(shape):
    nd = len(shape)
    return pl.BlockSpec(shape, lambda *_: (0,) * nd, pipeline_mode=pl.Buffered(1))


def _ctx_kv(ctx, mod3, g1, wkv, kvg, wkbT, wvT, kgn, kgr, cosT, sinT):
    b, t, d = ctx.shape
    return pl.pallas_call(
        _ctx_body,
        grid=(b,),
        in_specs=[pl.BlockSpec((1, t, d), lambda i: (i, 0, 0)),
                  pl.BlockSpec((1, 6, d), lambda i: (b, 0, 0)),
                  _full(g1.shape), _full(wkv.shape), _full(kvg.shape), _full(wkbT.shape),
                  _full(wvT.shape), _full(kgn.shape), _full(kgr.shape),
                  _full(cosT.shape), _full(sinT.shape)],
        out_specs=[pl.BlockSpec((1, N_HEADS, t, HEAD_PAD), lambda i: (i, 0, 0, 0)),
                   pl.BlockSpec((1, N_HEADS, V_ROWS, t), lambda i: (i, 0, 0, 0))],
        out_shape=[jax.ShapeDtypeStruct((b, N_HEADS, t, HEAD_PAD), BF16),
                   jax.ShapeDtypeStruct((b, N_HEADS, V_ROWS, t), BF16)],
        compiler_params=_params("arbitrary"),
        name="ctx_kv",
    )(ctx, mod3, g1, wkv, kvg, wkbT, wvT, kgn, kgr, cosT, sinT)


def _proj(x, mod3, g1, wf, cs, wg, bg, wq, qg, wqbT, qgain, wkv, kvg, wkbT, wvT, kgn, kgr,
          cosT, sinT, ts):
    b, s, d = x.shape
    tok = lambda shape: pl.BlockSpec(shape, lambda i, j: (0, j))
    return pl.pallas_call(
        _proj_body,
        grid=(b, s // ts),
        in_specs=[pl.BlockSpec((1, ts, d), lambda i, j: (i, j, 0)),
                  pl.BlockSpec((1, 6, d), lambda i, j: (i, 0, 0)),
                  _full(g1.shape), _full(wf.shape), _full(cs.shape), _full(wg.shape),
                  _full(bg.shape), _full(wq.shape), _full(qg.shape), _full(wqbT.shape),
                  tok((HEAD_PAD, ts)),
                  _full(wkv.shape), _full(kvg.shape), _full(wkbT.shape), _full(wvT.shape),
                  tok((QK_NOPE, ts)), tok((QK_ROPE, ts)), tok((QK_ROPE, ts)), tok((QK_ROPE, ts))],
        out_specs=[pl.BlockSpec((1, 2, ts, FOURIER_W), lambda i, j: (i, 0, j, 0)),
                   pl.BlockSpec((1, ts, 2 * d), lambda i, j: (i, j, 0)),
                   pl.BlockSpec((1, N_HEADS, HEAD_PAD, ts), lambda i, j: (i, 0, 0, j)),
                   pl.BlockSpec((1, N_HEADS, ts, HEAD_PAD), lambda i, j: (i, 0, j, 0)),
                   pl.BlockSpec((1, N_HEADS, V_ROWS, ts), lambda i, j: (i, 0, 0, j))],
        out_shape=[jax.ShapeDtypeStruct((b, 2, s, FOURIER_W), F32),
                   jax.ShapeDtypeStruct((b, s, 2 * d), BF16),
                   jax.ShapeDtypeStruct((b, N_HEADS, HEAD_PAD, s), BF16),
                   jax.ShapeDtypeStruct((b, N_HEADS, s, HEAD_PAD), BF16),
                   jax.ShapeDtypeStruct((b, N_HEADS, V_ROWS, s), BF16)],
        compiler_params=_params("arbitrary", "arbitrary"),
        name="proj",
    )(x, mod3, g1, wf, cs, wg, bg, wq, qg, wqbT, qgain, wkv, kvg, wkbT, wvT, kgn, kgr,
      cosT, sinT)


def _dft_body(ga_ref, gb_ref, z_ref, o_ref, a_scr):
    nstep = RADIX // (DFT_BLK * DFT_STEP)
    j = pl.program_id(1)

    @pl.when(j < nstep)
    def _():
        for t in range(DFT_STEP):
            rows = slice(t * DFT_BLK, (t + 1) * DFT_BLK)
            z = z_ref[0, :, :, rows, :].reshape(2 * RADIX * DFT_BLK, FOURIER_W)
            a = _dot(ga_ref[...], z.astype(BF16))
            off = pl.multiple_of((j * DFT_STEP + t) * DFT_BLK, DFT_BLK)
            a_scr[:, :, pl.ds(off, DFT_BLK), :] = a.reshape(2, RADIX, DFT_BLK, FOURIER_W)

    @pl.when(j >= nstep)
    def _():
        for t in range(DFT_STEP):
            blk = (j - nstep) * DFT_STEP + t
            off = pl.multiple_of(blk * DFT_BLK, DFT_BLK)
            a = a_scr[:, pl.ds(off, DFT_BLK), :, :]
            a = a.reshape(2 * DFT_BLK * RADIX, FOURIER_W).astype(BF16)
            y = _dot(gb_ref[blk], a)
            o_ref[0, :, t * DFT_BLK:(t + 1) * DFT_BLK, :] = y.reshape(RADIX, DFT_BLK, FOURIER_W)


def _dft(ga, gb, z5):
    b = z5.shape[0]
    nstep = RADIX // (DFT_BLK * DFT_STEP)
    rows = DFT_BLK * DFT_STEP
    return pl.pallas_call(
        _dft_body,
        grid=(b, 2 * nstep),
        in_specs=[_full(ga.shape), _full(gb.shape),
                  pl.BlockSpec((1, 2, RADIX, rows, FOURIER_W),
                               lambda i, j: (i, 0, 0, jnp.minimum(j, nstep - 1), 0))],
        out_specs=pl.BlockSpec((1, RADIX, rows, FOURIER_W),
                               lambda i, j: (i, 0, jnp.maximum(j - nstep, 0), 0)),
        out_shape=jax.ShapeDtypeStruct((b, RADIX, RADIX, FOURIER_W), F32),
        scratch_shapes=[pltpu.VMEM((2, RADIX, RADIX, FOURIER_W), F32)],
        compiler_params=_params("arbitrary", "arbitrary"),
        name="dft",
    )(ga, gb, z5)


def _attn_body(qT_ref, k_ref, vT_ref, kc_ref, vcT_ref, *rest, tq, tkb, n_cast):
    for src, dst in zip(rest[:n_cast], rest[n_cast + 1:]):
        dst[...] = src[...].astype(dst.dtype)
    o_ref = rest[n_cast]
    nkb = k_ref.shape[2] // tkb

    def colmax(s):
        r = s.shape[0] // 8
        return jnp.max(jnp.max(s.reshape(8, r, s.shape[1]), axis=0), axis=0, keepdims=True)

    def update(carry, s, vT_blk):
        m, acc = carry
        m_new = jnp.maximum(m, colmax(s))
        p = jnp.exp2(s - m_new).astype(BF16)
        acc = jnp.exp2(m - m_new) * acc + _dot(vT_blk, p)
        return m_new, acc

    nq = qT_ref.shape[3] // tq

    def k_blk(i):
        return k_ref[0, 0, i * tkb:(i + 1) * tkb, :] if i < nkb else kc_ref[0, 0]

    def vT_blk(i):
        return vT_ref[0, 0, :, i * tkb:(i + 1) * tkb] if i < nkb else vcT_ref[0, 0]

    def q_cols(qi):
        return qT_ref[0, 0, :, pl.ds(pl.multiple_of(qi * tq, tq), tq)]

    def q_tile(qi, pending):
        qT = q_cols(qi)
        qT_next = q_cols(jnp.minimum(qi + 1, nq - 1))
        pending = list(pending)
        carry = (jnp.full((1, tq), -jnp.inf, F32), jnp.zeros((V_ROWS, tq), F32))
        for i in range(nkb + 1):
            j = i + ATTN_AHEAD
            pending.append(_dot(k_blk(j), qT) if j <= nkb else _dot(k_blk(j - nkb - 1), qT_next))
            carry = update(carry, pending.pop(0), vT_blk(i))
        _, acc = carry
        o_ref[0, :, pl.ds(pl.multiple_of(qi * tq, tq), tq)] = (
            acc[:V_DIM] / acc[V_DIM:V_DIM + 1]).astype(o_ref.dtype)
        return tuple(pending)

    lax.fori_loop(0, nq, q_tile, tuple(_dot(k_blk(i), q_cols(0)) for i in range(ATTN_AHEAD)))


def _attention(qT, k, vT, kc, vcT, weights, tq, tkb):
    b, h, _, s = qT.shape
    sc = kc.shape[2]
    steps = b * h
    w_specs = []
    for w, axis in weights:
        blk = tuple(n // steps if a == axis else n for a, n in enumerate(w.shape))
        w_specs.append(pl.BlockSpec(blk, (lambda i, j: (i * h + j, 0)) if axis == 0
                                    else (lambda i, j: (0, i * h + j))))
    outs = pl.pallas_call(
        functools.partial(_attn_body, tq=tq, tkb=tkb, n_cast=len(weights)),
        grid=(b, h),
        in_specs=[pl.BlockSpec((1, 1, HEAD_PAD, s), lambda i, j: (i, j, 0, 0)),
                  pl.BlockSpec((1, 1, s, HEAD_PAD), lambda i, j: (i, j, 0, 0)),
                  pl.BlockSpec((1, 1, V_ROWS, s), lambda i, j: (i, j, 0, 0)),
                  pl.BlockSpec((1, 1, sc, HEAD_PAD), lambda i, j: (i, j, 0, 0)),
                  pl.BlockSpec((1, 1, V_ROWS, sc), lambda i, j: (i, j, 0, 0))] + w_specs,
        out_specs=[pl.BlockSpec((1, V_DIM, s), lambda i, j: (i, j, 0))] + w_specs,
        out_shape=[jax.ShapeDtypeStruct((b, h * V_DIM, s), BF16)]
        + [jax.ShapeDtypeStruct(w.shape, BF16) for w, _ in weights],
        compiler_params=_params("arbitrary", "arbitrary"),
        name="attention",
    )(qT, k, vT, kc, vcT, *[w for w, _ in weights])
    return outs[0], outs[1:]


def _post_body(fm_ref, aT_ref, g_ref, x_ref, mod_ref, wfo_ref, wao_ref, wout_ref,
               g2_ref, wup_ref, wdn_ref, o_ref, *, fc):
    d = x_ref.shape[-1]
    m = mod_ref[0]
    yf = _dot(fm_ref[0].astype(BF16), wfo_ref[...])
    ya = lax.dot_general(aT_ref[0], wao_ref[...], (((0,), (0,)), ((), ())),
                         preferred_element_type=F32)
    g = g_ref[0]
    y = g[:, :d].astype(F32) * yf + g[:, d:].astype(F32) * ya
    x1 = x_ref[0] + m[2:3] * _dot(y.astype(BF16), wout_ref[...])
    hb = _norm_mod(x1, m[3:5], g2_ref[...]).astype(BF16)
    acc = jnp.zeros(x1.shape, F32)
    for c in range(wup_ref.shape[1] // fc):
        u = jnp.maximum(_dot(hb, wup_ref[:, c * fc:(c + 1) * fc]), 0.0)
        acc = acc + _dot((u * u).astype(BF16), wdn_ref[c * fc:(c + 1) * fc, :])
    o_ref[0] = x1 + m[5:6] * acc


def _post(fm, aT, gates, x, mod3, wfo, wao, wout, g2, wup, wdn, ts, fc):
    b, s, d = x.shape
    return pl.pallas_call(
        functools.partial(_post_body, fc=fc),
        grid=(b, s // ts),
        in_specs=[pl.BlockSpec((1, ts, FOURIER_W), lambda i, j: (i, j, 0)),
                  pl.BlockSpec((1, aT.shape[1], ts), lambda i, j: (i, 0, j)),
                  pl.BlockSpec((1, ts, 2 * d), lambda i, j: (i, j, 0)),
                  pl.BlockSpec((1, ts, d), lambda i, j: (i, j, 0)),
                  pl.BlockSpec((1, 6, d), lambda i, j: (i, 0, 0)),
                  _full(wfo.shape), _full(wao.shape), _full(wout.shape),
                  _full(g2.shape), _full(wup.shape), _full(wdn.shape)],
        out_specs=pl.BlockSpec((1, ts, d), lambda i, j: (i, j, 0)),
        out_shape=jax.ShapeDtypeStruct((b, s, d), F32),
        compiler_params=_params("arbitrary", "arbitrary"),
        name="post",
    )(fm, aT, gates, x, mod3, wfo, wao, wout, g2, wup, wdn)


@functools.lru_cache(maxsize=None)
def _dft_tables():
    j = np.arange(GROUP_W)
    ang = 2.0 * np.pi * np.outer(j, j) / GROUP_W
    cs = np.concatenate([np.cos(ang), -np.sin(ang)], axis=1) / math.sqrt(GROUP_W)

    a = np.arange(RADIX)
    ang = 2.0 * np.pi * np.outer(a, a) / RADIX
    c64, s64 = np.cos(ang), np.sin(ang)
    m1 = np.block([[c64, s64], [-s64, c64]]) / math.sqrt(RADIX)

    n = RADIX * RADIX
    k1 = np.arange(RADIX)[:, None, None]
    k2 = np.arange(RADIX)[None, :, None]
    bb = np.arange(RADIX)[None, None, :]
    th = 2.0 * np.pi * ((bb * (RADIX * k1 + k2)) % n) / n
    gc = np.cos(th) / math.sqrt(RADIX)
    gs = np.sin(th) / math.sqrt(RADIX)
    nblk = RADIX // DFT_BLK
    gtab = np.zeros((nblk, RADIX, DFT_BLK, 2, DFT_BLK, RADIX), np.float32)
    for jb in range(nblk):
        for q in range(DFT_BLK):
            gtab[jb, :, q, 0, q, :] = gc[:, jb * DFT_BLK + q, :]
            gtab[jb, :, q, 1, q, :] = gs[:, jb * DFT_BLK + q, :]
    gtab = gtab.reshape(nblk, RADIX * DFT_BLK, 2 * DFT_BLK * RADIX)
    ga = np.kron(m1, np.eye(DFT_BLK))
    return cs.astype(np.float32), ga.astype(np.float32), gtab


def _rope_tables_t(n):
    rows = n // GRID_W
    row = jnp.repeat(jnp.arange(rows, dtype=F32), GRID_W)
    col = jnp.tile(jnp.arange(GRID_W, dtype=F32), rows)
    n_freq = QK_ROPE // 4
    freqs = ROPE_THETA ** (-jnp.arange(n_freq, dtype=F32) / n_freq)
    ang_r = row[:, None] * freqs[None, :]
    ang_c = col[:, None] * freqs[None, :]
    ang = jnp.concatenate([ang_r, ang_r, ang_c, ang_c], axis=-1)
    return jnp.cos(ang).T, jnp.sin(ang).T


def kernel(x, c, ctx, c_ctx, ada_w, ada_b, norm1_g, norm2_g, w_in, b_gate, w_fourier, q_norm_g, w_qb,
           kv_norm_g, w_kvb, q_gain, k_gain, w_mla_o, w_out, w_up, w_down):
    bsz, s, d = x.shape
    sc = ctx.shape[1]
    assert ada_w.shape[0] == 1 and s == RADIX * RADIX and d == D_MODEL
    ts = 512

    cc = jnp.zeros((8, d), F32).at[:bsz].set(c).at[bsz].set(c_ctx)
    mod, wf, wq, wkv, wg = _adaln(cc, ada_w[0], ada_b[0][None, :], w_in[0].T)
    mod3 = mod.reshape(8, 6, d)

    wqbT = jnp.pad(w_qb[0].reshape(Q_LORA, N_HEADS, HEAD_QK),
                   ((0, 0), (0, 0), (0, HEAD_PAD - HEAD_QK))).reshape(Q_LORA, -1).T.astype(BF16)
    wkvb = w_kvb[0].reshape(KV_LORA, N_HEADS, QK_NOPE + V_DIM)
    wkbT = wkvb[:, :, :QK_NOPE].reshape(KV_LORA, -1).T.astype(BF16)
    wvT = wkvb[:, :, QK_NOPE:].reshape(KV_LORA, -1).T.astype(BF16)

    g1 = norm1_g[0][None, :]
    g2 = norm2_g[0][None, :]
    qg = q_norm_g[0][None, :]
    kvg = kv_norm_g[0][None, :]
    bg = b_gate[0][None, :]
    qscale = (HEAD_QK ** -0.5) * math.log2(math.e)
    qgain = jnp.broadcast_to(jnp.pad(q_gain[0] * qscale, (0, HEAD_PAD - HEAD_QK))[:, None], (HEAD_PAD, s))
    kgn = jnp.broadcast_to(k_gain[0][:QK_NOPE, None], (QK_NOPE, s))
    kgr = jnp.broadcast_to(k_gain[0][QK_NOPE:, None], (QK_ROPE, s))
    cosT, sinT = _rope_tables_t(s)
    cos1 = jnp.ones((QK_ROPE, sc), F32)
    sin0 = jnp.zeros((QK_ROPE, sc), F32)

    cs_np, ga_np, gtab_np = _dft_tables()
    cs = jnp.asarray(cs_np).astype(BF16)
    ga = jnp.asarray(ga_np).astype(BF16)
    gtab = jnp.asarray(gtab_np).astype(BF16)

    kc, vcT = _ctx_kv(ctx, mod3, g1, wkv, kvg, wkbT, wvT, kgn[:, :sc], kgr[:, :sc], cos1, sin0)

    z, gates, qT, k, vT = _proj(x, mod3, g1, wf, cs, wg, bg, wq, qg, wqbT, qgain, wkv, kvg, wkbT, wvT,
                                kgn, kgr, cosT, sinT, ts)

    fm = _dft(ga, gtab, z.reshape(bsz, 2, RADIX, RADIX, FOURIER_W)).reshape(bsz, s, FOURIER_W)

    aT, (wfo, wao, wout, wup, wdn) = _attention(
        qT, k, vT, kc, vcT,
        [(w_fourier[0], 0), (w_mla_o[0], 0), (w_out[0], 0), (w_up[0], 1), (w_down[0], 0)], 512, 256)

    return _post(fm, aT, gates, x, mod3, wfo, wao, wout, g2, wup, wdn, ts, 1024)
```

```python
import functools
import math

import numpy as np
import jax
import jax.numpy as jnp
from jax import lax
from jax.experimental import pallas as pl
from jax.experimental.pallas import tpu as pltpu

F32 = jnp.float32
BF16 = jnp.bfloat16

D_MODEL = 1024
GRID_W = 64
N_GROUPS = 4
GROUP_W = 128
FOURIER_W = N_GROUPS * GROUP_W
N_HEADS = 8
QK_NOPE = 64
QK_ROPE = 32
HEAD_QK = QK_NOPE + QK_ROPE
HEAD_PAD = 128
V_DIM = 64
V_ROWS = V_DIM + 16
Q_LORA = 256
KV_LORA = 128
W_IN_GATES = FOURIER_W + Q_LORA + KV_LORA + QK_ROPE
ROPE_THETA = 10000.0
EPS = 1e-6
RADIX = 64
DFT_BLK = 8
DFT_STEP = 2
GATE_CHUNK = 512
ATTN_AHEAD = 2

V7X_VMEM_LIMIT = 56 * 1024 * 1024


def _params(*sem):
    return pltpu.CompilerParams(dimension_semantics=sem, vmem_limit_bytes=V7X_VMEM_LIMIT)


def _dot(a, b):
    return jnp.dot(a, b, preferred_element_type=F32)


def _dot_nt(a, bt):
    return lax.dot_general(a, bt, (((1,), (1,)), ((), ())), preferred_element_type=F32)


def _rms_lanes(x, n):
    return x * lax.rsqrt(jnp.sum(x * x, axis=-1, keepdims=True) * (1.0 / n) + EPS)


def _rot_half_rows(x):
    return jnp.concatenate([-x[8:16], x[0:8], -x[24:32], x[16:24]], axis=0)


W_IN_BLK = 256


def _adaln_body(c_ref, w_ref, b_ref, tf_ref, tq_ref, tkv_ref, tga_ref, tgb_ref,
                o_ref, wf_ref, wq_ref, wkv_ref, wg_ref):
    c = c_ref[...]
    s = c * jax.nn.sigmoid(c)
    o_ref[...] = _dot(s, w_ref[...]) + b_ref[...]

    @pl.when(pl.program_id(0) == 0)
    def _():
        wf_ref[...] = tf_ref[...].astype(BF16)
        wq_ref[...] = tq_ref[...].astype(BF16)
        wkv_ref[...] = tkv_ref[...].astype(BF16)

    off = W_IN_GATES % W_IN_BLK
    both = jnp.concatenate([tga_ref[...], tgb_ref[...]], axis=0)
    wg_ref[...] = both[off:off + W_IN_BLK].astype(BF16)


def _adaln(cc, ada_w, ada_b, w_inT):
    rows, d = cc.shape
    n = ada_w.shape[1]
    n_gates = w_inT.shape[0] - W_IN_GATES
    steps = n_gates // W_IN_BLK
    tn = n // steps
    g0 = W_IN_GATES // W_IN_BLK
    blk = lambda r, f: pl.BlockSpec((r, d), f)
    return pl.pallas_call(
        _adaln_body,
        grid=(steps,),
        in_specs=[pl.BlockSpec((rows, d), lambda j: (0, 0)),
                  pl.BlockSpec((d, tn), lambda j: (0, j)),
                  pl.BlockSpec((1, tn), lambda j: (0, j)),
                  blk(FOURIER_W, lambda j: (0, 0)),
                  blk(Q_LORA, lambda j: (FOURIER_W // Q_LORA, 0)),
                  blk(2 * KV_LORA, lambda j: ((FOURIER_W + Q_LORA) // (2 * KV_LORA), 0)),
                  blk(W_IN_BLK, lambda j: (g0 + j, 0)),
                  blk(W_IN_BLK, lambda j: (g0 + 1 + j, 0))],
        out_specs=[pl.BlockSpec((rows, tn), lambda j: (0, j)),
                   blk(FOURIER_W, lambda j: (0, 0)), blk(Q_LORA, lambda j: (0, 0)),
                   blk(2 * KV_LORA, lambda j: (0, 0)), blk(W_IN_BLK, lambda j: (j, 0))],
        out_shape=[jax.ShapeDtypeStruct((rows, n), F32),
                   jax.ShapeDtypeStruct((FOURIER_W, d), BF16),
                   jax.ShapeDtypeStruct((Q_LORA, d), BF16),
                   jax.ShapeDtypeStruct((2 * KV_LORA, d), BF16),
                   jax.ShapeDtypeStruct((n_gates, d), BF16)],
        compiler_params=_params("arbitrary"),
        name="adaln",
    )(cc, ada_w, ada_b, w_inT, w_inT, w_inT, w_inT, w_inT)


def _norm_mod(x, m, g):
    ms = jnp.sum(x * x, axis=-1, keepdims=True) * (1.0 / x.shape[-1])
    return (x * lax.rsqrt(ms + EPS)) * (g * (1.0 + m[1:2])) + m[0:1]


def _kv_latents(kvp, kvg_ref):
    kvn = _rms_lanes(kvp[:, :KV_LORA], KV_LORA) * kvg_ref[...]
    return kvn.T.astype(BF16), kvp[:, KV_LORA:].T[0:QK_ROPE]


def _kv_heads(knT, vT, krT, kgn_ref, kgr_ref, cos_ref, sin_ref, k_ref, vT_ref):
    t = knT.shape[1]
    sr = jnp.sum(krT * krT, axis=0, keepdims=True)
    krg = krT * kgr_ref[...]
    kr_rot = krg * cos_ref[...] + _rot_half_rows(krg) * sin_ref[...]
    ones = jnp.ones((V_ROWS - V_DIM, t), F32)
    zpad = jnp.zeros((HEAD_PAD - HEAD_QK, t), F32)
    for h in range(N_HEADS):
        kn = knT[h * QK_NOPE:(h + 1) * QK_NOPE]
        ss = jnp.sum(kn * kn, axis=0, keepdims=True) + sr
        r = lax.rsqrt(ss * (1.0 / HEAD_QK) + EPS)
        kT = jnp.concatenate([kn * kgn_ref[...] * r, kr_rot * r, zpad], axis=0)
        k_ref[0, h] = kT.T.astype(k_ref.dtype)
        vh = vT[h * V_DIM:(h + 1) * V_DIM]
        vT_ref[0, h] = jnp.concatenate([vh, ones], axis=0).astype(vT_ref.dtype)


def _q_heads(qT, qgain_ref, cos_ref, sin_ref, qT_ref):
    for h in range(N_HEADS):
        qh = qT[h * HEAD_PAD:(h + 1) * HEAD_PAD]
        r = lax.rsqrt(jnp.sum(qh * qh, axis=0, keepdims=True) * (1.0 / HEAD_QK) + EPS)
        qh = qh * qgain_ref[...] * r
        qr = qh[QK_NOPE:HEAD_QK]
        qr = qr * cos_ref[...] + _rot_half_rows(qr) * sin_ref[...]
        qT_ref[0, h] = jnp.concatenate([qh[:QK_NOPE], qr, qh[HEAD_QK:]], axis=0).astype(qT_ref.dtype)


def _ctx_body(x_ref, mod_ref, g1_ref, wkv_ref, kvg_ref, wkbT_ref, wvT_ref, kgn_ref, kgr_ref,
              cos_ref, sin_ref, k_ref, vT_ref):
    hb = _norm_mod(x_ref[0], mod_ref[0], g1_ref[...]).astype(BF16)
    kvnT, krT = _kv_latents(_dot_nt(hb, wkv_ref[...]), kvg_ref)
    _kv_heads(_dot(wkbT_ref[...], kvnT), _dot(wvT_ref[...], kvnT), krT,
              kgn_ref, kgr_ref, cos_ref, sin_ref, k_ref, vT_ref)


def _proj_body(x_ref, mod_ref, g1_ref, wf_ref, cs_ref, wg_ref, bg_ref,
               wq_ref, qg_ref, wqbT_ref, qgain_ref,
               wkv_ref, kvg_ref, wkbT_ref, wvT_ref, kgn_ref, kgr_ref, cos_ref, sin_ref,
               z_ref, g_ref, qT_ref, k_ref, vT_ref):
    hb = _norm_mod(x_ref[0], mod_ref[0], g1_ref[...]).astype(BF16)
    qp = _dot_nt(hb, wq_ref[...])
    kvp = _dot_nt(hb, wkv_ref[...])
    f = _dot_nt(hb, wf_ref[...]).astype(BF16)

    qnT = (_rms_lanes(qp, Q_LORA) * qg_ref[...]).T.astype(BF16)
    kvnT, krT = _kv_latents(kvp, kvg_ref)

    for g in range(N_GROUPS):
        z = _dot(f[:, g * GROUP_W:(g + 1) * GROUP_W], cs_ref[...])
        z_ref[0, 0, :, g * GROUP_W:(g + 1) * GROUP_W] = z[:, :GROUP_W].astype(z_ref.dtype)
        z_ref[0, 1, :, g * GROUP_W:(g + 1) * GROUP_W] = z[:, GROUP_W:].astype(z_ref.dtype)

    qT = _dot(wqbT_ref[...], qnT)
    knT = _dot(wkbT_ref[...], kvnT)
    vT = _dot(wvT_ref[...], kvnT)

    def gate_chunk(c):
        cols = slice(c * GATE_CHUNK, (c + 1) * GATE_CHUNK)
        g_ref[0, :, cols] = jax.nn.sigmoid(_dot_nt(hb, wg_ref[cols, :]) + bg_ref[:, cols]).astype(g_ref.dtype)

    gate_chunk(0)
    _q_heads(qT, qgain_ref, cos_ref, sin_ref, qT_ref)
    gate_chunk(1)
    _kv_heads(knT, vT, krT, kgn_ref, kgr_ref, cos_ref, sin_ref, k_ref, vT_ref)
    for c in range(2, g_ref.shape[2] // GATE_CHUNK):
        gate_chunk(c)


def _full(shape):
    nd = len(shape)
    return pl.BlockSpec(shape, lambda *_: (0,) * nd, pipeline_mode=pl.Buffered(1))


def _ctx_kv(ctx, mod3, g1, wkv, kvg, wkbT, wvT, kgn, kgr, cosT, sinT):
    b, t, d = ctx.shape
    return pl.pallas_call(
        _ctx_body,
        grid=(b,),
        in_specs=[pl.BlockSpec((1, t, d), lambda i: (i, 0, 0)),
                  pl.BlockSpec((1, 6, d), lambda i: (b, 0, 0)),
                  _full(g1.shape), _full(wkv.shape), _full(kvg.shape), _full(wkbT.shape),
                  _full(wvT.shape), _full(kgn.shape), _full(kgr.shape),
                  _full(cosT.shape), _full(sinT.shape)],
        out_specs=[pl.BlockSpec((1, N_HEADS, t, HEAD_PAD), lambda i: (i, 0, 0, 0)),
                   pl.BlockSpec((1, N_HEADS, V_ROWS, t), lambda i: (i, 0, 0, 0))],
        out_shape=[jax.ShapeDtypeStruct((b, N_HEADS, t, HEAD_PAD), BF16),
                   jax.ShapeDtypeStruct((b, N_HEADS, V_ROWS, t), BF16)],
        compiler_params=_params("arbitrary"),
        name="ctx_kv",
    )(ctx, mod3, g1, wkv, kvg, wkbT, wvT, kgn, kgr, cosT, sinT)


def _proj(x, mod3, g1, wf, cs, wg, bg, wq, qg, wqbT, qgain, wkv, kvg, wkbT, wvT, kgn, kgr,
          cosT, sinT, ts):
    b, s, d = x.shape
    tok = lambda shape: pl.BlockSpec(shape, lambda i, j: (0, j))
    return pl.pallas_call(
        _proj_body,
        grid=(b, s // ts),
        in_specs=[pl.BlockSpec((1, ts, d), lambda i, j: (i, j, 0)),
                  pl.BlockSpec((1, 6, d), lambda i, j: (i, 0, 0)),
                  _full(g1.shape), _full(wf.shape), _full(cs.shape), _full(wg.shape),
                  _full(bg.shape), _full(wq.shape), _full(qg.shape), _full(wqbT.shape),
                  tok((HEAD_PAD, ts)),
                  _full(wkv.shape), _full(kvg.shape), _full(wkbT.shape), _full(wvT.shape),
                  tok((QK_NOPE, ts)), tok((QK_ROPE, ts)), tok((QK_ROPE, ts)), tok((QK_ROPE, ts))],
        out_specs=[pl.BlockSpec((1, 2, ts, FOURIER_W), lambda i, j: (i, 0, j, 0)),
                   pl.BlockSpec((1, ts, 2 * d), lambda i, j: (i, j, 0)),
                   pl.BlockSpec((1, N_HEADS, HEAD_PAD, ts), lambda i, j: (i, 0, 0, j)),
                   pl.BlockSpec((1, N_HEADS, ts, HEAD_PAD), lambda i, j: (i, 0, j, 0)),
                   pl.BlockSpec((1, N_HEADS, V_ROWS, ts), lambda i, j: (i, 0, 0, j))],
        out_shape=[jax.ShapeDtypeStruct((b, 2, s, FOURIER_W), F32),
                   jax.ShapeDtypeStruct((b, s, 2 * d), BF16),
                   jax.ShapeDtypeStruct((b, N_HEADS, HEAD_PAD, s), BF16),
                   jax.ShapeDtypeStruct((b, N_HEADS, s, HEAD_PAD), BF16),
                   jax.ShapeDtypeStruct((b, N_HEADS, V_ROWS, s), BF16)],
        compiler_params=_params("arbitrary", "arbitrary"),
        name="proj",
    )(x, mod3, g1, wf, cs, wg, bg, wq, qg, wqbT, qgain, wkv, kvg, wkbT, wvT, kgn, kgr,
      cosT, sinT)


def _dft_body(ga_ref, gb_ref, z_ref, o_ref, a_scr):
    nstep = RADIX // (DFT_BLK * DFT_STEP)
    j = pl.program_id(1)

    @pl.when(j < nstep)
    def _():
        for t in range(DFT_STEP):
            rows = slice(t * DFT_BLK, (t + 1) * DFT_BLK)
            z = z_ref[0, :, :, rows, :].reshape(2 * RADIX * DFT_BLK, FOURIER_W)
            a = _dot(ga_ref[...], z.astype(BF16))
            off = pl.multiple_of((j * DFT_STEP + t) * DFT_BLK, DFT_BLK)
            a_scr[:, :, pl.ds(off, DFT_BLK), :] = a.reshape(2, RADIX, DFT_BLK, FOURIER_W)

    @pl.when(j >= nstep)
    def _():
        for t in range(DFT_STEP):
            blk = (j - nstep) * DFT_STEP + t
            off = pl.multiple_of(blk * DFT_BLK, DFT_BLK)
            a = a_scr[:, pl.ds(off, DFT_BLK), :, :]
            a = a.reshape(2 * DFT_BLK * RADIX, FOURIER_W).astype(BF16)
            y = _dot(gb_ref[blk], a)
            o_ref[0, :, t * DFT_BLK:(t + 1) * DFT_BLK, :] = y.reshape(RADIX, DFT_BLK, FOURIER_W)


def _dft(ga, gb, z5):
    b = z5.shape[0]
    nstep = RADIX // (DFT_BLK * DFT_STEP)
    rows = DFT_BLK * DFT_STEP
    return pl.pallas_call(
        _dft_body,
        grid=(b, 2 * nstep),
        in_specs=[_full(ga.shape), _full(gb.shape),
                  pl.BlockSpec((1, 2, RADIX, rows, FOURIER_W),
                               lambda i, j: (i, 0, 0, jnp.minimum(j, nstep - 1), 0))],
        out_specs=pl.BlockSpec((1, RADIX, rows, FOURIER_W),
                               lambda i, j: (i, 0, jnp.maximum(j - nstep, 0), 0)),
        out_shape=jax.ShapeDtypeStruct((b, RADIX, RADIX, FOURIER_W), F32),
        scratch_shapes=[pltpu.VMEM((2, RADIX, RADIX, FOURIER_W), F32)],
        compiler_params=_params("arbitrary", "arbitrary"),
        name="dft",
    )(ga, gb, z5)


def _attn_body(qT_ref, k_ref, vT_ref, kc_ref, vcT_ref, *rest, tq, tkb, n_cast):
    for src, dst in zip(rest[:n_cast], rest[n_cast + 1:]):
        dst[...] = src[...].astype(dst.dtype)
    o_ref = rest[n_cast]
    nkb = k_ref.shape[2] // tkb

    def colmax(s):
        r = s.shape[0] // 8
        return jnp.max(jnp.max(s.reshape(8, r, s.shape[1]), axis=0), axis=0, keepdims=True)

    def update(carry, s, vT_blk):
        m, acc = carry
        m_new = jnp.maximum(m, colmax(s))
        p = jnp.exp2(s - m_new).astype(BF16)
        acc = jnp.exp2(m - m_new) * acc + _dot(vT_blk, p)
        return m_new, acc

    nq = qT_ref.shape[3] // tq

    def k_blk(i):
        return k_ref[0, 0, i * tkb:(i + 1) * tkb, :] if i < nkb else kc_ref[0, 0]

    def vT_blk(i):
        return vT_ref[0, 0, :, i * tkb:(i + 1) * tkb] if i < nkb else vcT_ref[0, 0]

    def q_cols(qi):
        return qT_ref[0, 0, :, pl.ds(pl.multiple_of(qi * tq, tq), tq)]

    def q_tile(qi, pending):
        qT = q_cols(qi)
        qT_next = q_cols(jnp.minimum(qi + 1, nq - 1))
        pending = list(pending)
        carry = (jnp.full((1, tq), -jnp.inf, F32), jnp.zeros((V_ROWS, tq), F32))
        for i in range(nkb + 1):
            j = i + ATTN_AHEAD
            pending.append(_dot(k_blk(j), qT) if j <= nkb else _dot(k_blk(j - nkb - 1), qT_next))
            carry = update(carry, pending.pop(0), vT_blk(i))
        _, acc = carry
        o_ref[0, :, pl.ds(pl.multiple_of(qi * tq, tq), tq)] = (
            acc[:V_DIM] / acc[V_DIM:V_DIM + 1]).astype(o_ref.dtype)
        return tuple(pending)

    lax.fori_loop(0, nq, q_tile, tuple(_dot(k_blk(i), q_cols(0)) for i in range(ATTN_AHEAD)),
                  unroll=4)


def _attention(qT, k, vT, kc, vcT, weights, tq, tkb):
    b, h, _, s = qT.shape
    sc = kc.shape[2]
    steps = b * h
    w_specs = []
    for w, axis in weights:
        blk = tuple(n // steps if a == axis else n for a, n in enumerate(w.shape))
        w_specs.append(pl.BlockSpec(blk, (lambda i, j: (i * h + j, 0)) if axis == 0
                                    else (lambda i, j: (0, i * h + j))))
    outs = pl.pallas_call(
        functools.partial(_attn_body, tq=tq, tkb=tkb, n_cast=len(weights)),
        grid=(b, h),
        in_specs=[pl.BlockSpec((1, 1, HEAD_PAD, s), lambda i, j: (i, j, 0, 0)),
                  pl.BlockSpec((1, 1, s, HEAD_PAD), lambda i, j: (i, j, 0, 0)),
                  pl.BlockSpec((1, 1, V_ROWS, s), lambda i, j: (i, j, 0, 0)),
                  pl.BlockSpec((1, 1, sc, HEAD_PAD), lambda i, j: (i, j, 0, 0)),
                  pl.BlockSpec((1, 1, V_ROWS, sc), lambda i, j: (i, j, 0, 0))] + w_specs,
        out_specs=[pl.BlockSpec((1, V_DIM, s), lambda i, j: (i, j, 0))] + w_specs,
        out_shape=[jax.ShapeDtypeStruct((b, h * V_DIM, s), BF16)]
        + [jax.ShapeDtypeStruct(w.shape, BF16) for w, _ in weights],
        compiler_params=_params("arbitrary", "arbitrary"),
        name="attention",
    )(qT, k, vT, kc, vcT, *[w for w, _ in weights])
    return outs[0], outs[1:]


def _post_body(fm_ref, aT_ref, g_ref, x_ref, mod_ref, wfo_ref, wao_ref, wout_ref,
               g2_ref, wup_ref, wdn_ref, o_ref, *, fc):
    d = x_ref.shape[-1]
    m = mod_ref[0]
    yf = _dot(fm_ref[0].astype(BF16), wfo_ref[...])
    ya = lax.dot_general(aT_ref[0], wao_ref[...], (((0,), (0,)), ((), ())),
                         preferred_element_type=F32)
    g = g_ref[0]
    y = g[:, :d].astype(F32) * yf + g[:, d:].astype(F32) * ya
    x1 = x_ref[0] + m[2:3] * _dot(y.astype(BF16), wout_ref[...])
    hb = _norm_mod(x1, m[3:5], g2_ref[...]).astype(BF16)
    acc = jnp.zeros(x1.shape, F32)
    for c in range(wup_ref.shape[1] // fc):
        u = jnp.maximum(_dot(hb, wup_ref[:, c * fc:(c + 1) * fc]), 0.0)
        acc = acc + _dot((u * u).astype(BF16), wdn_ref[c * fc:(c + 1) * fc, :])
    o_ref[0] = x1 + m[5:6] * acc


def _post(fm, aT, gates, x, mod3, wfo, wao, wout, g2, wup, wdn, ts, fc):
    b, s, d = x.shape
    return pl.pallas_call(
        functools.partial(_post_body, fc=fc),
        grid=(b, s // ts),
        in_specs=[pl.BlockSpec((1, ts, FOURIER_W), lambda i, j: (i, j, 0)),
                  pl.BlockSpec((1, aT.shape[1], ts), lambda i, j: (i, 0, j)),
                  pl.BlockSpec((1, ts, 2 * d), lambda i, j: (i, j, 0)),
                  pl.BlockSpec((1, ts, d), lambda i, j: (i, j, 0)),
                  pl.BlockSpec((1, 6, d), lambda i, j: (i, 0, 0)),
                  _full(wfo.shape), _full(wao.shape), _full(wout.shape),
                  _full(g2.shape), _full(wup.shape), _full(wdn.shape)],
        out_specs=pl.BlockSpec((1, ts, d), lambda i, j: (i, j, 0)),
        out_shape=jax.ShapeDtypeStruct((b, s, d), F32),
        compiler_params=_params("arbitrary", "arbitrary"),
        name="post",
    )(fm, aT, gates, x, mod3, wfo, wao, wout, g2, wup, wdn)


@functools.lru_cache(maxsize=None)
def _dft_tables():
    j = np.arange(GROUP_W)
    ang = 2.0 * np.pi * np.outer(j, j) / GROUP_W
    cs = np.concatenate([np.cos(ang), -np.sin(ang)], axis=1) / math.sqrt(GROUP_W)

    a = np.arange(RADIX)
    ang = 2.0 * np.pi * np.outer(a, a) / RADIX
    c64, s64 = np.cos(ang), np.sin(ang)
    m1 = np.block([[c64, s64], [-s64, c64]]) / math.sqrt(RADIX)

    n = RADIX * RADIX
    k1 = np.arange(RADIX)[:, None, None]
    k2 = np.arange(RADIX)[None, :, None]
    bb = np.arange(RADIX)[None, None, :]
    th = 2.0 * np.pi * ((bb * (RADIX * k1 + k2)) % n) / n
    gc = np.cos(th) / math.sqrt(RADIX)
    gs = np.sin(th) / math.sqrt(RADIX)
    nblk = RADIX // DFT_BLK
    gtab = np.zeros((nblk, RADIX, DFT_BLK, 2, DFT_BLK, RADIX), np.float32)
    for jb in range(nblk):
        for q in range(DFT_BLK):
            gtab[jb, :, q, 0, q, :] = gc[:, jb * DFT_BLK + q, :]
            gtab[jb, :, q, 1, q, :] = gs[:, jb * DFT_BLK + q, :]
    gtab = gtab.reshape(nblk, RADIX * DFT_BLK, 2 * DFT_BLK * RADIX)
    ga = np.kron(m1, np.eye(DFT_BLK))
    return cs.astype(np.float32), ga.astype(np.float32), gtab


def _rope_tables_t(n):
    rows = n // GRID_W
    row = jnp.repeat(jnp.arange(rows, dtype=F32), GRID_W)
    col = jnp.tile(jnp.arange(GRID_W, dtype=F32), rows)
    n_freq = QK_ROPE // 4
    freqs = ROPE_THETA ** (-jnp.arange(n_freq, dtype=F32) / n_freq)
    ang_r = row[:, None] * freqs[None, :]
    ang_c = col[:, None] * freqs[None, :]
    ang = jnp.concatenate([ang_r, ang_r, ang_c, ang_c], axis=-1)
    return jnp.cos(ang).T, jnp.sin(ang).T


def kernel(x, c, ctx, c_ctx, ada_w, ada_b, norm1_g, norm2_g, w_in, b_gate, w_fourier, q_norm_g, w_qb,
           kv_norm_g, w_kvb, q_gain, k_gain, w_mla_o, w_out, w_up, w_down):
    bsz, s, d = x.shape
    sc = ctx.shape[1]
    assert ada_w.shape[0] == 1 and s == RADIX * RADIX and d == D_MODEL
    ts = 512

    cc = jnp.zeros((8, d), F32).at[:bsz].set(c).at[bsz].set(c_ctx)
    mod, wf, wq, wkv, wg = _adaln(cc, ada_w[0], ada_b[0][None, :], w_in[0].T)
    mod3 = mod.reshape(8, 6, d)

    wqbT = jnp.pad(w_qb[0].reshape(Q_LORA, N_HEADS, HEAD_QK),
                   ((0, 0), (0, 0), (0, HEAD_PAD - HEAD_QK))).reshape(Q_LORA, -1).T.astype(BF16)
    wkvb = w_kvb[0].reshape(KV_LORA, N_HEADS, QK_NOPE + V_DIM)
    wkbT = wkvb[:, :, :QK_NOPE].reshape(KV_LORA, -1).T.astype(BF16)
    wvT = wkvb[:, :, QK_NOPE:].reshape(KV_LORA, -1).T.astype(BF16)

    g1 = norm1_g[0][None, :]
    g2 = norm2_g[0][None, :]
    qg = q_norm_g[0][None, :]
    kvg = kv_norm_g[0][None, :]
    bg = b_gate[0][None, :]
    qscale = (HEAD_QK ** -0.5) * math.log2(math.e)
    qgain = jnp.broadcast_to(jnp.pad(q_gain[0] * qscale, (0, HEAD_PAD - HEAD_QK))[:, None], (HEAD_PAD, s))
    kgn = jnp.broadcast_to(k_gain[0][:QK_NOPE, None], (QK_NOPE, s))
    kgr = jnp.broadcast_to(k_gain[0][QK_NOPE:, None], (QK_ROPE, s))
    cosT, sinT = _rope_tables_t(s)
    cos1 = jnp.ones((QK_ROPE, sc), F32)
    sin0 = jnp.zeros((QK_ROPE, sc), F32)

    cs_np, ga_np, gtab_np = _dft_tables()
    cs = jnp.asarray(cs_np).astype(BF16)
    ga = jnp.asarray(ga_np).astype(BF16)
    gtab = jnp.asarray(gtab_np).astype(BF16)

    kc, vcT = _ctx_kv(ctx, mod3, g1, wkv, kvg, wkbT, wvT, kgn[:, :sc], kgr[:, :sc], cos1, sin0)

    z, gates, qT, k, vT = _proj(x, mod3, g1, wf, cs, wg, bg, wq, qg, wqbT, qgain, wkv, kvg, wkbT, wvT,
                                kgn, kgr, cosT, sinT, ts)

    fm = _dft(ga, gtab, z.reshape(bsz, 2, RADIX, RADIX, FOURIER_W)).reshape(bsz, s, FOURIER_W)

    aT, (wfo, wao, wout, wup, wdn) = _attention(
        qT, k, vT, kc, vcT,
        [(w_fourier[0], 0), (w_mla_o[0], 0), (w_out[0], 0), (w_up[0], 1), (w_down[0], 0)], 512, 256)

    return _post(fm, aT, gates, x, mod3, wfo, wao, wout, g2, wup, wdn, ts, 1024)
```

```python
import functools
import math

import numpy as np
import jax
import jax.numpy as jnp
from jax import lax
from jax.experimental import pallas as pl
from jax.experimental.pallas import tpu as pltpu

F32 = jnp.float32
BF16 = jnp.bfloat16

D_MODEL = 1024
GRID_W = 64
N_GROUPS = 4
GROUP_W = 128
FOURIER_W = N_GROUPS * GROUP_W
N_HEADS = 8
QK_NOPE = 64
QK_ROPE = 32
HEAD_QK = QK_NOPE + QK_ROPE
HEAD_PAD = 128
V_DIM = 64
V_ROWS = V_DIM + 16
Q_LORA = 256
KV_LORA = 128
W_IN_GATES = FOURIER_W + Q_LORA + KV_LORA + QK_ROPE
ROPE_THETA = 10000.0
EPS = 1e-6
RADIX = 64
DFT_BLK = 8
DFT_STEP = 2
POST_PARTS = 2
PROJ_PARTS = 2
GATE_CHUNK = 512
ATTN_AHEAD = 2

V7X_VMEM_LIMIT = 56 * 1024 * 1024


def _params(*sem):
    return pltpu.CompilerParams(dimension_semantics=sem, vmem_limit_bytes=V7X_VMEM_LIMIT)


def _dot(a, b):
    return jnp.dot(a, b, preferred_element_type=F32)


def _dot_nt(a, bt):
    return lax.dot_general(a, bt, (((1,), (1,)), ((), ())), preferred_element_type=F32)


def _rms_lanes(x, n):
    return x * lax.rsqrt(jnp.sum(x * x, axis=-1, keepdims=True) * (1.0 / n) + EPS)


def _rot_half_rows(x):
    return jnp.concatenate([-x[8:16], x[0:8], -x[24:32], x[16:24]], axis=0)


W_IN_BLK = 256


def _adaln_body(c_ref, w_ref, b_ref, tf_ref, tq_ref, tkv_ref, tga_ref, tgb_ref,
                o_ref, wf_ref, wq_ref, wkv_ref, wg_ref):
    c = c_ref[...]
    s = c * jax.nn.sigmoid(c)
    o_ref[...] = _dot(s, w_ref[...]) + b_ref[...]

    @pl.when(pl.program_id(0) == 0)
    def _():
        wf_ref[...] = tf_ref[...].astype(BF16)
        wq_ref[...] = tq_ref[...].astype(BF16)
        wkv_ref[...] = tkv_ref[...].astype(BF16)

    off = W_IN_GATES % W_IN_BLK
    both = jnp.concatenate([tga_ref[...], tgb_ref[...]], axis=0)
    wg_ref[...] = both[off:off + W_IN_BLK].astype(BF16)


def _adaln(cc, ada_w, ada_b, w_inT):
    rows, d = cc.shape
    n = ada_w.shape[1]
    n_gates = w_inT.shape[0] - W_IN_GATES
    steps = n_gates // W_IN_BLK
    tn = n // steps
    g0 = W_IN_GATES // W_IN_BLK
    blk = lambda r, f: pl.BlockSpec((r, d), f)
    return pl.pallas_call(
        _adaln_body,
        grid=(steps,),
        in_specs=[pl.BlockSpec((rows, d), lambda j: (0, 0)),
                  pl.BlockSpec((d, tn), lambda j: (0, j)),
                  pl.BlockSpec((1, tn), lambda j: (0, j)),
                  blk(FOURIER_W, lambda j: (0, 0)),
                  blk(Q_LORA, lambda j: (FOURIER_W // Q_LORA, 0)),
                  blk(2 * KV_LORA, lambda j: ((FOURIER_W + Q_LORA) // (2 * KV_LORA), 0)),
                  blk(W_IN_BLK, lambda j: (g0 + j, 0)),
                  blk(W_IN_BLK, lambda j: (g0 + 1 + j, 0))],
        out_specs=[pl.BlockSpec((rows, tn), lambda j: (0, j)),
                   blk(FOURIER_W, lambda j: (0, 0)), blk(Q_LORA, lambda j: (0, 0)),
                   blk(2 * KV_LORA, lambda j: (0, 0)), blk(W_IN_BLK, lambda j: (j, 0))],
        out_shape=[jax.ShapeDtypeStruct((rows, n), F32),
                   jax.ShapeDtypeStruct((FOURIER_W, d), BF16),
                   jax.ShapeDtypeStruct((Q_LORA, d), BF16),
                   jax.ShapeDtypeStruct((2 * KV_LORA, d), BF16),
                   jax.ShapeDtypeStruct((n_gates, d), BF16)],
        compiler_params=_params("arbitrary"),
        name="adaln",
    )(cc, ada_w, ada_b, w_inT, w_inT, w_inT, w_inT, w_inT)


def _norm_mod(x, m, g):
    ms = jnp.sum(x * x, axis=-1, keepdims=True) * (1.0 / x.shape[-1])
    return (x * lax.rsqrt(ms + EPS)) * (g * (1.0 + m[1:2])) + m[0:1]


def _kv_latents(kvp, kvg_ref):
    kvn = _rms_lanes(kvp[:, :KV_LORA], KV_LORA) * kvg_ref[...]
    return kvn.T.astype(BF16), kvp[:, KV_LORA:].T[0:QK_ROPE]


def _kv_heads(knT, vT, krT, kgn_ref, kgr_ref, cos_ref, sin_ref, k_ref, vT_ref, tok):
    t = knT.shape[1]
    sr = jnp.sum(krT * krT, axis=0, keepdims=True)
    krg = krT * kgr_ref[:, tok]
    kr_rot = krg * cos_ref[:, tok] + _rot_half_rows(krg) * sin_ref[:, tok]
    kgn = kgn_ref[:, tok]
    ones = jnp.ones((V_ROWS - V_DIM, t), F32)
    zpad = jnp.zeros((HEAD_PAD - HEAD_QK, t), F32)
    for h in range(N_HEADS):
        kn = knT[h * QK_NOPE:(h + 1) * QK_NOPE]
        ss = jnp.sum(kn * kn, axis=0, keepdims=True) + sr
        r = lax.rsqrt(ss * (1.0 / HEAD_QK) + EPS)
        kT = jnp.concatenate([kn * kgn * r, kr_rot * r, zpad], axis=0)
        k_ref[0, h, tok, :] = kT.T.astype(k_ref.dtype)
        vh = vT[h * V_DIM:(h + 1) * V_DIM]
        vT_ref[0, h, :, tok] = jnp.concatenate([vh, ones], axis=0).astype(vT_ref.dtype)


def _q_heads(qT, qgain_ref, cos_ref, sin_ref, qT_ref, tok):
    qgain, cos, sin = qgain_ref[:, tok], cos_ref[:, tok], sin_ref[:, tok]
    for h in range(N_HEADS):
        qh = qT[h * HEAD_PAD:(h + 1) * HEAD_PAD]
        r = lax.rsqrt(jnp.sum(qh * qh, axis=0, keepdims=True) * (1.0 / HEAD_QK) + EPS)
        qh = qh * qgain * r
        qr = qh[QK_NOPE:HEAD_QK]
        qr = qr * cos + _rot_half_rows(qr) * sin
        qT_ref[0, h, :, tok] = jnp.concatenate([qh[:QK_NOPE], qr, qh[HEAD_QK:]],
                                               axis=0).astype(qT_ref.dtype)


def _ctx_body(x_ref, mod_ref, g1_ref, wkv_ref, kvg_ref, wkbT_ref, wvT_ref, kgn_ref, kgr_ref,
              cos_ref, sin_ref, k_ref, vT_ref):
    hb = _norm_mod(x_ref[0], mod_ref[0], g1_ref[...]).astype(BF16)
    kvnT, krT = _kv_latents(_dot_nt(hb, wkv_ref[...]), kvg_ref)
    _kv_heads(_dot(wkbT_ref[...], kvnT), _dot(wvT_ref[...], kvnT), krT,
              kgn_ref, kgr_ref, cos_ref, sin_ref, k_ref, vT_ref, slice(None))


def _proj_body(x_ref, mod_ref, g1_ref, wf_ref, cs_ref, wg_ref, bg_ref,
               wq_ref, qg_ref, wqbT_ref, qgain_ref,
               wkv_ref, kvg_ref, wkbT_ref, wvT_ref, kgn_ref, kgr_ref, cos_ref, sin_ref,
               z_ref, g_ref, qT_ref, k_ref, vT_ref):
    part = x_ref.shape[1] // PROJ_PARTS
    for p in range(PROJ_PARTS):
        tok = slice(p * part, (p + 1) * part)
        hb = _norm_mod(x_ref[0, tok, :], mod_ref[0], g1_ref[...]).astype(BF16)
        qp = _dot_nt(hb, wq_ref[...])
        kvp = _dot_nt(hb, wkv_ref[...])
        f = _dot_nt(hb, wf_ref[...]).astype(BF16)

        qnT = (_rms_lanes(qp, Q_LORA) * qg_ref[...]).T.astype(BF16)
        kvnT, krT = _kv_latents(kvp, kvg_ref)

        def gate_chunk(c):
            cols = slice(c * GATE_CHUNK, (c + 1) * GATE_CHUNK)
            g_ref[0, tok, cols] = jax.nn.sigmoid(
                _dot_nt(hb, wg_ref[cols, :]) + bg_ref[:, cols]).astype(g_ref.dtype)

        gate_chunk(0)

        qT = _dot(wqbT_ref[...], qnT)
        knT = _dot(wkbT_ref[...], kvnT)
        vT = _dot(wvT_ref[...], kvnT)

        gate_chunk(1)
        _q_heads(qT, qgain_ref, cos_ref, sin_ref, qT_ref, tok)
        gate_chunk(2)
        _kv_heads(knT, vT, krT, kgn_ref, kgr_ref, cos_ref, sin_ref, k_ref, vT_ref, tok)
        for c in range(3, g_ref.shape[2] // GATE_CHUNK):
            gate_chunk(c)

        for g in range(N_GROUPS):
            cols = slice(g * GROUP_W, (g + 1) * GROUP_W)
            z = _dot(f[:, cols], cs_ref[...])
            z_ref[0, 0, tok, cols] = z[:, :GROUP_W].astype(z_ref.dtype)
            z_ref[0, 1, tok, cols] = z[:, GROUP_W:].astype(z_ref.dtype)


def _full(shape):
    nd = len(shape)
    return pl.BlockSpec(shape, lambda *_: (0,) * nd, pipeline_mode=pl.Buffered(1))


def _ctx_kv(ctx, mod3, g1, wkv, kvg, wkbT, wvT, kgn, kgr, cosT, sinT):
    b, t, d = ctx.shape
    return pl.pallas_call(
        _ctx_body,
        grid=(b,),
        in_specs=[pl.BlockSpec((1, t, d), lambda i: (i, 0, 0)),
                  pl.BlockSpec((1, 6, d), lambda i: (b, 0, 0)),
                  _full(g1.shape), _full(wkv.shape), _full(kvg.shape), _full(wkbT.shape),
                  _full(wvT.shape), _full(kgn.shape), _full(kgr.shape),
                  _full(cosT.shape), _full(sinT.shape)],
        out_specs=[pl.BlockSpec((1, N_HEADS, t, HEAD_PAD), lambda i: (i, 0, 0, 0)),
                   pl.BlockSpec((1, N_HEADS, V_ROWS, t), lambda i: (i, 0, 0, 0))],
        out_shape=[jax.ShapeDtypeStruct((b, N_HEADS, t, HEAD_PAD), BF16),
                   jax.ShapeDtypeStruct((b, N_HEADS, V_ROWS, t), BF16)],
        compiler_params=_params("arbitrary"),
        name="ctx_kv",
    )(ctx, mod3, g1, wkv, kvg, wkbT, wvT, kgn, kgr, cosT, sinT)


def _proj(x, mod3, g1, wf, cs, wg, bg, wq, qg, wqbT, qgain, wkv, kvg, wkbT, wvT, kgn, kgr,
          cosT, sinT, ts):
    b, s, d = x.shape
    tok = lambda shape: pl.BlockSpec(shape, lambda i, j: (0, j))
    return pl.pallas_call(
        _proj_body,
        grid=(b, s // ts),
        in_specs=[pl.BlockSpec((1, ts, d), lambda i, j: (i, j, 0)),
                  pl.BlockSpec((1, 6, d), lambda i, j: (i, 0, 0)),
                  _full(g1.shape), _full(wf.shape), _full(cs.shape), _full(wg.shape),
                  _full(bg.shape), _full(wq.shape), _full(qg.shape), _full(wqbT.shape),
                  tok((HEAD_PAD, ts)),
                  _full(wkv.shape), _full(kvg.shape), _full(wkbT.shape), _full(wvT.shape),
                  tok((QK_NOPE, ts)), tok((QK_ROPE, ts)), tok((QK_ROPE, ts)), tok((QK_ROPE, ts))],
        out_specs=[pl.BlockSpec((1, 2, ts, FOURIER_W), lambda i, j: (i, 0, j, 0)),
                   pl.BlockSpec((1, ts, 2 * d), lambda i, j: (i, j, 0)),
                   pl.BlockSpec((1, N_HEADS, HEAD_PAD, ts), lambda i, j: (i, 0, 0, j)),
                   pl.BlockSpec((1, N_HEADS, ts, HEAD_PAD), lambda i, j: (i, 0, j, 0)),
                   pl.BlockSpec((1, N_HEADS, V_ROWS, ts), lambda i, j: (i, 0, 0, j))],
        out_shape=[jax.ShapeDtypeStruct((b, 2, s, FOURIER_W), F32),
                   jax.ShapeDtypeStruct((b, s, 2 * d), BF16),
                   jax.ShapeDtypeStruct((b, N_HEADS, HEAD_PAD, s), BF16),
                   jax.ShapeDtypeStruct((b, N_HEADS, s, HEAD_PAD), BF16),
                   jax.ShapeDtypeStruct((b, N_HEADS, V_ROWS, s), BF16)],
        compiler_params=_params("arbitrary", "arbitrary"),
        name="proj",
    )(x, mod3, g1, wf, cs, wg, bg, wq, qg, wqbT, qgain, wkv, kvg, wkbT, wvT, kgn, kgr,
      cosT, sinT)


def _dft_body(ga_ref, gb_ref, z_ref, o_ref, a_scr):
    nstep = RADIX // (DFT_BLK * DFT_STEP)
    j = pl.program_id(1)

    @pl.when(j < nstep)
    def _():
        for t in range(DFT_STEP):
            rows = slice(t * DFT_BLK, (t + 1) * DFT_BLK)
            z = z_ref[0, :, :, rows, :].reshape(2 * RADIX * DFT_BLK, FOURIER_W)
            a = _dot(ga_ref[...], z.astype(BF16))
            off = pl.multiple_of((j * DFT_STEP + t) * DFT_BLK, DFT_BLK)
            a_scr[:, :, pl.ds(off, DFT_BLK), :] = a.reshape(2, RADIX, DFT_BLK, FOURIER_W)

    @pl.when(j >= nstep)
    def _():
        for t in range(DFT_STEP):
            blk = (j - nstep) * DFT_STEP + t
            off = pl.multiple_of(blk * DFT_BLK, DFT_BLK)
            a = a_scr[:, pl.ds(off, DFT_BLK), :, :]
            a = a.reshape(2 * DFT_BLK * RADIX, FOURIER_W).astype(BF16)
            y = _dot(gb_ref[blk], a)
            o_ref[0, :, t * DFT_BLK:(t + 1) * DFT_BLK, :] = y.reshape(RADIX, DFT_BLK, FOURIER_W)


def _dft(ga, gb, z5):
    b = z5.shape[0]
    nstep = RADIX // (DFT_BLK * DFT_STEP)
    rows = DFT_BLK * DFT_STEP
    return pl.pallas_call(
        _dft_body,
        grid=(b, 2 * nstep),
        in_specs=[_full(ga.shape), _full(gb.shape),
                  pl.BlockSpec((1, 2, RADIX, rows, FOURIER_W),
                               lambda i, j: (i, 0, 0, jnp.minimum(j, nstep - 1), 0))],
        out_specs=pl.BlockSpec((1, RADIX, rows, FOURIER_W),
                               lambda i, j: (i, 0, jnp.maximum(j - nstep, 0), 0)),
        out_shape=jax.ShapeDtypeStruct((b, RADIX, RADIX, FOURIER_W), F32),
        scratch_shapes=[pltpu.VMEM((2, RADIX, RADIX, FOURIER_W), F32)],
        compiler_params=_params("arbitrary", "arbitrary"),
        name="dft",
    )(ga, gb, z5)


def _attn_body(qT_ref, k_ref, vT_ref, kc_ref, vcT_ref, *rest, tq, tkb, n_cast):
    for src, dst in zip(rest[:n_cast], rest[n_cast + 1:]):
        dst[...] = src[...].astype(dst.dtype)
    o_ref = rest[n_cast]
    nkb = k_ref.shape[2] // tkb

    def colmax(s):
        r = s.shape[0] // 8
        return jnp.max(jnp.max(s.reshape(8, r, s.shape[1]), axis=0), axis=0, keepdims=True)

    def update(carry, s, vT_blk):
        m, acc = carry
        m_new = jnp.maximum(m, colmax(s))
        p = jnp.exp2(s - m_new).astype(BF16)
        acc = jnp.exp2(m - m_new) * acc + _dot(vT_blk, p)
        return m_new, acc

    nq = qT_ref.shape[3] // tq

    def k_blk(i):
        return k_ref[0, 0, i * tkb:(i + 1) * tkb, :] if i < nkb else kc_ref[0, 0]

    def vT_blk(i):
        return vT_ref[0, 0, :, i * tkb:(i + 1) * tkb] if i < nkb else vcT_ref[0, 0]

    def q_cols(qi):
        return qT_ref[0, 0, :, pl.ds(pl.multiple_of(qi * tq, tq), tq)]

    def q_tile(qi, pending):
        qT = q_cols(qi)
        qT_next = q_cols(jnp.minimum(qi + 1, nq - 1))
        pending = list(pending)
        carry = (jnp.full((1, tq), -jnp.inf, F32), jnp.zeros((V_ROWS, tq), F32))
        for i in range(nkb + 1):
            j = i + ATTN_AHEAD
            pending.append(_dot(k_blk(j), qT) if j <= nkb else _dot(k_blk(j - nkb - 1), qT_next))
            carry = update(carry, pending.pop(0), vT_blk(i))
        _, acc = carry
        o_ref[0, :, pl.ds(pl.multiple_of(qi * tq, tq), tq)] = (
            acc[:V_DIM] / acc[V_DIM:V_DIM + 1]).astype(o_ref.dtype)
        return tuple(pending)

    lax.fori_loop(0, nq, q_tile, tuple(_dot(k_blk(i), q_cols(0)) for i in range(ATTN_AHEAD)),
                  unroll=4)


def _attention(qT, k, vT, kc, vcT, weights, tq, tkb):
    b, h, _, s = qT.shape
    sc = kc.shape[2]
    steps = b * h
    w_specs = []
    for w, axis in weights:
        blk = tuple(n // steps if a == axis else n for a, n in enumerate(w.shape))
        w_specs.append(pl.BlockSpec(blk, (lambda i, j: (i * h + j, 0)) if axis == 0
                                    else (lambda i, j: (0, i * h + j))))
    outs = pl.pallas_call(
        functools.partial(_attn_body, tq=tq, tkb=tkb, n_cast=len(weights)),
        grid=(b, h),
        in_specs=[pl.BlockSpec((1, 1, HEAD_PAD, s), lambda i, j: (i, j, 0, 0)),
                  pl.BlockSpec((1, 1, s, HEAD_PAD), lambda i, j: (i, j, 0, 0)),
                  pl.BlockSpec((1, 1, V_ROWS, s), lambda i, j: (i, j, 0, 0)),
                  pl.BlockSpec((1, 1, sc, HEAD_PAD), lambda i, j: (i, j, 0, 0)),
                  pl.BlockSpec((1, 1, V_ROWS, sc), lambda i, j: (i, j, 0, 0))] + w_specs,
        out_specs=[pl.BlockSpec((1, V_DIM, s), lambda i, j: (i, j, 0))] + w_specs,
        out_shape=[jax.ShapeDtypeStruct((b, h * V_DIM, s), BF16)]
        + [jax.ShapeDtypeStruct(w.shape, BF16) for w, _ in weights],
        compiler_params=_params("arbitrary", "arbitrary"),
        name="attention",
    )(qT, k, vT, kc, vcT, *[w for w, _ in weights])
    return outs[0], outs[1:]


def _post_body(fm_ref, aT_ref, g_ref, x_ref, mod_ref, wfo_ref, wao_ref, wout_ref,
               g2_ref, wup_ref, wdn_ref, o_ref, *, fc):
    d = x_ref.shape[-1]
    m = mod_ref[0]
    part = x_ref.shape[1] // POST_PARTS
    toks = [slice(p * part, (p + 1) * part) for p in range(POST_PARTS)]
    x1s, hbs = [], []
    for tok in toks:
        yf = _dot(fm_ref[0, tok, :].astype(BF16), wfo_ref[...])
        ya = lax.dot_general(aT_ref[0, :, tok], wao_ref[...], (((0,), (0,)), ((), ())),
                             preferred_element_type=F32)
        g = g_ref[0, tok, :]
        y = g[:, :d].astype(F32) * yf + g[:, d:].astype(F32) * ya
        x1 = x_ref[0, tok, :] + m[2:3] * _dot(y.astype(BF16), wout_ref[...])
        x1s.append(x1)
        hbs.append(_norm_mod(x1, m[3:5], g2_ref[...]).astype(BF16))
    for tok, x1, hb in zip(toks, x1s, hbs):
        acc = jnp.zeros(x1.shape, F32)
        for c in range(wup_ref.shape[1] // fc):
            u = jnp.maximum(_dot(hb, wup_ref[:, c * fc:(c + 1) * fc]), 0.0)
            acc = acc + _dot((u * u).astype(BF16), wdn_ref[c * fc:(c + 1) * fc, :])
        o_ref[0, tok, :] = x1 + m[5:6] * acc


def _post(fm, aT, gates, x, mod3, wfo, wao, wout, g2, wup, wdn, ts, fc):
    b, s, d = x.shape
    return pl.pallas_call(
        functools.partial(_post_body, fc=fc),
        grid=(b, s // ts),
        in_specs=[pl.BlockSpec((1, ts, FOURIER_W), lambda i, j: (i, j, 0)),
                  pl.BlockSpec((1, aT.shape[1], ts), lambda i, j: (i, 0, j)),
                  pl.BlockSpec((1, ts, 2 * d), lambda i, j: (i, j, 0)),
                  pl.BlockSpec((1, ts, d), lambda i, j: (i, j, 0)),
                  pl.BlockSpec((1, 6, d), lambda i, j: (i, 0, 0)),
                  _full(wfo.shape), _full(wao.shape), _full(wout.shape),
                  _full(g2.shape), _full(wup.shape), _full(wdn.shape)],
        out_specs=pl.BlockSpec((1, ts, d), lambda i, j: (i, j, 0)),
        out_shape=jax.ShapeDtypeStruct((b, s, d), F32),
        compiler_params=_params("arbitrary", "arbitrary"),
        name="post",
    )(fm, aT, gates, x, mod3, wfo, wao, wout, g2, wup, wdn)


@functools.lru_cache(maxsize=None)
def _dft_tables():
    j = np.arange(GROUP_W)
    ang = 2.0 * np.pi * np.outer(j, j) / GROUP_W
    cs = np.concatenate([np.cos(ang), -np.sin(ang)], axis=1) / math.sqrt(GROUP_W)

    a = np.arange(RADIX)
    ang = 2.0 * np.pi * np.outer(a, a) / RADIX
    c64, s64 = np.cos(ang), np.sin(ang)
    m1 = np.block([[c64, s64], [-s64, c64]]) / math.sqrt(RADIX)

    n = RADIX * RADIX
    k1 = np.arange(RADIX)[:, None, None]
    k2 = np.arange(RADIX)[None, :, None]
    bb = np.arange(RADIX)[None, None, :]
    th = 2.0 * np.pi * ((bb * (RADIX * k1 + k2)) % n) / n
    gc = np.cos(th) / math.sqrt(RADIX)
    gs = np.sin(th) / math.sqrt(RADIX)
    nblk = RADIX // DFT_BLK
    gtab = np.zeros((nblk, RADIX, DFT_BLK, 2, DFT_BLK, RADIX), np.float32)
    for jb in range(nblk):
        for q in range(DFT_BLK):
            gtab[jb, :, q, 0, q, :] = gc[:, jb * DFT_BLK + q, :]
            gtab[jb, :, q, 1, q, :] = gs[:, jb * DFT_BLK + q, :]
    gtab = gtab.reshape(nblk, RADIX * DFT_BLK, 2 * DFT_BLK * RADIX)
    ga = np.kron(m1, np.eye(DFT_BLK))
    return cs.astype(np.float32), ga.astype(np.float32), gtab


def _rope_tables_t(n):
    rows = n // GRID_W
    row = jnp.repeat(jnp.arange(rows, dtype=F32), GRID_W)
    col = jnp.tile(jnp.arange(GRID_W, dtype=F32), rows)
    n_freq = QK_ROPE // 4
    freqs = ROPE_THETA ** (-jnp.arange(n_freq, dtype=F32) / n_freq)
    ang_r = row[:, None] * freqs[None, :]
    ang_c = col[:, None] * freqs[None, :]
    ang = jnp.concatenate([ang_r, ang_r, ang_c, ang_c], axis=-1)
    return jnp.cos(ang).T, jnp.sin(ang).T


def kernel(x, c, ctx, c_ctx, ada_w, ada_b, norm1_g, norm2_g, w_in, b_gate, w_fourier, q_norm_g, w_qb,
           kv_norm_g, w_kvb, q_gain, k_gain, w_mla_o, w_out, w_up, w_down):
    bsz, s, d = x.shape
    sc = ctx.shape[1]
    assert ada_w.shape[0] == 1 and s == RADIX * RADIX and d == D_MODEL
    ts = 512

    cc = jnp.zeros((8, d), F32).at[:bsz].set(c).at[bsz].set(c_ctx)
    mod, wf, wq, wkv, wg = _adaln(cc, ada_w[0], ada_b[0][None, :], w_in[0].T)
    mod3 = mod.reshape(8, 6, d)

    wqbT = jnp.pad(w_qb[0].reshape(Q_LORA, N_HEADS, HEAD_QK),
                   ((0, 0), (0, 0), (0, HEAD_PAD - HEAD_QK))).reshape(Q_LORA, -1).T.astype(BF16)
    wkvb = w_kvb[0].reshape(KV_LORA, N_HEADS, QK_NOPE + V_DIM)
    wkbT = wkvb[:, :, :QK_NOPE].reshape(KV_LORA, -1).T.astype(BF16)
    wvT = wkvb[:, :, QK_NOPE:].reshape(KV_LORA, -1).T.astype(BF16)

    g1 = norm1_g[0][None, :]
    g2 = norm2_g[0][None, :]
    qg = q_norm_g[0][None, :]
    kvg = kv_norm_g[0][None, :]
    bg = b_gate[0][None, :]
    qscale = (HEAD_QK ** -0.5) * math.log2(math.e)
    qgain = jnp.broadcast_to(jnp.pad(q_gain[0] * qscale, (0, HEAD_PAD - HEAD_QK))[:, None], (HEAD_PAD, s))
    kgn = jnp.broadcast_to(k_gain[0][:QK_NOPE, None], (QK_NOPE, s))
    kgr = jnp.broadcast_to(k_gain[0][QK_NOPE:, None], (QK_ROPE, s))
    cosT, sinT = _rope_tables_t(s)
    cos1 = jnp.ones((QK_ROPE, sc), F32)
    sin0 = jnp.zeros((QK_ROPE, sc), F32)

    cs_np, ga_np, gtab_np = _dft_tables()
    cs = jnp.asarray(cs_np).astype(BF16)
    ga = jnp.asarray(ga_np).astype(BF16)
    gtab = jnp.asarray(gtab_np).astype(BF16)

    kc, vcT = _ctx_kv(ctx, mod3, g1, wkv, kvg, wkbT, wvT, kgn[:, :sc], kgr[:, :sc], cos1, sin0)

    z, gates, qT, k, vT = _proj(x, mod3, g1, wf, cs, wg, bg, wq, qg, wqbT, qgain, wkv, kvg, wkbT, wvT,
                                kgn, kgr, cosT, sinT, ts)

    fm = _dft(ga, gtab, z.reshape(bsz, 2, RADIX, RADIX, FOURIER_W)).reshape(bsz, s, FOURIER_W)

    aT, (wfo, wao, wout, wup, wdn) = _attention(
        qT, k, vT, kc, vcT,
        [(w_fourier[0], 0), (w_mla_o[0], 0), (w_out[0], 0), (w_up[0], 1), (w_down[0], 0)], 512, 256)

    return _post(fm, aT, gates, x, mod3, wfo, wao, wout, g2, wup, wdn, ts, 1024)
```

```python
import functools
import math

import numpy as np
import jax
import jax.numpy as jnp
from jax import lax
from jax.experimental import pallas as pl
from jax.experimental.pallas import tpu as pltpu

F32 = jnp.float32
BF16 = jnp.bfloat16

D_MODEL = 1024
GRID_W = 64
N_GROUPS = 4
GROUP_W = 128
FOURIER_W = N_GROUPS * GROUP_W
N_HEADS = 8
QK_NOPE = 64
QK_ROPE = 32
HEAD_QK = QK_NOPE + QK_ROPE
HEAD_PAD = 128
V_DIM = 64
V_ROWS = V_DIM + 16
Q_LORA = 256
KV_LORA = 128
W_IN_GATES = FOURIER_W + Q_LORA + KV_LORA + QK_ROPE
ROPE_THETA = 10000.0
EPS = 1e-6
RADIX = 64
DFT_BLK = 8
DFT_STEP = 2
TOKEN_TILE = 512
POST_PARTS = 2
PROJ_PARTS = 2
GATE_CHUNK = 512
FF_CHUNK = 1024
ATTN_TQ = 512
ATTN_TK = 256
ATTN_RING = 4
ATTN_AHEAD = 2
ATTN_UNROLL = 4

V7X_VMEM_LIMIT = 56 * 1024 * 1024


def _params(*sem):
    return pltpu.CompilerParams(dimension_semantics=sem, vmem_limit_bytes=V7X_VMEM_LIMIT)


def _dot(a, b):
    return jnp.dot(a, b, preferred_element_type=F32)


def _dot_nt(a, bt):
    return lax.dot_general(a, bt, (((1,), (1,)), ((), ())), preferred_element_type=F32)


def _rms_lanes(x, n):
    return x * lax.rsqrt(jnp.sum(x * x, axis=-1, keepdims=True) * (1.0 / n) + EPS)


def _rot_half_rows(x):
    return jnp.concatenate([-x[8:16], x[0:8], -x[24:32], x[16:24]], axis=0)


W_IN_BLK = 256


def _adaln_body(c_ref, w_ref, b_ref, tf_ref, tq_ref, tkv_ref, tga_ref, tgb_ref,
                o_ref, wf_ref, wq_ref, wkv_ref, wg_ref):
    c = c_ref[...]
    s = c * jax.nn.sigmoid(c)
    o_ref[...] = _dot(s, w_ref[...]) + b_ref[...]

    @pl.when(pl.program_id(0) == 0)
    def _():
        wf_ref[...] = tf_ref[...].astype(BF16)
        wq_ref[...] = tq_ref[...].astype(BF16)
        wkv_ref[...] = tkv_ref[...].astype(BF16)

    off = W_IN_GATES % W_IN_BLK
    both = jnp.concatenate([tga_ref[...], tgb_ref[...]], axis=0)
    wg_ref[...] = both[off:off + W_IN_BLK].astype(BF16)


def _adaln(cc, ada_w, ada_b, w_inT):
    rows, d = cc.shape
    n = ada_w.shape[1]
    n_gates = w_inT.shape[0] - W_IN_GATES
    steps = n_gates // W_IN_BLK
    tn = n // steps
    g0 = W_IN_GATES // W_IN_BLK
    blk = lambda r, f: pl.BlockSpec((r, d), f)
    return pl.pallas_call(
        _adaln_body,
        grid=(steps,),
        in_specs=[pl.BlockSpec((rows, d), lambda j: (0, 0)),
                  pl.BlockSpec((d, tn), lambda j: (0, j)),
                  pl.BlockSpec((1, tn), lambda j: (0, j)),
                  blk(FOURIER_W, lambda j: (0, 0)),
                  blk(Q_LORA, lambda j: (FOURIER_W // Q_LORA, 0)),
                  blk(2 * KV_LORA, lambda j: ((FOURIER_W + Q_LORA) // (2 * KV_LORA), 0)),
                  blk(W_IN_BLK, lambda j: (g0 + j, 0)),
                  blk(W_IN_BLK, lambda j: (g0 + 1 + j, 0))],
        out_specs=[pl.BlockSpec((rows, tn), lambda j: (0, j)),
                   blk(FOURIER_W, lambda j: (0, 0)), blk(Q_LORA, lambda j: (0, 0)),
                   blk(2 * KV_LORA, lambda j: (0, 0)), blk(W_IN_BLK, lambda j: (j, 0))],
        out_shape=[jax.ShapeDtypeStruct((rows, n), F32),
                   jax.ShapeDtypeStruct((FOURIER_W, d), BF16),
                   jax.ShapeDtypeStruct((Q_LORA, d), BF16),
                   jax.ShapeDtypeStruct((2 * KV_LORA, d), BF16),
                   jax.ShapeDtypeStruct((n_gates, d), BF16)],
        compiler_params=_params("arbitrary"),
        name="adaln",
    )(cc, ada_w, ada_b, w_inT, w_inT, w_inT, w_inT, w_inT)


def _norm_mod(x, m, g):
    ms = jnp.sum(x * x, axis=-1, keepdims=True) * (1.0 / x.shape[-1])
    return (x * lax.rsqrt(ms + EPS)) * (g * (1.0 + m[1:2])) + m[0:1]


def _kv_latents(kvp, kvg_ref):
    kvn = _rms_lanes(kvp[:, :KV_LORA], KV_LORA) * kvg_ref[...]
    return kvn.T.astype(BF16), kvp[:, KV_LORA:].T[0:QK_ROPE]


def _kv_heads(knT, vT, krT, kgn_ref, kgr_ref, cos_ref, sin_ref, k_ref, vT_ref, tok):
    t = knT.shape[1]
    sr = jnp.sum(krT * krT, axis=0, keepdims=True)
    krg = krT * kgr_ref[:, tok]
    kr_rot = krg * cos_ref[:, tok] + _rot_half_rows(krg) * sin_ref[:, tok]
    kgn = kgn_ref[:, tok]
    ones = jnp.ones((V_ROWS - V_DIM, t), F32)
    zpad = jnp.zeros((HEAD_PAD - HEAD_QK, t), F32)
    for h in range(N_HEADS):
        kn = knT[h * QK_NOPE:(h + 1) * QK_NOPE]
        ss = jnp.sum(kn * kn, axis=0, keepdims=True) + sr
        r = lax.rsqrt(ss * (1.0 / HEAD_QK) + EPS)
        kT = jnp.concatenate([kn * kgn * r, kr_rot * r, zpad], axis=0)
        k_ref[0, h, tok, :] = kT.T.astype(k_ref.dtype)
        vh = vT[h * V_DIM:(h + 1) * V_DIM]
        vT_ref[0, h, :, tok] = jnp.concatenate([vh, ones], axis=0).astype(vT_ref.dtype)


def _q_heads(qT, qgain_ref, cos_ref, sin_ref, qT_ref, tok):
    qgain, cos, sin = qgain_ref[:, tok], cos_ref[:, tok], sin_ref[:, tok]
    for h in range(N_HEADS):
        qh = qT[h * HEAD_PAD:(h + 1) * HEAD_PAD]
        r = lax.rsqrt(jnp.sum(qh * qh, axis=0, keepdims=True) * (1.0 / HEAD_QK) + EPS)
        qh = qh * qgain * r
        qr = qh[QK_NOPE:HEAD_QK]
        qr = qr * cos + _rot_half_rows(qr) * sin
        qT_ref[0, h, :, tok] = jnp.concatenate([qh[:QK_NOPE], qr, qh[HEAD_QK:]],
                                               axis=0).astype(qT_ref.dtype)


def _ctx_body(x_ref, mod_ref, g1_ref, wkv_ref, kvg_ref, wkbT_ref, wvT_ref, kgn_ref, kgr_ref,
              cos_ref, sin_ref, k_ref, vT_ref):
    hb = _norm_mod(x_ref[0], mod_ref[0], g1_ref[...]).astype(BF16)
    kvnT, krT = _kv_latents(_dot_nt(hb, wkv_ref[...]), kvg_ref)
    _kv_heads(_dot(wkbT_ref[...], kvnT), _dot(wvT_ref[...], kvnT), krT,
              kgn_ref, kgr_ref, cos_ref, sin_ref, k_ref, vT_ref, slice(None))


def _proj_body(x_ref, mod_ref, g1_ref, wf_ref, cs_ref, wg_ref, bg_ref,
               wq_ref, qg_ref, wqbT_ref, qgain_ref,
               wkv_ref, kvg_ref, wkbT_ref, wvT_ref, kgn_ref, kgr_ref, cos_ref, sin_ref,
               z_ref, g_ref, qT_ref, k_ref, vT_ref):
    part = x_ref.shape[1] // PROJ_PARTS
    for p in range(PROJ_PARTS):
        tok = slice(p * part, (p + 1) * part)
        hb = _norm_mod(x_ref[0, tok, :], mod_ref[0], g1_ref[...]).astype(BF16)
        qp = _dot_nt(hb, wq_ref[...])
        kvp = _dot_nt(hb, wkv_ref[...])
        f = _dot_nt(hb, wf_ref[...]).astype(BF16)

        qnT = (_rms_lanes(qp, Q_LORA) * qg_ref[...]).T.astype(BF16)
        kvnT, krT = _kv_latents(kvp, kvg_ref)

        def gate_chunk(c):
            cols = slice(c * GATE_CHUNK, (c + 1) * GATE_CHUNK)
            g_ref[0, tok, cols] = jax.nn.sigmoid(
                _dot_nt(hb, wg_ref[cols, :]) + bg_ref[:, cols]).astype(g_ref.dtype)

        gate_chunk(0)

        qT = _dot(wqbT_ref[...], qnT)
        knT = _dot(wkbT_ref[...], kvnT)
        vT = _dot(wvT_ref[...], kvnT)

        gate_chunk(1)
        _q_heads(qT, qgain_ref, cos_ref, sin_ref, qT_ref, tok)
        gate_chunk(2)
        _kv_heads(knT, vT, krT, kgn_ref, kgr_ref, cos_ref, sin_ref, k_ref, vT_ref, tok)
        for c in range(3, g_ref.shape[2] // GATE_CHUNK):
            gate_chunk(c)

        for g in range(N_GROUPS):
            cols = slice(g * GROUP_W, (g + 1) * GROUP_W)
            z = _dot(f[:, cols], cs_ref[...])
            z_ref[0, 0, tok, cols] = z[:, :GROUP_W].astype(z_ref.dtype)
            z_ref[0, 1, tok, cols] = z[:, GROUP_W:].astype(z_ref.dtype)


def _full(shape):
    nd = len(shape)
    return pl.BlockSpec(shape, lambda *_: (0,) * nd, pipeline_mode=pl.Buffered(1))


def _ctx_kv(ctx, mod3, g1, wkv, kvg, wkbT, wvT, kgn, kgr, cosT, sinT):
    b, t, d = ctx.shape
    return pl.pallas_call(
        _ctx_body,
        grid=(b,),
        in_specs=[pl.BlockSpec((1, t, d), lambda i: (i, 0, 0)),
                  pl.BlockSpec((1, 6, d), lambda i: (b, 0, 0)),
                  _full(g1.shape), _full(wkv.shape), _full(kvg.shape), _full(wkbT.shape),
                  _full(wvT.shape), _full(kgn.shape), _full(kgr.shape),
                  _full(cosT.shape), _full(sinT.shape)],
        out_specs=[pl.BlockSpec((1, N_HEADS, t, HEAD_PAD), lambda i: (i, 0, 0, 0)),
                   pl.BlockSpec((1, N_HEADS, V_ROWS, t), lambda i: (i, 0, 0, 0))],
        out_shape=[jax.ShapeDtypeStruct((b, N_HEADS, t, HEAD_PAD), BF16),
                   jax.ShapeDtypeStruct((b, N_HEADS, V_ROWS, t), BF16)],
        compiler_params=_params("arbitrary"),
        name="ctx_kv",
    )(ctx, mod3, g1, wkv, kvg, wkbT, wvT, kgn, kgr, cosT, sinT)


def _proj(x, mod3, g1, wf, cs, wg, bg, wq, qg, wqbT, qgain, wkv, kvg, wkbT, wvT, kgn, kgr,
          cosT, sinT, ts):
    b, s, d = x.shape
    tok = lambda shape: pl.BlockSpec(shape, lambda i, j: (0, j))
    return pl.pallas_call(
        _proj_body,
        grid=(b, s // ts),
        in_specs=[pl.BlockSpec((1, ts, d), lambda i, j: (i, j, 0)),
                  pl.BlockSpec((1, 6, d), lambda i, j: (i, 0, 0)),
                  _full(g1.shape), _full(wf.shape), _full(cs.shape), _full(wg.shape),
                  _full(bg.shape), _full(wq.shape), _full(qg.shape), _full(wqbT.shape),
                  tok((HEAD_PAD, ts)),
                  _full(wkv.shape), _full(kvg.shape), _full(wkbT.shape), _full(wvT.shape),
                  tok((QK_NOPE, ts)), tok((QK_ROPE, ts)), tok((QK_ROPE, ts)), tok((QK_ROPE, ts))],
        out_specs=[pl.BlockSpec((1, 2, ts, FOURIER_W), lambda i, j: (i, 0, j, 0)),
                   pl.BlockSpec((1, ts, 2 * d), lambda i, j: (i, j, 0)),
                   pl.BlockSpec((1, N_HEADS, HEAD_PAD, ts), lambda i, j: (i, 0, 0, j)),
                   pl.BlockSpec((1, N_HEADS, ts, HEAD_PAD), lambda i, j: (i, 0, j, 0)),
                   pl.BlockSpec((1, N_HEADS, V_ROWS, ts), lambda i, j: (i, 0, 0, j))],
        out_shape=[jax.ShapeDtypeStruct((b, 2, s, FOURIER_W), F32),
                   jax.ShapeDtypeStruct((b, s, 2 * d), BF16),
                   jax.ShapeDtypeStruct((b, N_HEADS, HEAD_PAD, s), BF16),
                   jax.ShapeDtypeStruct((b, N_HEADS, s, HEAD_PAD), BF16),
                   jax.ShapeDtypeStruct((b, N_HEADS, V_ROWS, s), BF16)],
        compiler_params=_params("arbitrary", "arbitrary"),
        name="proj",
    )(x, mod3, g1, wf, cs, wg, bg, wq, qg, wqbT, qgain, wkv, kvg, wkbT, wvT, kgn, kgr,
      cosT, sinT)


def _dft_body(ga_ref, gb_ref, z_ref, o_ref, a_scr):
    nstep = RADIX // (DFT_BLK * DFT_STEP)
    j = pl.program_id(1)

    @pl.when(j < nstep)
    def _():
        for t in range(DFT_STEP):
            rows = slice(t * DFT_BLK, (t + 1) * DFT_BLK)
            z = z_ref[0, :, :, rows, :].reshape(2 * RADIX * DFT_BLK, FOURIER_W)
            a = _dot(ga_ref[...], z.astype(BF16))
            off = pl.multiple_of((j * DFT_STEP + t) * DFT_BLK, DFT_BLK)
            a_scr[:, :, pl.ds(off, DFT_BLK), :] = a.reshape(2, RADIX, DFT_BLK, FOURIER_W)

    @pl.when(j >= nstep)
    def _():
        for t in range(DFT_STEP):
            blk = (j - nstep) * DFT_STEP + t
            off = pl.multiple_of(blk * DFT_BLK, DFT_BLK)
            a = a_scr[:, pl.ds(off, DFT_BLK), :, :]
            a = a.reshape(2 * DFT_BLK * RADIX, FOURIER_W).astype(BF16)
            y = _dot(gb_ref[blk], a)
            o_ref[0, :, t * DFT_BLK:(t + 1) * DFT_BLK, :] = y.reshape(RADIX, DFT_BLK, FOURIER_W)


def _dft(ga, gb, z5):
    b = z5.shape[0]
    nstep = RADIX // (DFT_BLK * DFT_STEP)
    rows = DFT_BLK * DFT_STEP
    return pl.pallas_call(
        _dft_body,
        grid=(b, 2 * nstep),
        in_specs=[_full(ga.shape), _full(gb.shape),
                  pl.BlockSpec((1, 2, RADIX, rows, FOURIER_W),
                               lambda i, j: (i, 0, 0, jnp.minimum(j, nstep - 1), 0))],
        out_specs=pl.BlockSpec((1, RADIX, rows, FOURIER_W),
                               lambda i, j: (i, 0, jnp.maximum(j - nstep, 0), 0)),
        out_shape=jax.ShapeDtypeStruct((b, RADIX, RADIX, FOURIER_W), F32),
        scratch_shapes=[pltpu.VMEM((2, RADIX, RADIX, FOURIER_W), F32)],
        compiler_params=_params("arbitrary", "arbitrary"),
        name="dft",
    )(ga, gb, z5)


def _attn_body(qT_ref, k_ref, vT_ref, kc_ref, vcT_ref, *rest, tq, tkb, n_cast):
    for src, dst in zip(rest[:n_cast], rest[n_cast + 1:]):
        dst[...] = src[...].astype(dst.dtype)
    o_ref = rest[n_cast]
    s_scr = rest[-1]
    nkb = k_ref.shape[2] // tkb
    nblk = nkb + 1
    ring = s_scr.shape[0]
    assert kc_ref.shape[2] == tkb and (ATTN_UNROLL * nblk) % ring == 0 and ATTN_AHEAD < ring

    def colmax(s):
        r = s.shape[0] // 8
        return jnp.max(jnp.max(s.reshape(8, r, s.shape[1]), axis=0), axis=0, keepdims=True)

    def update(carry, s, vT_blk):
        m, acc = carry
        m_new = jnp.maximum(m, colmax(s))
        p = jnp.exp2(s - m_new).astype(BF16)
        acc = jnp.exp2(m - m_new) * acc + _dot(vT_blk, p)
        return m_new, acc

    nq = qT_ref.shape[3] // tq

    def k_blk(i):
        return k_ref[0, 0, i * tkb:(i + 1) * tkb, :] if i < nkb else kc_ref[0, 0]

    def vT_blk(i):
        return vT_ref[0, 0, :, i * tkb:(i + 1) * tkb] if i < nkb else vcT_ref[0, 0]

    def q_cols(qi):
        return qT_ref[0, 0, :, pl.ds(pl.multiple_of(qi * tq, tq), tq)]

    def q_group(gi, _):
        for t in range(ATTN_UNROLL):
            qi = gi * ATTN_UNROLL + t
            qT = q_cols(qi)
            qT_next = q_cols(jnp.minimum(qi + 1, nq - 1))
            carry = (jnp.full((1, tq), -jnp.inf, F32), jnp.zeros((V_ROWS, tq), F32))
            for i in range(nblk):
                g = t * nblk + i
                j = i + ATTN_AHEAD
                s_scr[(g + ATTN_AHEAD) % ring] = (_dot(k_blk(j), qT) if j < nblk
                                                  else _dot(k_blk(j - nblk), qT_next))
                carry = update(carry, s_scr[g % ring], vT_blk(i))
            _, acc = carry
            o_ref[0, :, pl.ds(pl.multiple_of(qi * tq, tq), tq)] = (
                acc[:V_DIM] / acc[V_DIM:V_DIM + 1]).astype(o_ref.dtype)
        return 0

    for i in range(ATTN_AHEAD):
        s_scr[i] = _dot(k_blk(i), q_cols(0))
    lax.fori_loop(0, nq // ATTN_UNROLL, q_group, 0)


def _attention(qT, k, vT, kc, vcT, weights, tq, tkb):
    b, h, _, s = qT.shape
    sc = kc.shape[2]
    steps = b * h
    w_specs = []
    for w, axis in weights:
        blk = tuple(n // steps if a == axis else n for a, n in enumerate(w.shape))
        w_specs.append(pl.BlockSpec(blk, (lambda i, j: (i * h + j, 0)) if axis == 0
                                    else (lambda i, j: (0, i * h + j))))
    outs = pl.pallas_call(
        functools.partial(_attn_body, tq=tq, tkb=tkb, n_cast=len(weights)),
        grid=(b, h),
        in_specs=[pl.BlockSpec((1, 1, HEAD_PAD, s), lambda i, j: (i, j, 0, 0)),
                  pl.BlockSpec((1, 1, s, HEAD_PAD), lambda i, j: (i, j, 0, 0)),
                  pl.BlockSpec((1, 1, V_ROWS, s), lambda i, j: (i, j, 0, 0)),
                  pl.BlockSpec((1, 1, sc, HEAD_PAD), lambda i, j: (i, j, 0, 0)),
                  pl.BlockSpec((1, 1, V_ROWS, sc), lambda i, j: (i, j, 0, 0))] + w_specs,
        out_specs=[pl.BlockSpec((1, V_DIM, s), lambda i, j: (i, j, 0))] + w_specs,
        out_shape=[jax.ShapeDtypeStruct((b, h * V_DIM, s), BF16)]
        + [jax.ShapeDtypeStruct(w.shape, BF16) for w, _ in weights],
        scratch_shapes=[pltpu.VMEM((ATTN_RING, tkb, tq), F32)],
        compiler_params=_params("arbitrary", "arbitrary"),
        name="attention",
    )(qT, k, vT, kc, vcT, *[w for w, _ in weights])
    return outs[0], outs[1:]


def _post_body(fm_ref, aT_ref, g_ref, x_ref, mod_ref, wfo_ref, wao_ref, wout_ref,
               g2_ref, wup_ref, wdn_ref, o_ref, *, fc):
    d = x_ref.shape[-1]
    m = mod_ref[0]
    part = x_ref.shape[1] // POST_PARTS
    toks = [slice(p * part, (p + 1) * part) for p in range(POST_PARTS)]
    x1s, hbs = [], []
    for tok in toks:
        yf = _dot(fm_ref[0, tok, :].astype(BF16), wfo_ref[...])
        ya = lax.dot_general(aT_ref[0, :, tok], wao_ref[...], (((0,), (0,)), ((), ())),
                             preferred_element_type=F32)
        g = g_ref[0, tok, :]
        y = g[:, :d].astype(F32) * yf + g[:, d:].astype(F32) * ya
        x1 = x_ref[0, tok, :] + m[2:3] * _dot(y.astype(BF16), wout_ref[...])
        x1s.append(x1)
        hbs.append(_norm_mod(x1, m[3:5], g2_ref[...]).astype(BF16))
    for tok, x1, hb in zip(toks, x1s, hbs):
        acc = jnp.zeros(x1.shape, F32)
        for c in range(wup_ref.shape[1] // fc):
            u = jnp.maximum(_dot(hb, wup_ref[:, c * fc:(c + 1) * fc]), 0.0)
            acc = acc + _dot((u * u).astype(BF16), wdn_ref[c * fc:(c + 1) * fc, :])
        o_ref[0, tok, :] = x1 + m[5:6] * acc


def _post(fm, aT, gates, x, mod3, wfo, wao, wout, g2, wup, wdn, ts, fc):
    b, s, d = x.shape
    return pl.pallas_call(
        functools.partial(_post_body, fc=fc),
        grid=(b, s // ts),
        in_specs=[pl.BlockSpec((1, ts, FOURIER_W), lambda i, j: (i, j, 0)),
                  pl.BlockSpec((1, aT.shape[1], ts), lambda i, j: (i, 0, j)),
                  pl.BlockSpec((1, ts, 2 * d), lambda i, j: (i, j, 0)),
                  pl.BlockSpec((1, ts, d), lambda i, j: (i, j, 0)),
                  pl.BlockSpec((1, 6, d), lambda i, j: (i, 0, 0)),
                  _full(wfo.shape), _full(wao.shape), _full(wout.shape),
                  _full(g2.shape), _full(wup.shape), _full(wdn.shape)],
        out_specs=pl.BlockSpec((1, ts, d), lambda i, j: (i, j, 0)),
        out_shape=jax.ShapeDtypeStruct((b, s, d), F32),
        compiler_params=_params("arbitrary", "arbitrary"),
        name="post",
    )(fm, aT, gates, x, mod3, wfo, wao, wout, g2, wup, wdn)


@functools.lru_cache(maxsize=None)
def _dft_tables():
    j = np.arange(GROUP_W)
    ang = 2.0 * np.pi * np.outer(j, j) / GROUP_W
    cs = np.concatenate([np.cos(ang), -np.sin(ang)], axis=1) / math.sqrt(GROUP_W)

    a = np.arange(RADIX)
    ang = 2.0 * np.pi * np.outer(a, a) / RADIX
    c64, s64 = np.cos(ang), np.sin(ang)
    m1 = np.block([[c64, s64], [-s64, c64]]) / math.sqrt(RADIX)

    n = RADIX * RADIX
    k1 = np.arange(RADIX)[:, None, None]
    k2 = np.arange(RADIX)[None, :, None]
    bb = np.arange(RADIX)[None, None, :]
    th = 2.0 * np.pi * ((bb * (RADIX * k1 + k2)) % n) / n
    gc = np.cos(th) / math.sqrt(RADIX)
    gs = np.sin(th) / math.sqrt(RADIX)
    nblk = RADIX // DFT_BLK
    gtab = np.zeros((nblk, RADIX, DFT_BLK, 2, DFT_BLK, RADIX), np.float32)
    for jb in range(nblk):
        for q in range(DFT_BLK):
            gtab[jb, :, q, 0, q, :] = gc[:, jb * DFT_BLK + q, :]
            gtab[jb, :, q, 1, q, :] = gs[:, jb * DFT_BLK + q, :]
    gtab = gtab.reshape(nblk, RADIX * DFT_BLK, 2 * DFT_BLK * RADIX)
    ga = np.kron(m1, np.eye(DFT_BLK))
    return cs.astype(np.float32), ga.astype(np.float32), gtab


def _rope_tables_t(n):
    rows = n // GRID_W
    row = jnp.repeat(jnp.arange(rows, dtype=F32), GRID_W)
    col = jnp.tile(jnp.arange(GRID_W, dtype=F32), rows)
    n_freq = QK_ROPE // 4
    freqs = ROPE_THETA ** (-jnp.arange(n_freq, dtype=F32) / n_freq)
    ang_r = row[:, None] * freqs[None, :]
    ang_c = col[:, None] * freqs[None, :]
    ang = jnp.concatenate([ang_r, ang_r, ang_c, ang_c], axis=-1)
    return jnp.cos(ang).T, jnp.sin(ang).T


def kernel(x, c, ctx, c_ctx, ada_w, ada_b, norm1_g, norm2_g, w_in, b_gate, w_fourier, q_norm_g, w_qb,
           kv_norm_g, w_kvb, q_gain, k_gain, w_mla_o, w_out, w_up, w_down):
    bsz, s, d = x.shape
    sc = ctx.shape[1]
    assert ada_w.shape[0] == 1 and s == RADIX * RADIX and d == D_MODEL
    ts = TOKEN_TILE

    cc = jnp.zeros((8, d), F32).at[:bsz].set(c).at[bsz].set(c_ctx)
    mod, wf, wq, wkv, wg = _adaln(cc, ada_w[0], ada_b[0][None, :], w_in[0].T)
    mod3 = mod.reshape(8, 6, d)

    wqbT = jnp.pad(w_qb[0].reshape(Q_LORA, N_HEADS, HEAD_QK),
                   ((0, 0), (0, 0), (0, HEAD_PAD - HEAD_QK))).reshape(Q_LORA, -1).T.astype(BF16)
    wkvb = w_kvb[0].reshape(KV_LORA, N_HEADS, QK_NOPE + V_DIM)
    wkbT = wkvb[:, :, :QK_NOPE].reshape(KV_LORA, -1).T.astype(BF16)
    wvT = wkvb[:, :, QK_NOPE:].reshape(KV_LORA, -1).T.astype(BF16)

    g1 = norm1_g[0][None, :]
    g2 = norm2_g[0][None, :]
    qg = q_norm_g[0][None, :]
    kvg = kv_norm_g[0][None, :]
    bg = b_gate[0][None, :]
    qscale = (HEAD_QK ** -0.5) * math.log2(math.e)
    qgain = jnp.broadcast_to(jnp.pad(q_gain[0] * qscale, (0, HEAD_PAD - HEAD_QK))[:, None], (HEAD_PAD, s))
    kgn = jnp.broadcast_to(k_gain[0][:QK_NOPE, None], (QK_NOPE, s))
    kgr = jnp.broadcast_to(k_gain[0][QK_NOPE:, None], (QK_ROPE, s))
    cosT, sinT = _rope_tables_t(s)
    cos1 = jnp.ones((QK_ROPE, sc), F32)
    sin0 = jnp.zeros((QK_ROPE, sc), F32)

    cs_np, ga_np, gtab_np = _dft_tables()
    cs = jnp.asarray(cs_np).astype(BF16)
    ga = jnp.asarray(ga_np).astype(BF16)
    gtab = jnp.asarray(gtab_np).astype(BF16)

    kc, vcT = _ctx_kv(ctx, mod3, g1, wkv, kvg, wkbT, wvT, kgn[:, :sc], kgr[:, :sc], cos1, sin0)

    z, gates, qT, k, vT = _proj(x, mod3, g1, wf, cs, wg, bg, wq, qg, wqbT, qgain, wkv, kvg, wkbT, wvT,
                                kgn, kgr, cosT, sinT, ts)

    fm = _dft(ga, gtab, z.reshape(bsz, 2, RADIX, RADIX, FOURIER_W)).reshape(bsz, s, FOURIER_W)

    aT, (wfo, wao, wout, wup, wdn) = _attention(
        qT, k, vT, kc, vcT,
        [(w_fourier[0], 0), (w_mla_o[0], 0), (w_out[0], 0), (w_up[0], 1), (w_down[0], 0)],
        ATTN_TQ, ATTN_TK)

    return _post(fm, aT, gates, x, mod3, wfo, wao, wout, g2, wup, wdn, ts, FF_CHUNK)
```

```python
import functools
import math

import numpy as np
import jax
import jax.numpy as jnp
from jax import lax
from jax.experimental import pallas as pl
from jax.experimental.pallas import tpu as pltpu

F32 = jnp.float32
BF16 = jnp.bfloat16

D_MODEL = 1024
GRID_W = 64
N_GROUPS = 4
GROUP_W = 128
FOURIER_W = N_GROUPS * GROUP_W
N_HEADS = 8
QK_NOPE = 64
QK_ROPE = 32
HEAD_QK = QK_NOPE + QK_ROPE
HEAD_PAD = 128
V_DIM = 64
V_ROWS = V_DIM + 16
Q_LORA = 256
KV_LORA = 128
W_IN_GATES = FOURIER_W + Q_LORA + KV_LORA + QK_ROPE
ROPE_THETA = 10000.0
EPS = 1e-6
RADIX = 64
DFT_BLK = 8
DFT_STEP = 2
POST_TILE = 512
POST_PARTS = 2
PROJ_TILE = 1024
PROJ_PARTS = 4
GATE_CHUNK = 512
FF_CHUNK = 1024
ATTN_TQ = 512
ATTN_TK = 256
ATTN_AHEAD = 2
ATTN_UNROLL = 4

V7X_VMEM_LIMIT = 56 * 1024 * 1024


def _params(*sem):
    return pltpu.CompilerParams(dimension_semantics=sem, vmem_limit_bytes=V7X_VMEM_LIMIT)


def _dot(a, b):
    return jnp.dot(a, b, preferred_element_type=F32)


def _dot_nt(a, bt):
    return lax.dot_general(a, bt, (((1,), (1,)), ((), ())), preferred_element_type=F32)


def _rms_lanes(x, n):
    return x * lax.rsqrt(jnp.sum(x * x, axis=-1, keepdims=True) * (1.0 / n) + EPS)


def _rot_half_rows(x):
    return jnp.concatenate([-x[8:16], x[0:8], -x[24:32], x[16:24]], axis=0)


W_IN_BLK = 256


def _adaln_body(c_ref, w_ref, b_ref, tf_ref, tq_ref, tkv_ref, tga_ref, tgb_ref,
                o_ref, wf_ref, wq_ref, wkv_ref, wg_ref):
    c = c_ref[...]
    s = c * jax.nn.sigmoid(c)
    o_ref[...] = _dot(s, w_ref[...]) + b_ref[...]

    @pl.when(pl.program_id(0) == 0)
    def _():
        wf_ref[...] = tf_ref[...].astype(BF16)
        wq_ref[...] = tq_ref[...].astype(BF16)
        wkv_ref[...] = tkv_ref[...].astype(BF16)

    off = W_IN_GATES % W_IN_BLK
    both = jnp.concatenate([tga_ref[...], tgb_ref[...]], axis=0)
    wg_ref[...] = both[off:off + W_IN_BLK].astype(BF16)


def _adaln(cc, ada_w, ada_b, w_inT):
    rows, d = cc.shape
    n = ada_w.shape[1]
    n_gates = w_inT.shape[0] - W_IN_GATES
    steps = n_gates // W_IN_BLK
    tn = n // steps
    g0 = W_IN_GATES // W_IN_BLK
    blk = lambda r, f: pl.BlockSpec((r, d), f)
    return pl.pallas_call(
        _adaln_body,
        grid=(steps,),
        in_specs=[pl.BlockSpec((rows, d), lambda j: (0, 0)),
                  pl.BlockSpec((d, tn), lambda j: (0, j)),
                  pl.BlockSpec((1, tn), lambda j: (0, j)),
                  blk(FOURIER_W, lambda j: (0, 0)),
                  blk(Q_LORA, lambda j: (FOURIER_W // Q_LORA, 0)),
                  blk(2 * KV_LORA, lambda j: ((FOURIER_W + Q_LORA) // (2 * KV_LORA), 0)),
                  blk(W_IN_BLK, lambda j: (g0 + j, 0)),
                  blk(W_IN_BLK, lambda j: (g0 + 1 + j, 0))],
        out_specs=[pl.BlockSpec((rows, tn), lambda j: (0, j)),
                   blk(FOURIER_W, lambda j: (0, 0)), blk(Q_LORA, lambda j: (0, 0)),
                   blk(2 * KV_LORA, lambda j: (0, 0)), blk(W_IN_BLK, lambda j: (j, 0))],
        out_shape=[jax.ShapeDtypeStruct((rows, n), F32),
                   jax.ShapeDtypeStruct((FOURIER_W, d), BF16),
                   jax.ShapeDtypeStruct((Q_LORA, d), BF16),
                   jax.ShapeDtypeStruct((2 * KV_LORA, d), BF16),
                   jax.ShapeDtypeStruct((n_gates, d), BF16)],
        compiler_params=_params("arbitrary"),
        name="adaln",
    )(cc, ada_w, ada_b, w_inT, w_inT, w_inT, w_inT, w_inT)


def _norm_mod(x, m, g):
    ms = jnp.sum(x * x, axis=-1, keepdims=True) * (1.0 / x.shape[-1])
    return (x * lax.rsqrt(ms + EPS)) * (g * (1.0 + m[1:2])) + m[0:1]


def _kv_latents(kvp, kvg_ref):
    kvn = _rms_lanes(kvp[:, :KV_LORA], KV_LORA) * kvg_ref[...]
    return kvn.T.astype(BF16), kvp[:, KV_LORA:].T[0:QK_ROPE]


def _kv_heads(knT, vT, krT, kgn_ref, kgr_ref, cos_ref, sin_ref, k_ref, vT_ref, tok):
    t = knT.shape[1]
    sr = jnp.sum(krT * krT, axis=0, keepdims=True)
    krg = krT * kgr_ref[:, tok]
    kr_rot = krg * cos_ref[:, tok] + _rot_half_rows(krg) * sin_ref[:, tok]
    kgn = kgn_ref[:, tok]
    ones = jnp.ones((V_ROWS - V_DIM, t), F32)
    zpad = jnp.zeros((HEAD_PAD - HEAD_QK, t), F32)
    for h in range(N_HEADS):
        kn = knT[h * QK_NOPE:(h + 1) * QK_NOPE]
        ss = jnp.sum(kn * kn, axis=0, keepdims=True) + sr
        r = lax.rsqrt(ss * (1.0 / HEAD_QK) + EPS)
        kT = jnp.concatenate([kn * kgn * r, kr_rot * r, zpad], axis=0)
        k_ref[0, h, tok, :] = kT.T.astype(k_ref.dtype)
        vh = vT[h * V_DIM:(h + 1) * V_DIM]
        vT_ref[0, h, :, tok] = jnp.concatenate([vh, ones], axis=0).astype(vT_ref.dtype)


def _q_heads(qT, qgain_ref, cos_ref, sin_ref, qT_ref, tok):
    qgain, cos, sin = qgain_ref[:, tok], cos_ref[:, tok], sin_ref[:, tok]
    for h in range(N_HEADS):
        qh = qT[h * HEAD_PAD:(h + 1) * HEAD_PAD]
        r = lax.rsqrt(jnp.sum(qh * qh, axis=0, keepdims=True) * (1.0 / HEAD_QK) + EPS)
        qh = qh * qgain * r
        qr = qh[QK_NOPE:HEAD_QK]
        qr = qr * cos + _rot_half_rows(qr) * sin
        qT_ref[0, h, :, tok] = jnp.concatenate([qh[:QK_NOPE], qr, qh[HEAD_QK:]],
                                               axis=0).astype(qT_ref.dtype)


def _ctx_body(x_ref, mod_ref, g1_ref, wkv_ref, kvg_ref, wkbT_ref, wvT_ref, kgn_ref, kgr_ref,
              cos_ref, sin_ref, k_ref, vT_ref):
    hb = _norm_mod(x_ref[0], mod_ref[0], g1_ref[...]).astype(BF16)
    kvnT, krT = _kv_latents(_dot_nt(hb, wkv_ref[...]), kvg_ref)
    _kv_heads(_dot(wkbT_ref[...], kvnT), _dot(wvT_ref[...], kvnT), krT,
              kgn_ref, kgr_ref, cos_ref, sin_ref, k_ref, vT_ref, slice(None))


def _proj_body(x_ref, mod_ref, g1_ref, wf_ref, cs_ref, wg_ref, bg_ref,
               wq_ref, qg_ref, wqbT_ref, qgain_ref,
               wkv_ref, kvg_ref, wkbT_ref, wvT_ref, kgn_ref, kgr_ref, cos_ref, sin_ref,
               z_ref, g_ref, qT_ref, k_ref, vT_ref):
    part = x_ref.shape[1] // PROJ_PARTS
    for p in range(PROJ_PARTS):
        tok = slice(p * part, (p + 1) * part)
        hb = _norm_mod(x_ref[0, tok, :], mod_ref[0], g1_ref[...]).astype(BF16)
        qp = _dot_nt(hb, wq_ref[...])
        kvp = _dot_nt(hb, wkv_ref[...])
        f = _dot_nt(hb, wf_ref[...]).astype(BF16)

        qnT = (_rms_lanes(qp, Q_LORA) * qg_ref[...]).T.astype(BF16)
        kvnT, krT = _kv_latents(kvp, kvg_ref)

        def gate_chunk(c):
            cols = slice(c * GATE_CHUNK, (c + 1) * GATE_CHUNK)
            g_ref[0, tok, cols] = jax.nn.sigmoid(
                _dot_nt(hb, wg_ref[cols, :]) + bg_ref[:, cols]).astype(g_ref.dtype)

        gate_chunk(0)

        qT = _dot(wqbT_ref[...], qnT)
        knT = _dot(wkbT_ref[...], kvnT)
        vT = _dot(wvT_ref[...], kvnT)

        gate_chunk(1)
        _q_heads(qT, qgain_ref, cos_ref, sin_ref, qT_ref, tok)
        gate_chunk(2)
        _kv_heads(knT, vT, krT, kgn_ref, kgr_ref, cos_ref, sin_ref, k_ref, vT_ref, tok)
        for c in range(3, g_ref.shape[2] // GATE_CHUNK):
            gate_chunk(c)

        for g in range(N_GROUPS):
            cols = slice(g * GROUP_W, (g + 1) * GROUP_W)
            z = _dot(f[:, cols], cs_ref[...])
            z_ref[0, 0, tok, cols] = z[:, :GROUP_W].astype(z_ref.dtype)
            z_ref[0, 1, tok, cols] = z[:, GROUP_W:].astype(z_ref.dtype)


def _full(shape):
    nd = len(shape)
    return pl.BlockSpec(shape, lambda *_: (0,) * nd, pipeline_mode=pl.Buffered(1))


def _ctx_kv(ctx, mod3, g1, wkv, kvg, wkbT, wvT, kgn, kgr, cosT, sinT):
    b, t, d = ctx.shape
    return pl.pallas_call(
        _ctx_body,
        grid=(b,),
        in_specs=[pl.BlockSpec((1, t, d), lambda i: (i, 0, 0)),
                  pl.BlockSpec((1, 6, d), lambda i: (b, 0, 0)),
                  _full(g1.shape), _full(wkv.shape), _full(kvg.shape), _full(wkbT.shape),
                  _full(wvT.shape), _full(kgn.shape), _full(kgr.shape),
                  _full(cosT.shape), _full(sinT.shape)],
        out_specs=[pl.BlockSpec((1, N_HEADS, t, HEAD_PAD), lambda i: (i, 0, 0, 0)),
                   pl.BlockSpec((1, N_HEADS, V_ROWS, t), lambda i: (i, 0, 0, 0))],
        out_shape=[jax.ShapeDtypeStruct((b, N_HEADS, t, HEAD_PAD), BF16),
                   jax.ShapeDtypeStruct((b, N_HEADS, V_ROWS, t), BF16)],
        compiler_params=_params("arbitrary"),
        name="ctx_kv",
    )(ctx, mod3, g1, wkv, kvg, wkbT, wvT, kgn, kgr, cosT, sinT)


def _proj(x, mod3, g1, wf, cs, wg, bg, wq, qg, wqbT, qgain, wkv, kvg, wkbT, wvT, kgn, kgr,
          cosT, sinT, ts):
    b, s, d = x.shape
    tok = lambda shape: pl.BlockSpec(shape, lambda i, j: (0, j))
    return pl.pallas_call(
        _proj_body,
        grid=(b, s // ts),
        in_specs=[pl.BlockSpec((1, ts, d), lambda i, j: (i, j, 0)),
                  pl.BlockSpec((1, 6, d), lambda i, j: (i, 0, 0)),
                  _full(g1.shape), _full(wf.shape), _full(cs.shape), _full(wg.shape),
                  _full(bg.shape), _full(wq.shape), _full(qg.shape), _full(wqbT.shape),
                  tok((HEAD_PAD, ts)),
                  _full(wkv.shape), _full(kvg.shape), _full(wkbT.shape), _full(wvT.shape),
                  tok((QK_NOPE, ts)), tok((QK_ROPE, ts)), tok((QK_ROPE, ts)), tok((QK_ROPE, ts))],
        out_specs=[pl.BlockSpec((1, 2, ts, FOURIER_W), lambda i, j: (i, 0, j, 0)),
                   pl.BlockSpec((1, ts, 2 * d), lambda i, j: (i, j, 0)),
                   pl.BlockSpec((1, N_HEADS, HEAD_PAD, ts), lambda i, j: (i, 0, 0, j)),
                   pl.BlockSpec((1, N_HEADS, ts, HEAD_PAD), lambda i, j: (i, 0, j, 0)),
                   pl.BlockSpec((1, N_HEADS, V_ROWS, ts), lambda i, j: (i, 0, 0, j))],
        out_shape=[jax.ShapeDtypeStruct((b, 2, s, FOURIER_W), F32),
                   jax.ShapeDtypeStruct((b, s, 2 * d), BF16),
                   jax.ShapeDtypeStruct((b, N_HEADS, HEAD_PAD, s), BF16),
                   jax.ShapeDtypeStruct((b, N_HEADS, s, HEAD_PAD), BF16),
                   jax.ShapeDtypeStruct((b, N_HEADS, V_ROWS, s), BF16)],
        compiler_params=_params("arbitrary", "arbitrary"),
        name="proj",
    )(x, mod3, g1, wf, cs, wg, bg, wq, qg, wqbT, qgain, wkv, kvg, wkbT, wvT, kgn, kgr,
      cosT, sinT)


def _dft_body(g1_ref, g2_ref, gb_ref, z_ref, o_ref, a_scr):
    nstep = RADIX // (DFT_BLK * DFT_STEP)
    j = pl.program_id(1)
    r8 = RADIX // DFT_BLK
    slab = (2, r8, DFT_BLK, FOURIER_W)

    @pl.when(j < nstep)
    def _():
        for t in range(DFT_STEP):
            rows = slice(t * DFT_BLK, (t + 1) * DFT_BLK)
            off = pl.multiple_of((j * DFT_STEP + t) * DFT_BLK, DFT_BLK)
            t1 = []
            for a2 in range(r8):
                z = z_ref[0, :, :, a2, rows, :].reshape(2 * r8 * DFT_BLK, FOURIER_W)
                t1.append(_dot(g1_ref[...], z.astype(BF16)).reshape(slab))
            for k2a in range(r8):
                x = jnp.concatenate([t1[a2][r, k2a] for r in range(2) for a2 in range(r8)], axis=0)
                a = _dot(g2_ref[k2a], x.astype(BF16)).reshape(slab)
                for r in range(2):
                    for k2b in range(r8):
                        a_scr[r, k2a + r8 * k2b, pl.ds(off, DFT_BLK), :] = a[r, k2b]

    @pl.when(j >= nstep)
    def _():
        for t in range(DFT_STEP):
            blk = (j - nstep) * DFT_STEP + t
            off = pl.multiple_of(blk * DFT_BLK, DFT_BLK)
            a = a_scr[:, pl.ds(off, DFT_BLK), :, :]
            a = a.reshape(2 * DFT_BLK * RADIX, FOURIER_W).astype(BF16)
            y = _dot(gb_ref[blk], a)
            o_ref[0, :, t * DFT_BLK:(t + 1) * DFT_BLK, :] = y.reshape(RADIX, DFT_BLK, FOURIER_W)


def _dft(g1, g2, gb, z6):
    b = z6.shape[0]
    nstep = RADIX // (DFT_BLK * DFT_STEP)
    rows = DFT_BLK * DFT_STEP
    r8 = RADIX // DFT_BLK
    return pl.pallas_call(
        _dft_body,
        grid=(b, 2 * nstep),
        in_specs=[_full(g1.shape), _full(g2.shape), _full(gb.shape),
                  pl.BlockSpec((1, 2, r8, r8, rows, FOURIER_W),
                               lambda i, j: (i, 0, 0, 0, jnp.minimum(j, nstep - 1), 0))],
        out_specs=pl.BlockSpec((1, RADIX, rows, FOURIER_W),
                               lambda i, j: (i, 0, jnp.maximum(j - nstep, 0), 0)),
        out_shape=jax.ShapeDtypeStruct((b, RADIX, RADIX, FOURIER_W), F32),
        scratch_shapes=[pltpu.VMEM((2, RADIX, RADIX, FOURIER_W), F32)],
        compiler_params=_params("arbitrary", "arbitrary"),
        name="dft",
    )(g1, g2, gb, z6)


def _attn_body(qT_ref, k_ref, vT_ref, kc_ref, vcT_ref, *rest, tq, tkb, n_cast):
    for src, dst in zip(rest[:n_cast], rest[n_cast + 1:]):
        dst[...] = src[...].astype(dst.dtype)
    o_ref = rest[n_cast]
    nkb = k_ref.shape[2] // tkb
    nblk = nkb + 1

    def colmax(s):
        r = s.shape[0] // 8
        return jnp.max(jnp.max(s.reshape(8, r, s.shape[1]), axis=0), axis=0, keepdims=True)

    def update(carry, s, vT_blk):
        m, acc = carry
        m_new = jnp.maximum(m, colmax(s))
        p = jnp.exp2(s - m_new).astype(BF16)
        acc = jnp.exp2(m - m_new) * acc + _dot(vT_blk, p)
        return m_new, acc

    nq = qT_ref.shape[3] // tq

    def k_blk(i):
        return k_ref[0, 0, i * tkb:(i + 1) * tkb, :] if i < nkb else kc_ref[0, 0]

    def vT_blk(i):
        return vT_ref[0, 0, :, i * tkb:(i + 1) * tkb] if i < nkb else vcT_ref[0, 0]

    def q_cols(qi):
        return qT_ref[0, 0, :, pl.ds(pl.multiple_of(qi * tq, tq), tq)]

    def q_tile(qi, pending):
        qT = q_cols(qi)
        qT_next = q_cols(jnp.minimum(qi + 1, nq - 1))
        pending = list(pending)
        carry = (jnp.full((1, tq), -jnp.inf, F32), jnp.zeros((V_ROWS, tq), F32))
        for i in range(nblk):
            j = i + ATTN_AHEAD
            pending.append(_dot(k_blk(j), qT) if j < nblk else _dot(k_blk(j - nblk), qT_next))
            carry = update(carry, pending.pop(0), vT_blk(i))
        _, acc = carry
        o_ref[0, :, pl.ds(pl.multiple_of(qi * tq, tq), tq)] = (
            acc[:V_DIM] / acc[V_DIM:V_DIM + 1]).astype(o_ref.dtype)
        return tuple(pending)

    lax.fori_loop(0, nq, q_tile, tuple(_dot(k_blk(i), q_cols(0)) for i in range(ATTN_AHEAD)),
                  unroll=ATTN_UNROLL)


def _attention(qT, k, vT, kc, vcT, weights, tq, tkb):
    b, h, _, s = qT.shape
    sc = kc.shape[2]
    steps = b * h
    w_specs = []
    for w, axis in weights:
        blk = tuple(n // steps if a == axis else n for a, n in enumerate(w.shape))
        w_specs.append(pl.BlockSpec(blk, (lambda i, j: (i * h + j, 0)) if axis == 0
                                    else (lambda i, j: (0, i * h + j))))
    outs = pl.pallas_call(
        functools.partial(_attn_body, tq=tq, tkb=tkb, n_cast=len(weights)),
        grid=(b, h),
        in_specs=[pl.BlockSpec((1, 1, HEAD_PAD, s), lambda i, j: (i, j, 0, 0)),
                  pl.BlockSpec((1, 1, s, HEAD_PAD), lambda i, j: (i, j, 0, 0)),
                  pl.BlockSpec((1, 1, V_ROWS, s), lambda i, j: (i, j, 0, 0)),
                  pl.BlockSpec((1, 1, sc, HEAD_PAD), lambda i, j: (i, j, 0, 0)),
                  pl.BlockSpec((1, 1, V_ROWS, sc), lambda i, j: (i, j, 0, 0))] + w_specs,
        out_specs=[pl.BlockSpec((1, V_DIM, s), lambda i, j: (i, j, 0))] + w_specs,
        out_shape=[jax.ShapeDtypeStruct((b, h * V_DIM, s), BF16)]
        + [jax.ShapeDtypeStruct(w.shape, BF16) for w, _ in weights],
        compiler_params=_params("arbitrary", "arbitrary"),
        name="attention",
    )(qT, k, vT, kc, vcT, *[w for w, _ in weights])
    return outs[0], outs[1:]


def _post_body(fm_ref, aT_ref, g_ref, x_ref, mod_ref, wfo_ref, wao_ref, wout_ref,
               g2_ref, wup_ref, wdn_ref, o_ref, *, fc):
    d = x_ref.shape[-1]
    m = mod_ref[0]
    part = x_ref.shape[1] // POST_PARTS
    toks = [slice(p * part, (p + 1) * part) for p in range(POST_PARTS)]
    x1s, hbs = [], []
    for tok in toks:
        yf = _dot(fm_ref[0, tok, :].astype(BF16), wfo_ref[...])
        ya = lax.dot_general(aT_ref[0, :, tok], wao_ref[...], (((0,), (0,)), ((), ())),
                             preferred_element_type=F32)
        g = g_ref[0, tok, :]
        y = g[:, :d].astype(F32) * yf + g[:, d:].astype(F32) * ya
        x1 = x_ref[0, tok, :] + m[2:3] * _dot(y.astype(BF16), wout_ref[...])
        x1s.append(x1)
        hbs.append(_norm_mod(x1, m[3:5], g2_ref[...]).astype(BF16))
    for tok, x1, hb in zip(toks, x1s, hbs):
        acc = jnp.zeros(x1.shape, F32)
        for c in range(wup_ref.shape[1] // fc):
            u = jnp.maximum(_dot(hb, wup_ref[:, c * fc:(c + 1) * fc]), 0.0)
            acc = acc + _dot((u * u).astype(BF16), wdn_ref[c * fc:(c + 1) * fc, :])
        o_ref[0, tok, :] = x1 + m[5:6] * acc


def _post(fm, aT, gates, x, mod3, wfo, wao, wout, g2, wup, wdn, ts, fc):
    b, s, d = x.shape
    return pl.pallas_call(
        functools.partial(_post_body, fc=fc),
        grid=(b, s // ts),
        in_specs=[pl.BlockSpec((1, ts, FOURIER_W), lambda i, j: (i, j, 0)),
                  pl.BlockSpec((1, aT.shape[1], ts), lambda i, j: (i, 0, j)),
                  pl.BlockSpec((1, ts, 2 * d), lambda i, j: (i, j, 0)),
                  pl.BlockSpec((1, ts, d), lambda i, j: (i, j, 0)),
                  pl.BlockSpec((1, 6, d), lambda i, j: (i, 0, 0)),
                  _full(wfo.shape), _full(wao.shape), _full(wout.shape),
                  _full(g2.shape), _full(wup.shape), _full(wdn.shape)],
        out_specs=pl.BlockSpec((1, ts, d), lambda i, j: (i, j, 0)),
        out_shape=jax.ShapeDtypeStruct((b, s, d), F32),
        compiler_params=_params("arbitrary", "arbitrary"),
        name="post",
    )(fm, aT, gates, x, mod3, wfo, wao, wout, g2, wup, wdn)


@functools.lru_cache(maxsize=None)
def _dft_tables():
    j = np.arange(GROUP_W)
    ang = 2.0 * np.pi * np.outer(j, j) / GROUP_W
    cs = np.concatenate([np.cos(ang), -np.sin(ang)], axis=1) / math.sqrt(GROUP_W)

    r8 = RADIX // DFT_BLK
    d = np.arange(r8)
    eye = np.eye(DFT_BLK)

    def real_block(theta):
        c, s = np.cos(theta), np.sin(theta)
        return np.kron(np.block([[c, s], [-s, c]]) / math.sqrt(r8), eye)

    g1 = real_block(2.0 * np.pi * np.outer(d, d) / r8)
    g2 = np.stack([real_block(2.0 * np.pi * (k2a * d[None, :] / RADIX + np.outer(d, d) / r8))
                   for k2a in range(r8)])

    n = RADIX * RADIX
    k1 = np.arange(RADIX)[:, None, None]
    k2 = np.arange(RADIX)[None, :, None]
    bb = np.arange(RADIX)[None, None, :]
    th = 2.0 * np.pi * ((bb * (RADIX * k1 + k2)) % n) / n
    gc = np.cos(th) / math.sqrt(RADIX)
    gs = np.sin(th) / math.sqrt(RADIX)
    nblk = RADIX // DFT_BLK
    gtab = np.zeros((nblk, RADIX, DFT_BLK, 2, DFT_BLK, RADIX), np.float32)
    for jb in range(nblk):
        for q in range(DFT_BLK):
            gtab[jb, :, q, 0, q, :] = gc[:, jb * DFT_BLK + q, :]
            gtab[jb, :, q, 1, q, :] = gs[:, jb * DFT_BLK + q, :]
    gtab = gtab.reshape(nblk, RADIX * DFT_BLK, 2 * DFT_BLK * RADIX)
    return cs.astype(np.float32), g1.astype(np.float32), g2.astype(np.float32), gtab


def _rope_tables_t(n):
    rows = n // GRID_W
    row = jnp.repeat(jnp.arange(rows, dtype=F32), GRID_W)
    col = jnp.tile(jnp.arange(GRID_W, dtype=F32), rows)
    n_freq = QK_ROPE // 4
    freqs = ROPE_THETA ** (-jnp.arange(n_freq, dtype=F32) / n_freq)
    ang_r = row[:, None] * freqs[None, :]
    ang_c = col[:, None] * freqs[None, :]
    ang = jnp.concatenate([ang_r, ang_r, ang_c, ang_c], axis=-1)
    return jnp.cos(ang).T, jnp.sin(ang).T


def kernel(x, c, ctx, c_ctx, ada_w, ada_b, norm1_g, norm2_g, w_in, b_gate, w_fourier, q_norm_g, w_qb,
           kv_norm_g, w_kvb, q_gain, k_gain, w_mla_o, w_out, w_up, w_down):
    bsz, s, d = x.shape
    sc = ctx.shape[1]
    assert ada_w.shape[0] == 1 and s == RADIX * RADIX and d == D_MODEL

    cc = jnp.zeros((8, d), F32).at[:bsz].set(c).at[bsz].set(c_ctx)
    mod, wf, wq, wkv, wg = _adaln(cc, ada_w[0], ada_b[0][None, :], w_in[0].T)
    mod3 = mod.reshape(8, 6, d)

    wqbT = jnp.pad(w_qb[0].reshape(Q_LORA, N_HEADS, HEAD_QK),
                   ((0, 0), (0, 0), (0, HEAD_PAD - HEAD_QK))).reshape(Q_LORA, -1).T.astype(BF16)
    wkvb = w_kvb[0].reshape(KV_LORA, N_HEADS, QK_NOPE + V_DIM)
    wkbT = wkvb[:, :, :QK_NOPE].reshape(KV_LORA, -1).T.astype(BF16)
    wvT = wkvb[:, :, QK_NOPE:].reshape(KV_LORA, -1).T.astype(BF16)

    g1 = norm1_g[0][None, :]
    g2 = norm2_g[0][None, :]
    qg = q_norm_g[0][None, :]
    kvg = kv_norm_g[0][None, :]
    bg = b_gate[0][None, :]
    qscale = (HEAD_QK ** -0.5) * math.log2(math.e)
    qgain = jnp.broadcast_to(jnp.pad(q_gain[0] * qscale, (0, HEAD_PAD - HEAD_QK))[:, None], (HEAD_PAD, s))
    kgn = jnp.broadcast_to(k_gain[0][:QK_NOPE, None], (QK_NOPE, s))
    kgr = jnp.broadcast_to(k_gain[0][QK_NOPE:, None], (QK_ROPE, s))
    cosT, sinT = _rope_tables_t(s)
    cos1 = jnp.ones((QK_ROPE, sc), F32)
    sin0 = jnp.zeros((QK_ROPE, sc), F32)

    cs, ga1, ga2, gtab = (jnp.asarray(t).astype(BF16) for t in _dft_tables())

    kc, vcT = _ctx_kv(ctx, mod3, g1, wkv, kvg, wkbT, wvT, kgn[:, :sc], kgr[:, :sc], cos1, sin0)

    z, gates, qT, k, vT = _proj(x, mod3, g1, wf, cs, wg, bg, wq, qg, wqbT, qgain, wkv, kvg, wkbT, wvT,
                                kgn, kgr, cosT, sinT, PROJ_TILE)

    r8 = RADIX // DFT_BLK
    fm = _dft(ga1, ga2, gtab, z.reshape(bsz, 2, r8, r8, RADIX, FOURIER_W)).reshape(bsz, s, FOURIER_W)

    aT, (wfo, wao, wout, wup, wdn) = _attention(
        qT, k, vT, kc, vcT,
        [(w_fourier[0], 0), (w_mla_o[0], 0), (w_out[0], 0), (w_up[0], 1), (w_down[0], 0)],
        ATTN_TQ, ATTN_TK)

    return _post(fm, aT, gates, x, mod3, wfo, wao, wout, g2, wup, wdn, POST_TILE, FF_CHUNK)
```

```python
import functools
import math

import numpy as np
import jax
import jax.numpy as jnp
from jax import lax
from jax.experimental import pallas as pl
from jax.experimental.pallas import tpu as pltpu

F32 = jnp.float32
BF16 = jnp.bfloat16

D_MODEL = 1024
GRID_W = 64
N_GROUPS = 4
GROUP_W = 128
FOURIER_W = N_GROUPS * GROUP_W
N_HEADS = 8
QK_NOPE = 64
QK_ROPE = 32
HEAD_QK = QK_NOPE + QK_ROPE
HEAD_PAD = 128
V_DIM = 64
V_ROWS = V_DIM + 16
Q_LORA = 256
KV_LORA = 128
W_IN_GATES = FOURIER_W + Q_LORA + KV_LORA + QK_ROPE
ROPE_THETA = 10000.0
EPS = 1e-6
RADIX = 64
DFT_BLK = 8
DFT_STEP = 4
POST_TILE = 512
POST_PARTS = 2
PROJ_TILE = 1024
PROJ_PARTS = 4
GATE_CHUNK = 512
FF_CHUNK = 1024
ATTN_TQ = 512
ATTN_TK = 256
ATTN_AHEAD = 2
ATTN_UNROLL = 4

V7X_VMEM_LIMIT = 56 * 1024 * 1024


def _params(*sem):
    return pltpu.CompilerParams(dimension_semantics=sem, vmem_limit_bytes=V7X_VMEM_LIMIT)


def _dot(a, b):
    return jnp.dot(a, b, preferred_element_type=F32)


def _dot_nt(a, bt):
    return lax.dot_general(a, bt, (((1,), (1,)), ((), ())), preferred_element_type=F32)


def _rms_lanes(x, n):
    return x * lax.rsqrt(jnp.sum(x * x, axis=-1, keepdims=True) * (1.0 / n) + EPS)


def _rot_half_rows(x):
    return jnp.concatenate([-x[8:16], x[0:8], -x[24:32], x[16:24]], axis=0)


W_IN_BLK = 256


def _adaln_body(c_ref, w_ref, b_ref, tf_ref, tq_ref, tkv_ref, tga_ref, tgb_ref,
                o_ref, wf_ref, wq_ref, wkv_ref, wg_ref):
    c = c_ref[...]
    s = c * jax.nn.sigmoid(c)
    o_ref[...] = _dot(s, w_ref[...]) + b_ref[...]

    @pl.when(pl.program_id(0) == 0)
    def _():
        wf_ref[...] = tf_ref[...].astype(BF16)
        wq_ref[...] = tq_ref[...].astype(BF16)
        wkv_ref[...] = tkv_ref[...].astype(BF16)

    off = W_IN_GATES % W_IN_BLK
    both = jnp.concatenate([tga_ref[...], tgb_ref[...]], axis=0)
    wg_ref[...] = both[off:off + W_IN_BLK].astype(BF16)


def _adaln(cc, ada_w, ada_b, w_inT):
    rows, d = cc.shape
    n = ada_w.shape[1]
    n_gates = w_inT.shape[0] - W_IN_GATES
    steps = n_gates // W_IN_BLK
    tn = n // steps
    g0 = W_IN_GATES // W_IN_BLK
    blk = lambda r, f: pl.BlockSpec((r, d), f)
    return pl.pallas_call(
        _adaln_body,
        grid=(steps,),
        in_specs=[pl.BlockSpec((rows, d), lambda j: (0, 0)),
                  pl.BlockSpec((d, tn), lambda j: (0, j)),
                  pl.BlockSpec((1, tn), lambda j: (0, j)),
                  blk(FOURIER_W, lambda j: (0, 0)),
                  blk(Q_LORA, lambda j: (FOURIER_W // Q_LORA, 0)),
                  blk(2 * KV_LORA, lambda j: ((FOURIER_W + Q_LORA) // (2 * KV_LORA), 0)),
                  blk(W_IN_BLK, lambda j: (g0 + j, 0)),
                  blk(W_IN_BLK, lambda j: (g0 + 1 + j, 0))],
        out_specs=[pl.BlockSpec((rows, tn), lambda j: (0, j)),
                   blk(FOURIER_W, lambda j: (0, 0)), blk(Q_LORA, lambda j: (0, 0)),
                   blk(2 * KV_LORA, lambda j: (0, 0)), blk(W_IN_BLK, lambda j: (j, 0))],
        out_shape=[jax.ShapeDtypeStruct((rows, n), F32),
                   jax.ShapeDtypeStruct((FOURIER_W, d), BF16),
                   jax.ShapeDtypeStruct((Q_LORA, d), BF16),
                   jax.ShapeDtypeStruct((2 * KV_LORA, d), BF16),
                   jax.ShapeDtypeStruct((n_gates, d), BF16)],
        compiler_params=_params("arbitrary"),
        name="adaln",
    )(cc, ada_w, ada_b, w_inT, w_inT, w_inT, w_inT, w_inT)


def _norm_mod(x, m, g):
    ms = jnp.sum(x * x, axis=-1, keepdims=True) * (1.0 / x.shape[-1])
    return (x * lax.rsqrt(ms + EPS)) * (g * (1.0 + m[1:2])) + m[0:1]


def _kv_latents(kvp, kvg_ref):
    kvn = _rms_lanes(kvp[:, :KV_LORA], KV_LORA) * kvg_ref[...]
    return kvn.T.astype(BF16), kvp[:, KV_LORA:].T[0:QK_ROPE]


def _kv_heads(knT, vT, krT, kgn_ref, kgr_ref, cos_ref, sin_ref, k_ref, vT_ref, tok):
    t = knT.shape[1]
    sr = jnp.sum(krT * krT, axis=0, keepdims=True)
    krg = krT * kgr_ref[:, tok]
    kr_rot = krg * cos_ref[:, tok] + _rot_half_rows(krg) * sin_ref[:, tok]
    kgn = kgn_ref[:, tok]
    ones = jnp.ones((V_ROWS - V_DIM, t), F32)
    zpad = jnp.zeros((HEAD_PAD - HEAD_QK, t), F32)
    for h in range(N_HEADS):
        kn = knT[h * QK_NOPE:(h + 1) * QK_NOPE]
        ss = jnp.sum(kn * kn, axis=0, keepdims=True) + sr
        r = lax.rsqrt(ss * (1.0 / HEAD_QK) + EPS)
        kT = jnp.concatenate([kn * kgn * r, kr_rot * r, zpad], axis=0)
        k_ref[0, h, tok, :] = kT.T.astype(k_ref.dtype)
        vh = vT[h * V_DIM:(h + 1) * V_DIM]
        vT_ref[0, h, :, tok] = jnp.concatenate([vh, ones], axis=0).astype(vT_ref.dtype)


def _q_heads(qT, qgain_ref, cos_ref, sin_ref, qT_ref, tok):
    qgain, cos, sin = qgain_ref[:, tok], cos_ref[:, tok], sin_ref[:, tok]
    for h in range(N_HEADS):
        qh = qT[h * HEAD_PAD:(h + 1) * HEAD_PAD]
        r = lax.rsqrt(jnp.sum(qh * qh, axis=0, keepdims=True) * (1.0 / HEAD_QK) + EPS)
        qh = qh * qgain * r
        qr = qh[QK_NOPE:HEAD_QK]
        qr = qr * cos + _rot_half_rows(qr) * sin
        qT_ref[0, h, :, tok] = jnp.concatenate([qh[:QK_NOPE], qr, qh[HEAD_QK:]],
                                               axis=0).astype(qT_ref.dtype)


def _ctx_body(x_ref, mod_ref, g1_ref, wkv_ref, kvg_ref, wkbT_ref, wvT_ref, kgn_ref, kgr_ref,
              cos_ref, sin_ref, k_ref, vT_ref):
    hb = _norm_mod(x_ref[0], mod_ref[0], g1_ref[...]).astype(BF16)
    kvnT, krT = _kv_latents(_dot_nt(hb, wkv_ref[...]), kvg_ref)
    _kv_heads(_dot(wkbT_ref[...], kvnT), _dot(wvT_ref[...], kvnT), krT,
              kgn_ref, kgr_ref, cos_ref, sin_ref, k_ref, vT_ref, slice(None))


def _proj_body(x_ref, mod_ref, g1_ref, wf_ref, cs_ref, wg_ref, bg_ref,
               wq_ref, qg_ref, wqbT_ref, qgain_ref,
               wkv_ref, kvg_ref, wkbT_ref, wvT_ref, kgn_ref, kgr_ref, cos_ref, sin_ref,
               z_ref, g_ref, qT_ref, k_ref, vT_ref):
    part = x_ref.shape[1] // PROJ_PARTS
    for p in range(PROJ_PARTS):
        tok = slice(p * part, (p + 1) * part)
        hb = _norm_mod(x_ref[0, tok, :], mod_ref[0], g1_ref[...]).astype(BF16)
        qp = _dot_nt(hb, wq_ref[...])
        kvp = _dot_nt(hb, wkv_ref[...])
        f = _dot_nt(hb, wf_ref[...]).astype(BF16)

        qnT = (_rms_lanes(qp, Q_LORA) * qg_ref[...]).T.astype(BF16)
        kvnT, krT = _kv_latents(kvp, kvg_ref)

        def gate_chunk(c):
            cols = slice(c * GATE_CHUNK, (c + 1) * GATE_CHUNK)
            g_ref[0, tok, cols] = jax.nn.sigmoid(
                _dot_nt(hb, wg_ref[cols, :]) + bg_ref[:, cols]).astype(g_ref.dtype)

        gate_chunk(0)

        qT = _dot(wqbT_ref[...], qnT)
        knT = _dot(wkbT_ref[...], kvnT)
        vT = _dot(wvT_ref[...], kvnT)

        gate_chunk(1)
        _q_heads(qT, qgain_ref, cos_ref, sin_ref, qT_ref, tok)
        gate_chunk(2)
        _kv_heads(knT, vT, krT, kgn_ref, kgr_ref, cos_ref, sin_ref, k_ref, vT_ref, tok)
        for c in range(3, g_ref.shape[2] // GATE_CHUNK):
            gate_chunk(c)

        for g in range(N_GROUPS):
            cols = slice(g * GROUP_W, (g + 1) * GROUP_W)
            z = _dot(f[:, cols], cs_ref[...])
            z_ref[0, 0, tok, cols] = z[:, :GROUP_W].astype(z_ref.dtype)
            z_ref[0, 1, tok, cols] = z[:, GROUP_W:].astype(z_ref.dtype)


def _full(shape):
    nd = len(shape)
    return pl.BlockSpec(shape, lambda *_: (0,) * nd, pipeline_mode=pl.Buffered(1))


def _ctx_kv(ctx, mod3, g1, wkv, kvg, wkbT, wvT, kgn, kgr, cosT, sinT):
    b, t, d = ctx.shape
    return pl.pallas_call(
        _ctx_body,
        grid=(b,),
        in_specs=[pl.BlockSpec((1, t, d), lambda i: (i, 0, 0)),
                  pl.BlockSpec((1, 6, d), lambda i: (b, 0, 0)),
                  _full(g1.shape), _full(wkv.shape), _full(kvg.shape), _full(wkbT.shape),
                  _full(wvT.shape), _full(kgn.shape), _full(kgr.shape),
                  _full(cosT.shape), _full(sinT.shape)],
        out_specs=[pl.BlockSpec((1, N_HEADS, t, HEAD_PAD), lambda i: (i, 0, 0, 0)),
                   pl.BlockSpec((1, N_HEADS, V_ROWS, t), lambda i: (i, 0, 0, 0))],
        out_shape=[jax.ShapeDtypeStruct((b, N_HEADS, t, HEAD_PAD), BF16),
                   jax.ShapeDtypeStruct((b, N_HEADS, V_ROWS, t), BF16)],
        compiler_params=_params("arbitrary"),
        name="ctx_kv",
    )(ctx, mod3, g1, wkv, kvg, wkbT, wvT, kgn, kgr, cosT, sinT)


def _proj(x, mod3, g1, wf, cs, wg, bg, wq, qg, wqbT, qgain, wkv, kvg, wkbT, wvT, kgn, kgr,
          cosT, sinT, ts):
    b, s, d = x.shape
    tok = lambda shape: pl.BlockSpec(shape, lambda i, j: (0, j))
    return pl.pallas_call(
        _proj_body,
        grid=(b, s // ts),
        in_specs=[pl.BlockSpec((1, ts, d), lambda i, j: (i, j, 0)),
                  pl.BlockSpec((1, 6, d), lambda i, j: (i, 0, 0)),
                  _full(g1.shape), _full(wf.shape), _full(cs.shape), _full(wg.shape),
                  _full(bg.shape), _full(wq.shape), _full(qg.shape), _full(wqbT.shape),
                  tok((HEAD_PAD, ts)),
                  _full(wkv.shape), _full(kvg.shape), _full(wkbT.shape), _full(wvT.shape),
                  tok((QK_NOPE, ts)), tok((QK_ROPE, ts)), tok((QK_ROPE, ts)), tok((QK_ROPE, ts))],
        out_specs=[pl.BlockSpec((1, 2, ts, FOURIER_W), lambda i, j: (i, 0, j, 0)),
                   pl.BlockSpec((1, ts, 2 * d), lambda i, j: (i, j, 0)),
                   pl.BlockSpec((1, N_HEADS, HEAD_PAD, ts), lambda i, j: (i, 0, 0, j)),
                   pl.BlockSpec((1, N_HEADS, ts, HEAD_PAD), lambda i, j: (i, 0, j, 0)),
                   pl.BlockSpec((1, N_HEADS, V_ROWS, ts), lambda i, j: (i, 0, 0, j))],
        out_shape=[jax.ShapeDtypeStruct((b, 2, s, FOURIER_W), F32),
                   jax.ShapeDtypeStruct((b, s, 2 * d), BF16),
                   jax.ShapeDtypeStruct((b, N_HEADS, HEAD_PAD, s), BF16),
                   jax.ShapeDtypeStruct((b, N_HEADS, s, HEAD_PAD), BF16),
                   jax.ShapeDtypeStruct((b, N_HEADS, V_ROWS, s), BF16)],
        compiler_params=_params("arbitrary", "arbitrary"),
        name="proj",
    )(x, mod3, g1, wf, cs, wg, bg, wq, qg, wqbT, qgain, wkv, kvg, wkbT, wvT, kgn, kgr,
      cosT, sinT)


def _dft_body(g1_ref, g2_ref, gb_ref, z_ref, o_ref, a_scr):
    nstep = RADIX // (DFT_BLK * DFT_STEP)
    j = pl.program_id(1)
    r8 = RADIX // DFT_BLK
    slab = (2, r8, DFT_BLK, FOURIER_W)

    @pl.when(j < nstep)
    def _():
        for t in range(DFT_STEP):
            rows = slice(t * DFT_BLK, (t + 1) * DFT_BLK)
            off = pl.multiple_of((j * DFT_STEP + t) * DFT_BLK, DFT_BLK)
            t1 = []
            for a2 in range(r8):
                z = z_ref[0, :, :, a2, rows, :].reshape(2 * r8 * DFT_BLK, FOURIER_W)
                t1.append(_dot(g1_ref[...], z.astype(BF16)).reshape(slab))
            for k2a in range(r8):
                x = jnp.concatenate([t1[a2][r, k2a] for r in range(2) for a2 in range(r8)], axis=0)
                a = _dot(g2_ref[k2a], x.astype(BF16)).reshape(slab)
                for r in range(2):
                    for k2b in range(r8):
                        a_scr[r, k2a + r8 * k2b, pl.ds(off, DFT_BLK), :] = a[r, k2b]

    @pl.when(j >= nstep)
    def _():
        for t in range(DFT_STEP):
            blk = (j - nstep) * DFT_STEP + t
            off = pl.multiple_of(blk * DFT_BLK, DFT_BLK)
            a = a_scr[:, pl.ds(off, DFT_BLK), :, :]
            a = a.reshape(2 * DFT_BLK * RADIX, FOURIER_W).astype(BF16)
            y = _dot(gb_ref[blk], a)
            o_ref[0, :, t * DFT_BLK:(t + 1) * DFT_BLK, :] = y.reshape(RADIX, DFT_BLK, FOURIER_W)


def _dft(g1, g2, gb, z6):
    b = z6.shape[0]
    nstep = RADIX // (DFT_BLK * DFT_STEP)
    rows = DFT_BLK * DFT_STEP
    r8 = RADIX // DFT_BLK
    return pl.pallas_call(
        _dft_body,
        grid=(b, 2 * nstep),
        in_specs=[_full(g1.shape), _full(g2.shape), _full(gb.shape),
                  pl.BlockSpec((1, 2, r8, r8, rows, FOURIER_W),
                               lambda i, j: (i, 0, 0, 0, jnp.minimum(j, nstep - 1), 0))],
        out_specs=pl.BlockSpec((1, RADIX, rows, FOURIER_W),
                               lambda i, j: (i, 0, jnp.maximum(j - nstep, 0), 0)),
        out_shape=jax.ShapeDtypeStruct((b, RADIX, RADIX, FOURIER_W), F32),
        scratch_shapes=[pltpu.VMEM((2, RADIX, RADIX, FOURIER_W), F32)],
        compiler_params=_params("arbitrary", "arbitrary"),
        name="dft",
    )(g1, g2, gb, z6)


def _attn_body(qT_ref, k_ref, vT_ref, kc_ref, vcT_ref, *rest, tq, tkb, n_cast):
    for src, dst in zip(rest[:n_cast], rest[n_cast + 1:]):
        dst[...] = src[...].astype(dst.dtype)
    o_ref = rest[n_cast]
    nkb = k_ref.shape[2] // tkb
    nblk = nkb + 1

    def colmax(s):
        r = s.shape[0] // 8
        return jnp.max(jnp.max(s.reshape(8, r, s.shape[1]), axis=0), axis=0, keepdims=True)

    def update(carry, s, vT_blk):
        m, acc = carry
        m_new = jnp.maximum(m, colmax(s))
        p = jnp.exp2(s - m_new).astype(BF16)
        acc = jnp.exp2(m - m_new) * acc + _dot(vT_blk, p)
        return m_new, acc

    nq = qT_ref.shape[3] // tq

    def k_blk(i):
        return k_ref[0, 0, i * tkb:(i + 1) * tkb, :] if i < nkb else kc_ref[0, 0]

    def vT_blk(i):
        return vT_ref[0, 0, :, i * tkb:(i + 1) * tkb] if i < nkb else vcT_ref[0, 0]

    def q_cols(qi):
        return qT_ref[0, 0, :, pl.ds(pl.multiple_of(qi * tq, tq), tq)]

    def q_tile(qi, pending):
        qT = q_cols(qi)
        qT_next = q_cols(jnp.minimum(qi + 1, nq - 1))
        pending = list(pending)
        carry = (jnp.full((1, tq), -jnp.inf, F32), jnp.zeros((V_ROWS, tq), F32))
        for i in range(nblk):
            j = i + ATTN_AHEAD
            pending.append(_dot(k_blk(j), qT) if j < nblk else _dot(k_blk(j - nblk), qT_next))
            carry = update(carry, pending.pop(0), vT_blk(i))
        _, acc = carry
        o_ref[0, :, pl.ds(pl.multiple_of(qi * tq, tq), tq)] = (
            acc[:V_DIM] / acc[V_DIM:V_DIM + 1]).astype(o_ref.dtype)
        return tuple(pending)

    lax.fori_loop(0, nq, q_tile, tuple(_dot(k_blk(i), q_cols(0)) for i in range(ATTN_AHEAD)),
                  unroll=ATTN_UNROLL)


def _attention(qT, k, vT, kc, vcT, weights, tq, tkb):
    b, h, _, s = qT.shape
    sc = kc.shape[2]
    steps = b * h
    w_specs = []
    for w, axis in weights:
        blk = tuple(n // steps if a == axis else n for a, n in enumerate(w.shape))
        w_specs.append(pl.BlockSpec(blk, (lambda i, j: (i * h + j, 0)) if axis == 0
                                    else (lambda i, j: (0, i * h + j))))
    outs = pl.pallas_call(
        functools.partial(_attn_body, tq=tq, tkb=tkb, n_cast=len(weights)),
        grid=(b, h),
        in_specs=[pl.BlockSpec((1, 1, HEAD_PAD, s), lambda i, j: (i, j, 0, 0)),
                  pl.BlockSpec((1, 1, s, HEAD_PAD), lambda i, j: (i, j, 0, 0)),
                  pl.BlockSpec((1, 1, V_ROWS, s), lambda i, j: (i, j, 0, 0)),
                  pl.BlockSpec((1, 1, sc, HEAD_PAD), lambda i, j: (i, j, 0, 0)),
                  pl.BlockSpec((1, 1, V_ROWS, sc), lambda i, j: (i, j, 0, 0))] + w_specs,
        out_specs=[pl.BlockSpec((1, V_DIM, s), lambda i, j: (i, j, 0))] + w_specs,
        out_shape=[jax.ShapeDtypeStruct((b, h * V_DIM, s), BF16)]
        + [jax.ShapeDtypeStruct(w.shape, BF16) for w, _ in weights],
        compiler_params=_params("arbitrary", "arbitrary"),
        name="attention",
    )(qT, k, vT, kc, vcT, *[w for w, _ in weights])
    return outs[0], outs[1:]


def _post_body(fm_ref, aT_ref, g_ref, x_ref, mod_ref, wfo_ref, wao_ref, wout_ref,
               g2_ref, wup_ref, wdn_ref, o_ref, *, fc):
    d = x_ref.shape[-1]
    m = mod_ref[0]
    part = x_ref.shape[1] // POST_PARTS
    toks = [slice(p * part, (p + 1) * part) for p in range(POST_PARTS)]
    x1s, hbs = [], []
    for tok in toks:
        yf = _dot(fm_ref[0, tok, :].astype(BF16), wfo_ref[...])
        ya = lax.dot_general(aT_ref[0, :, tok], wao_ref[...], (((0,), (0,)), ((), ())),
                             preferred_element_type=F32)
        g = g_ref[0, tok, :]
        y = g[:, :d].astype(F32) * yf + g[:, d:].astype(F32) * ya
        x1 = x_ref[0, tok, :] + m[2:3] * _dot(y.astype(BF16), wout_ref[...])
        x1s.append(x1)
        hbs.append(_norm_mod(x1, m[3:5], g2_ref[...]).astype(BF16))
    for tok, x1, hb in zip(toks, x1s, hbs):
        acc = jnp.zeros(x1.shape, F32)
        for c in range(wup_ref.shape[1] // fc):
            u = jnp.maximum(_dot(hb, wup_ref[:, c * fc:(c + 1) * fc]), 0.0)
            acc = acc + _dot((u * u).astype(BF16), wdn_ref[c * fc:(c + 1) * fc, :])
        o_ref[0, tok, :] = x1 + m[5:6] * acc


def _post(fm, aT, gates, x, mod3, wfo, wao, wout, g2, wup, wdn, ts, fc):
    b, s, d = x.shape
    return pl.pallas_call(
        functools.partial(_post_body, fc=fc),
        grid=(b, s // ts),
        in_specs=[pl.BlockSpec((1, ts, FOURIER_W), lambda i, j: (i, j, 0)),
                  pl.BlockSpec((1, aT.shape[1], ts), lambda i, j: (i, 0, j)),
                  pl.BlockSpec((1, ts, 2 * d), lambda i, j: (i, j, 0)),
                  pl.BlockSpec((1, ts, d), lambda i, j: (i, j, 0)),
                  pl.BlockSpec((1, 6, d), lambda i, j: (i, 0, 0)),
                  _full(wfo.shape), _full(wao.shape), _full(wout.shape),
                  _full(g2.shape), _full(wup.shape), _full(wdn.shape)],
        out_specs=pl.BlockSpec((1, ts, d), lambda i, j: (i, j, 0)),
        out_shape=jax.ShapeDtypeStruct((b, s, d), F32),
        compiler_params=_params("arbitrary", "arbitrary"),
        name="post",
    )(fm, aT, gates, x, mod3, wfo, wao, wout, g2, wup, wdn)


@functools.lru_cache(maxsize=None)
def _dft_tables():
    j = np.arange(GROUP_W)
    ang = 2.0 * np.pi * np.outer(j, j) / GROUP_W
    cs = np.concatenate([np.cos(ang), -np.sin(ang)], axis=1) / math.sqrt(GROUP_W)

    r8 = RADIX // DFT_BLK
    d = np.arange(r8)
    eye = np.eye(DFT_BLK)

    def real_block(theta):
        c, s = np.cos(theta), np.sin(theta)
        return np.kron(np.block([[c, s], [-s, c]]) / math.sqrt(r8), eye)

    g1 = real_block(2.0 * np.pi * np.outer(d, d) / r8)
    g2 = np.stack([real_block(2.0 * np.pi * (k2a * d[None, :] / RADIX + np.outer(d, d) / r8))
                   for k2a in range(r8)])

    n = RADIX * RADIX
    k1 = np.arange(RADIX)[:, None, None]
    k2 = np.arange(RADIX)[None, :, None]
    bb = np.arange(RADIX)[None, None, :]
    th = 2.0 * np.pi * ((bb * (RADIX * k1 + k2)) % n) / n
    gc = np.cos(th) / math.sqrt(RADIX)
    gs = np.sin(th) / math.sqrt(RADIX)
    nblk = RADIX // DFT_BLK
    gtab = np.zeros((nblk, RADIX, DFT_BLK, 2, DFT_BLK, RADIX), np.float32)
    for jb in range(nblk):
        for q in range(DFT_BLK):
            gtab[jb, :, q, 0, q, :] = gc[:, jb * DFT_BLK + q, :]
            gtab[jb, :, q, 1, q, :] = gs[:, jb * DFT_BLK + q, :]
    gtab = gtab.reshape(nblk, RADIX * DFT_BLK, 2 * DFT_BLK * RADIX)
    return cs.astype(np.float32), g1.astype(np.float32), g2.astype(np.float32), gtab


def _rope_tables_t(n):
    rows = n // GRID_W
    row = jnp.repeat(jnp.arange(rows, dtype=F32), GRID_W)
    col = jnp.tile(jnp.arange(GRID_W, dtype=F32), rows)
    n_freq = QK_ROPE // 4
    freqs = ROPE_THETA ** (-jnp.arange(n_freq, dtype=F32) / n_freq)
    ang_r = row[:, None] * freqs[None, :]
    ang_c = col[:, None] * freqs[None, :]
    ang = jnp.concatenate([ang_r, ang_r, ang_c, ang_c], axis=-1)
    return jnp.cos(ang).T, jnp.sin(ang).T


def kernel(x, c, ctx, c_ctx, ada_w, ada_b, norm1_g, norm2_g, w_in, b_gate, w_fourier, q_norm_g, w_qb,
           kv_norm_g, w_kvb, q_gain, k_gain, w_mla_o, w_out, w_up, w_down):
    bsz, s, d = x.shape
    sc = ctx.shape[1]
    assert ada_w.shape[0] == 1 and s == RADIX * RADIX and d == D_MODEL

    cc = jnp.zeros((8, d), F32).at[:bsz].set(c).at[bsz].set(c_ctx)
    mod, wf, wq, wkv, wg = _adaln(cc, ada_w[0], ada_b[0][None, :], w_in[0].T)
    mod3 = mod.reshape(8, 6, d)

    wqbT = jnp.pad(w_qb[0].reshape(Q_LORA, N_HEADS, HEAD_QK),
                   ((0, 0), (0, 0), (0, HEAD_PAD - HEAD_QK))).reshape(Q_LORA, -1).T.astype(BF16)
    wkvb = w_kvb[0].reshape(KV_LORA, N_HEADS, QK_NOPE + V_DIM)
    wkbT = wkvb[:, :, :QK_NOPE].reshape(KV_LORA, -1).T.astype(BF16)
    wvT = wkvb[:, :, QK_NOPE:].reshape(KV_LORA, -1).T.astype(BF16)

    g1 = norm1_g[0][None, :]
    g2 = norm2_g[0][None, :]
    qg = q_norm_g[0][None, :]
    kvg = kv_norm_g[0][None, :]
    bg = b_gate[0][None, :]
    qscale = (HEAD_QK ** -0.5) * math.log2(math.e)
    qgain = jnp.broadcast_to(jnp.pad(q_gain[0] * qscale, (0, HEAD_PAD - HEAD_QK))[:, None], (HEAD_PAD, s))
    kgn = jnp.broadcast_to(k_gain[0][:QK_NOPE, None], (QK_NOPE, s))
    kgr = jnp.broadcast_to(k_gain[0][QK_NOPE:, None], (QK_ROPE, s))
    cosT, sinT = _rope_tables_t(s)
    cos1 = jnp.ones((QK_ROPE, sc), F32)
    sin0 = jnp.zeros((QK_ROPE, sc), F32)

    cs, ga1, ga2, gtab = (jnp.asarray(t).astype(BF16) for t in _dft_tables())

    kc, vcT = _ctx_kv(ctx, mod3, g1, wkv, kvg, wkbT, wvT, kgn[:, :sc], kgr[:, :sc], cos1, sin0)

    z, gates, qT, k, vT = _proj(x, mod3, g1, wf, cs, wg, bg, wq, qg, wqbT, qgain, wkv, kvg, wkbT, wvT,
                                kgn, kgr, cosT, sinT, PROJ_TILE)

    r8 = RADIX // DFT_BLK
    fm = _dft(ga1, ga2, gtab, z.reshape(bsz, 2, r8, r8, RADIX, FOURIER_W)).reshape(bsz, s, FOURIER_W)

    aT, (wfo, wao, wout, wup, wdn) = _attention(
        qT, k, vT, kc, vcT,
        [(w_fourier[0], 0), (w_mla_o[0], 0), (w_out[0], 0), (w_up[0], 1), (w_down[0], 0)],
        ATTN_TQ, ATTN_TK)

    return _post(fm, aT, gates, x, mod3, wfo, wao, wout, g2, wup, wdn, POST_TILE, FF_CHUNK)
```

```python
import functools
import math

import numpy as np
import jax
import jax.numpy as jnp
from jax import lax
from jax.experimental import pallas as pl
from jax.experimental.pallas import tpu as pltpu

F32 = jnp.float32
BF16 = jnp.bfloat16

D_MODEL = 1024
GRID_W = 64
N_GROUPS = 4
GROUP_W = 128
FOURIER_W = N_GROUPS * GROUP_W
N_HEADS = 8
QK_NOPE = 64
QK_ROPE = 32
HEAD_QK = QK_NOPE + QK_ROPE
HEAD_PAD = 128
V_DIM = 64
V_ROWS = V_DIM + 16
Q_LORA = 256
KV_LORA = 128
W_IN_GATES = FOURIER_W + Q_LORA + KV_LORA + QK_ROPE
ROPE_THETA = 10000.0
EPS = 1e-6
RADIX = 64
DFT_RADIX_A = 8
DFT_ABLK = 16
DFT_BLK = 8
DFT_STEP = 2
POST_TILE = 512
POST_PARTS = 2
PROJ_TILE = 1024
PROJ_PARTS = 4
GATE_CHUNK = 512
FF_CHUNK = 1024
ATTN_TQ = 512
ATTN_TK = 256
ATTN_AHEAD = 2
ATTN_UNROLL = 4

V7X_VMEM_LIMIT = 56 * 1024 * 1024


def _params(*sem):
    return pltpu.CompilerParams(dimension_semantics=sem, vmem_limit_bytes=V7X_VMEM_LIMIT)


def _dot(a, b):
    return jnp.dot(a, b, preferred_element_type=F32)


def _dot_nt(a, bt):
    return lax.dot_general(a, bt, (((1,), (1,)), ((), ())), preferred_element_type=F32)


def _rms_lanes(x, n):
    return x * lax.rsqrt(jnp.sum(x * x, axis=-1, keepdims=True) * (1.0 / n) + EPS)


def _rot_half_rows(x):
    return jnp.concatenate([-x[8:16], x[0:8], -x[24:32], x[16:24]], axis=0)


W_IN_BLK = 256


def _adaln_body(c_ref, w_ref, b_ref, tf_ref, tq_ref, tkv_ref, tga_ref, tgb_ref,
                o_ref, wf_ref, wq_ref, wkv_ref, wg_ref):
    c = c_ref[...]
    s = c * jax.nn.sigmoid(c)
    o_ref[...] = _dot(s, w_ref[...]) + b_ref[...]

    @pl.when(pl.program_id(0) == 0)
    def _():
        wf_ref[...] = tf_ref[...].astype(BF16)
        wq_ref[...] = tq_ref[...].astype(BF16)
        wkv_ref[...] = tkv_ref[...].astype(BF16)

    off = W_IN_GATES % W_IN_BLK
    both = jnp.concatenate([tga_ref[...], tgb_ref[...]], axis=0)
    wg_ref[...] = both[off:off + W_IN_BLK].astype(BF16)


def _adaln(cc, ada_w, ada_b, w_inT):
    rows, d = cc.shape
    n = ada_w.shape[1]
    n_gates = w_inT.shape[0] - W_IN_GATES
    steps = n_gates // W_IN_BLK
    tn = n // steps
    g0 = W_IN_GATES // W_IN_BLK
    blk = lambda r, f: pl.BlockSpec((r, d), f)
    return pl.pallas_call(
        _adaln_body,
        grid=(steps,),
        in_specs=[pl.BlockSpec((rows, d), lambda j: (0, 0)),
                  pl.BlockSpec((d, tn), lambda j: (0, j)),
                  pl.BlockSpec((1, tn), lambda j: (0, j)),
                  blk(FOURIER_W, lambda j: (0, 0)),
                  blk(Q_LORA, lambda j: (FOURIER_W // Q_LORA, 0)),
                  blk(2 * KV_LORA, lambda j: ((FOURIER_W + Q_LORA) // (2 * KV_LORA), 0)),
                  blk(W_IN_BLK, lambda j: (g0 + j, 0)),
                  blk(W_IN_BLK, lambda j: (g0 + 1 + j, 0))],
        out_specs=[pl.BlockSpec((rows, tn), lambda j: (0, j)),
                   blk(FOURIER_W, lambda j: (0, 0)), blk(Q_LORA, lambda j: (0, 0)),
                   blk(2 * KV_LORA, lambda j: (0, 0)), blk(W_IN_BLK, lambda j: (j, 0))],
        out_shape=[jax.ShapeDtypeStruct((rows, n), F32),
                   jax.ShapeDtypeStruct((FOURIER_W, d), BF16),
                   jax.ShapeDtypeStruct((Q_LORA, d), BF16),
                   jax.ShapeDtypeStruct((2 * KV_LORA, d), BF16),
                   jax.ShapeDtypeStruct((n_gates, d), BF16)],
        compiler_params=_params("arbitrary"),
        name="adaln",
    )(cc, ada_w, ada_b, w_inT, w_inT, w_inT, w_inT, w_inT)


def _norm_mod(x, m, g):
    ms = jnp.sum(x * x, axis=-1, keepdims=True) * (1.0 / x.shape[-1])
    return (x * lax.rsqrt(ms + EPS)) * (g * (1.0 + m[1:2])) + m[0:1]


def _kv_latents(kvp, kvg_ref):
    kvn = _rms_lanes(kvp[:, :KV_LORA], KV_LORA) * kvg_ref[...]
    return kvn.T.astype(BF16), kvp[:, KV_LORA:].T[0:QK_ROPE]


def _kv_heads(knT, vT, krT, kgn_ref, kgr_ref, cos_ref, sin_ref, k_ref, vT_ref, tok):
    t = knT.shape[1]
    sr = jnp.sum(krT * krT, axis=0, keepdims=True)
    krg = krT * kgr_ref[:, tok]
    kr_rot = krg * cos_ref[:, tok] + _rot_half_rows(krg) * sin_ref[:, tok]
    kgn = kgn_ref[:, tok]
    ones = jnp.ones((V_ROWS - V_DIM, t), F32)
    zpad = jnp.zeros((HEAD_PAD - HEAD_QK, t), F32)
    for h in range(N_HEADS):
        kn = knT[h * QK_NOPE:(h + 1) * QK_NOPE]
        ss = jnp.sum(kn * kn, axis=0, keepdims=True) + sr
        r = lax.rsqrt(ss * (1.0 / HEAD_QK) + EPS)
        kT = jnp.concatenate([kn * kgn * r, kr_rot * r, zpad], axis=0)
        k_ref[0, h, tok, :] = kT.T.astype(k_ref.dtype)
        vh = vT[h * V_DIM:(h + 1) * V_DIM]
        vT_ref[0, h, :, tok] = jnp.concatenate([vh, ones], axis=0).astype(vT_ref.dtype)


def _q_heads(qT, qgain_ref, cos_ref, sin_ref, qT_ref, tok):
    qgain, cos, sin = qgain_ref[:, tok], cos_ref[:, tok], sin_ref[:, tok]
    for h in range(N_HEADS):
        qh = qT[h * HEAD_PAD:(h + 1) * HEAD_PAD]
        r = lax.rsqrt(jnp.sum(qh * qh, axis=0, keepdims=True) * (1.0 / HEAD_QK) + EPS)
        qh = qh * qgain * r
        qr = qh[QK_NOPE:HEAD_QK]
        qr = qr * cos + _rot_half_rows(qr) * sin
        qT_ref[0, h, :, tok] = jnp.concatenate([qh[:QK_NOPE], qr, qh[HEAD_QK:]],
                                               axis=0).astype(qT_ref.dtype)


def _ctx_body(x_ref, mod_ref, g1_ref, wkv_ref, kvg_ref, wkbT_ref, wvT_ref, kgn_ref, kgr_ref,
              cos_ref, sin_ref, k_ref, vT_ref):
    hb = _norm_mod(x_ref[0], mod_ref[0], g1_ref[...]).astype(BF16)
    kvnT, krT = _kv_latents(_dot_nt(hb, wkv_ref[...]), kvg_ref)
    _kv_heads(_dot(wkbT_ref[...], kvnT), _dot(wvT_ref[...], kvnT), krT,
              kgn_ref, kgr_ref, cos_ref, sin_ref, k_ref, vT_ref, slice(None))


def _proj_body(x_ref, mod_ref, g1_ref, wf_ref, cs_ref, wg_ref, bg_ref,
               wq_ref, qg_ref, wqbT_ref, qgain_ref,
               wkv_ref, kvg_ref, wkbT_ref, wvT_ref, kgn_ref, kgr_ref, cos_ref, sin_ref,
               z_ref, g_ref, qT_ref, k_ref, vT_ref):
    part = x_ref.shape[1] // PROJ_PARTS
    for p in range(PROJ_PARTS):
        tok = slice(p * part, (p + 1) * part)
        hb = _norm_mod(x_ref[0, tok, :], mod_ref[0], g1_ref[...]).astype(BF16)
        qp = _dot_nt(hb, wq_ref[...])
        kvp = _dot_nt(hb, wkv_ref[...])
        f = _dot_nt(hb, wf_ref[...]).astype(BF16)

        qnT = (_rms_lanes(qp, Q_LORA) * qg_ref[...]).T.astype(BF16)
        kvnT, krT = _kv_latents(kvp, kvg_ref)

        def gate_chunk(c):
            cols = slice(c * GATE_CHUNK, (c + 1) * GATE_CHUNK)
            g_ref[0, tok, cols] = jax.nn.sigmoid(
                _dot_nt(hb, wg_ref[cols, :]) + bg_ref[:, cols]).astype(g_ref.dtype)

        gate_chunk(0)

        qT = _dot(wqbT_ref[...], qnT)
        knT = _dot(wkbT_ref[...], kvnT)
        vT = _dot(wvT_ref[...], kvnT)

        gate_chunk(1)
        _q_heads(qT, qgain_ref, cos_ref, sin_ref, qT_ref, tok)
        gate_chunk(2)
        _kv_heads(knT, vT, krT, kgn_ref, kgr_ref, cos_ref, sin_ref, k_ref, vT_ref, tok)
        for c in range(3, g_ref.shape[2] // GATE_CHUNK):
            gate_chunk(c)

        for g in range(N_GROUPS):
            cols = slice(g * GROUP_W, (g + 1) * GROUP_W)
            z = _dot(f[:, cols], cs_ref[...])
            z_ref[0, 0, tok, cols] = z[:, :GROUP_W].astype(z_ref.dtype)
            z_ref[0, 1, tok, cols] = z[:, GROUP_W:].astype(z_ref.dtype)


def _full(shape):
    nd = len(shape)
    return pl.BlockSpec(shape, lambda *_: (0,) * nd, pipeline_mode=pl.Buffered(1))


def _ctx_kv(ctx, mod3, g1, wkv, kvg, wkbT, wvT, kgn, kgr, cosT, sinT):
    b, t, d = ctx.shape
    return pl.pallas_call(
        _ctx_body,
        grid=(b,),
        in_specs=[pl.BlockSpec((1, t, d), lambda i: (i, 0, 0)),
                  pl.BlockSpec((1, 6, d), lambda i: (b, 0, 0)),
                  _full(g1.shape), _full(wkv.shape), _full(kvg.shape), _full(wkbT.shape),
                  _full(wvT.shape), _full(kgn.shape), _full(kgr.shape),
                  _full(cosT.shape), _full(sinT.shape)],
        out_specs=[pl.BlockSpec((1, N_HEADS, t, HEAD_PAD), lambda i: (i, 0, 0, 0)),
                   pl.BlockSpec((1, N_HEADS, V_ROWS, t), lambda i: (i, 0, 0, 0))],
        out_shape=[jax.ShapeDtypeStruct((b, N_HEADS, t, HEAD_PAD), BF16),
                   jax.ShapeDtypeStruct((b, N_HEADS, V_ROWS, t), BF16)],
        compiler_params=_params("arbitrary"),
        name="ctx_kv",
    )(ctx, mod3, g1, wkv, kvg, wkbT, wvT, kgn, kgr, cosT, sinT)


def _proj(x, mod3, g1, wf, cs, wg, bg, wq, qg, wqbT, qgain, wkv, kvg, wkbT, wvT, kgn, kgr,
          cosT, sinT, ts):
    b, s, d = x.shape
    tok = lambda shape: pl.BlockSpec(shape, lambda i, j: (0, j))
    return pl.pallas_call(
        _proj_body,
        grid=(b, s // ts),
        in_specs=[pl.BlockSpec((1, ts, d), lambda i, j: (i, j, 0)),
                  pl.BlockSpec((1, 6, d), lambda i, j: (i, 0, 0)),
                  _full(g1.shape), _full(wf.shape), _full(cs.shape), _full(wg.shape),
                  _full(bg.shape), _full(wq.shape), _full(qg.shape), _full(wqbT.shape),
                  tok((HEAD_PAD, ts)),
                  _full(wkv.shape), _full(kvg.shape), _full(wkbT.shape), _full(wvT.shape),
                  tok((QK_NOPE, ts)), tok((QK_ROPE, ts)), tok((QK_ROPE, ts)), tok((QK_ROPE, ts))],
        out_specs=[pl.BlockSpec((1, 2, ts, FOURIER_W), lambda i, j: (i, 0, j, 0)),
                   pl.BlockSpec((1, ts, 2 * d), lambda i, j: (i, j, 0)),
                   pl.BlockSpec((1, N_HEADS, HEAD_PAD, ts), lambda i, j: (i, 0, 0, j)),
                   pl.BlockSpec((1, N_HEADS, ts, HEAD_PAD), lambda i, j: (i, 0, j, 0)),
                   pl.BlockSpec((1, N_HEADS, V_ROWS, ts), lambda i, j: (i, 0, 0, j))],
        out_shape=[jax.ShapeDtypeStruct((b, 2, s, FOURIER_W), BF16),
                   jax.ShapeDtypeStruct((b, s, 2 * d), BF16),
                   jax.ShapeDtypeStruct((b, N_HEADS, HEAD_PAD, s), BF16),
                   jax.ShapeDtypeStruct((b, N_HEADS, s, HEAD_PAD), BF16),
                   jax.ShapeDtypeStruct((b, N_HEADS, V_ROWS, s), BF16)],
        compiler_params=_params("arbitrary", "arbitrary"),
        name="proj",
    )(x, mod3, g1, wf, cs, wg, bg, wq, qg, wqbT, qgain, wkv, kvg, wkbT, wvT, kgn, kgr,
      cosT, sinT)


def _dft_body(g1_ref, g2_ref, gb_ref, z_ref, o_ref, a_scr):
    step_rows = DFT_BLK * DFT_STEP
    nstep = RADIX // step_rows
    j = pl.program_id(1)
    r8 = DFT_RADIX_A
    slab = (2, r8, DFT_ABLK, FOURIER_W)

    @pl.when(j < nstep)
    def _():
        for t in range(step_rows // DFT_ABLK):
            rows = slice(t * DFT_ABLK, (t + 1) * DFT_ABLK)
            off = pl.multiple_of(j * step_rows + t * DFT_ABLK, DFT_ABLK)
            t1 = []
            for a2 in range(r8):
                z = z_ref[0, :, :, a2, rows, :].reshape(2 * r8 * DFT_ABLK, FOURIER_W)
                t1.append(_dot(g1_ref[...], z).reshape(slab))
            for k2a in range(r8):
                x = jnp.concatenate([t1[a2][r, k2a] for r in range(2) for a2 in range(r8)], axis=0)
                a = _dot(g2_ref[k2a], x.astype(BF16)).reshape(slab)
                for r in range(2):
                    for k2b in range(r8):
                        a_scr[r, k2a + r8 * k2b, pl.ds(off, DFT_ABLK), :] = a[r, k2b]

    @pl.when(j >= nstep)
    def _():
        for t in range(DFT_STEP):
            blk = (j - nstep) * DFT_STEP + t
            off = pl.multiple_of(blk * DFT_BLK, DFT_BLK)
            a = a_scr[:, pl.ds(off, DFT_BLK), :, :]
            a = a.reshape(2 * DFT_BLK * RADIX, FOURIER_W).astype(BF16)
            y = _dot(gb_ref[blk], a)
            o_ref[0, :, t * DFT_BLK:(t + 1) * DFT_BLK, :] = y.reshape(RADIX, DFT_BLK, FOURIER_W)


def _dft(g1, g2, gb, z6):
    b = z6.shape[0]
    nstep = RADIX // (DFT_BLK * DFT_STEP)
    rows = DFT_BLK * DFT_STEP
    r8 = DFT_RADIX_A
    assert rows % DFT_ABLK == 0 and z6.shape[2:4] == (r8, r8)
    return pl.pallas_call(
        _dft_body,
        grid=(b, 2 * nstep),
        in_specs=[_full(g1.shape), _full(g2.shape), _full(gb.shape),
                  pl.BlockSpec((1, 2, r8, r8, rows, FOURIER_W),
                               lambda i, j: (i, 0, 0, 0, jnp.minimum(j, nstep - 1), 0))],
        out_specs=pl.BlockSpec((1, RADIX, rows, FOURIER_W),
                               lambda i, j: (i, 0, jnp.maximum(j - nstep, 0), 0)),
        out_shape=jax.ShapeDtypeStruct((b, RADIX, RADIX, FOURIER_W), F32),
        scratch_shapes=[pltpu.VMEM((2, RADIX, RADIX, FOURIER_W), F32)],
        compiler_params=_params("arbitrary", "arbitrary"),
        name="dft",
    )(g1, g2, gb, z6)


def _attn_body(qT_ref, k_ref, vT_ref, kc_ref, vcT_ref, *rest, tq, tkb, n_cast):
    for src, dst in zip(rest[:n_cast], rest[n_cast + 1:]):
        dst[...] = src[...].astype(dst.dtype)
    o_ref = rest[n_cast]
    nkb = k_ref.shape[2] // tkb
    nblk = nkb + 1

    def colmax(s):
        r = s.shape[0] // 8
        return jnp.max(jnp.max(s.reshape(8, r, s.shape[1]), axis=0), axis=0, keepdims=True)

    def update(carry, s, vT_blk):
        m, acc = carry
        m_new = jnp.maximum(m, colmax(s))
        p = jnp.exp2(s - m_new).astype(BF16)
        acc = jnp.exp2(m - m_new) * acc + _dot(vT_blk, p)
        return m_new, acc

    nq = qT_ref.shape[3] // tq

    def k_blk(i):
        return k_ref[0, 0, i * tkb:(i + 1) * tkb, :] if i < nkb else kc_ref[0, 0]

    def vT_blk(i):
        return vT_ref[0, 0, :, i * tkb:(i + 1) * tkb] if i < nkb else vcT_ref[0, 0]

    def q_cols(qi):
        return qT_ref[0, 0, :, pl.ds(pl.multiple_of(qi * tq, tq), tq)]

    def q_tile(qi, pending):
        qT = q_cols(qi)
        qT_next = q_cols(jnp.minimum(qi + 1, nq - 1))
        pending = list(pending)
        carry = (jnp.full((1, tq), -jnp.inf, F32), jnp.zeros((V_ROWS, tq), F32))
        for i in range(nblk):
            j = i + ATTN_AHEAD
            pending.append(_dot(k_blk(j), qT) if j < nblk else _dot(k_blk(j - nblk), qT_next))
            carry = update(carry, pending.pop(0), vT_blk(i))
        _, acc = carry
        o_ref[0, :, pl.ds(pl.multiple_of(qi * tq, tq), tq)] = (
            acc[:V_DIM] / acc[V_DIM:V_DIM + 1]).astype(o_ref.dtype)
        return tuple(pending)

    lax.fori_loop(0, nq, q_tile, tuple(_dot(k_blk(i), q_cols(0)) for i in range(ATTN_AHEAD)),
                  unroll=ATTN_UNROLL)


def _attention(qT, k, vT, kc, vcT, weights, tq, tkb):
    b, h, _, s = qT.shape
    sc = kc.shape[2]
    steps = b * h
    w_specs = []
    for w, axis in weights:
        blk = tuple(n // steps if a == axis else n for a, n in enumerate(w.shape))
        w_specs.append(pl.BlockSpec(blk, (lambda i, j: (i * h + j, 0)) if axis == 0
                                    else (lambda i, j: (0, i * h + j))))
    outs = pl.pallas_call(
        functools.partial(_attn_body, tq=tq, tkb=tkb, n_cast=len(weights)),
        grid=(b, h),
        in_specs=[pl.BlockSpec((1, 1, HEAD_PAD, s), lambda i, j: (i, j, 0, 0)),
                  pl.BlockSpec((1, 1, s, HEAD_PAD), lambda i, j: (i, j, 0, 0)),
                  pl.BlockSpec((1, 1, V_ROWS, s), lambda i, j: (i, j, 0, 0)),
                  pl.BlockSpec((1, 1, sc, HEAD_PAD), lambda i, j: (i, j, 0, 0)),
                  pl.BlockSpec((1, 1, V_ROWS, sc), lambda i, j: (i, j, 0, 0))] + w_specs,
        out_specs=[pl.BlockSpec((1, V_DIM, s), lambda i, j: (i, j, 0))] + w_specs,
        out_shape=[jax.ShapeDtypeStruct((b, h * V_DIM, s), BF16)]
        + [jax.ShapeDtypeStruct(w.shape, BF16) for w, _ in weights],
        compiler_params=_params("arbitrary", "arbitrary"),
        name="attention",
    )(qT, k, vT, kc, vcT, *[w for w, _ in weights])
    return outs[0], outs[1:]


def _post_body(fm_ref, aT_ref, g_ref, x_ref, mod_ref, wfo_ref, wao_ref, wout_ref,
               g2_ref, wup_ref, wdn_ref, o_ref, *, fc):
    d = x_ref.shape[-1]
    m = mod_ref[0]
    part = x_ref.shape[1] // POST_PARTS
    toks = [slice(p * part, (p + 1) * part) for p in range(POST_PARTS)]
    x1s, hbs = [], []
    for tok in toks:
        yf = _dot(fm_ref[0, tok, :].astype(BF16), wfo_ref[...])
        ya = lax.dot_general(aT_ref[0, :, tok], wao_ref[...], (((0,), (0,)), ((), ())),
                             preferred_element_type=F32)
        g = g_ref[0, tok, :]
        y = g[:, :d].astype(F32) * yf + g[:, d:].astype(F32) * ya
        x1 = x_ref[0, tok, :] + m[2:3] * _dot(y.astype(BF16), wout_ref[...])
        x1s.append(x1)
        hbs.append(_norm_mod(x1, m[3:5], g2_ref[...]).astype(BF16))
    for tok, x1, hb in zip(toks, x1s, hbs):
        acc = jnp.zeros(x1.shape, F32)
        for c in range(wup_ref.shape[1] // fc):
            u = jnp.maximum(_dot(hb, wup_ref[:, c * fc:(c + 1) * fc]), 0.0)
            acc = acc + _dot((u * u).astype(BF16), wdn_ref[c * fc:(c + 1) * fc, :])
        o_ref[0, tok, :] = x1 + m[5:6] * acc


def _post(fm, aT, gates, x, mod3, wfo, wao, wout, g2, wup, wdn, ts, fc):
    b, s, d = x.shape
    return pl.pallas_call(
        functools.partial(_post_body, fc=fc),
        grid=(b, s // ts),
        in_specs=[pl.BlockSpec((1, ts, FOURIER_W), lambda i, j: (i, j, 0)),
                  pl.BlockSpec((1, aT.shape[1], ts), lambda i, j: (i, 0, j)),
                  pl.BlockSpec((1, ts, 2 * d), lambda i, j: (i, j, 0)),
                  pl.BlockSpec((1, ts, d), lambda i, j: (i, j, 0)),
                  pl.BlockSpec((1, 6, d), lambda i, j: (i, 0, 0)),
                  _full(wfo.shape), _full(wao.shape), _full(wout.shape),
                  _full(g2.shape), _full(wup.shape), _full(wdn.shape)],
        out_specs=pl.BlockSpec((1, ts, d), lambda i, j: (i, j, 0)),
        out_shape=jax.ShapeDtypeStruct((b, s, d), F32),
        compiler_params=_params("arbitrary", "arbitrary"),
        name="post",
    )(fm, aT, gates, x, mod3, wfo, wao, wout, g2, wup, wdn)


@functools.lru_cache(maxsize=None)
def _dft_tables():
    j = np.arange(GROUP_W)
    ang = 2.0 * np.pi * np.outer(j, j) / GROUP_W
    cs = np.concatenate([np.cos(ang), -np.sin(ang)], axis=1) / math.sqrt(GROUP_W)

    r8 = DFT_RADIX_A
    d = np.arange(r8)
    eye = np.eye(DFT_ABLK)

    def real_block(theta):
        c, s = np.cos(theta), np.sin(theta)
        return np.kron(np.block([[c, s], [-s, c]]) / math.sqrt(r8), eye)

    g1 = real_block(2.0 * np.pi * np.outer(d, d) / r8)
    g2 = np.stack([real_block(2.0 * np.pi * (k2a * d[None, :] / RADIX + np.outer(d, d) / r8))
                   for k2a in range(r8)])

    n = RADIX * RADIX
    k1 = np.arange(RADIX)[:, None, None]
    k2 = np.arange(RADIX)[None, :, None]
    bb = np.arange(RADIX)[None, None, :]
    th = 2.0 * np.pi * ((bb * (RADIX * k1 + k2)) % n) / n
    gc = np.cos(th) / math.sqrt(RADIX)
    gs = np.sin(th) / math.sqrt(RADIX)
    nblk = RADIX // DFT_BLK
    gtab = np.zeros((nblk, RADIX, DFT_BLK, 2, DFT_BLK, RADIX), np.float32)
    for jb in range(nblk):
        for q in range(DFT_BLK):
            gtab[jb, :, q, 0, q, :] = gc[:, jb * DFT_BLK + q, :]
            gtab[jb, :, q, 1, q, :] = gs[:, jb * DFT_BLK + q, :]
    gtab = gtab.reshape(nblk, RADIX * DFT_BLK, 2 * DFT_BLK * RADIX)
    return cs.astype(np.float32), g1.astype(np.float32), g2.astype(np.float32), gtab


def _rope_tables_t(n):
    rows = n // GRID_W
    row = jnp.repeat(jnp.arange(rows, dtype=F32), GRID_W)
    col = jnp.tile(jnp.arange(GRID_W, dtype=F32), rows)
    n_freq = QK_ROPE // 4
    freqs = ROPE_THETA ** (-jnp.arange(n_freq, dtype=F32) / n_freq)
    ang_r = row[:, None] * freqs[None, :]
    ang_c = col[:, None] * freqs[None, :]
    ang = jnp.concatenate([ang_r, ang_r, ang_c, ang_c], axis=-1)
    return jnp.cos(ang).T, jnp.sin(ang).T


def kernel(x, c, ctx, c_ctx, ada_w, ada_b, norm1_g, norm2_g, w_in, b_gate, w_fourier, q_norm_g, w_qb,
           kv_norm_g, w_kvb, q_gain, k_gain, w_mla_o, w_out, w_up, w_down):
    bsz, s, d = x.shape
    sc = ctx.shape[1]
    assert ada_w.shape[0] == 1 and s == RADIX * RADIX and d == D_MODEL

    cc = jnp.zeros((8, d), F32).at[:bsz].set(c).at[bsz].set(c_ctx)
    mod, wf, wq, wkv, wg = _adaln(cc, ada_w[0], ada_b[0][None, :], w_in[0].T)
    mod3 = mod.reshape(8, 6, d)

    wqbT = jnp.pad(w_qb[0].reshape(Q_LORA, N_HEADS, HEAD_QK),
                   ((0, 0), (0, 0), (0, HEAD_PAD - HEAD_QK))).reshape(Q_LORA, -1).T.astype(BF16)
    wkvb = w_kvb[0].reshape(KV_LORA, N_HEADS, QK_NOPE + V_DIM)
    wkbT = wkvb[:, :, :QK_NOPE].reshape(KV_LORA, -1).T.astype(BF16)
    wvT = wkvb[:, :, QK_NOPE:].reshape(KV_LORA, -1).T.astype(BF16)

    g1 = norm1_g[0][None, :]
    g2 = norm2_g[0][None, :]
    qg = q_norm_g[0][None, :]
    kvg = kv_norm_g[0][None, :]
    bg = b_gate[0][None, :]
    qscale = (HEAD_QK ** -0.5) * math.log2(math.e)
    qgain = jnp.broadcast_to(jnp.pad(q_gain[0] * qscale, (0, HEAD_PAD - HEAD_QK))[:, None], (HEAD_PAD, s))
    kgn = jnp.broadcast_to(k_gain[0][:QK_NOPE, None], (QK_NOPE, s))
    kgr = jnp.broadcast_to(k_gain[0][QK_NOPE:, None], (QK_ROPE, s))
    cosT, sinT = _rope_tables_t(s)
    cos1 = jnp.ones((QK_ROPE, sc), F32)
    sin0 = jnp.zeros((QK_ROPE, sc), F32)

    cs, ga1, ga2, gtab = (jnp.asarray(t).astype(BF16) for t in _dft_tables())

    kc, vcT = _ctx_kv(ctx, mod3, g1, wkv, kvg, wkbT, wvT, kgn[:, :sc], kgr[:, :sc], cos1, sin0)

    z, gates, qT, k, vT = _proj(x, mod3, g1, wf, cs, wg, bg, wq, qg, wqbT, qgain, wkv, kvg, wkbT, wvT,
                                kgn, kgr, cosT, sinT, PROJ_TILE)

    r8 = DFT_RADIX_A
    fm = _dft(ga1, ga2, gtab, z.reshape(bsz, 2, r8, r8, RADIX, FOURIER_W)).reshape(bsz, s, FOURIER_W)

    aT, (wfo, wao, wout, wup, wdn) = _attention(
        qT, k, vT, kc, vcT,
        [(w_fourier[0], 0), (w_mla_o[0], 0), (w_out[0], 0), (w_up[0], 1), (w_down[0], 0)],
        ATTN_TQ, ATTN_TK)

    return _post(fm, aT, gates, x, mod3, wfo, wao, wout, g2, wup, wdn, POST_TILE, FF_CHUNK)
```

```python
import functools
import math

import numpy as np
import jax
import jax.numpy as jnp
from jax import lax
from jax.experimental import pallas as pl
from jax.experimental.pallas import tpu as pltpu

F32 = jnp.float32
BF16 = jnp.bfloat16

D_MODEL = 1024
GRID_W = 64
N_GROUPS = 4
GROUP_W = 128
FOURIER_W = N_GROUPS * GROUP_W
N_HEADS = 8
QK_NOPE = 64
QK_ROPE = 32
HEAD_QK = QK_NOPE + QK_ROPE
HEAD_PAD = 128
V_DIM = 64
V_ROWS = V_DIM + 16
Q_LORA = 256
KV_LORA = 128
W_IN_GATES = FOURIER_W + Q_LORA + KV_LORA + QK_ROPE
ROPE_THETA = 10000.0
EPS = 1e-6
RADIX = 64
DFT_RADIX_A = 8
DFT_ABLK = 16
DFT_BLK = 8
DFT_STEP = 2
POST_TILE = 512
POST_PARTS = 2
PROJ_TILE = 1024
PROJ_PARTS = 4
GATE_CHUNK = 256
FF_CHUNK = 1024
ATTN_TQ = 512
ATTN_TK = 256
ATTN_AHEAD = 2
ATTN_UNROLL = 4

V7X_VMEM_LIMIT = 56 * 1024 * 1024


def _params(*sem):
    return pltpu.CompilerParams(dimension_semantics=sem, vmem_limit_bytes=V7X_VMEM_LIMIT)


def _dot(a, b):
    return jnp.dot(a, b, preferred_element_type=F32)


def _rms_lanes(x, n):
    return x * lax.rsqrt(jnp.sum(x * x, axis=-1, keepdims=True) * (1.0 / n) + EPS)


def _rot_half_rows(x):
    return jnp.concatenate([-x[8:16], x[0:8], -x[24:32], x[16:24]], axis=0)


W_IN_BLK = 256


def _adaln_body(c_ref, w_ref, b_ref, tf_ref, tq_ref, tkv_ref, tga_ref, tgb_ref,
                o_ref, wf_ref, wq_ref, wkv_ref, wg_ref):
    c = c_ref[...]
    s = c * jax.nn.sigmoid(c)
    o_ref[...] = _dot(s, w_ref[...]) + b_ref[...]

    @pl.when(pl.program_id(0) == 0)
    def _():
        wf_ref[...] = tf_ref[...].T.astype(BF16)
        wq_ref[...] = tq_ref[...].T.astype(BF16)
        wkv_ref[...] = tkv_ref[...].T.astype(BF16)

    off = W_IN_GATES % W_IN_BLK
    both = jnp.concatenate([tga_ref[...], tgb_ref[...]], axis=0)
    wg_ref[...] = both[off:off + W_IN_BLK].T.astype(BF16)


def _adaln(cc, ada_w, ada_b, w_inT):
    rows, d = cc.shape
    n = ada_w.shape[1]
    n_gates = w_inT.shape[0] - W_IN_GATES
    steps = n_gates // W_IN_BLK
    tn = n // steps
    g0 = W_IN_GATES // W_IN_BLK
    blk = lambda r, f: pl.BlockSpec((r, d), f)
    return pl.pallas_call(
        _adaln_body,
        grid=(steps,),
        in_specs=[pl.BlockSpec((rows, d), lambda j: (0, 0)),
                  pl.BlockSpec((d, tn), lambda j: (0, j)),
                  pl.BlockSpec((1, tn), lambda j: (0, j)),
                  blk(FOURIER_W, lambda j: (0, 0)),
                  blk(Q_LORA, lambda j: (FOURIER_W // Q_LORA, 0)),
                  blk(2 * KV_LORA, lambda j: ((FOURIER_W + Q_LORA) // (2 * KV_LORA), 0)),
                  blk(W_IN_BLK, lambda j: (g0 + j, 0)),
                  blk(W_IN_BLK, lambda j: (g0 + 1 + j, 0))],
        out_specs=[pl.BlockSpec((rows, tn), lambda j: (0, j)),
                   pl.BlockSpec((d, FOURIER_W), lambda j: (0, 0)),
                   pl.BlockSpec((d, Q_LORA), lambda j: (0, 0)),
                   pl.BlockSpec((d, 2 * KV_LORA), lambda j: (0, 0)),
                   pl.BlockSpec((d, W_IN_BLK), lambda j: (0, j))],
        out_shape=[jax.ShapeDtypeStruct((rows, n), F32),
                   jax.ShapeDtypeStruct((d, FOURIER_W), BF16),
                   jax.ShapeDtypeStruct((d, Q_LORA), BF16),
                   jax.ShapeDtypeStruct((d, 2 * KV_LORA), BF16),
                   jax.ShapeDtypeStruct((d, n_gates), BF16)],
        compiler_params=_params("arbitrary"),
        name="adaln",
    )(cc, ada_w, ada_b, w_inT, w_inT, w_inT, w_inT, w_inT)


def _norm_mod(x, m, g):
    ms = jnp.sum(x * x, axis=-1, keepdims=True) * (1.0 / x.shape[-1])
    return (x * lax.rsqrt(ms + EPS)) * (g * (1.0 + m[1:2])) + m[0:1]


def _kv_latents(kvp, kvg_ref):
    kvn = _rms_lanes(kvp[:, :KV_LORA], KV_LORA) * kvg_ref[...]
    return kvn.T.astype(BF16), kvp[:, KV_LORA:].T[0:QK_ROPE]


def _kv_heads(knT, vT, krT, kgn_ref, kgr_ref, cos_ref, sin_ref, k_ref, vT_ref, tok):
    t = knT.shape[1]
    sr = jnp.sum(krT * krT, axis=0, keepdims=True)
    krg = krT * kgr_ref[:, tok]
    kr_rot = krg * cos_ref[:, tok] + _rot_half_rows(krg) * sin_ref[:, tok]
    kgn = kgn_ref[:, tok]
    ones = jnp.ones((V_ROWS - V_DIM, t), F32)
    zpad = jnp.zeros((HEAD_PAD - HEAD_QK, t), F32)
    for h in range(N_HEADS):
        kn = knT[h * QK_NOPE:(h + 1) * QK_NOPE]
        ss = jnp.sum(kn * kn, axis=0, keepdims=True) + sr
        r = lax.rsqrt(ss * (1.0 / HEAD_QK) + EPS)
        kT = jnp.concatenate([kn * kgn * r, kr_rot * r, zpad], axis=0)
        k_ref[0, h, tok, :] = kT.T.astype(k_ref.dtype)
        vh = vT[h * V_DIM:(h + 1) * V_DIM]
        vT_ref[0, h, :, tok] = jnp.concatenate([vh, ones], axis=0).astype(vT_ref.dtype)


def _q_heads(qT, qgain_ref, cos_ref, sin_ref, qT_ref, tok):
    qgain, cos, sin = qgain_ref[:, tok], cos_ref[:, tok], sin_ref[:, tok]
    for h in range(N_HEADS):
        qh = qT[h * HEAD_PAD:(h + 1) * HEAD_PAD]
        r = lax.rsqrt(jnp.sum(qh * qh, axis=0, keepdims=True) * (1.0 / HEAD_QK) + EPS)
        qh = qh * qgain * r
        qr = qh[QK_NOPE:HEAD_QK]
        qr = qr * cos + _rot_half_rows(qr) * sin
        qT_ref[0, h, :, tok] = jnp.concatenate([qh[:QK_NOPE], qr, qh[HEAD_QK:]],
                                               axis=0).astype(qT_ref.dtype)


def _ctx_body(x_ref, mod_ref, g1_ref, wkv_ref, kvg_ref, wkbT_ref, wvT_ref, kgn_ref, kgr_ref,
              cos_ref, sin_ref, k_ref, vT_ref):
    hb = _norm_mod(x_ref[0], mod_ref[0], g1_ref[...]).astype(BF16)
    kvnT, krT = _kv_latents(_dot(hb, wkv_ref[...]), kvg_ref)
    _kv_heads(_dot(wkbT_ref[...], kvnT), _dot(wvT_ref[...], kvnT), krT,
              kgn_ref, kgr_ref, cos_ref, sin_ref, k_ref, vT_ref, slice(None))


def _proj_body(x_ref, mod_ref, g1_ref, wf_ref, cs_ref, wg_ref, bg_ref,
               wq_ref, qg_ref, wqbT_ref, qgain_ref,
               wkv_ref, kvg_ref, wkbT_ref, wvT_ref, kgn_ref, kgr_ref, cos_ref, sin_ref,
               z_ref, g_ref, qT_ref, k_ref, vT_ref):
    part = x_ref.shape[1] // PROJ_PARTS
    for p in range(PROJ_PARTS):
        tok = slice(p * part, (p + 1) * part)
        hb = _norm_mod(x_ref[0, tok, :], mod_ref[0], g1_ref[...]).astype(BF16)
        qp = _dot(hb, wq_ref[...])
        kvp = _dot(hb, wkv_ref[...])
        f = _dot(hb, wf_ref[...]).astype(BF16)

        qnT = (_rms_lanes(qp, Q_LORA) * qg_ref[...]).T.astype(BF16)
        kvnT, krT = _kv_latents(kvp, kvg_ref)

        def gate_chunk(c):
            cols = slice(c * GATE_CHUNK, (c + 1) * GATE_CHUNK)
            g_ref[0, tok, cols] = jax.nn.sigmoid(
                _dot(hb, wg_ref[:, cols]) + bg_ref[:, cols]).astype(g_ref.dtype)

        gate_chunk(0)

        qT = _dot(wqbT_ref[...], qnT)
        knT = _dot(wkbT_ref[...], kvnT)
        vT = _dot(wvT_ref[...], kvnT)

        _q_heads(qT, qgain_ref, cos_ref, sin_ref, qT_ref, tok)
        _kv_heads(knT, vT, krT, kgn_ref, kgr_ref, cos_ref, sin_ref, k_ref, vT_ref, tok)
        for c in range(1, g_ref.shape[2] // GATE_CHUNK):
            gate_chunk(c)

        for g in range(N_GROUPS):
            cols = slice(g * GROUP_W, (g + 1) * GROUP_W)
            z = _dot(f[:, cols], cs_ref[...])
            z_ref[0, 0, tok, cols] = z[:, :GROUP_W].astype(z_ref.dtype)
            z_ref[0, 1, tok, cols] = z[:, GROUP_W:].astype(z_ref.dtype)


def _full(shape):
    nd = len(shape)
    return pl.BlockSpec(shape, lambda *_: (0,) * nd, pipeline_mode=pl.Buffered(1))


def _ctx_kv(ctx, mod3, g1, wkv, kvg, wkbT, wvT, kgn, kgr, cosT, sinT):
    b, t, d = ctx.shape
    return pl.pallas_call(
        _ctx_body,
        grid=(b,),
        in_specs=[pl.BlockSpec((1, t, d), lambda i: (i, 0, 0)),
                  pl.BlockSpec((1, 6, d), lambda i: (b, 0, 0)),
                  _full(g1.shape), _full(wkv.shape), _full(kvg.shape), _full(wkbT.shape),
                  _full(wvT.shape), _full(kgn.shape), _full(kgr.shape),
                  _full(cosT.shape), _full(sinT.shape)],
        out_specs=[pl.BlockSpec((1, N_HEADS, t, HEAD_PAD), lambda i: (i, 0, 0, 0)),
                   pl.BlockSpec((1, N_HEADS, V_ROWS, t), lambda i: (i, 0, 0, 0))],
        out_shape=[jax.ShapeDtypeStruct((b, N_HEADS, t, HEAD_PAD), BF16),
                   jax.ShapeDtypeStruct((b, N_HEADS, V_ROWS, t), BF16)],
        compiler_params=_params("arbitrary"),
        name="ctx_kv",
    )(ctx, mod3, g1, wkv, kvg, wkbT, wvT, kgn, kgr, cosT, sinT)


def _proj(x, mod3, g1, wf, cs, wg, bg, wq, qg, wqbT, qgain, wkv, kvg, wkbT, wvT, kgn, kgr,
          cosT, sinT, ts):
    b, s, d = x.shape
    tok = lambda shape: pl.BlockSpec(shape, lambda i, j: (0, j))
    return pl.pallas_call(
        _proj_body,
        grid=(b, s // ts),
        in_specs=[pl.BlockSpec((1, ts, d), lambda i, j: (i, j, 0)),
                  pl.BlockSpec((1, 6, d), lambda i, j: (i, 0, 0)),
                  _full(g1.shape), _full(wf.shape), _full(cs.shape), _full(wg.shape),
                  _full(bg.shape), _full(wq.shape), _full(qg.shape), _full(wqbT.shape),
                  tok((HEAD_PAD, ts)),
                  _full(wkv.shape), _full(kvg.shape), _full(wkbT.shape), _full(wvT.shape),
                  tok((QK_NOPE, ts)), tok((QK_ROPE, ts)), tok((QK_ROPE, ts)), tok((QK_ROPE, ts))],
        out_specs=[pl.BlockSpec((1, 2, ts, FOURIER_W), lambda i, j: (i, 0, j, 0)),
                   pl.BlockSpec((1, ts, 2 * d), lambda i, j: (i, j, 0)),
                   pl.BlockSpec((1, N_HEADS, HEAD_PAD, ts), lambda i, j: (i, 0, 0, j)),
                   pl.BlockSpec((1, N_HEADS, ts, HEAD_PAD), lambda i, j: (i, 0, j, 0)),
                   pl.BlockSpec((1, N_HEADS, V_ROWS, ts), lambda i, j: (i, 0, 0, j))],
        out_shape=[jax.ShapeDtypeStruct((b, 2, s, FOURIER_W), BF16),
                   jax.ShapeDtypeStruct((b, s, 2 * d), BF16),
                   jax.ShapeDtypeStruct((b, N_HEADS, HEAD_PAD, s), BF16),
                   jax.ShapeDtypeStruct((b, N_HEADS, s, HEAD_PAD), BF16),
                   jax.ShapeDtypeStruct((b, N_HEADS, V_ROWS, s), BF16)],
        compiler_params=_params("arbitrary", "arbitrary"),
        name="proj",
    )(x, mod3, g1, wf, cs, wg, bg, wq, qg, wqbT, qgain, wkv, kvg, wkbT, wvT, kgn, kgr,
      cosT, sinT)


def _dft_body(g1_ref, g2_ref, gb_ref, z_ref, o_ref, a_scr):
    step_rows = DFT_BLK * DFT_STEP
    nstep = RADIX // step_rows
    j = pl.program_id(1)
    r8 = DFT_RADIX_A
    slab = (2, r8, DFT_ABLK, FOURIER_W)

    @pl.when(j < nstep)
    def _():
        for t in range(step_rows // DFT_ABLK):
            rows = slice(t * DFT_ABLK, (t + 1) * DFT_ABLK)
            off = pl.multiple_of(j * step_rows + t * DFT_ABLK, DFT_ABLK)
            t1 = []
            for a2 in range(r8):
                z = z_ref[0, :, :, a2, rows, :].reshape(2 * r8 * DFT_ABLK, FOURIER_W)
                t1.append(_dot(g1_ref[...], z).reshape(slab))
            for k2a in range(r8):
                x = jnp.concatenate([t1[a2][r, k2a] for r in range(2) for a2 in range(r8)], axis=0)
                a = _dot(g2_ref[k2a], x.astype(BF16)).reshape(slab)
                for r in range(2):
                    for k2b in range(r8):
                        a_scr[r, k2a + r8 * k2b, pl.ds(off, DFT_ABLK), :] = a[r, k2b]

    @pl.when(j >= nstep)
    def _():
        for t in range(DFT_STEP):
            blk = (j - nstep) * DFT_STEP + t
            off = pl.multiple_of(blk * DFT_BLK, DFT_BLK)
            a = a_scr[:, pl.ds(off, DFT_BLK), :, :]
            a = a.reshape(2 * DFT_BLK * RADIX, FOURIER_W).astype(BF16)
            y = _dot(gb_ref[blk], a)
            o_ref[0, :, t * DFT_BLK:(t + 1) * DFT_BLK, :] = y.reshape(RADIX, DFT_BLK, FOURIER_W)


def _dft(g1, g2, gb, z6):
    b = z6.shape[0]
    nstep = RADIX // (DFT_BLK * DFT_STEP)
    rows = DFT_BLK * DFT_STEP
    r8 = DFT_RADIX_A
    assert rows % DFT_ABLK == 0 and z6.shape[2:4] == (r8, r8)
    return pl.pallas_call(
        _dft_body,
        grid=(b, 2 * nstep),
        in_specs=[_full(g1.shape), _full(g2.shape), _full(gb.shape),
                  pl.BlockSpec((1, 2, r8, r8, rows, FOURIER_W),
                               lambda i, j: (i, 0, 0, 0, jnp.minimum(j, nstep - 1), 0))],
        out_specs=pl.BlockSpec((1, RADIX, rows, FOURIER_W),
                               lambda i, j: (i, 0, jnp.maximum(j - nstep, 0), 0)),
        out_shape=jax.ShapeDtypeStruct((b, RADIX, RADIX, FOURIER_W), F32),
        scratch_shapes=[pltpu.VMEM((2, RADIX, RADIX, FOURIER_W), F32)],
        compiler_params=_params("arbitrary", "arbitrary"),
        name="dft",
    )(g1, g2, gb, z6)


def _attn_body(qT_ref, k_ref, vT_ref, kc_ref, vcT_ref, *rest, tq, tkb, n_cast):
    for src, dst in zip(rest[:n_cast], rest[n_cast + 1:]):
        dst[...] = src[...].astype(dst.dtype)
    o_ref = rest[n_cast]
    nkb = k_ref.shape[2] // tkb
    nblk = nkb + 1

    def colmax(s):
        r = s.shape[0] // 8
        return jnp.max(jnp.max(s.reshape(8, r, s.shape[1]), axis=0), axis=0, keepdims=True)

    def update(carry, s, vT_blk):
        m, acc = carry
        m_new = jnp.maximum(m, colmax(s))
        p = jnp.exp2(s - m_new).astype(BF16)
        acc = jnp.exp2(m - m_new) * acc + _dot(vT_blk, p)
        return m_new, acc

    nq = qT_ref.shape[3] // tq

    def k_blk(i):
        return k_ref[0, 0, i * tkb:(i + 1) * tkb, :] if i < nkb else kc_ref[0, 0]

    def vT_blk(i):
        return vT_ref[0, 0, :, i * tkb:(i + 1) * tkb] if i < nkb else vcT_ref[0, 0]

    def q_cols(qi):
        return qT_ref[0, 0, :, pl.ds(pl.multiple_of(qi * tq, tq), tq)]

    def q_tile(qi, pending):
        qT = q_cols(qi)
        qT_next = q_cols(jnp.minimum(qi + 1, nq - 1))
        pending = list(pending)
        carry = (jnp.full((1, tq), -jnp.inf, F32), jnp.zeros((V_ROWS, tq), F32))
        for i in range(nblk):
            j = i + ATTN_AHEAD
            pending.append(_dot(k_blk(j), qT) if j < nblk else _dot(k_blk(j - nblk), qT_next))
            carry = update(carry, pending.pop(0), vT_blk(i))
        _, acc = carry
        o_ref[0, :, pl.ds(pl.multiple_of(qi * tq, tq), tq)] = (
            acc[:V_DIM] / acc[V_DIM:V_DIM + 1]).astype(o_ref.dtype)
        return tuple(pending)

    lax.fori_loop(0, nq, q_tile, tuple(_dot(k_blk(i), q_cols(0)) for i in range(ATTN_AHEAD)),
                  unroll=ATTN_UNROLL)


def _attention(qT, k, vT, kc, vcT, weights, tq, tkb):
    b, h, _, s = qT.shape
    sc = kc.shape[2]
    steps = b * h
    w_specs = []
    for w, axis in weights:
        blk = tuple(n // steps if a == axis else n for a, n in enumerate(w.shape))
        w_specs.append(pl.BlockSpec(blk, (lambda i, j: (i * h + j, 0)) if axis == 0
                                    else (lambda i, j: (0, i * h + j))))
    outs = pl.pallas_call(
        functools.partial(_attn_body, tq=tq, tkb=tkb, n_cast=len(weights)),
        grid=(b, h),
        in_specs=[pl.BlockSpec((1, 1, HEAD_PAD, s), lambda i, j: (i, j, 0, 0)),
                  pl.BlockSpec((1, 1, s, HEAD_PAD), lambda i, j: (i, j, 0, 0)),
                  pl.BlockSpec((1, 1, V_ROWS, s), lambda i, j: (i, j, 0, 0)),
                  pl.BlockSpec((1, 1, sc, HEAD_PAD), lambda i, j: (i, j, 0, 0)),
                  pl.BlockSpec((1, 1, V_ROWS, sc), lambda i, j: (i, j, 0, 0))] + w_specs,
        out_specs=[pl.BlockSpec((1, V_DIM, s), lambda i, j: (i, j, 0))] + w_specs,
        out_shape=[jax.ShapeDtypeStruct((b, h * V_DIM, s), BF16)]
        + [jax.ShapeDtypeStruct(w.shape, BF16) for w, _ in weights],
        compiler_params=_params("arbitrary", "arbitrary"),
        name="attention",
    )(qT, k, vT, kc, vcT, *[w for w, _ in weights])
    return outs[0], outs[1:]


def _post_body(fm_ref, aT_ref, g_ref, x_ref, mod_ref, wfo_ref, wao_ref, wout_ref,
               g2_ref, wup_ref, wdn_ref, o_ref, *, fc):
    d = x_ref.shape[-1]
    m = mod_ref[0]
    part = x_ref.shape[1] // POST_PARTS
    toks = [slice(p * part, (p + 1) * part) for p in range(POST_PARTS)]
    x1s, hbs = [], []
    for tok in toks:
        yf = _dot(fm_ref[0, tok, :].astype(BF16), wfo_ref[...])
        ya = lax.dot_general(aT_ref[0, :, tok], wao_ref[...], (((0,), (0,)), ((), ())),
                             preferred_element_type=F32)
        g = g_ref[0, tok, :]
        y = g[:, :d].astype(F32) * yf + g[:, d:].astype(F32) * ya
        x1 = x_ref[0, tok, :] + m[2:3] * _dot(y.astype(BF16), wout_ref[...])
        x1s.append(x1)
        hbs.append(_norm_mod(x1, m[3:5], g2_ref[...]).astype(BF16))
    for tok, x1, hb in zip(toks, x1s, hbs):
        acc = jnp.zeros(x1.shape, F32)
        for c in range(wup_ref.shape[1] // fc):
            u = jnp.maximum(_dot(hb, wup_ref[:, c * fc:(c + 1) * fc]), 0.0)
            acc = acc + _dot((u * u).astype(BF16), wdn_ref[c * fc:(c + 1) * fc, :])
        o_ref[0, tok, :] = x1 + m[5:6] * acc


def _post(fm, aT, gates, x, mod3, wfo, wao, wout, g2, wup, wdn, ts, fc):
    b, s, d = x.shape
    return pl.pallas_call(
        functools.partial(_post_body, fc=fc),
        grid=(b, s // ts),
        in_specs=[pl.BlockSpec((1, ts, FOURIER_W), lambda i, j: (i, j, 0)),
                  pl.BlockSpec((1, aT.shape[1], ts), lambda i, j: (i, 0, j)),
                  pl.BlockSpec((1, ts, 2 * d), lambda i, j: (i, j, 0)),
                  pl.BlockSpec((1, ts, d), lambda i, j: (i, j, 0)),
                  pl.BlockSpec((1, 6, d), lambda i, j: (i, 0, 0)),
                  _full(wfo.shape), _full(wao.shape), _full(wout.shape),
                  _full(g2.shape), _full(wup.shape), _full(wdn.shape)],
        out_specs=pl.BlockSpec((1, ts, d), lambda i, j: (i, j, 0)),
        out_shape=jax.ShapeDtypeStruct((b, s, d), F32),
        compiler_params=_params("arbitrary", "arbitrary"),
        name="post",
    )(fm, aT, gates, x, mod3, wfo, wao, wout, g2, wup, wdn)


@functools.lru_cache(maxsize=None)
def _dft_tables():
    j = np.arange(GROUP_W)
    ang = 2.0 * np.pi * np.outer(j, j) / GROUP_W
    cs = np.concatenate([np.cos(ang), -np.sin(ang)], axis=1) / math.sqrt(GROUP_W)

    r8 = DFT_RADIX_A
    d = np.arange(r8)
    eye = np.eye(DFT_ABLK)

    def real_block(theta):
        c, s = np.cos(theta), np.sin(theta)
        return np.kron(np.block([[c, s], [-s, c]]) / math.sqrt(r8), eye)

    g1 = real_block(2.0 * np.pi * np.outer(d, d) / r8)
    g2 = np.stack([real_block(2.0 * np.pi * (k2a * d[None, :] / RADIX + np.outer(d, d) / r8))
                   for k2a in range(r8)])

    n = RADIX * RADIX
    k1 = np.arange(RADIX)[:, None, None]
    k2 = np.arange(RADIX)[None, :, None]
    bb = np.arange(RADIX)[None, None, :]
    th = 2.0 * np.pi * ((bb * (RADIX * k1 + k2)) % n) / n
    gc = np.cos(th) / math.sqrt(RADIX)
    gs = np.sin(th) / math.sqrt(RADIX)
    nblk = RADIX // DFT_BLK
    gtab = np.zeros((nblk, RADIX, DFT_BLK, 2, DFT_BLK, RADIX), np.float32)
    for jb in range(nblk):
        for q in range(DFT_BLK):
            gtab[jb, :, q, 0, q, :] = gc[:, jb * DFT_BLK + q, :]
            gtab[jb, :, q, 1, q, :] = gs[:, jb * DFT_BLK + q, :]
    gtab = gtab.reshape(nblk, RADIX * DFT_BLK, 2 * DFT_BLK * RADIX)
    return cs.astype(np.float32), g1.astype(np.float32), g2.astype(np.float32), gtab


def _rope_tables_t(n):
    rows = n // GRID_W
    row = jnp.repeat(jnp.arange(rows, dtype=F32), GRID_W)
    col = jnp.tile(jnp.arange(GRID_W, dtype=F32), rows)
    n_freq = QK_ROPE // 4
    freqs = ROPE_THETA ** (-jnp.arange(n_freq, dtype=F32) / n_freq)
    ang_r = row[:, None] * freqs[None, :]
    ang_c = col[:, None] * freqs[None, :]
    ang = jnp.concatenate([ang_r, ang_r, ang_c, ang_c], axis=-1)
    return jnp.cos(ang).T, jnp.sin(ang).T


def kernel(x, c, ctx, c_ctx, ada_w, ada_b, norm1_g, norm2_g, w_in, b_gate, w_fourier, q_norm_g, w_qb,
           kv_norm_g, w_kvb, q_gain, k_gain, w_mla_o, w_out, w_up, w_down):
    bsz, s, d = x.shape
    sc = ctx.shape[1]
    assert ada_w.shape[0] == 1 and s == RADIX * RADIX and d == D_MODEL

    cc = jnp.zeros((8, d), F32).at[:bsz].set(c).at[bsz].set(c_ctx)
    mod, wf, wq, wkv, wg = _adaln(cc, ada_w[0], ada_b[0][None, :], w_in[0].T)
    mod3 = mod.reshape(8, 6, d)

    wqbT = jnp.pad(w_qb[0].reshape(Q_LORA, N_HEADS, HEAD_QK),
                   ((0, 0), (0, 0), (0, HEAD_PAD - HEAD_QK))).reshape(Q_LORA, -1).T.astype(BF16)
    wkvb = w_kvb[0].reshape(KV_LORA, N_HEADS, QK_NOPE + V_DIM)
    wkbT = wkvb[:, :, :QK_NOPE].reshape(KV_LORA, -1).T.astype(BF16)
    wvT = wkvb[:, :, QK_NOPE:].reshape(KV_LORA, -1).T.astype(BF16)

    g1 = norm1_g[0][None, :]
    g2 = norm2_g[0][None, :]
    qg = q_norm_g[0][None, :]
    kvg = kv_norm_g[0][None, :]
    bg = b_gate[0][None, :]
    qscale = (HEAD_QK ** -0.5) * math.log2(math.e)
    qgain = jnp.broadcast_to(jnp.pad(q_gain[0] * qscale, (0, HEAD_PAD - HEAD_QK))[:, None], (HEAD_PAD, s))
    kgn = jnp.broadcast_to(k_gain[0][:QK_NOPE, None], (QK_NOPE, s))
    kgr = jnp.broadcast_to(k_gain[0][QK_NOPE:, None], (QK_ROPE, s))
    cosT, sinT = _rope_tables_t(s)
    cos1 = jnp.ones((QK_ROPE, sc), F32)
    sin0 = jnp.zeros((QK_ROPE, sc), F32)

    cs, ga1, ga2, gtab = (jnp.asarray(t).astype(BF16) for t in _dft_tables())

    kc, vcT = _ctx_kv(ctx, mod3, g1, wkv, kvg, wkbT, wvT, kgn[:, :sc], kgr[:, :sc], cos1, sin0)

    z, gates, qT, k, vT = _proj(x, mod3, g1, wf, cs, wg, bg, wq, qg, wqbT, qgain, wkv, kvg, wkbT, wvT,
                                kgn, kgr, cosT, sinT, PROJ_TILE)

    r8 = DFT_RADIX_A
    fm = _dft(ga1, ga2, gtab, z.reshape(bsz, 2, r8, r8, RADIX, FOURIER_W)).reshape(bsz, s, FOURIER_W)

    aT, (wfo, wao, wout, wup, wdn) = _attention(
        qT, k, vT, kc, vcT,
        [(w_fourier[0], 0), (w_mla_o[0], 0), (w_out[0], 0), (w_up[0], 1), (w_down[0], 0)],
        ATTN_TQ, ATTN_TK)

    return _post(fm, aT, gates, x, mod3, wfo, wao, wout, g2, wup, wdn, POST_TILE, FF_CHUNK)
```

```python
import functools
import math

import numpy as np
import jax
import jax.numpy as jnp
from jax import lax
from jax.experimental import pallas as pl
from jax.experimental.pallas import tpu as pltpu

F32 = jnp.float32
BF16 = jnp.bfloat16

D_MODEL = 1024
GRID_W = 64
N_GROUPS = 4
GROUP_W = 128
FOURIER_W = N_GROUPS * GROUP_W
N_HEADS = 8
QK_NOPE = 64
QK_ROPE = 32
HEAD_QK = QK_NOPE + QK_ROPE
HEAD_PAD = 128
V_DIM = 64
V_ROWS = V_DIM + 16
Q_LORA = 256
KV_LORA = 128
W_IN_GATES = FOURIER_W + Q_LORA + KV_LORA + QK_ROPE
ROPE_THETA = 10000.0
EPS = 1e-6
RADIX = 64
DFT_RADIX_A = 8
DFT_ABLK = 16
DFT_BLK = 8
DFT_STEP = 2
POST_TILE = 512
POST_PARTS = 2
PROJ_TILE = 1024
PROJ_PARTS = 4
GATE_CHUNK = 256
FF_CHUNK = 1024
ATTN_TQ = 512
ATTN_TK = 256
ATTN_AHEAD = 2
ATTN_UNROLL = 4

V7X_VMEM_LIMIT = 56 * 1024 * 1024


def _params(*sem):
    return pltpu.CompilerParams(dimension_semantics=sem, vmem_limit_bytes=V7X_VMEM_LIMIT)


def _dot(a, b):
    return jnp.dot(a, b, preferred_element_type=F32)


def _rms_lanes(x, n):
    return x * lax.rsqrt(jnp.sum(x * x, axis=-1, keepdims=True) * (1.0 / n) + EPS)


def _rot_half_rows(x):
    return jnp.concatenate([-x[8:16], x[0:8], -x[24:32], x[16:24]], axis=0)


W_IN_BLK = 256


def _adaln_body(c_ref, w_ref, b_ref, tf_ref, tq_ref, tkv_ref, tga_ref, tgb_ref,
                o_ref, wf_ref, wq_ref, wkv_ref, wg_ref):
    c = c_ref[...]
    s = c * jax.nn.sigmoid(c)
    o_ref[...] = _dot(s, w_ref[...]) + b_ref[...]

    @pl.when(pl.program_id(0) == 0)
    def _():
        wf_ref[...] = tf_ref[...].T.astype(BF16)
        wq_ref[...] = tq_ref[...].T.astype(BF16)
        wkv_ref[...] = tkv_ref[...].T.astype(BF16)

    off = W_IN_GATES % W_IN_BLK
    both = jnp.concatenate([tga_ref[...], tgb_ref[...]], axis=0)
    wg_ref[0] = both[off:off + W_IN_BLK].T.astype(BF16)


def _adaln(cc, ada_w, ada_b, w_inT):
    rows, d = cc.shape
    n = ada_w.shape[1]
    n_gates = w_inT.shape[0] - W_IN_GATES
    steps = n_gates // W_IN_BLK
    tn = n // steps
    g0 = W_IN_GATES // W_IN_BLK
    blk = lambda r, f: pl.BlockSpec((r, d), f)
    return pl.pallas_call(
        _adaln_body,
        grid=(steps,),
        in_specs=[pl.BlockSpec((rows, d), lambda j: (0, 0)),
                  pl.BlockSpec((d, tn), lambda j: (0, j)),
                  pl.BlockSpec((1, tn), lambda j: (0, j)),
                  blk(FOURIER_W, lambda j: (0, 0)),
                  blk(Q_LORA, lambda j: (FOURIER_W // Q_LORA, 0)),
                  blk(2 * KV_LORA, lambda j: ((FOURIER_W + Q_LORA) // (2 * KV_LORA), 0)),
                  blk(W_IN_BLK, lambda j: (g0 + j, 0)),
                  blk(W_IN_BLK, lambda j: (g0 + 1 + j, 0))],
        out_specs=[pl.BlockSpec((rows, tn), lambda j: (0, j)),
                   pl.BlockSpec((d, FOURIER_W), lambda j: (0, 0)),
                   pl.BlockSpec((d, Q_LORA), lambda j: (0, 0)),
                   pl.BlockSpec((d, 2 * KV_LORA), lambda j: (0, 0)),
                   pl.BlockSpec((1, d, W_IN_BLK), lambda j: (j, 0, 0))],
        out_shape=[jax.ShapeDtypeStruct((rows, n), F32),
                   jax.ShapeDtypeStruct((d, FOURIER_W), BF16),
                   jax.ShapeDtypeStruct((d, Q_LORA), BF16),
                   jax.ShapeDtypeStruct((d, 2 * KV_LORA), BF16),
                   jax.ShapeDtypeStruct((steps, d, W_IN_BLK), BF16)],
        compiler_params=_params("arbitrary"),
        name="adaln",
    )(cc, ada_w, ada_b, w_inT, w_inT, w_inT, w_inT, w_inT)


def _norm_mod(x, m, g):
    ms = jnp.sum(x * x, axis=-1, keepdims=True) * (1.0 / x.shape[-1])
    return (x * lax.rsqrt(ms + EPS)) * (g * (1.0 + m[1:2])) + m[0:1]


def _kv_latents(kvp, kvg_ref):
    kvn = _rms_lanes(kvp[:, :KV_LORA], KV_LORA) * kvg_ref[...]
    return kvn.T.astype(BF16), kvp[:, KV_LORA:].T[0:QK_ROPE]


def _kv_heads(knT, vT, krT, kgn_ref, kgr_ref, cos_ref, sin_ref, k_ref, vT_ref, tok):
    t = knT.shape[1]
    sr = jnp.sum(krT * krT, axis=0, keepdims=True)
    krg = krT * kgr_ref[:, tok]
    kr_rot = krg * cos_ref[:, tok] + _rot_half_rows(krg) * sin_ref[:, tok]
    kgn = kgn_ref[:, tok]
    ones = jnp.ones((V_ROWS - V_DIM, t), F32)
    zpad = jnp.zeros((HEAD_PAD - HEAD_QK, t), F32)
    for h in range(N_HEADS):
        kn = knT[h * QK_NOPE:(h + 1) * QK_NOPE]
        ss = jnp.sum(kn * kn, axis=0, keepdims=True) + sr
        r = lax.rsqrt(ss * (1.0 / HEAD_QK) + EPS)
        kT = jnp.concatenate([kn * kgn * r, kr_rot * r, zpad], axis=0)
        k_ref[0, h, tok, :] = kT.T.astype(k_ref.dtype)
        vh = vT[h * V_DIM:(h + 1) * V_DIM]
        vT_ref[0, h, :, tok] = jnp.concatenate([vh, ones], axis=0).astype(vT_ref.dtype)


def _q_heads(qT, qgain_ref, cos_ref, sin_ref, qT_ref, tok):
    qgain, cos, sin = qgain_ref[:, tok], cos_ref[:, tok], sin_ref[:, tok]
    for h in range(N_HEADS):
        qh = qT[h * HEAD_PAD:(h + 1) * HEAD_PAD]
        r = lax.rsqrt(jnp.sum(qh * qh, axis=0, keepdims=True) * (1.0 / HEAD_QK) + EPS)
        qh = qh * qgain * r
        qr = qh[QK_NOPE:HEAD_QK]
        qr = qr * cos + _rot_half_rows(qr) * sin
        qT_ref[0, h, :, tok] = jnp.concatenate([qh[:QK_NOPE], qr, qh[HEAD_QK:]],
                                               axis=0).astype(qT_ref.dtype)


def _ctx_body(x_ref, mod_ref, g1_ref, wkv_ref, kvg_ref, wkbT_ref, wvT_ref, kgn_ref, kgr_ref,
              cos_ref, sin_ref, k_ref, vT_ref):
    hb = _norm_mod(x_ref[0], mod_ref[0], g1_ref[...]).astype(BF16)
    kvnT, krT = _kv_latents(_dot(hb, wkv_ref[...]), kvg_ref)
    _kv_heads(_dot(wkbT_ref[...], kvnT), _dot(wvT_ref[...], kvnT), krT,
              kgn_ref, kgr_ref, cos_ref, sin_ref, k_ref, vT_ref, slice(None))


def _proj_body(x_ref, mod_ref, g1_ref, wf_ref, cs_ref, wg_ref, bg_ref,
               wq_ref, qg_ref, wqbT_ref, qgain_ref,
               wkv_ref, kvg_ref, wkbT_ref, wvT_ref, kgn_ref, kgr_ref, cos_ref, sin_ref,
               z_ref, g_ref, qT_ref, k_ref, vT_ref):
    part = x_ref.shape[1] // PROJ_PARTS
    for p in range(PROJ_PARTS):
        tok = slice(p * part, (p + 1) * part)
        hb = _norm_mod(x_ref[0, tok, :], mod_ref[0], g1_ref[...]).astype(BF16)
        qp = _dot(hb, wq_ref[...])
        kvp = _dot(hb, wkv_ref[...])
        f = _dot(hb, wf_ref[...]).astype(BF16)

        qnT = (_rms_lanes(qp, Q_LORA) * qg_ref[...]).T.astype(BF16)
        kvnT, krT = _kv_latents(kvp, kvg_ref)

        def gate_chunk(c):
            cols = slice(c * GATE_CHUNK, (c + 1) * GATE_CHUNK)
            g_ref[0, tok, cols] = jax.nn.sigmoid(
                _dot(hb, wg_ref[c]) + bg_ref[:, cols]).astype(g_ref.dtype)

        gate_chunk(0)

        qT = _dot(wqbT_ref[...], qnT)
        knT = _dot(wkbT_ref[...], kvnT)
        vT = _dot(wvT_ref[...], kvnT)

        _q_heads(qT, qgain_ref, cos_ref, sin_ref, qT_ref, tok)
        _kv_heads(knT, vT, krT, kgn_ref, kgr_ref, cos_ref, sin_ref, k_ref, vT_ref, tok)
        for c in range(1, g_ref.shape[2] // GATE_CHUNK):
            gate_chunk(c)

        for g in range(N_GROUPS):
            cols = slice(g * GROUP_W, (g + 1) * GROUP_W)
            z = _dot(f[:, cols], cs_ref[...])
            z_ref[0, 0, tok, cols] = z[:, :GROUP_W].astype(z_ref.dtype)
            z_ref[0, 1, tok, cols] = z[:, GROUP_W:].astype(z_ref.dtype)


def _full(shape):
    nd = len(shape)
    return pl.BlockSpec(shape, lambda *_: (0,) * nd, pipeline_mode=pl.Buffered(1))


def _ctx_kv(ctx, mod3, g1, wkv, kvg, wkbT, wvT, kgn, kgr, cosT, sinT):
    b, t, d = ctx.shape
    return pl.pallas_call(
        _ctx_body,
        grid=(b,),
        in_specs=[pl.BlockSpec((1, t, d), lambda i: (i, 0, 0)),
                  pl.BlockSpec((1, 6, d), lambda i: (b, 0, 0)),
                  _full(g1.shape), _full(wkv.shape), _full(kvg.shape), _full(wkbT.shape),
                  _full(wvT.shape), _full(kgn.shape), _full(kgr.shape),
                  _full(cosT.shape), _full(sinT.shape)],
        out_specs=[pl.BlockSpec((1, N_HEADS, t, HEAD_PAD), lambda i: (i, 0, 0, 0)),
                   pl.BlockSpec((1, N_HEADS, V_ROWS, t), lambda i: (i, 0, 0, 0))],
        out_shape=[jax.ShapeDtypeStruct((b, N_HEADS, t, HEAD_PAD), BF16),
                   jax.ShapeDtypeStruct((b, N_HEADS, V_ROWS, t), BF16)],
        compiler_params=_params("arbitrary"),
        name="ctx_kv",
    )(ctx, mod3, g1, wkv, kvg, wkbT, wvT, kgn, kgr, cosT, sinT)


def _proj(x, mod3, g1, wf, cs, wg, bg, wq, qg, wqbT, qgain, wkv, kvg, wkbT, wvT, kgn, kgr,
          cosT, sinT, ts):
    b, s, d = x.shape
    assert wg.shape == (2 * d // GATE_CHUNK, d, GATE_CHUNK)
    tok = lambda shape: pl.BlockSpec(shape, lambda i, j: (0, j))
    return pl.pallas_call(
        _proj_body,
        grid=(b, s // ts),
        in_specs=[pl.BlockSpec((1, ts, d), lambda i, j: (i, j, 0)),
                  pl.BlockSpec((1, 6, d), lambda i, j: (i, 0, 0)),
                  _full(g1.shape), _full(wf.shape), _full(cs.shape), _full(wg.shape),
                  _full(bg.shape), _full(wq.shape), _full(qg.shape), _full(wqbT.shape),
                  tok((HEAD_PAD, ts)),
                  _full(wkv.shape), _full(kvg.shape), _full(wkbT.shape), _full(wvT.shape),
                  tok((QK_NOPE, ts)), tok((QK_ROPE, ts)), tok((QK_ROPE, ts)), tok((QK_ROPE, ts))],
        out_specs=[pl.BlockSpec((1, 2, ts, FOURIER_W), lambda i, j: (i, 0, j, 0)),
                   pl.BlockSpec((1, ts, 2 * d), lambda i, j: (i, j, 0)),
                   pl.BlockSpec((1, N_HEADS, HEAD_PAD, ts), lambda i, j: (i, 0, 0, j)),
                   pl.BlockSpec((1, N_HEADS, ts, HEAD_PAD), lambda i, j: (i, 0, j, 0)),
                   pl.BlockSpec((1, N_HEADS, V_ROWS, ts), lambda i, j: (i, 0, 0, j))],
        out_shape=[jax.ShapeDtypeStruct((b, 2, s, FOURIER_W), BF16),
                   jax.ShapeDtypeStruct((b, s, 2 * d), BF16),
                   jax.ShapeDtypeStruct((b, N_HEADS, HEAD_PAD, s), BF16),
                   jax.ShapeDtypeStruct((b, N_HEADS, s, HEAD_PAD), BF16),
                   jax.ShapeDtypeStruct((b, N_HEADS, V_ROWS, s), BF16)],
        compiler_params=_params("arbitrary", "arbitrary"),
        name="proj",
    )(x, mod3, g1, wf, cs, wg, bg, wq, qg, wqbT, qgain, wkv, kvg, wkbT, wvT, kgn, kgr,
      cosT, sinT)


def _dft_body(g1_ref, g2_ref, gb_ref, z_ref, o_ref, a_scr):
    step_rows = DFT_BLK * DFT_STEP
    nstep = RADIX // step_rows
    j = pl.program_id(1)
    r8 = DFT_RADIX_A
    slab = (2, r8, DFT_ABLK, FOURIER_W)

    @pl.when(j < nstep)
    def _():
        for t in range(step_rows // DFT_ABLK):
            rows = slice(t * DFT_ABLK, (t + 1) * DFT_ABLK)
            off = pl.multiple_of(j * step_rows + t * DFT_ABLK, DFT_ABLK)
            t1 = []
            for a2 in range(r8):
                z = z_ref[0, :, :, a2, rows, :].reshape(2 * r8 * DFT_ABLK, FOURIER_W)
                t1.append(_dot(g1_ref[...], z).reshape(slab))
            for k2a in range(r8):
                x = jnp.concatenate([t1[a2][r, k2a] for r in range(2) for a2 in range(r8)], axis=0)
                a = _dot(g2_ref[k2a], x.astype(BF16)).reshape(slab)
                for r in range(2):
                    for k2b in range(r8):
                        a_scr[r, k2a + r8 * k2b, pl.ds(off, DFT_ABLK), :] = a[r, k2b]

    @pl.when(j >= nstep)
    def _():
        for t in range(DFT_STEP):
            blk = (j - nstep) * DFT_STEP + t
            off = pl.multiple_of(blk * DFT_BLK, DFT_BLK)
            a = a_scr[:, pl.ds(off, DFT_BLK), :, :]
            a = a.reshape(2 * DFT_BLK * RADIX, FOURIER_W).astype(BF16)
            y = _dot(gb_ref[blk], a)
            o_ref[0, :, t * DFT_BLK:(t + 1) * DFT_BLK, :] = y.reshape(RADIX, DFT_BLK, FOURIER_W)


def _dft(g1, g2, gb, z6):
    b = z6.shape[0]
    nstep = RADIX // (DFT_BLK * DFT_STEP)
    rows = DFT_BLK * DFT_STEP
    r8 = DFT_RADIX_A
    assert rows % DFT_ABLK == 0 and z6.shape[2:4] == (r8, r8)
    return pl.pallas_call(
        _dft_body,
        grid=(b, 2 * nstep),
        in_specs=[_full(g1.shape), _full(g2.shape), _full(gb.shape),
                  pl.BlockSpec((1, 2, r8, r8, rows, FOURIER_W),
                               lambda i, j: (i, 0, 0, 0, jnp.minimum(j, nstep - 1), 0))],
        out_specs=pl.BlockSpec((1, RADIX, rows, FOURIER_W),
                               lambda i, j: (i, 0, jnp.maximum(j - nstep, 0), 0)),
        out_shape=jax.ShapeDtypeStruct((b, RADIX, RADIX, FOURIER_W), F32),
        scratch_shapes=[pltpu.VMEM((2, RADIX, RADIX, FOURIER_W), F32)],
        compiler_params=_params("arbitrary", "arbitrary"),
        name="dft",
    )(g1, g2, gb, z6)


def _attn_body(qT_ref, k_ref, vT_ref, kc_ref, vcT_ref, *rest, tq, tkb, n_cast):
    for src, dst in zip(rest[:n_cast], rest[n_cast + 1:]):
        dst[...] = src[...].astype(dst.dtype).reshape(dst.shape)
    o_ref = rest[n_cast]
    nkb = k_ref.shape[2] // tkb
    nblk = nkb + 1

    def colmax(s):
        r = s.shape[0] // 8
        return jnp.max(jnp.max(s.reshape(8, r, s.shape[1]), axis=0), axis=0, keepdims=True)

    def update(carry, s, vT_blk):
        m, acc = carry
        m_new = jnp.maximum(m, colmax(s))
        p = jnp.exp2(s - m_new).astype(BF16)
        acc = jnp.exp2(m - m_new) * acc + _dot(vT_blk, p)
        return m_new, acc

    nq = qT_ref.shape[3] // tq

    def k_blk(i):
        return k_ref[0, 0, i * tkb:(i + 1) * tkb, :] if i < nkb else kc_ref[0, 0]

    def vT_blk(i):
        return vT_ref[0, 0, :, i * tkb:(i + 1) * tkb] if i < nkb else vcT_ref[0, 0]

    def q_cols(qi):
        return qT_ref[0, 0, :, pl.ds(pl.multiple_of(qi * tq, tq), tq)]

    def q_tile(qi, pending):
        qT = q_cols(qi)
        qT_next = q_cols(jnp.minimum(qi + 1, nq - 1))
        pending = list(pending)
        carry = (jnp.full((1, tq), -jnp.inf, F32), jnp.zeros((V_ROWS, tq), F32))
        for i in range(nblk):
            j = i + ATTN_AHEAD
            pending.append(_dot(k_blk(j), qT) if j < nblk else _dot(k_blk(j - nblk), qT_next))
            carry = update(carry, pending.pop(0), vT_blk(i))
        _, acc = carry
        o_ref[0, :, pl.ds(pl.multiple_of(qi * tq, tq), tq)] = (
            acc[:V_DIM] / acc[V_DIM:V_DIM + 1]).astype(o_ref.dtype)
        return tuple(pending)

    lax.fori_loop(0, nq, q_tile, tuple(_dot(k_blk(i), q_cols(0)) for i in range(ATTN_AHEAD)),
                  unroll=ATTN_UNROLL)


def _attention(qT, k, vT, kc, vcT, weights, tq, tkb):
    b, h, _, s = qT.shape
    sc = kc.shape[2]
    steps = b * h
    w_in_specs, w_out_specs, w_out_shapes = [], [], []
    for w, axis, chunk in weights:
        blk = tuple(n // steps if a == axis else n for a, n in enumerate(w.shape))
        spec = pl.BlockSpec(blk, (lambda i, j: (i * h + j, 0)) if axis == 0
                            else (lambda i, j: (0, i * h + j)))
        w_in_specs.append(spec)
        if chunk is None:
            w_out_specs.append(spec)
            w_out_shapes.append(jax.ShapeDtypeStruct(w.shape, BF16))
        else:
            per = chunk // blk[1]
            w_out_specs.append(pl.BlockSpec(
                (1,) + blk, lambda i, j, per=per: ((i * h + j) // per, 0, (i * h + j) % per)))
            w_out_shapes.append(jax.ShapeDtypeStruct((w.shape[1] // chunk, w.shape[0], chunk), BF16))
    outs = pl.pallas_call(
        functools.partial(_attn_body, tq=tq, tkb=tkb, n_cast=len(weights)),
        grid=(b, h),
        in_specs=[pl.BlockSpec((1, 1, HEAD_PAD, s), lambda i, j: (i, j, 0, 0)),
                  pl.BlockSpec((1, 1, s, HEAD_PAD), lambda i, j: (i, j, 0, 0)),
                  pl.BlockSpec((1, 1, V_ROWS, s), lambda i, j: (i, j, 0, 0)),
                  pl.BlockSpec((1, 1, sc, HEAD_PAD), lambda i, j: (i, j, 0, 0)),
                  pl.BlockSpec((1, 1, V_ROWS, sc), lambda i, j: (i, j, 0, 0))] + w_in_specs,
        out_specs=[pl.BlockSpec((1, V_DIM, s), lambda i, j: (i, j, 0))] + w_out_specs,
        out_shape=[jax.ShapeDtypeStruct((b, h * V_DIM, s), BF16)] + w_out_shapes,
        compiler_params=_params("arbitrary", "arbitrary"),
        name="attention",
    )(qT, k, vT, kc, vcT, *[w for w, _, _ in weights])
    return outs[0], outs[1:]


def _post_body(fm_ref, aT_ref, g_ref, x_ref, mod_ref, wfo_ref, wao_ref, wout_ref,
               g2_ref, wup_ref, wdn_ref, o_ref):
    d = x_ref.shape[-1]
    m = mod_ref[0]
    part = x_ref.shape[1] // POST_PARTS
    toks = [slice(p * part, (p + 1) * part) for p in range(POST_PARTS)]
    x1s, hbs = [], []
    for tok in toks:
        yf = _dot(fm_ref[0, tok, :].astype(BF16), wfo_ref[...])
        ya = lax.dot_general(aT_ref[0, :, tok], wao_ref[...], (((0,), (0,)), ((), ())),
                             preferred_element_type=F32)
        g = g_ref[0, tok, :]
        y = g[:, :d].astype(F32) * yf + g[:, d:].astype(F32) * ya
        x1 = x_ref[0, tok, :] + m[2:3] * _dot(y.astype(BF16), wout_ref[...])
        x1s.append(x1)
        hbs.append(_norm_mod(x1, m[3:5], g2_ref[...]).astype(BF16))
    for tok, x1, hb in zip(toks, x1s, hbs):
        acc = jnp.zeros(x1.shape, F32)
        for c in range(wup_ref.shape[0]):
            u = jnp.maximum(_dot(hb, wup_ref[c]), 0.0)
            acc = acc + _dot((u * u).astype(BF16), wdn_ref[c])
        o_ref[0, tok, :] = x1 + m[5:6] * acc


def _post(fm, aT, gates, x, mod3, wfo, wao, wout, g2, wup, wdn, ts):
    b, s, d = x.shape
    return pl.pallas_call(
        _post_body,
        grid=(b, s // ts),
        in_specs=[pl.BlockSpec((1, ts, FOURIER_W), lambda i, j: (i, j, 0)),
                  pl.BlockSpec((1, aT.shape[1], ts), lambda i, j: (i, 0, j)),
                  pl.BlockSpec((1, ts, 2 * d), lambda i, j: (i, j, 0)),
                  pl.BlockSpec((1, ts, d), lambda i, j: (i, j, 0)),
                  pl.BlockSpec((1, 6, d), lambda i, j: (i, 0, 0)),
                  _full(wfo.shape), _full(wao.shape), _full(wout.shape),
                  _full(g2.shape), _full(wup.shape), _full(wdn.shape)],
        out_specs=pl.BlockSpec((1, ts, d), lambda i, j: (i, j, 0)),
        out_shape=jax.ShapeDtypeStruct((b, s, d), F32),
        compiler_params=_params("arbitrary", "arbitrary"),
        name="post",
    )(fm, aT, gates, x, mod3, wfo, wao, wout, g2, wup, wdn)


@functools.lru_cache(maxsize=None)
def _dft_tables():
    j = np.arange(GROUP_W)
    ang = 2.0 * np.pi * np.outer(j, j) / GROUP_W
    cs = np.concatenate([np.cos(ang), -np.sin(ang)], axis=1) / math.sqrt(GROUP_W)

    r8 = DFT_RADIX_A
    d = np.arange(r8)
    eye = np.eye(DFT_ABLK)

    def real_block(theta):
        c, s = np.cos(theta), np.sin(theta)
        return np.kron(np.block([[c, s], [-s, c]]) / math.sqrt(r8), eye)

    g1 = real_block(2.0 * np.pi * np.outer(d, d) / r8)
    g2 = np.stack([real_block(2.0 * np.pi * (k2a * d[None, :] / RADIX + np.outer(d, d) / r8))
                   for k2a in range(r8)])

    n = RADIX * RADIX
    k1 = np.arange(RADIX)[:, None, None]
    k2 = np.arange(RADIX)[None, :, None]
    bb = np.arange(RADIX)[None, None, :]
    th = 2.0 * np.pi * ((bb * (RADIX * k1 + k2)) % n) / n
    gc = np.cos(th) / math.sqrt(RADIX)
    gs = np.sin(th) / math.sqrt(RADIX)
    nblk = RADIX // DFT_BLK
    gtab = np.zeros((nblk, RADIX, DFT_BLK, 2, DFT_BLK, RADIX), np.float32)
    for jb in range(nblk):
        for q in range(DFT_BLK):
            gtab[jb, :, q, 0, q, :] = gc[:, jb * DFT_BLK + q, :]
            gtab[jb, :, q, 1, q, :] = gs[:, jb * DFT_BLK + q, :]
    gtab = gtab.reshape(nblk, RADIX * DFT_BLK, 2 * DFT_BLK * RADIX)
    return cs.astype(np.float32), g1.astype(np.float32), g2.astype(np.float32), gtab


def _rope_tables_t(n):
    rows = n // GRID_W
    row = jnp.repeat(jnp.arange(rows, dtype=F32), GRID_W)
    col = jnp.tile(jnp.arange(GRID_W, dtype=F32), rows)
    n_freq = QK_ROPE // 4
    freqs = ROPE_THETA ** (-jnp.arange(n_freq, dtype=F32) / n_freq)
    ang_r = row[:, None] * freqs[None, :]
    ang_c = col[:, None] * freqs[None, :]
    ang = jnp.concatenate([ang_r, ang_r, ang_c, ang_c], axis=-1)
    return jnp.cos(ang).T, jnp.sin(ang).T


def kernel(x, c, ctx, c_ctx, ada_w, ada_b, norm1_g, norm2_g, w_in, b_gate, w_fourier, q_norm_g, w_qb,
           kv_norm_g, w_kvb, q_gain, k_gain, w_mla_o, w_out, w_up, w_down):
    bsz, s, d = x.shape
    sc = ctx.shape[1]
    assert ada_w.shape[0] == 1 and s == RADIX * RADIX and d == D_MODEL

    cc = jnp.zeros((8, d), F32).at[:bsz].set(c).at[bsz].set(c_ctx)
    mod, wf, wq, wkv, wg = _adaln(cc, ada_w[0], ada_b[0][None, :], w_in[0].T)
    mod3 = mod.reshape(8, 6, d)

    wqbT = jnp.pad(w_qb[0].reshape(Q_LORA, N_HEADS, HEAD_QK),
                   ((0, 0), (0, 0), (0, HEAD_PAD - HEAD_QK))).reshape(Q_LORA, -1).T.astype(BF16)
    wkvb = w_kvb[0].reshape(KV_LORA, N_HEADS, QK_NOPE + V_DIM)
    wkbT = wkvb[:, :, :QK_NOPE].reshape(KV_LORA, -1).T.astype(BF16)
    wvT = wkvb[:, :, QK_NOPE:].reshape(KV_LORA, -1).T.astype(BF16)

    g1 = norm1_g[0][None, :]
    g2 = norm2_g[0][None, :]
    qg = q_norm_g[0][None, :]
    kvg = kv_norm_g[0][None, :]
    bg = b_gate[0][None, :]
    qscale = (HEAD_QK ** -0.5) * math.log2(math.e)
    qgain = jnp.broadcast_to(jnp.pad(q_gain[0] * qscale, (0, HEAD_PAD - HEAD_QK))[:, None], (HEAD_PAD, s))
    kgn = jnp.broadcast_to(k_gain[0][:QK_NOPE, None], (QK_NOPE, s))
    kgr = jnp.broadcast_to(k_gain[0][QK_NOPE:, None], (QK_ROPE, s))
    cosT, sinT = _rope_tables_t(s)
    cos1 = jnp.ones((QK_ROPE, sc), F32)
    sin0 = jnp.zeros((QK_ROPE, sc), F32)

    cs, ga1, ga2, gtab = (jnp.asarray(t).astype(BF16) for t in _dft_tables())

    kc, vcT = _ctx_kv(ctx, mod3, g1, wkv, kvg, wkbT, wvT, kgn[:, :sc], kgr[:, :sc], cos1, sin0)

    z, gates, qT, k, vT = _proj(x, mod3, g1, wf, cs, wg, bg, wq, qg, wqbT, qgain, wkv, kvg, wkbT, wvT,
                                kgn, kgr, cosT, sinT, PROJ_TILE)

    r8 = DFT_RADIX_A
    fm = _dft(ga1, ga2, gtab, z.reshape(bsz, 2, r8, r8, RADIX, FOURIER_W)).reshape(bsz, s, FOURIER_W)

    aT, (wfo, wao, wout, wup, wdn) = _attention(
        qT, k, vT, kc, vcT,
        [(w_fourier[0], 0, None), (w_mla_o[0], 0, None), (w_out[0], 0, None),
         (w_up[0], 1, FF_CHUNK), (w_down[0], 0, None)],
        ATTN_TQ, ATTN_TK)
    wdn = wdn.reshape(-1, FF_CHUNK, d)

    return _post(fm, aT, gates, x, mod3, wfo, wao, wout, g2, wup, wdn, POST_TILE)
```

```python
import functools
import math

import numpy as np
import jax
import jax.numpy as jnp
from jax import lax
from jax.experimental import pallas as pl
from jax.experimental.pallas import tpu as pltpu

F32 = jnp.float32
BF16 = jnp.bfloat16

D_MODEL = 1024
GRID_W = 64
N_GROUPS = 4
GROUP_W = 128
FOURIER_W = N_GROUPS * GROUP_W
N_HEADS = 8
QK_NOPE = 64
QK_ROPE = 32
HEAD_QK = QK_NOPE + QK_ROPE
HEAD_PAD = 128
V_DIM = 64
V_ROWS = V_DIM + 16
Q_LORA = 256
KV_LORA = 128
W_IN_GATES = FOURIER_W + Q_LORA + KV_LORA + QK_ROPE
ROPE_THETA = 10000.0
EPS = 1e-6
RADIX = 64
DFT_RADIX_A = 8
DFT_ABLK = 16
DFT_BLK = 8
DFT_STEP = 2
POST_TILE = 512
POST_PARTS = 2
PROJ_TILE = 1024
PROJ_PARTS = 4
W_IN_BLK = 256
GATE_CHUNK = W_IN_BLK
FF_CHUNK = 1024
ATTN_TQ = 512
ATTN_TK = 256
ATTN_AHEAD = 2
ATTN_UNROLL = 4

V7X_VMEM_LIMIT = 56 * 1024 * 1024


def _params(*sem):
    return pltpu.CompilerParams(dimension_semantics=sem, vmem_limit_bytes=V7X_VMEM_LIMIT)


def _dot(a, b):
    return jnp.dot(a, b, preferred_element_type=F32)


def _rms_lanes(x, n):
    return x * lax.rsqrt(jnp.sum(x * x, axis=-1, keepdims=True) * (1.0 / n) + EPS)


def _rot_half_rows(x):
    return jnp.concatenate([-x[8:16], x[0:8], -x[24:32], x[16:24]], axis=0)


def _adaln_body(c_ref, w_ref, b_ref, tf_ref, tq_ref, tkv_ref, tga_ref, tgb_ref,
                o_ref, wf_ref, wq_ref, wkv_ref, wg_ref):
    c = c_ref[...]
    s = c * jax.nn.sigmoid(c)
    o_ref[...] = _dot(s, w_ref[...]) + b_ref[...]

    @pl.when(pl.program_id(0) == 0)
    def _():
        wf_ref[...] = tf_ref[...].T.astype(BF16)
        wq_ref[...] = tq_ref[...].T.astype(BF16)
        wkv_ref[...] = tkv_ref[...].T.astype(BF16)

    off = W_IN_GATES % W_IN_BLK
    both = jnp.concatenate([tga_ref[...], tgb_ref[...]], axis=0)
    wg_ref[0] = both[off:off + W_IN_BLK].T.astype(BF16)


def _adaln(cc, ada_w, ada_b, w_inT):
    rows, d = cc.shape
    n = ada_w.shape[1]
    n_gates = w_inT.shape[0] - W_IN_GATES
    steps = n_gates // W_IN_BLK
    tn = n // steps
    g0 = W_IN_GATES // W_IN_BLK
    blk = lambda r, f: pl.BlockSpec((r, d), f)
    return pl.pallas_call(
        _adaln_body,
        grid=(steps,),
        in_specs=[pl.BlockSpec((rows, d), lambda j: (0, 0)),
                  pl.BlockSpec((d, tn), lambda j: (0, j)),
                  pl.BlockSpec((1, tn), lambda j: (0, j)),
                  blk(FOURIER_W, lambda j: (0, 0)),
                  blk(Q_LORA, lambda j: (FOURIER_W // Q_LORA, 0)),
                  blk(2 * KV_LORA, lambda j: ((FOURIER_W + Q_LORA) // (2 * KV_LORA), 0)),
                  blk(W_IN_BLK, lambda j: (g0 + j, 0)),
                  blk(W_IN_BLK, lambda j: (g0 + 1 + j, 0))],
        out_specs=[pl.BlockSpec((rows, tn), lambda j: (0, j)),
                   pl.BlockSpec((d, FOURIER_W), lambda j: (0, 0)),
                   pl.BlockSpec((d, Q_LORA), lambda j: (0, 0)),
                   pl.BlockSpec((d, 2 * KV_LORA), lambda j: (0, 0)),
                   pl.BlockSpec((1, d, W_IN_BLK), lambda j: (j, 0, 0))],
        out_shape=[jax.ShapeDtypeStruct((rows, n), F32),
                   jax.ShapeDtypeStruct((d, FOURIER_W), BF16),
                   jax.ShapeDtypeStruct((d, Q_LORA), BF16),
                   jax.ShapeDtypeStruct((d, 2 * KV_LORA), BF16),
                   jax.ShapeDtypeStruct((steps, d, W_IN_BLK), BF16)],
        compiler_params=_params("arbitrary"),
        name="adaln",
    )(cc, ada_w, ada_b, w_inT, w_inT, w_inT, w_inT, w_inT)


def _mod_rows(mod_ref, row):
    m = mod_ref[pl.ds(row, 1), :]
    d = m.shape[1] // 6
    return jnp.concatenate([m[:, k * d:(k + 1) * d] for k in range(6)], axis=0)


def _norm_mod(x, m, g):
    ms = jnp.sum(x * x, axis=-1, keepdims=True) * (1.0 / x.shape[-1])
    return (x * lax.rsqrt(ms + EPS)) * (g * (1.0 + m[1:2])) + m[0:1]


def _kv_latents(kvp, kvg_ref):
    kvn = _rms_lanes(kvp[:, :KV_LORA], KV_LORA) * kvg_ref[...]
    return kvn.T.astype(BF16), kvp[:, KV_LORA:].T[0:QK_ROPE]


def _kv_heads(knT, vT, krT, kgn_ref, kgr_ref, cos_ref, sin_ref, k_ref, vT_ref, tok):
    t = knT.shape[1]
    sr = jnp.sum(krT * krT, axis=0, keepdims=True)
    krg = krT * kgr_ref[:, tok]
    kr_rot = krg * cos_ref[:, tok] + _rot_half_rows(krg) * sin_ref[:, tok]
    kgn = kgn_ref[:, tok]
    ones = jnp.ones((V_ROWS - V_DIM, t), F32)
    zpad = jnp.zeros((HEAD_PAD - HEAD_QK, t), F32)
    for h in range(N_HEADS):
        kn = knT[h * QK_NOPE:(h + 1) * QK_NOPE]
        ss = jnp.sum(kn * kn, axis=0, keepdims=True) + sr
        r = lax.rsqrt(ss * (1.0 / HEAD_QK) + EPS)
        kT = jnp.concatenate([kn * kgn * r, kr_rot * r, zpad], axis=0)
        k_ref[0, h, tok, :] = kT.T.astype(k_ref.dtype)
        vh = vT[h * V_DIM:(h + 1) * V_DIM]
        vT_ref[0, h, :, tok] = jnp.concatenate([vh, ones], axis=0).astype(vT_ref.dtype)


def _q_heads(qT, qgain_ref, cos_ref, sin_ref, qT_ref, tok):
    qgain, cos, sin = qgain_ref[:, tok], cos_ref[:, tok], sin_ref[:, tok]
    for h in range(N_HEADS):
        qh = qT[h * HEAD_PAD:(h + 1) * HEAD_PAD]
        r = lax.rsqrt(jnp.sum(qh * qh, axis=0, keepdims=True) * (1.0 / HEAD_QK) + EPS)
        qh = qh * qgain * r
        qr = qh[QK_NOPE:HEAD_QK]
        qr = qr * cos + _rot_half_rows(qr) * sin
        qT_ref[0, h, :, tok] = jnp.concatenate([qh[:QK_NOPE], qr, qh[HEAD_QK:]],
                                               axis=0).astype(qT_ref.dtype)


def _ctx_body(x_ref, mod_ref, g1_ref, wkv_ref, kvg_ref, wkbT_ref, wvT_ref, kgn_ref, kgr_ref,
              cos_ref, sin_ref, k_ref, vT_ref):
    ctx_row = pl.num_programs(0)
    hb = _norm_mod(x_ref[0], _mod_rows(mod_ref, ctx_row), g1_ref[...]).astype(BF16)
    kvnT, krT = _kv_latents(_dot(hb, wkv_ref[...]), kvg_ref)
    _kv_heads(_dot(wkbT_ref[...], kvnT), _dot(wvT_ref[...], kvnT), krT,
              kgn_ref, kgr_ref, cos_ref, sin_ref, k_ref, vT_ref, slice(None))


def _proj_body(x_ref, mod_ref, g1_ref, wf_ref, cs_ref, wg_ref, bg_ref,
               wq_ref, qg_ref, wqbT_ref, qgain_ref,
               wkv_ref, kvg_ref, wkbT_ref, wvT_ref, kgn_ref, kgr_ref, cos_ref, sin_ref,
               z_ref, g_ref, qT_ref, k_ref, vT_ref):
    part = x_ref.shape[1] // PROJ_PARTS
    m = _mod_rows(mod_ref, pl.program_id(0))
    for p in range(PROJ_PARTS):
        tok = slice(p * part, (p + 1) * part)
        hb = _norm_mod(x_ref[0, tok, :], m, g1_ref[...]).astype(BF16)
        qp = _dot(hb, wq_ref[...])
        kvp = _dot(hb, wkv_ref[...])
        f = _dot(hb, wf_ref[...]).astype(BF16)

        qnT = (_rms_lanes(qp, Q_LORA) * qg_ref[...]).T.astype(BF16)
        kvnT, krT = _kv_latents(kvp, kvg_ref)

        def gate_chunk(c):
            cols = slice(c * GATE_CHUNK, (c + 1) * GATE_CHUNK)
            g_ref[0, tok, cols] = jax.nn.sigmoid(
                _dot(hb, wg_ref[c]) + bg_ref[:, cols]).astype(g_ref.dtype)

        gate_chunk(0)

        qT = _dot(wqbT_ref[...], qnT)
        knT = _dot(wkbT_ref[...], kvnT)
        vT = _dot(wvT_ref[...], kvnT)

        _q_heads(qT, qgain_ref, cos_ref, sin_ref, qT_ref, tok)
        _kv_heads(knT, vT, krT, kgn_ref, kgr_ref, cos_ref, sin_ref, k_ref, vT_ref, tok)
        for c in range(1, g_ref.shape[2] // GATE_CHUNK):
            gate_chunk(c)

        for g in range(N_GROUPS):
            cols = slice(g * GROUP_W, (g + 1) * GROUP_W)
            z = _dot(f[:, cols], cs_ref[...])
            z_ref[0, 0, tok, cols] = z[:, :GROUP_W].astype(z_ref.dtype)
            z_ref[0, 1, tok, cols] = z[:, GROUP_W:].astype(z_ref.dtype)


def _full(shape):
    nd = len(shape)
    return pl.BlockSpec(shape, lambda *_: (0,) * nd, pipeline_mode=pl.Buffered(1))


def _ctx_kv(ctx, mod, g1, wkv, kvg, wkbT, wvT, kgn, kgr, cosT, sinT):
    b, t, d = ctx.shape
    return pl.pallas_call(
        _ctx_body,
        grid=(b,),
        in_specs=[pl.BlockSpec((1, t, d), lambda i: (i, 0, 0)),
                  _full(mod.shape),
                  _full(g1.shape), _full(wkv.shape), _full(kvg.shape), _full(wkbT.shape),
                  _full(wvT.shape), _full(kgn.shape), _full(kgr.shape),
                  _full(cosT.shape), _full(sinT.shape)],
        out_specs=[pl.BlockSpec((1, N_HEADS, t, HEAD_PAD), lambda i: (i, 0, 0, 0)),
                   pl.BlockSpec((1, N_HEADS, V_ROWS, t), lambda i: (i, 0, 0, 0))],
        out_shape=[jax.ShapeDtypeStruct((b, N_HEADS, t, HEAD_PAD), BF16),
                   jax.ShapeDtypeStruct((b, N_HEADS, V_ROWS, t), BF16)],
        compiler_params=_params("arbitrary"),
        name="ctx_kv",
    )(ctx, mod, g1, wkv, kvg, wkbT, wvT, kgn, kgr, cosT, sinT)


def _proj(x, mod, g1, wf, cs, wg, bg, wq, qg, wqbT, qgain, wkv, kvg, wkbT, wvT, kgn, kgr,
          cosT, sinT, ts):
    b, s, d = x.shape
    assert wg.shape == (2 * d // GATE_CHUNK, d, GATE_CHUNK)
    tok = lambda shape: pl.BlockSpec(shape, lambda i, j: (0, j))
    return pl.pallas_call(
        _proj_body,
        grid=(b, s // ts),
        in_specs=[pl.BlockSpec((1, ts, d), lambda i, j: (i, j, 0)),
                  _full(mod.shape),
                  _full(g1.shape), _full(wf.shape), _full(cs.shape), _full(wg.shape),
                  _full(bg.shape), _full(wq.shape), _full(qg.shape), _full(wqbT.shape),
                  tok((HEAD_PAD, ts)),
                  _full(wkv.shape), _full(kvg.shape), _full(wkbT.shape), _full(wvT.shape),
                  tok((QK_NOPE, ts)), tok((QK_ROPE, ts)), tok((QK_ROPE, ts)), tok((QK_ROPE, ts))],
        out_specs=[pl.BlockSpec((1, 2, ts, FOURIER_W), lambda i, j: (i, 0, j, 0)),
                   pl.BlockSpec((1, ts, 2 * d), lambda i, j: (i, j, 0)),
                   pl.BlockSpec((1, N_HEADS, HEAD_PAD, ts), lambda i, j: (i, 0, 0, j)),
                   pl.BlockSpec((1, N_HEADS, ts, HEAD_PAD), lambda i, j: (i, 0, j, 0)),
                   pl.BlockSpec((1, N_HEADS, V_ROWS, ts), lambda i, j: (i, 0, 0, j))],
        out_shape=[jax.ShapeDtypeStruct((b, 2, s, FOURIER_W), BF16),
                   jax.ShapeDtypeStruct((b, s, 2 * d), BF16),
                   jax.ShapeDtypeStruct((b, N_HEADS, HEAD_PAD, s), BF16),
                   jax.ShapeDtypeStruct((b, N_HEADS, s, HEAD_PAD), BF16),
                   jax.ShapeDtypeStruct((b, N_HEADS, V_ROWS, s), BF16)],
        compiler_params=_params("arbitrary", "arbitrary"),
        name="proj",
    )(x, mod, g1, wf, cs, wg, bg, wq, qg, wqbT, qgain, wkv, kvg, wkbT, wvT, kgn, kgr,
      cosT, sinT)


def _dft_body(g1_ref, g2_ref, gb_ref, z_ref, o_ref, a_scr):
    step_rows = DFT_BLK * DFT_STEP
    nstep = RADIX // step_rows
    j = pl.program_id(1)
    r8 = DFT_RADIX_A
    slab = (2, r8, DFT_ABLK, FOURIER_W)

    @pl.when(j < nstep)
    def _():
        for t in range(step_rows // DFT_ABLK):
            rows = slice(t * DFT_ABLK, (t + 1) * DFT_ABLK)
            off = pl.multiple_of(j * step_rows + t * DFT_ABLK, DFT_ABLK)
            t1 = []
            for a2 in range(r8):
                z = z_ref[0, :, :, a2, rows, :].reshape(2 * r8 * DFT_ABLK, FOURIER_W)
                t1.append(_dot(g1_ref[...], z).reshape(slab))
            for k2a in range(r8):
                x = jnp.concatenate([t1[a2][r, k2a] for r in range(2) for a2 in range(r8)], axis=0)
                a = _dot(g2_ref[k2a], x.astype(BF16)).reshape(slab)
                for r in range(2):
                    for k2b in range(r8):
                        a_scr[r, k2a + r8 * k2b, pl.ds(off, DFT_ABLK), :] = a[r, k2b]

    @pl.when(j >= nstep)
    def _():
        for t in range(DFT_STEP):
            blk = (j - nstep) * DFT_STEP + t
            off = pl.multiple_of(blk * DFT_BLK, DFT_BLK)
            a = a_scr[:, pl.ds(off, DFT_BLK), :, :]
            a = a.reshape(2 * DFT_BLK * RADIX, FOURIER_W).astype(BF16)
            y = _dot(gb_ref[blk], a)
            o_ref[0, :, t * DFT_BLK:(t + 1) * DFT_BLK, :] = y.reshape(RADIX, DFT_BLK, FOURIER_W)


def _dft(g1, g2, gb, z6):
    b = z6.shape[0]
    nstep = RADIX // (DFT_BLK * DFT_STEP)
    rows = DFT_BLK * DFT_STEP
    r8 = DFT_RADIX_A
    assert rows % DFT_ABLK == 0 and z6.shape[2:4] == (r8, r8)
    return pl.pallas_call(
        _dft_body,
        grid=(b, 2 * nstep),
        in_specs=[_full(g1.shape), _full(g2.shape), _full(gb.shape),
                  pl.BlockSpec((1, 2, r8, r8, rows, FOURIER_W),
                               lambda i, j: (i, 0, 0, 0, jnp.minimum(j, nstep - 1), 0))],
        out_specs=pl.BlockSpec((1, RADIX, rows, FOURIER_W),
                               lambda i, j: (i, 0, jnp.maximum(j - nstep, 0), 0)),
        out_shape=jax.ShapeDtypeStruct((b, RADIX, RADIX, FOURIER_W), F32),
        scratch_shapes=[pltpu.VMEM((2, RADIX, RADIX, FOURIER_W), F32)],
        compiler_params=_params("arbitrary", "arbitrary"),
        name="dft",
    )(g1, g2, gb, z6)


def _attn_body(qT_ref, k_ref, vT_ref, kc_ref, vcT_ref, *rest, tq, tkb, n_cast):
    for src, dst in zip(rest[:n_cast], rest[n_cast + 1:]):
        dst[...] = src[...].astype(dst.dtype).reshape(dst.shape)
    o_ref = rest[n_cast]
    nkb = k_ref.shape[2] // tkb
    nblk = nkb + 1

    def colmax(s):
        r = s.shape[0] // 8
        return jnp.max(jnp.max(s.reshape(8, r, s.shape[1]), axis=0), axis=0, keepdims=True)

    def update(carry, s, vT_blk):
        m, acc = carry
        m_new = jnp.maximum(m, colmax(s))
        p = jnp.exp2(s - m_new).astype(BF16)
        acc = jnp.exp2(m - m_new) * acc + _dot(vT_blk, p)
        return m_new, acc

    nq = qT_ref.shape[3] // tq

    def k_blk(i):
        return k_ref[0, 0, i * tkb:(i + 1) * tkb, :] if i < nkb else kc_ref[0, 0]

    def vT_blk(i):
        return vT_ref[0, 0, :, i * tkb:(i + 1) * tkb] if i < nkb else vcT_ref[0, 0]

    def q_cols(qi):
        return qT_ref[0, 0, :, pl.ds(pl.multiple_of(qi * tq, tq), tq)]

    def q_tile(qi, pending):
        qT = q_cols(qi)
        qT_next = q_cols(jnp.minimum(qi + 1, nq - 1))
        pending = list(pending)
        carry = (jnp.full((1, tq), -jnp.inf, F32), jnp.zeros((V_ROWS, tq), F32))
        for i in range(nblk):
            j = i + ATTN_AHEAD
            pending.append(_dot(k_blk(j), qT) if j < nblk else _dot(k_blk(j - nblk), qT_next))
            carry = update(carry, pending.pop(0), vT_blk(i))
        _, acc = carry
        o_ref[0, :, pl.ds(pl.multiple_of(qi * tq, tq), tq)] = (
            acc[:V_DIM] / acc[V_DIM:V_DIM + 1]).astype(o_ref.dtype)
        return tuple(pending)

    lax.fori_loop(0, nq, q_tile, tuple(_dot(k_blk(i), q_cols(0)) for i in range(ATTN_AHEAD)),
                  unroll=ATTN_UNROLL)


def _attention(qT, k, vT, kc, vcT, weights, tq, tkb):
    b, h, _, s = qT.shape
    sc = kc.shape[2]
    steps = b * h
    w_in_specs, w_out_specs, w_out_shapes = [], [], []
    for w, axis, chunk in weights:
        blk = tuple(n // steps if a == axis else n for a, n in enumerate(w.shape))
        spec = pl.BlockSpec(blk, (lambda i, j: (i * h + j, 0)) if axis == 0
                            else (lambda i, j: (0, i * h + j)))
        w_in_specs.append(spec)
        if chunk is None:
            w_out_specs.append(spec)
            w_out_shapes.append(jax.ShapeDtypeStruct(w.shape, BF16))
        else:
            per = chunk // blk[1]
            w_out_specs.append(pl.BlockSpec(
                (1,) + blk, lambda i, j, per=per: ((i * h + j) // per, 0, (i * h + j) % per)))
            w_out_shapes.append(jax.ShapeDtypeStruct((w.shape[1] // chunk, w.shape[0], chunk), BF16))
    outs = pl.pallas_call(
        functools.partial(_attn_body, tq=tq, tkb=tkb, n_cast=len(weights)),
        grid=(b, h),
        in_specs=[pl.BlockSpec((1, 1, HEAD_PAD, s), lambda i, j: (i, j, 0, 0)),
                  pl.BlockSpec((1, 1, s, HEAD_PAD), lambda i, j: (i, j, 0, 0)),
                  pl.BlockSpec((1, 1, V_ROWS, s), lambda i, j: (i, j, 0, 0)),
                  pl.BlockSpec((1, 1, sc, HEAD_PAD), lambda i, j: (i, j, 0, 0)),
                  pl.BlockSpec((1, 1, V_ROWS, sc), lambda i, j: (i, j, 0, 0))] + w_in_specs,
        out_specs=[pl.BlockSpec((1, V_DIM, s), lambda i, j: (i, j, 0))] + w_out_specs,
        out_shape=[jax.ShapeDtypeStruct((b, h * V_DIM, s), BF16)] + w_out_shapes,
        compiler_params=_params("arbitrary", "arbitrary"),
        name="attention",
    )(qT, k, vT, kc, vcT, *[w for w, _, _ in weights])
    return outs[0], outs[1:]


def _post_body(fm_ref, aT_ref, g_ref, x_ref, mod_ref, wfo_ref, wao_ref, wout_ref,
               g2_ref, wup_ref, wdn_ref, o_ref):
    d = x_ref.shape[-1]
    m = _mod_rows(mod_ref, pl.program_id(0))
    part = x_ref.shape[1] // POST_PARTS
    toks = [slice(p * part, (p + 1) * part) for p in range(POST_PARTS)]
    x1s, hbs = [], []
    for tok in toks:
        yf = _dot(fm_ref[0, tok, :].astype(BF16), wfo_ref[...])
        ya = lax.dot_general(aT_ref[0, :, tok], wao_ref[...], (((0,), (0,)), ((), ())),
                             preferred_element_type=F32)
        g = g_ref[0, tok, :]
        y = g[:, :d].astype(F32) * yf + g[:, d:].astype(F32) * ya
        x1 = x_ref[0, tok, :] + m[2:3] * _dot(y.astype(BF16), wout_ref[...])
        x1s.append(x1)
        hbs.append(_norm_mod(x1, m[3:5], g2_ref[...]).astype(BF16))
    for tok, x1, hb in zip(toks, x1s, hbs):
        acc = jnp.zeros(x1.shape, F32)
        for c in range(wup_ref.shape[0]):
            u = jnp.maximum(_dot(hb, wup_ref[c]), 0.0)
            acc = acc + _dot((u * u).astype(BF16), wdn_ref[c])
        o_ref[0, tok, :] = x1 + m[5:6] * acc


def _post(fm, aT, gates, x, mod, wfo, wao, wout, g2, wup, wdn, ts):
    b, s, d = x.shape
    return pl.pallas_call(
        _post_body,
        grid=(b, s // ts),
        in_specs=[pl.BlockSpec((1, ts, FOURIER_W), lambda i, j: (i, j, 0)),
                  pl.BlockSpec((1, aT.shape[1], ts), lambda i, j: (i, 0, j)),
                  pl.BlockSpec((1, ts, 2 * d), lambda i, j: (i, j, 0)),
                  pl.BlockSpec((1, ts, d), lambda i, j: (i, j, 0)),
                  _full(mod.shape),
                  _full(wfo.shape), _full(wao.shape), _full(wout.shape),
                  _full(g2.shape), _full(wup.shape), _full(wdn.shape)],
        out_specs=pl.BlockSpec((1, ts, d), lambda i, j: (i, j, 0)),
        out_shape=jax.ShapeDtypeStruct((b, s, d), F32),
        compiler_params=_params("arbitrary", "arbitrary"),
        name="post",
    )(fm, aT, gates, x, mod, wfo, wao, wout, g2, wup, wdn)


@functools.lru_cache(maxsize=None)
def _dft_tables():
    j = np.arange(GROUP_W)
    ang = 2.0 * np.pi * np.outer(j, j) / GROUP_W
    cs = np.concatenate([np.cos(ang), -np.sin(ang)], axis=1) / math.sqrt(GROUP_W)

    r8 = DFT_RADIX_A
    d = np.arange(r8)
    eye = np.eye(DFT_ABLK)

    def real_block(theta):
        c, s = np.cos(theta), np.sin(theta)
        return np.kron(np.block([[c, s], [-s, c]]) / math.sqrt(r8), eye)

    g1 = real_block(2.0 * np.pi * np.outer(d, d) / r8)
    g2 = np.stack([real_block(2.0 * np.pi * (k2a * d[None, :] / RADIX + np.outer(d, d) / r8))
                   for k2a in range(r8)])

    n = RADIX * RADIX
    k1 = np.arange(RADIX)[:, None, None]
    k2 = np.arange(RADIX)[None, :, None]
    bb = np.arange(RADIX)[None, None, :]
    th = 2.0 * np.pi * ((bb * (RADIX * k1 + k2)) % n) / n
    gc = np.cos(th) / math.sqrt(RADIX)
    gs = np.sin(th) / math.sqrt(RADIX)
    nblk = RADIX // DFT_BLK
    gtab = np.zeros((nblk, RADIX, DFT_BLK, 2, DFT_BLK, RADIX), np.float32)
    for jb in range(nblk):
        for q in range(DFT_BLK):
            gtab[jb, :, q, 0, q, :] = gc[:, jb * DFT_BLK + q, :]
            gtab[jb, :, q, 1, q, :] = gs[:, jb * DFT_BLK + q, :]
    gtab = gtab.reshape(nblk, RADIX * DFT_BLK, 2 * DFT_BLK * RADIX)
    return cs.astype(np.float32), g1.astype(np.float32), g2.astype(np.float32), gtab


@functools.lru_cache(maxsize=None)
def _rope_tables_t(n):
    rows = n // GRID_W
    row = np.repeat(np.arange(rows, dtype=np.float64), GRID_W)
    col = np.tile(np.arange(GRID_W, dtype=np.float64), rows)
    n_freq = QK_ROPE // 4
    freqs = ROPE_THETA ** (-np.arange(n_freq, dtype=np.float64) / n_freq)
    ang_r = row[:, None] * freqs[None, :]
    ang_c = col[:, None] * freqs[None, :]
    ang = np.concatenate([ang_r, ang_r, ang_c, ang_c], axis=-1)
    return np.cos(ang).T.astype(np.float32), np.sin(ang).T.astype(np.float32)


def kernel(x, c, ctx, c_ctx, ada_w, ada_b, norm1_g, norm2_g, w_in, b_gate, w_fourier, q_norm_g, w_qb,
           kv_norm_g, w_kvb, q_gain, k_gain, w_mla_o, w_out, w_up, w_down):
    bsz, s, d = x.shape
    sc = ctx.shape[1]
    assert ada_w.shape[0] == 1 and s == RADIX * RADIX and d == D_MODEL

    cc = jnp.zeros((8, d), F32).at[:bsz].set(c).at[bsz].set(c_ctx)
    mod, wf, wq, wkv, wg = _adaln(cc, ada_w[0], ada_b[0][None, :], w_in[0].T)

    wqbT = jnp.pad(w_qb[0].reshape(Q_LORA, N_HEADS, HEAD_QK),
                   ((0, 0), (0, 0), (0, HEAD_PAD - HEAD_QK))).reshape(Q_LORA, -1).T.astype(BF16)
    wkvb = w_kvb[0].reshape(KV_LORA, N_HEADS, QK_NOPE + V_DIM)
    wkbT = wkvb[:, :, :QK_NOPE].reshape(KV_LORA, -1).T.astype(BF16)
    wvT = wkvb[:, :, QK_NOPE:].reshape(KV_LORA, -1).T.astype(BF16)

    g1 = norm1_g[0][None, :]
    g2 = norm2_g[0][None, :]
    qg = q_norm_g[0][None, :]
    kvg = kv_norm_g[0][None, :]
    bg = b_gate[0][None, :]
    qscale = (HEAD_QK ** -0.5) * math.log2(math.e)
    qgain = jnp.broadcast_to(jnp.pad(q_gain[0] * qscale, (0, HEAD_PAD - HEAD_QK))[:, None], (HEAD_PAD, s))
    kgn = jnp.broadcast_to(k_gain[0][:QK_NOPE, None], (QK_NOPE, s))
    kgr = jnp.broadcast_to(k_gain[0][QK_NOPE:, None], (QK_ROPE, s))
    cosT, sinT = (jnp.asarray(t) for t in _rope_tables_t(s))
    cos1 = jnp.ones((QK_ROPE, sc), F32)
    sin0 = jnp.zeros((QK_ROPE, sc), F32)

    cs, ga1, ga2, gtab = (jnp.asarray(t).astype(BF16) for t in _dft_tables())

    kc, vcT = _ctx_kv(ctx, mod, g1, wkv, kvg, wkbT, wvT, kgn[:, :sc], kgr[:, :sc], cos1, sin0)

    z, gates, qT, k, vT = _proj(x, mod, g1, wf, cs, wg, bg, wq, qg, wqbT, qgain, wkv, kvg, wkbT, wvT,
                                kgn, kgr, cosT, sinT, PROJ_TILE)

    r8 = DFT_RADIX_A
    fm = _dft(ga1, ga2, gtab, z.reshape(bsz, 2, r8, r8, RADIX, FOURIER_W)).reshape(bsz, s, FOURIER_W)

    aT, (wfo, wao, wout, wup, wdn) = _attention(
        qT, k, vT, kc, vcT,
        [(w_fourier[0], 0, None), (w_mla_o[0], 0, None), (w_out[0], 0, None),
         (w_up[0], 1, FF_CHUNK), (w_down[0], 0, None)],
        ATTN_TQ, ATTN_TK)
    wdn = wdn.reshape(-1, FF_CHUNK, d)

    return _post(fm, aT, gates, x, mod, wfo, wao, wout, g2, wup, wdn, POST_TILE)
```

```python
import functools
import math

import numpy as np
import jax
import jax.numpy as jnp
from jax import lax
from jax.experimental import pallas as pl
from jax.experimental.pallas import tpu as pltpu

F32 = jnp.float32
BF16 = jnp.bfloat16

D_MODEL = 1024
GRID_W = 64
N_GROUPS = 4
GROUP_W = 128
FOURIER_W = N_GROUPS * GROUP_W
N_HEADS = 8
QK_NOPE = 64
QK_ROPE = 32
HEAD_QK = QK_NOPE + QK_ROPE
LANES = 128
HEAD_PAD = LANES
V_DIM = 64
V_ROWS = V_DIM + 16
Q_LORA = 256
KV_LORA = 128
W_IN_GATES = FOURIER_W + Q_LORA + KV_LORA + QK_ROPE
ROPE_THETA = 10000.0
EPS = 1e-6
RADIX = 64
DFT_RADIX_A = 8
DFT_ABLK = 16
DFT_BLK = 8
DFT_STEP = 2
POST_TILE = 512
POST_PARTS = 2
PROJ_TILE = 1024
PROJ_PARTS = 4
W_IN_BLK = 256
GATE_CHUNK = W_IN_BLK
FF_CHUNK = 1024
ATTN_TQ = 512
ATTN_TK = 256
ATTN_AHEAD = 2
ATTN_UNROLL = 4

V7X_VMEM_LIMIT = 56 * 1024 * 1024


def _params(*sem):
    return pltpu.CompilerParams(dimension_semantics=sem, vmem_limit_bytes=V7X_VMEM_LIMIT)


def _dot(a, b):
    return jnp.dot(a, b, preferred_element_type=F32)


def _rms_lanes(x, n):
    return x * lax.rsqrt(jnp.sum(x * x, axis=-1, keepdims=True) * (1.0 / n) + EPS)


def _rot_half_rows(x):
    return jnp.concatenate([-x[8:16], x[0:8], -x[24:32], x[16:24]], axis=0)


def _adaln_body(c_ref, w_ref, b_ref, tf_ref, tq_ref, tkv_ref, tga_ref, tgb_ref,
                o_ref, wf_ref, wq_ref, wkv_ref, wg_ref):
    c = c_ref[...]
    s = c * jax.nn.sigmoid(c)
    o_ref[...] = _dot(s, w_ref[...]) + b_ref[...]

    @pl.when(pl.program_id(0) == 0)
    def _():
        wf_ref[...] = tf_ref[...].T.astype(BF16)
        wq_ref[...] = tq_ref[...].T.astype(BF16)
        wkv_ref[...] = tkv_ref[...].T.astype(BF16)

    off = W_IN_GATES % W_IN_BLK
    both = jnp.concatenate([tga_ref[...], tgb_ref[...]], axis=0)
    wg_ref[0] = both[off:off + W_IN_BLK].T.astype(BF16)


def _adaln(cc, ada_w, ada_b, w_inT):
    rows, d = cc.shape
    n = ada_w.shape[1]
    n_gates = w_inT.shape[0] - W_IN_GATES
    steps = n_gates // W_IN_BLK
    tn = n // steps
    g0 = W_IN_GATES // W_IN_BLK
    blk = lambda r, f: pl.BlockSpec((r, d), f)
    return pl.pallas_call(
        _adaln_body,
        grid=(steps,),
        in_specs=[pl.BlockSpec((rows, d), lambda j: (0, 0)),
                  pl.BlockSpec((d, tn), lambda j: (0, j)),
                  pl.BlockSpec((1, tn), lambda j: (0, j)),
                  blk(FOURIER_W, lambda j: (0, 0)),
                  blk(Q_LORA, lambda j: (FOURIER_W // Q_LORA, 0)),
                  blk(2 * KV_LORA, lambda j: ((FOURIER_W + Q_LORA) // (2 * KV_LORA), 0)),
                  blk(W_IN_BLK, lambda j: (g0 + j, 0)),
                  blk(W_IN_BLK, lambda j: (g0 + 1 + j, 0))],
        out_specs=[pl.BlockSpec((rows, tn), lambda j: (0, j)),
                   pl.BlockSpec((d, FOURIER_W), lambda j: (0, 0)),
                   pl.BlockSpec((d, Q_LORA), lambda j: (0, 0)),
                   pl.BlockSpec((d, 2 * KV_LORA), lambda j: (0, 0)),
                   pl.BlockSpec((1, d, W_IN_BLK), lambda j: (j, 0, 0))],
        out_shape=[jax.ShapeDtypeStruct((rows, n), F32),
                   jax.ShapeDtypeStruct((d, FOURIER_W), BF16),
                   jax.ShapeDtypeStruct((d, Q_LORA), BF16),
                   jax.ShapeDtypeStruct((d, 2 * KV_LORA), BF16),
                   jax.ShapeDtypeStruct((steps, d, W_IN_BLK), BF16)],
        compiler_params=_params("arbitrary"),
        name="adaln",
    )(cc, ada_w, ada_b, w_inT, w_inT, w_inT, w_inT, w_inT)


def _mod_rows(mod_ref, row):
    m = mod_ref[pl.ds(row, 1), :]
    d = m.shape[1] // 6
    return jnp.concatenate([m[:, k * d:(k + 1) * d] for k in range(6)], axis=0)


def _norm_mod(x, m, g):
    ms = jnp.sum(x * x, axis=-1, keepdims=True) * (1.0 / x.shape[-1])
    return (x * lax.rsqrt(ms + EPS)) * (g * (1.0 + m[1:2])) + m[0:1]


def _kv_latents(kvp, kvg_ref):
    kvn = _rms_lanes(kvp[:, :KV_LORA], KV_LORA) * kvg_ref[...]
    return kvn.T.astype(BF16), kvp[:, KV_LORA:].T[0:QK_ROPE]


def _lanes(ref, t):
    return jnp.concatenate([ref[...]] * (t // LANES), axis=1)


def _kv_heads(knT, vT, krT, kgn_ref, kgr_ref, cos_ref, sin_ref, k_ref, vT_ref, tok):
    t = knT.shape[1]
    sr = jnp.sum(krT * krT, axis=0, keepdims=True)
    krg = krT * _lanes(kgr_ref, t)
    kr_rot = krg * cos_ref[:, tok] + _rot_half_rows(krg) * sin_ref[:, tok]
    kgn = _lanes(kgn_ref, t)
    ones = jnp.ones((V_ROWS - V_DIM, t), F32)
    zpad = jnp.zeros((HEAD_PAD - HEAD_QK, t), F32)
    for h in range(N_HEADS):
        kn = knT[h * QK_NOPE:(h + 1) * QK_NOPE]
        ss = jnp.sum(kn * kn, axis=0, keepdims=True) + sr
        r = lax.rsqrt(ss * (1.0 / HEAD_QK) + EPS)
        kT = jnp.concatenate([kn * kgn * r, kr_rot * r, zpad], axis=0)
        k_ref[0, h, tok, :] = kT.T.astype(k_ref.dtype)
        vh = vT[h * V_DIM:(h + 1) * V_DIM]
        vT_ref[0, h, :, tok] = jnp.concatenate([vh, ones], axis=0).astype(vT_ref.dtype)


def _q_heads(qT, qgain_ref, cos_ref, sin_ref, qT_ref, tok):
    qgain, cos, sin = _lanes(qgain_ref, qT.shape[1]), cos_ref[:, tok], sin_ref[:, tok]
    for h in range(N_HEADS):
        qh = qT[h * HEAD_PAD:(h + 1) * HEAD_PAD]
        r = lax.rsqrt(jnp.sum(qh * qh, axis=0, keepdims=True) * (1.0 / HEAD_QK) + EPS)
        qh = qh * qgain * r
        qr = qh[QK_NOPE:HEAD_QK]
        qr = qr * cos + _rot_half_rows(qr) * sin
        qT_ref[0, h, :, tok] = jnp.concatenate([qh[:QK_NOPE], qr, qh[HEAD_QK:]],
                                               axis=0).astype(qT_ref.dtype)


def _ctx_body(x_ref, mod_ref, g1_ref, wkv_ref, kvg_ref, wkbT_ref, wvT_ref, kgn_ref, kgr_ref,
              cos_ref, sin_ref, k_ref, vT_ref):
    ctx_row = pl.num_programs(0)
    hb = _norm_mod(x_ref[0], _mod_rows(mod_ref, ctx_row), g1_ref[...]).astype(BF16)
    kvnT, krT = _kv_latents(_dot(hb, wkv_ref[...]), kvg_ref)
    _kv_heads(_dot(wkbT_ref[...], kvnT), _dot(wvT_ref[...], kvnT), krT,
              kgn_ref, kgr_ref, cos_ref, sin_ref, k_ref, vT_ref, slice(None))


def _proj_body(x_ref, mod_ref, g1_ref, wf_ref, cs_ref, wg_ref, bg_ref,
               wq_ref, qg_ref, wqbT_ref, qgain_ref,
               wkv_ref, kvg_ref, wkbT_ref, wvT_ref, kgn_ref, kgr_ref, cos_ref, sin_ref,
               z_ref, g_ref, qT_ref, k_ref, vT_ref):
    part = x_ref.shape[1] // PROJ_PARTS
    m = _mod_rows(mod_ref, pl.program_id(0))
    for p in range(PROJ_PARTS):
        tok = slice(p * part, (p + 1) * part)
        hb = _norm_mod(x_ref[0, tok, :], m, g1_ref[...]).astype(BF16)
        qp = _dot(hb, wq_ref[...])
        kvp = _dot(hb, wkv_ref[...])
        f = _dot(hb, wf_ref[...]).astype(BF16)

        qnT = (_rms_lanes(qp, Q_LORA) * qg_ref[...]).T.astype(BF16)
        kvnT, krT = _kv_latents(kvp, kvg_ref)

        def gate_chunk(c):
            cols = slice(c * GATE_CHUNK, (c + 1) * GATE_CHUNK)
            g_ref[0, tok, cols] = jax.nn.sigmoid(
                _dot(hb, wg_ref[c]) + bg_ref[:, cols]).astype(g_ref.dtype)

        gate_chunk(0)

        qT = _dot(wqbT_ref[...], qnT)
        knT = _dot(wkbT_ref[...], kvnT)
        vT = _dot(wvT_ref[...], kvnT)

        _q_heads(qT, qgain_ref, cos_ref, sin_ref, qT_ref, tok)
        _kv_heads(knT, vT, krT, kgn_ref, kgr_ref, cos_ref, sin_ref, k_ref, vT_ref, tok)
        for c in range(1, g_ref.shape[2] // GATE_CHUNK):
            gate_chunk(c)

        for g in range(N_GROUPS):
            cols = slice(g * GROUP_W, (g + 1) * GROUP_W)
            z = _dot(f[:, cols], cs_ref[...])
            z_ref[0, 0, tok, cols] = z[:, :GROUP_W].astype(z_ref.dtype)
            z_ref[0, 1, tok, cols] = z[:, GROUP_W:].astype(z_ref.dtype)


def _full(shape):
    nd = len(shape)
    return pl.BlockSpec(shape, lambda *_: (0,) * nd, pipeline_mode=pl.Buffered(1))


def _ctx_kv(ctx, mod, g1, wkv, kvg, wkbT, wvT, kgn, kgr, cosT, sinT):
    b, t, d = ctx.shape
    return pl.pallas_call(
        _ctx_body,
        grid=(b,),
        in_specs=[pl.BlockSpec((1, t, d), lambda i: (i, 0, 0)),
                  _full(mod.shape),
                  _full(g1.shape), _full(wkv.shape), _full(kvg.shape), _full(wkbT.shape),
                  _full(wvT.shape), _full(kgn.shape), _full(kgr.shape),
                  _full(cosT.shape), _full(sinT.shape)],
        out_specs=[pl.BlockSpec((1, N_HEADS, t, HEAD_PAD), lambda i: (i, 0, 0, 0)),
                   pl.BlockSpec((1, N_HEADS, V_ROWS, t), lambda i: (i, 0, 0, 0))],
        out_shape=[jax.ShapeDtypeStruct((b, N_HEADS, t, HEAD_PAD), BF16),
                   jax.ShapeDtypeStruct((b, N_HEADS, V_ROWS, t), BF16)],
        compiler_params=_params("arbitrary"),
        name="ctx_kv",
    )(ctx, mod, g1, wkv, kvg, wkbT, wvT, kgn, kgr, cosT, sinT)


def _proj(x, mod, g1, wf, cs, wg, bg, wq, qg, wqbT, qgain, wkv, kvg, wkbT, wvT, kgn, kgr,
          cosT, sinT, ts):
    b, s, d = x.shape
    assert wg.shape == (2 * d // GATE_CHUNK, d, GATE_CHUNK)
    tok = lambda shape: pl.BlockSpec(shape, lambda i, j: (0, j))
    return pl.pallas_call(
        _proj_body,
        grid=(b, s // ts),
        in_specs=[pl.BlockSpec((1, ts, d), lambda i, j: (i, j, 0)),
                  _full(mod.shape),
                  _full(g1.shape), _full(wf.shape), _full(cs.shape), _full(wg.shape),
                  _full(bg.shape), _full(wq.shape), _full(qg.shape), _full(wqbT.shape),
                  _full(qgain.shape),
                  _full(wkv.shape), _full(kvg.shape), _full(wkbT.shape), _full(wvT.shape),
                  _full(kgn.shape), _full(kgr.shape), tok((QK_ROPE, ts)), tok((QK_ROPE, ts))],
        out_specs=[pl.BlockSpec((1, 2, ts, FOURIER_W), lambda i, j: (i, 0, j, 0)),
                   pl.BlockSpec((1, ts, 2 * d), lambda i, j: (i, j, 0)),
                   pl.BlockSpec((1, N_HEADS, HEAD_PAD, ts), lambda i, j: (i, 0, 0, j)),
                   pl.BlockSpec((1, N_HEADS, ts, HEAD_PAD), lambda i, j: (i, 0, j, 0)),
                   pl.BlockSpec((1, N_HEADS, V_ROWS, ts), lambda i, j: (i, 0, 0, j))],
        out_shape=[jax.ShapeDtypeStruct((b, 2, s, FOURIER_W), BF16),
                   jax.ShapeDtypeStruct((b, s, 2 * d), BF16),
                   jax.ShapeDtypeStruct((b, N_HEADS, HEAD_PAD, s), BF16),
                   jax.ShapeDtypeStruct((b, N_HEADS, s, HEAD_PAD), BF16),
                   jax.ShapeDtypeStruct((b, N_HEADS, V_ROWS, s), BF16)],
        compiler_params=_params("arbitrary", "arbitrary"),
        name="proj",
    )(x, mod, g1, wf, cs, wg, bg, wq, qg, wqbT, qgain, wkv, kvg, wkbT, wvT, kgn, kgr,
      cosT, sinT)


def _dft_body(g1_ref, g2_ref, gb_ref, z_ref, o_ref, a_scr):
    step_rows = DFT_BLK * DFT_STEP
    nstep = RADIX // step_rows
    j = pl.program_id(1)
    r8 = DFT_RADIX_A
    slab = (2, r8, DFT_ABLK, FOURIER_W)

    @pl.when(j < nstep)
    def _():
        for t in range(step_rows // DFT_ABLK):
            rows = slice(t * DFT_ABLK, (t + 1) * DFT_ABLK)
            off = pl.multiple_of(j * step_rows + t * DFT_ABLK, DFT_ABLK)
            t1 = []
            for a2 in range(r8):
                z = z_ref[0, :, :, a2, rows, :].reshape(2 * r8 * DFT_ABLK, FOURIER_W)
                t1.append(_dot(g1_ref[...], z).reshape(slab))
            for k2a in range(r8):
                x = jnp.concatenate([t1[a2][r, k2a] for r in range(2) for a2 in range(r8)], axis=0)
                a = _dot(g2_ref[k2a], x.astype(BF16)).reshape(slab)
                for r in range(2):
                    for k2b in range(r8):
                        a_scr[r, k2a + r8 * k2b, pl.ds(off, DFT_ABLK), :] = a[r, k2b]

    @pl.when(j >= nstep)
    def _():
        ys = []
        for t in range(DFT_STEP):
            blk = (j - nstep) * DFT_STEP + t
            off = pl.multiple_of(blk * DFT_BLK, DFT_BLK)
            a = a_scr[:, pl.ds(off, DFT_BLK), :, :]
            a = a.reshape(2 * DFT_BLK * RADIX, FOURIER_W).astype(BF16)
            y = _dot(gb_ref[blk], a)
            ys.append(y.reshape(RADIX, DFT_BLK, FOURIER_W))
        o_ref[0] = jnp.concatenate(ys, axis=1).astype(o_ref.dtype)


def _dft(g1, g2, gb, z6):
    b = z6.shape[0]
    nstep = RADIX // (DFT_BLK * DFT_STEP)
    rows = DFT_BLK * DFT_STEP
    r8 = DFT_RADIX_A
    assert rows % DFT_ABLK == 0 and z6.shape[2:4] == (r8, r8)
    return pl.pallas_call(
        _dft_body,
        grid=(b, 2 * nstep),
        in_specs=[_full(g1.shape), _full(g2.shape), _full(gb.shape),
                  pl.BlockSpec((1, 2, r8, r8, rows, FOURIER_W),
                               lambda i, j: (i, 0, 0, 0, jnp.minimum(j, nstep - 1), 0))],
        out_specs=pl.BlockSpec((1, RADIX, rows, FOURIER_W),
                               lambda i, j: (i, 0, jnp.maximum(j - nstep, 0), 0)),
        out_shape=jax.ShapeDtypeStruct((b, RADIX, RADIX, FOURIER_W), BF16),
        scratch_shapes=[pltpu.VMEM((2, RADIX, RADIX, FOURIER_W), F32)],
        compiler_params=_params("arbitrary", "arbitrary"),
        name="dft",
    )(g1, g2, gb, z6)


def _attn_body(qT_ref, k_ref, vT_ref, kc_ref, vcT_ref, *rest, tq, tkb, n_cast):
    for src, dst in zip(rest[:n_cast], rest[n_cast + 1:]):
        dst[...] = src[...].astype(dst.dtype).reshape(dst.shape)
    o_ref = rest[n_cast]
    nkb = k_ref.shape[2] // tkb
    nblk = nkb + 1

    def colmax(s):
        r = s.shape[0] // 8
        return jnp.max(jnp.max(s.reshape(8, r, s.shape[1]), axis=0), axis=0, keepdims=True)

    def update(carry, s, vT_blk):
        m, acc = carry
        m_new = jnp.maximum(m, colmax(s))
        p = jnp.exp2(s - m_new).astype(BF16)
        acc = jnp.exp2(m - m_new) * acc + _dot(vT_blk, p)
        return m_new, acc

    nq = qT_ref.shape[3] // tq

    def k_blk(i):
        return k_ref[0, 0, i * tkb:(i + 1) * tkb, :] if i < nkb else kc_ref[0, 0]

    def vT_blk(i):
        return vT_ref[0, 0, :, i * tkb:(i + 1) * tkb] if i < nkb else vcT_ref[0, 0]

    def q_cols(qi):
        return qT_ref[0, 0, :, pl.ds(pl.multiple_of(qi * tq, tq), tq)]

    def q_tile(qi, pending):
        qT = q_cols(qi)
        qT_next = q_cols(jnp.minimum(qi + 1, nq - 1))
        pending = list(pending)
        carry = (jnp.full((1, tq), -jnp.inf, F32), jnp.zeros((V_ROWS, tq), F32))
        for i in range(nblk):
            j = i + ATTN_AHEAD
            pending.append(_dot(k_blk(j), qT) if j < nblk else _dot(k_blk(j - nblk), qT_next))
            carry = update(carry, pending.pop(0), vT_blk(i))
        _, acc = carry
        o_ref[0, :, pl.ds(pl.multiple_of(qi * tq, tq), tq)] = (
            acc[:V_DIM] / acc[V_DIM:V_DIM + 1]).astype(o_ref.dtype)
        return tuple(pending)

    lax.fori_loop(0, nq, q_tile, tuple(_dot(k_blk(i), q_cols(0)) for i in range(ATTN_AHEAD)),
                  unroll=ATTN_UNROLL)


def _attention(qT, k, vT, kc, vcT, weights, tq, tkb):
    b, h, _, s = qT.shape
    sc = kc.shape[2]
    steps = b * h
    w_in_specs, w_out_specs, w_out_shapes = [], [], []
    for w, axis, chunk in weights:
        blk = tuple(n // steps if a == axis else n for a, n in enumerate(w.shape))
        spec = pl.BlockSpec(blk, (lambda i, j: (i * h + j, 0)) if axis == 0
                            else (lambda i, j: (0, i * h + j)))
        w_in_specs.append(spec)
        if chunk is None:
            w_out_specs.append(spec)
            w_out_shapes.append(jax.ShapeDtypeStruct(w.shape, BF16))
        else:
            per = chunk // blk[1]
            w_out_specs.append(pl.BlockSpec(
                (1,) + blk, lambda i, j, per=per: ((i * h + j) // per, 0, (i * h + j) % per)))
            w_out_shapes.append(jax.ShapeDtypeStruct((w.shape[1] // chunk, w.shape[0], chunk), BF16))
    outs = pl.pallas_call(
        functools.partial(_attn_body, tq=tq, tkb=tkb, n_cast=len(weights)),
        grid=(b, h),
        in_specs=[pl.BlockSpec((1, 1, HEAD_PAD, s), lambda i, j: (i, j, 0, 0)),
                  pl.BlockSpec((1, 1, s, HEAD_PAD), lambda i, j: (i, j, 0, 0)),
                  pl.BlockSpec((1, 1, V_ROWS, s), lambda i, j: (i, j, 0, 0)),
                  pl.BlockSpec((1, 1, sc, HEAD_PAD), lambda i, j: (i, j, 0, 0)),
                  pl.BlockSpec((1, 1, V_ROWS, sc), lambda i, j: (i, j, 0, 0))] + w_in_specs,
        out_specs=[pl.BlockSpec((1, V_DIM, s), lambda i, j: (i, j, 0))] + w_out_specs,
        out_shape=[jax.ShapeDtypeStruct((b, h * V_DIM, s), BF16)] + w_out_shapes,
        compiler_params=_params("arbitrary", "arbitrary"),
        name="attention",
    )(qT, k, vT, kc, vcT, *[w for w, _, _ in weights])
    return outs[0], outs[1:]


def _post_body(fm_ref, aT_ref, g_ref, x_ref, mod_ref, wfo_ref, wao_ref, wout_ref,
               g2_ref, wup_ref, wdn_ref, o_ref):
    d = x_ref.shape[-1]
    m = _mod_rows(mod_ref, pl.program_id(0))
    part = x_ref.shape[1] // POST_PARTS
    toks = [slice(p * part, (p + 1) * part) for p in range(POST_PARTS)]
    x1s, hbs = [], []
    for tok in toks:
        yf = _dot(fm_ref[0, tok, :], wfo_ref[...])
        ya = lax.dot_general(aT_ref[0, :, tok], wao_ref[...], (((0,), (0,)), ((), ())),
                             preferred_element_type=F32)
        g = g_ref[0, tok, :]
        y = g[:, :d].astype(F32) * yf + g[:, d:].astype(F32) * ya
        x1 = x_ref[0, tok, :] + m[2:3] * _dot(y.astype(BF16), wout_ref[...])
        x1s.append(x1)
        hbs.append(_norm_mod(x1, m[3:5], g2_ref[...]).astype(BF16))
    for tok, x1, hb in zip(toks, x1s, hbs):
        acc = jnp.zeros(x1.shape, F32)
        for c in range(wup_ref.shape[0]):
            u = jnp.maximum(_dot(hb, wup_ref[c]), 0.0)
            acc = acc + _dot((u * u).astype(BF16), wdn_ref[c])
        o_ref[0, tok, :] = x1 + m[5:6] * acc


def _post(fm, aT, gates, x, mod, wfo, wao, wout, g2, wup, wdn, ts):
    b, s, d = x.shape
    return pl.pallas_call(
        _post_body,
        grid=(b, s // ts),
        in_specs=[pl.BlockSpec((1, ts, FOURIER_W), lambda i, j: (i, j, 0)),
                  pl.BlockSpec((1, aT.shape[1], ts), lambda i, j: (i, 0, j)),
                  pl.BlockSpec((1, ts, 2 * d), lambda i, j: (i, j, 0)),
                  pl.BlockSpec((1, ts, d), lambda i, j: (i, j, 0)),
                  _full(mod.shape),
                  _full(wfo.shape), _full(wao.shape), _full(wout.shape),
                  _full(g2.shape), _full(wup.shape), _full(wdn.shape)],
        out_specs=pl.BlockSpec((1, ts, d), lambda i, j: (i, j, 0)),
        out_shape=jax.ShapeDtypeStruct((b, s, d), F32),
        compiler_params=_params("arbitrary", "arbitrary"),
        name="post",
    )(fm, aT, gates, x, mod, wfo, wao, wout, g2, wup, wdn)


@functools.lru_cache(maxsize=None)
def _dft_tables():
    j = np.arange(GROUP_W)
    ang = 2.0 * np.pi * np.outer(j, j) / GROUP_W
    cs = np.concatenate([np.cos(ang), -np.sin(ang)], axis=1) / math.sqrt(GROUP_W)

    r8 = DFT_RADIX_A
    d = np.arange(r8)
    eye = np.eye(DFT_ABLK)

    def real_block(theta):
        c, s = np.cos(theta), np.sin(theta)
        return np.kron(np.block([[c, s], [-s, c]]) / math.sqrt(r8), eye)

    g1 = real_block(2.0 * np.pi * np.outer(d, d) / r8)
    g2 = np.stack([real_block(2.0 * np.pi * (k2a * d[None, :] / RADIX + np.outer(d, d) / r8))
                   for k2a in range(r8)])

    n = RADIX * RADIX
    k1 = np.arange(RADIX)[:, None, None]
    k2 = np.arange(RADIX)[None, :, None]
    bb = np.arange(RADIX)[None, None, :]
    th = 2.0 * np.pi * ((bb * (RADIX * k1 + k2)) % n) / n
    gc = np.cos(th) / math.sqrt(RADIX)
    gs = np.sin(th) / math.sqrt(RADIX)
    nblk = RADIX // DFT_BLK
    gtab = np.zeros((nblk, RADIX, DFT_BLK, 2, DFT_BLK, RADIX), np.float32)
    for jb in range(nblk):
        for q in range(DFT_BLK):
            gtab[jb, :, q, 0, q, :] = gc[:, jb * DFT_BLK + q, :]
            gtab[jb, :, q, 1, q, :] = gs[:, jb * DFT_BLK + q, :]
    gtab = gtab.reshape(nblk, RADIX * DFT_BLK, 2 * DFT_BLK * RADIX)
    return cs.astype(np.float32), g1.astype(np.float32), g2.astype(np.float32), gtab


@functools.lru_cache(maxsize=None)
def _rope_tables_t(n):
    rows = n // GRID_W
    row = np.repeat(np.arange(rows, dtype=np.float64), GRID_W)
    col = np.tile(np.arange(GRID_W, dtype=np.float64), rows)
    n_freq = QK_ROPE // 4
    freqs = ROPE_THETA ** (-np.arange(n_freq, dtype=np.float64) / n_freq)
    ang_r = row[:, None] * freqs[None, :]
    ang_c = col[:, None] * freqs[None, :]
    ang = np.concatenate([ang_r, ang_r, ang_c, ang_c], axis=-1)
    return np.cos(ang).T.astype(np.float32), np.sin(ang).T.astype(np.float32)


def kernel(x, c, ctx, c_ctx, ada_w, ada_b, norm1_g, norm2_g, w_in, b_gate, w_fourier, q_norm_g, w_qb,
           kv_norm_g, w_kvb, q_gain, k_gain, w_mla_o, w_out, w_up, w_down):
    bsz, s, d = x.shape
    sc = ctx.shape[1]
    assert ada_w.shape[0] == 1 and s == RADIX * RADIX and d == D_MODEL

    cc = jnp.zeros((8, d), F32).at[:bsz].set(c).at[bsz].set(c_ctx)
    mod, wf, wq, wkv, wg = _adaln(cc, ada_w[0], ada_b[0][None, :], w_in[0].T)

    wqbT = jnp.pad(w_qb[0].reshape(Q_LORA, N_HEADS, HEAD_QK),
                   ((0, 0), (0, 0), (0, HEAD_PAD - HEAD_QK))).reshape(Q_LORA, -1).T.astype(BF16)
    wkvb = w_kvb[0].reshape(KV_LORA, N_HEADS, QK_NOPE + V_DIM)
    wkbT = wkvb[:, :, :QK_NOPE].reshape(KV_LORA, -1).T.astype(BF16)
    wvT = wkvb[:, :, QK_NOPE:].reshape(KV_LORA, -1).T.astype(BF16)

    g1 = norm1_g[0][None, :]
    g2 = norm2_g[0][None, :]
    qg = q_norm_g[0][None, :]
    kvg = kv_norm_g[0][None, :]
    bg = b_gate[0][None, :]
    qscale = (HEAD_QK ** -0.5) * math.log2(math.e)
    qgain = jnp.broadcast_to(jnp.pad(q_gain[0] * qscale, (0, HEAD_PAD - HEAD_QK))[:, None], (HEAD_PAD, LANES))
    kgn = jnp.broadcast_to(k_gain[0][:QK_NOPE, None], (QK_NOPE, LANES))
    kgr = jnp.broadcast_to(k_gain[0][QK_NOPE:, None], (QK_ROPE, LANES))
    cosT, sinT = (jnp.asarray(t) for t in _rope_tables_t(s))
    cos1 = jnp.ones((QK_ROPE, sc), F32)
    sin0 = jnp.zeros((QK_ROPE, sc), F32)

    cs, ga1, ga2, gtab = (jnp.asarray(t).astype(BF16) for t in _dft_tables())

    kc, vcT = _ctx_kv(ctx, mod, g1, wkv, kvg, wkbT, wvT, kgn, kgr, cos1, sin0)

    z, gates, qT, k, vT = _proj(x, mod, g1, wf, cs, wg, bg, wq, qg, wqbT, qgain, wkv, kvg, wkbT, wvT,
                                kgn, kgr, cosT, sinT, PROJ_TILE)

    r8 = DFT_RADIX_A
    fm = _dft(ga1, ga2, gtab, z.reshape(bsz, 2, r8, r8, RADIX, FOURIER_W)).reshape(bsz, s, FOURIER_W)

    aT, (wfo, wao, wout, wup, wdn) = _attention(
        qT, k, vT, kc, vcT,
        [(w_fourier[0], 0, None), (w_mla_o[0], 0, None), (w_out[0], 0, None),
         (w_up[0], 1, FF_CHUNK), (w_down[0], 0, None)],
        ATTN_TQ, ATTN_TK)
    wdn = wdn.reshape(-1, FF_CHUNK, d)

    return _post(fm, aT, gates, x, mod, wfo, wao, wout, g2, wup, wdn, POST_TILE)
```

```python
import functools
import math

import numpy as np
import jax
import jax.numpy as jnp
from jax import lax
from jax.experimental import pallas as pl
from jax.experimental.pallas import tpu as pltpu

F32 = jnp.float32
BF16 = jnp.bfloat16

D_MODEL = 1024
GRID_W = 64
N_GROUPS = 4
GROUP_W = 128
FOURIER_W = N_GROUPS * GROUP_W
N_HEADS = 8
QK_NOPE = 64
QK_ROPE = 32
HEAD_QK = QK_NOPE + QK_ROPE
LANES = 128
HEAD_PAD = LANES
V_DIM = 64
V_ROWS = V_DIM + 16
Q_LORA = 256
KV_LORA = 128
W_IN_GATES = FOURIER_W + Q_LORA + KV_LORA + QK_ROPE
ROPE_THETA = 10000.0
EPS = 1e-6
RADIX = 64
DFT_RADIX_A = 8
DFT_ABLK = 16
DFT_BLK = 8
DFT_STEP = 2
POST_TILE = 512
POST_PARTS = 2
PROJ_TILE = 1024
PROJ_PARTS = 4
W_IN_BLK = 256
GATE_CHUNK = W_IN_BLK
FF_CHUNK = 1024
ATTN_TQ = 512
ATTN_TK = 256
ATTN_AHEAD = 2
ATTN_UNROLL = 4

V7X_VMEM_LIMIT = 56 * 1024 * 1024


def _params(*sem):
    return pltpu.CompilerParams(dimension_semantics=sem, vmem_limit_bytes=V7X_VMEM_LIMIT)


def _dot(a, b):
    return jnp.dot(a, b, preferred_element_type=F32)


def _rms_lanes(x, n):
    return x * lax.rsqrt(jnp.sum(x * x, axis=-1, keepdims=True) * (1.0 / n) + EPS)


def _rot_half_rows(x):
    return jnp.concatenate([-x[8:16], x[0:8], -x[24:32], x[16:24]], axis=0)


def _adaln_body(c_ref, w_ref, b_ref, tf_ref, tq_ref, tkv_ref, tga_ref, tgb_ref,
                o_ref, wf_ref, wq_ref, wkv_ref, wg_ref):
    c = c_ref[...]
    s = c * jax.nn.sigmoid(c)
    o_ref[...] = _dot(s, w_ref[...]) + b_ref[...]

    @pl.when(pl.program_id(0) == 0)
    def _():
        wf_ref[...] = tf_ref[...].T.astype(BF16)
        wq_ref[...] = tq_ref[...].T.astype(BF16)
        wkv_ref[...] = tkv_ref[...].T.astype(BF16)

    off = W_IN_GATES % W_IN_BLK
    both = jnp.concatenate([tga_ref[...], tgb_ref[...]], axis=0)
    wg_ref[0] = both[off:off + W_IN_BLK].T.astype(BF16)


def _adaln(cc, ada_w, ada_b, w_inT):
    rows, d = cc.shape
    n = ada_w.shape[1]
    n_gates = w_inT.shape[0] - W_IN_GATES
    steps = n_gates // W_IN_BLK
    tn = n // steps
    g0 = W_IN_GATES // W_IN_BLK
    blk = lambda r, f: pl.BlockSpec((r, d), f)
    return pl.pallas_call(
        _adaln_body,
        grid=(steps,),
        in_specs=[pl.BlockSpec((rows, d), lambda j: (0, 0)),
                  pl.BlockSpec((d, tn), lambda j: (0, j)),
                  pl.BlockSpec((1, tn), lambda j: (0, j)),
                  blk(FOURIER_W, lambda j: (0, 0)),
                  blk(Q_LORA, lambda j: (FOURIER_W // Q_LORA, 0)),
                  blk(2 * KV_LORA, lambda j: ((FOURIER_W + Q_LORA) // (2 * KV_LORA), 0)),
                  blk(W_IN_BLK, lambda j: (g0 + j, 0)),
                  blk(W_IN_BLK, lambda j: (g0 + 1 + j, 0))],
        out_specs=[pl.BlockSpec((rows, tn), lambda j: (0, j)),
                   pl.BlockSpec((d, FOURIER_W), lambda j: (0, 0)),
                   pl.BlockSpec((d, Q_LORA), lambda j: (0, 0)),
                   pl.BlockSpec((d, 2 * KV_LORA), lambda j: (0, 0)),
                   pl.BlockSpec((1, d, W_IN_BLK), lambda j: (j, 0, 0))],
        out_shape=[jax.ShapeDtypeStruct((rows, n), F32),
                   jax.ShapeDtypeStruct((d, FOURIER_W), BF16),
                   jax.ShapeDtypeStruct((d, Q_LORA), BF16),
                   jax.ShapeDtypeStruct((d, 2 * KV_LORA), BF16),
                   jax.ShapeDtypeStruct((steps, d, W_IN_BLK), BF16)],
        compiler_params=_params("arbitrary"),
        name="adaln",
    )(cc, ada_w, ada_b, w_inT, w_inT, w_inT, w_inT, w_inT)


def _mod_rows(mod_ref, row):
    m = mod_ref[pl.ds(row, 1), :]
    d = m.shape[1] // 6
    return jnp.concatenate([m[:, k * d:(k + 1) * d] for k in range(6)], axis=0)


def _norm_mod(x, m, g):
    ms = jnp.sum(x * x, axis=-1, keepdims=True) * (1.0 / x.shape[-1])
    return (x * lax.rsqrt(ms + EPS)) * (g * (1.0 + m[1:2])) + m[0:1]


def _kv_latents(kvp, kvg_ref):
    kvn = _rms_lanes(kvp[:, :KV_LORA], KV_LORA) * kvg_ref[...]
    return kvn.T.astype(BF16), kvp[:, KV_LORA:].T[0:QK_ROPE]


def _lanes(ref, t):
    return jnp.concatenate([ref[...]] * (t // LANES), axis=1)


def _kv_heads(knT, vT, krT, kgn_ref, kgr_ref, cos_ref, sin_ref, k_ref, vT_ref, tok):
    t = knT.shape[1]
    sr = jnp.sum(krT * krT, axis=0, keepdims=True)
    krg = krT * _lanes(kgr_ref, t)
    kr_rot = krg * cos_ref[:, tok] + _rot_half_rows(krg) * sin_ref[:, tok]
    kgn = _lanes(kgn_ref, t)
    ones = jnp.ones((V_ROWS - V_DIM, t), F32)
    zpad = jnp.zeros((HEAD_PAD - HEAD_QK, t), F32)
    for h in range(N_HEADS):
        kn = knT[h * QK_NOPE:(h + 1) * QK_NOPE]
        ss = jnp.sum(kn * kn, axis=0, keepdims=True) + sr
        r = lax.rsqrt(ss * (1.0 / HEAD_QK) + EPS)
        kT = jnp.concatenate([kn * kgn * r, kr_rot * r, zpad], axis=0)
        k_ref[0, h, tok, :] = kT.T.astype(k_ref.dtype)
        vh = vT[h * V_DIM:(h + 1) * V_DIM]
        vT_ref[0, h, :, tok] = jnp.concatenate([vh, ones], axis=0).astype(vT_ref.dtype)


def _q_heads(qT, qgain_ref, cos_ref, sin_ref, qT_ref, tok):
    qgain, cos, sin = _lanes(qgain_ref, qT.shape[1]), cos_ref[:, tok], sin_ref[:, tok]
    for h in range(N_HEADS):
        qh = qT[h * HEAD_PAD:(h + 1) * HEAD_PAD]
        r = lax.rsqrt(jnp.sum(qh * qh, axis=0, keepdims=True) * (1.0 / HEAD_QK) + EPS)
        qh = qh * qgain * r
        qr = qh[QK_NOPE:HEAD_QK]
        qr = qr * cos + _rot_half_rows(qr) * sin
        qT_ref[0, h, :, tok] = jnp.concatenate([qh[:QK_NOPE], qr, qh[HEAD_QK:]],
                                               axis=0).astype(qT_ref.dtype)


def _ctx_body(x_ref, mod_ref, g1_ref, wkv_ref, kvg_ref, wkbT_ref, wvT_ref, kgn_ref, kgr_ref,
              cos_ref, sin_ref, k_ref, vT_ref):
    ctx_row = pl.num_programs(0)
    hb = _norm_mod(x_ref[0], _mod_rows(mod_ref, ctx_row), g1_ref[...]).astype(BF16)
    kvnT, krT = _kv_latents(_dot(hb, wkv_ref[...]), kvg_ref)
    _kv_heads(_dot(wkbT_ref[...], kvnT), _dot(wvT_ref[...], kvnT), krT,
              kgn_ref, kgr_ref, cos_ref, sin_ref, k_ref, vT_ref, slice(None))


def _proj_body(x_ref, mod_ref, g1_ref, wf_ref, cs_ref, wg_ref, bg_ref,
               wq_ref, qg_ref, wqbT_ref, qgain_ref,
               wkv_ref, kvg_ref, wkbT_ref, wvT_ref, kgn_ref, kgr_ref, cos_ref, sin_ref,
               z_ref, g_ref, qT_ref, k_ref, vT_ref):
    part = x_ref.shape[1] // PROJ_PARTS
    m = _mod_rows(mod_ref, pl.program_id(0))
    for p in range(PROJ_PARTS):
        tok = slice(p * part, (p + 1) * part)
        hb = _norm_mod(x_ref[0, tok, :], m, g1_ref[...]).astype(BF16)
        qp = _dot(hb, wq_ref[...])
        kvp = _dot(hb, wkv_ref[...])
        f = _dot(hb, wf_ref[...]).astype(BF16)

        qnT = (_rms_lanes(qp, Q_LORA) * qg_ref[...]).T.astype(BF16)
        kvnT, krT = _kv_latents(kvp, kvg_ref)

        def gate_chunk(c):
            cols = slice(c * GATE_CHUNK, (c + 1) * GATE_CHUNK)
            g_ref[0, tok, cols] = jax.nn.sigmoid(
                _dot(hb, wg_ref[c]) + bg_ref[:, cols]).astype(g_ref.dtype)

        gate_chunk(0)

        qT = _dot(wqbT_ref[...], qnT)
        knT = _dot(wkbT_ref[...], kvnT)
        vT = _dot(wvT_ref[...], kvnT)

        _q_heads(qT, qgain_ref, cos_ref, sin_ref, qT_ref, tok)
        _kv_heads(knT, vT, krT, kgn_ref, kgr_ref, cos_ref, sin_ref, k_ref, vT_ref, tok)
        for c in range(1, g_ref.shape[2] // GATE_CHUNK):
            gate_chunk(c)

        for g in range(N_GROUPS):
            cols = slice(g * GROUP_W, (g + 1) * GROUP_W)
            z = _dot(f[:, cols], cs_ref[...])
            z_ref[0, 0, tok, cols] = z[:, :GROUP_W].astype(z_ref.dtype)
            z_ref[0, 1, tok, cols] = z[:, GROUP_W:].astype(z_ref.dtype)


def _full(shape):
    nd = len(shape)
    return pl.BlockSpec(shape, lambda *_: (0,) * nd, pipeline_mode=pl.Buffered(1))


def _ctx_kv(ctx, mod, g1, wkv, kvg, wkbT, wvT, kgn, kgr, cosT, sinT):
    b, t, d = ctx.shape
    return pl.pallas_call(
        _ctx_body,
        grid=(b,),
        in_specs=[pl.BlockSpec((1, t, d), lambda i: (i, 0, 0)),
                  _full(mod.shape),
                  _full(g1.shape), _full(wkv.shape), _full(kvg.shape), _full(wkbT.shape),
                  _full(wvT.shape), _full(kgn.shape), _full(kgr.shape),
                  _full(cosT.shape), _full(sinT.shape)],
        out_specs=[pl.BlockSpec((1, N_HEADS, t, HEAD_PAD), lambda i: (i, 0, 0, 0)),
                   pl.BlockSpec((1, N_HEADS, V_ROWS, t), lambda i: (i, 0, 0, 0))],
        out_shape=[jax.ShapeDtypeStruct((b, N_HEADS, t, HEAD_PAD), BF16),
                   jax.ShapeDtypeStruct((b, N_HEADS, V_ROWS, t), BF16)],
        compiler_params=_params("arbitrary"),
        name="ctx_kv",
    )(ctx, mod, g1, wkv, kvg, wkbT, wvT, kgn, kgr, cosT, sinT)


def _proj(x, mod, g1, wf, cs, wg, bg, wq, qg, wqbT, qgain, wkv, kvg, wkbT, wvT, kgn, kgr,
          cosT, sinT, ts):
    b, s, d = x.shape
    assert wg.shape == (2 * d // GATE_CHUNK, d, GATE_CHUNK)
    tok = lambda shape: pl.BlockSpec(shape, lambda i, j: (0, j))
    return pl.pallas_call(
        _proj_body,
        grid=(b, s // ts),
        in_specs=[pl.BlockSpec((1, ts, d), lambda i, j: (i, j, 0)),
                  _full(mod.shape),
                  _full(g1.shape), _full(wf.shape), _full(cs.shape), _full(wg.shape),
                  _full(bg.shape), _full(wq.shape), _full(qg.shape), _full(wqbT.shape),
                  _full(qgain.shape),
                  _full(wkv.shape), _full(kvg.shape), _full(wkbT.shape), _full(wvT.shape),
                  _full(kgn.shape), _full(kgr.shape), tok((QK_ROPE, ts)), tok((QK_ROPE, ts))],
        out_specs=[pl.BlockSpec((1, 2, ts, FOURIER_W), lambda i, j: (i, 0, j, 0)),
                   pl.BlockSpec((1, ts, 2 * d), lambda i, j: (i, j, 0)),
                   pl.BlockSpec((1, N_HEADS, HEAD_PAD, ts), lambda i, j: (i, 0, 0, j)),
                   pl.BlockSpec((1, N_HEADS, ts, HEAD_PAD), lambda i, j: (i, 0, j, 0)),
                   pl.BlockSpec((1, N_HEADS, V_ROWS, ts), lambda i, j: (i, 0, 0, j))],
        out_shape=[jax.ShapeDtypeStruct((b, 2, s, FOURIER_W), BF16),
                   jax.ShapeDtypeStruct((b, s, 2 * d), BF16),
                   jax.ShapeDtypeStruct((b, N_HEADS, HEAD_PAD, s), BF16),
                   jax.ShapeDtypeStruct((b, N_HEADS, s, HEAD_PAD), BF16),
                   jax.ShapeDtypeStruct((b, N_HEADS, V_ROWS, s), BF16)],
        compiler_params=_params("arbitrary", "arbitrary"),
        name="proj",
    )(x, mod, g1, wf, cs, wg, bg, wq, qg, wqbT, qgain, wkv, kvg, wkbT, wvT, kgn, kgr,
      cosT, sinT)


def _dft_body(g1_ref, g2_ref, gb_ref, z_ref, o_ref, a_scr):
    step_rows = DFT_BLK * DFT_STEP
    nstep = RADIX // step_rows
    j = pl.program_id(1)
    r8 = DFT_RADIX_A
    slab = (2, r8, DFT_ABLK, FOURIER_W)

    @pl.when(j < nstep)
    def _():
        for t in range(step_rows // DFT_ABLK):
            rows = slice(t * DFT_ABLK, (t + 1) * DFT_ABLK)
            off = pl.multiple_of(j * step_rows + t * DFT_ABLK, DFT_ABLK)
            t1 = []
            for a2 in range(r8):
                z = z_ref[0, :, :, a2, rows, :].reshape(2 * r8 * DFT_ABLK, FOURIER_W)
                t1.append(_dot(g1_ref[...], z).reshape(slab))
            for k2a in range(r8):
                x = jnp.concatenate([t1[a2][r, k2a] for r in range(2) for a2 in range(r8)], axis=0)
                a = _dot(g2_ref[k2a], x.astype(BF16)).reshape(slab)
                for r in range(2):
                    for k2b in range(r8):
                        a_scr[r, k2a + r8 * k2b, pl.ds(off, DFT_ABLK), :] = a[r, k2b]

    @pl.when(j >= nstep)
    def _():
        ys = []
        for t in range(DFT_STEP):
            blk = (j - nstep) * DFT_STEP + t
            off = pl.multiple_of(blk * DFT_BLK, DFT_BLK)
            a = a_scr[:, pl.ds(off, DFT_BLK), :, :]
            a = a.reshape(2 * DFT_BLK * RADIX, FOURIER_W).astype(BF16)
            y = _dot(gb_ref[blk], a)
            ys.append(y.reshape(RADIX, DFT_BLK, FOURIER_W))
        o_ref[0] = jnp.concatenate(ys, axis=1).astype(o_ref.dtype)


def _dft(g1, g2, gb, z6):
    b = z6.shape[0]
    nstep = RADIX // (DFT_BLK * DFT_STEP)
    rows = DFT_BLK * DFT_STEP
    r8 = DFT_RADIX_A
    assert rows % DFT_ABLK == 0 and z6.shape[2:4] == (r8, r8)
    return pl.pallas_call(
        _dft_body,
        grid=(b, 2 * nstep),
        in_specs=[_full(g1.shape), _full(g2.shape), _full(gb.shape),
                  pl.BlockSpec((1, 2, r8, r8, rows, FOURIER_W),
                               lambda i, j: (i, 0, 0, 0, jnp.minimum(j, nstep - 1), 0))],
        out_specs=pl.BlockSpec((1, RADIX, rows, FOURIER_W),
                               lambda i, j: (i, 0, jnp.maximum(j - nstep, 0), 0)),
        out_shape=jax.ShapeDtypeStruct((b, RADIX, RADIX, FOURIER_W), BF16),
        scratch_shapes=[pltpu.VMEM((2, RADIX, RADIX, FOURIER_W), F32)],
        compiler_params=_params("arbitrary", "arbitrary"),
        name="dft",
    )(g1, g2, gb, z6)


def _attn_body(qT_ref, k_ref, vT_ref, kc_ref, vcT_ref, *rest, tq, tkb, n_cast):
    for src, dst in zip(rest[:n_cast], rest[n_cast + 1:]):
        dst[...] = src[...].astype(dst.dtype).reshape(dst.shape)
    o_ref = rest[n_cast]
    nkb = k_ref.shape[2] // tkb
    nblk = nkb + 1

    def colmax(s):
        r = s.shape[0] // 8
        return jnp.max(jnp.max(s.reshape(8, r, s.shape[1]), axis=0), axis=0, keepdims=True)

    def update(carry, s, vT_blk):
        m, acc = carry
        m_new = jnp.maximum(m, colmax(s))
        p = jnp.exp2(s - m_new).astype(BF16)
        acc = jnp.exp2(m - m_new) * acc + _dot(vT_blk, p)
        return m_new, acc

    nq = qT_ref.shape[3] // tq

    def k_blk(i):
        return k_ref[0, 0, i * tkb:(i + 1) * tkb, :] if i < nkb else kc_ref[0, 0]

    def vT_blk(i):
        return vT_ref[0, 0, :, i * tkb:(i + 1) * tkb] if i < nkb else vcT_ref[0, 0]

    def q_cols(qi):
        return qT_ref[0, 0, :, qi * tq:(qi + 1) * tq]

    pending = [_dot(k_blk(i), q_cols(0)) for i in range(ATTN_AHEAD)]
    for qi in range(nq):
        qT = q_cols(qi)
        carry = (jnp.full((1, tq), -jnp.inf, F32), jnp.zeros((V_ROWS, tq), F32))
        for i in range(nblk):
            j = i + ATTN_AHEAD
            if j < nblk:
                pending.append(_dot(k_blk(j), qT))
            elif qi + 1 < nq:
                pending.append(_dot(k_blk(j - nblk), q_cols(qi + 1)))
            carry = update(carry, pending.pop(0), vT_blk(i))
        _, acc = carry
        o_ref[0, :, qi * tq:(qi + 1) * tq] = (
            acc[:V_DIM] / acc[V_DIM:V_DIM + 1]).astype(o_ref.dtype)


def _attention(qT, k, vT, kc, vcT, weights, tq, tkb):
    b, h, _, s = qT.shape
    sc = kc.shape[2]
    steps = b * h
    w_in_specs, w_out_specs, w_out_shapes = [], [], []
    for w, axis, chunk in weights:
        blk = tuple(n // steps if a == axis else n for a, n in enumerate(w.shape))
        spec = pl.BlockSpec(blk, (lambda i, j: (i * h + j, 0)) if axis == 0
                            else (lambda i, j: (0, i * h + j)))
        w_in_specs.append(spec)
        if chunk is None:
            w_out_specs.append(spec)
            w_out_shapes.append(jax.ShapeDtypeStruct(w.shape, BF16))
        else:
            per = chunk // blk[1]
            w_out_specs.append(pl.BlockSpec(
                (1,) + blk, lambda i, j, per=per: ((i * h + j) // per, 0, (i * h + j) % per)))
            w_out_shapes.append(jax.ShapeDtypeStruct((w.shape[1] // chunk, w.shape[0], chunk), BF16))
    outs = pl.pallas_call(
        functools.partial(_attn_body, tq=tq, tkb=tkb, n_cast=len(weights)),
        grid=(b, h),
        in_specs=[pl.BlockSpec((1, 1, HEAD_PAD, s), lambda i, j: (i, j, 0, 0)),
                  pl.BlockSpec((1, 1, s, HEAD_PAD), lambda i, j: (i, j, 0, 0)),
                  pl.BlockSpec((1, 1, V_ROWS, s), lambda i, j: (i, j, 0, 0)),
                  pl.BlockSpec((1, 1, sc, HEAD_PAD), lambda i, j: (i, j, 0, 0)),
                  pl.BlockSpec((1, 1, V_ROWS, sc), lambda i, j: (i, j, 0, 0))] + w_in_specs,
        out_specs=[pl.BlockSpec((1, V_DIM, s), lambda i, j: (i, j, 0))] + w_out_specs,
        out_shape=[jax.ShapeDtypeStruct((b, h * V_DIM, s), BF16)] + w_out_shapes,
        compiler_params=_params("arbitrary", "arbitrary"),
        name="attention",
    )(qT, k, vT, kc, vcT, *[w for w, _, _ in weights])
    return outs[0], outs[1:]


def _post_body(fm_ref, aT_ref, g_ref, x_ref, mod_ref, wfo_ref, wao_ref, wout_ref,
               g2_ref, wup_ref, wdn_ref, o_ref):
    d = x_ref.shape[-1]
    m = _mod_rows(mod_ref, pl.program_id(0))
    part = x_ref.shape[1] // POST_PARTS
    toks = [slice(p * part, (p + 1) * part) for p in range(POST_PARTS)]
    x1s, hbs = [], []
    for tok in toks:
        yf = _dot(fm_ref[0, tok, :], wfo_ref[...])
        ya = lax.dot_general(aT_ref[0, :, tok], wao_ref[...], (((0,), (0,)), ((), ())),
                             preferred_element_type=F32)
        g = g_ref[0, tok, :]
        y = g[:, :d].astype(F32) * yf + g[:, d:].astype(F32) * ya
        x1 = x_ref[0, tok, :] + m[2:3] * _dot(y.astype(BF16), wout_ref[...])
        x1s.append(x1)
        hbs.append(_norm_mod(x1, m[3:5], g2_ref[...]).astype(BF16))
    for tok, x1, hb in zip(toks, x1s, hbs):
        acc = jnp.zeros(x1.shape, F32)
        for c in range(wup_ref.shape[0]):
            u = jnp.maximum(_dot(hb, wup_ref[c]), 0.0)
            acc = acc + _dot((u * u).astype(BF16), wdn_ref[c])
        o_ref[0, tok, :] = x1 + m[5:6] * acc


def _post(fm, aT, gates, x, mod, wfo, wao, wout, g2, wup, wdn, ts):
    b, s, d = x.shape
    return pl.pallas_call(
        _post_body,
        grid=(b, s // ts),
        in_specs=[pl.BlockSpec((1, ts, FOURIER_W), lambda i, j: (i, j, 0)),
                  pl.BlockSpec((1, aT.shape[1], ts), lambda i, j: (i, 0, j)),
                  pl.BlockSpec((1, ts, 2 * d), lambda i, j: (i, j, 0)),
                  pl.BlockSpec((1, ts, d), lambda i, j: (i, j, 0)),
                  _full(mod.shape),
                  _full(wfo.shape), _full(wao.shape), _full(wout.shape),
                  _full(g2.shape), _full(wup.shape), _full(wdn.shape)],
        out_specs=pl.BlockSpec((1, ts, d), lambda i, j: (i, j, 0)),
        out_shape=jax.ShapeDtypeStruct((b, s, d), F32),
        compiler_params=_params("arbitrary", "arbitrary"),
        name="post",
    )(fm, aT, gates, x, mod, wfo, wao, wout, g2, wup, wdn)


@functools.lru_cache(maxsize=None)
def _dft_tables():
    j = np.arange(GROUP_W)
    ang = 2.0 * np.pi * np.outer(j, j) / GROUP_W
    cs = np.concatenate([np.cos(ang), -np.sin(ang)], axis=1) / math.sqrt(GROUP_W)

    r8 = DFT_RADIX_A
    d = np.arange(r8)
    eye = np.eye(DFT_ABLK)

    def real_block(theta):
        c, s = np.cos(theta), np.sin(theta)
        return np.kron(np.block([[c, s], [-s, c]]) / math.sqrt(r8), eye)

    g1 = real_block(2.0 * np.pi * np.outer(d, d) / r8)
    g2 = np.stack([real_block(2.0 * np.pi * (k2a * d[None, :] / RADIX + np.outer(d, d) / r8))
                   for k2a in range(r8)])

    n = RADIX * RADIX
    k1 = np.arange(RADIX)[:, None, None]
    k2 = np.arange(RADIX)[None, :, None]
    bb = np.arange(RADIX)[None, None, :]
    th = 2.0 * np.pi * ((bb * (RADIX * k1 + k2)) % n) / n
    gc = np.cos(th) / math.sqrt(RADIX)
    gs = np.sin(th) / math.sqrt(RADIX)
    nblk = RADIX // DFT_BLK
    gtab = np.zeros((nblk, RADIX, DFT_BLK, 2, DFT_BLK, RADIX), np.float32)
    for jb in range(nblk):
        for q in range(DFT_BLK):
            gtab[jb, :, q, 0, q, :] = gc[:, jb * DFT_BLK + q, :]
            gtab[jb, :, q, 1, q, :] = gs[:, jb * DFT_BLK + q, :]
    gtab = gtab.reshape(nblk, RADIX * DFT_BLK, 2 * DFT_BLK * RADIX)
    return cs.astype(np.float32), g1.astype(np.float32), g2.astype(np.float32), gtab


@functools.lru_cache(maxsize=None)
def _rope_tables_t(n):
    rows = n // GRID_W
    row = np.repeat(np.arange(rows, dtype=np.float64), GRID_W)
    col = np.tile(np.arange(GRID_W, dtype=np.float64), rows)
    n_freq = QK_ROPE // 4
    freqs = ROPE_THETA ** (-np.arange(n_freq, dtype=np.float64) / n_freq)
    ang_r = row[:, None] * freqs[None, :]
    ang_c = col[:, None] * freqs[None, :]
    ang = np.concatenate([ang_r, ang_r, ang_c, ang_c], axis=-1)
    return np.cos(ang).T.astype(np.float32), np.sin(ang).T.astype(np.float32)


def kernel(x, c, ctx, c_ctx, ada_w, ada_b, norm1_g, norm2_g, w_in, b_gate, w_fourier, q_norm_g, w_qb,
           kv_norm_g, w_kvb, q_gain, k_gain, w_mla_o, w_out, w_up, w_down):
    bsz, s, d = x.shape
    sc = ctx.shape[1]
    assert ada_w.shape[0] == 1 and s == RADIX * RADIX and d == D_MODEL

    cc = jnp.zeros((8, d), F32).at[:bsz].set(c).at[bsz].set(c_ctx)
    mod, wf, wq, wkv, wg = _adaln(cc, ada_w[0], ada_b[0][None, :], w_in[0].T)

    wqbT = jnp.pad(w_qb[0].reshape(Q_LORA, N_HEADS, HEAD_QK),
                   ((0, 0), (0, 0), (0, HEAD_PAD - HEAD_QK))).reshape(Q_LORA, -1).T.astype(BF16)
    wkvb = w_kvb[0].reshape(KV_LORA, N_HEADS, QK_NOPE + V_DIM)
    wkbT = wkvb[:, :, :QK_NOPE].reshape(KV_LORA, -1).T.astype(BF16)
    wvT = wkvb[:, :, QK_NOPE:].reshape(KV_LORA, -1).T.astype(BF16)

    g1 = norm1_g[0][None, :]
    g2 = norm2_g[0][None, :]
    qg = q_norm_g[0][None, :]
    kvg = kv_norm_g[0][None, :]
    bg = b_gate[0][None, :]
    qscale = (HEAD_QK ** -0.5) * math.log2(math.e)
    qgain = jnp.broadcast_to(jnp.pad(q_gain[0] * qscale, (0, HEAD_PAD - HEAD_QK))[:, None], (HEAD_PAD, LANES))
    kgn = jnp.broadcast_to(k_gain[0][:QK_NOPE, None], (QK_NOPE, LANES))
    kgr = jnp.broadcast_to(k_gain[0][QK_NOPE:, None], (QK_ROPE, LANES))
    cosT, sinT = (jnp.asarray(t) for t in _rope_tables_t(s))
    cos1 = jnp.ones((QK_ROPE, sc), F32)
    sin0 = jnp.zeros((QK_ROPE, sc), F32)

    cs, ga1, ga2, gtab = (jnp.asarray(t).astype(BF16) for t in _dft_tables())

    kc, vcT = _ctx_kv(ctx, mod, g1, wkv, kvg, wkbT, wvT, kgn, kgr, cos1, sin0)

    z, gates, qT, k, vT = _proj(x, mod, g1, wf, cs, wg, bg, wq, qg, wqbT, qgain, wkv, kvg, wkbT, wvT,
                                kgn, kgr, cosT, sinT, PROJ_TILE)

    r8 = DFT_RADIX_A
    fm = _dft(ga1, ga2, gtab, z.reshape(bsz, 2, r8, r8, RADIX, FOURIER_W)).reshape(bsz, s, FOURIER_W)

    aT, (wfo, wao, wout, wup, wdn) = _attention(
        qT, k, vT, kc, vcT,
        [(w_fourier[0], 0, None), (w_mla_o[0], 0, None), (w_out[0], 0, None),
         (w_up[0], 1, FF_CHUNK), (w_down[0], 0, None)],
        ATTN_TQ, ATTN_TK)
    wdn = wdn.reshape(-1, FF_CHUNK, d)

    return _post(fm, aT, gates, x, mod, wfo, wao, wout, g2, wup, wdn, POST_TILE)
```

```python
import functools
import math

import numpy as np
import jax
import jax.numpy as jnp
from jax import lax
from jax.experimental import pallas as pl
from jax.experimental.pallas import tpu as pltpu

F32 = jnp.float32
BF16 = jnp.bfloat16

D_MODEL = 1024
GRID_W = 64
N_GROUPS = 4
GROUP_W = 128
FOURIER_W = N_GROUPS * GROUP_W
N_HEADS = 8
QK_NOPE = 64
QK_ROPE = 32
HEAD_QK = QK_NOPE + QK_ROPE
LANES = 128
HEAD_PAD = LANES
V_DIM = 64
V_ROWS = V_DIM + 16
Q_LORA = 256
KV_LORA = 128
W_IN_GATES = FOURIER_W + Q_LORA + KV_LORA + QK_ROPE
ROPE_THETA = 10000.0
EPS = 1e-6
RADIX = 64
DFT_RADIX_A = 8
DFT_ABLK = 16
DFT_BLK = 8
DFT_STEP = 2
POST_TILE = 512
POST_PARTS = 2
PROJ_TILE = 1024
PROJ_PARTS = 4
W_IN_BLK = 256
GATE_CHUNK = W_IN_BLK
FF_CHUNK = 1024
ATTN_TQ = 512
ATTN_TK = 256
ATTN_AHEAD = 2

V7X_VMEM_LIMIT = 56 * 1024 * 1024


def _params(*sem):
    return pltpu.CompilerParams(dimension_semantics=sem, vmem_limit_bytes=V7X_VMEM_LIMIT)


def _dot(a, b):
    return jnp.dot(a, b, preferred_element_type=F32)


def _rms_lanes(x, n):
    return x * lax.rsqrt(jnp.sum(x * x, axis=-1, keepdims=True) * (1.0 / n) + EPS)


def _rot_half_rows(x):
    return jnp.concatenate([-x[8:16], x[0:8], -x[24:32], x[16:24]], axis=0)


def _adaln_body(c_ref, w_ref, b_ref, tf_ref, tq_ref, tkv_ref, tga_ref, tgb_ref, wqb_ref, wkvb_ref,
                o_ref, wf_ref, wq_ref, wkv_ref, wg_ref, wqbT_ref, wkbT_ref, wvT_ref):
    c = c_ref[...]
    s = c * jax.nn.sigmoid(c)
    o_ref[...] = _dot(s, w_ref[...]) + b_ref[...]

    @pl.when(pl.program_id(0) == 0)
    def _():
        wf_ref[...] = tf_ref[...].T.astype(BF16)
        wq_ref[...] = tq_ref[...].T.astype(BF16)
        wkv_ref[...] = tkv_ref[...].T.astype(BF16)
        wqT = wqb_ref[...].T
        zrows = jnp.zeros((HEAD_PAD - HEAD_QK, wqT.shape[1]), F32)
        wqbT_ref[...] = jnp.concatenate(
            [blk for h in range(N_HEADS) for blk in (wqT[h * HEAD_QK:(h + 1) * HEAD_QK], zrows)],
            axis=0).astype(BF16)
        wkvT = wkvb_ref[...].T
        hw = QK_NOPE + V_DIM
        wkbT_ref[...] = jnp.concatenate(
            [wkvT[h * hw:h * hw + QK_NOPE] for h in range(N_HEADS)], axis=0).astype(BF16)
        wvT_ref[...] = jnp.concatenate(
            [wkvT[h * hw + QK_NOPE:(h + 1) * hw] for h in range(N_HEADS)], axis=0).astype(BF16)

    off = W_IN_GATES % W_IN_BLK
    both = jnp.concatenate([tga_ref[...], tgb_ref[...]], axis=0)
    wg_ref[0] = both[off:off + W_IN_BLK].T.astype(BF16)


def _adaln(cc, ada_w, ada_b, w_inT, w_qb, w_kvb):
    rows, d = cc.shape
    const = lambda shape: pl.BlockSpec(shape, lambda j: (0, 0))
    wqbT_shape = (N_HEADS * HEAD_PAD, w_qb.shape[0])
    wkT_shape = (N_HEADS * QK_NOPE, w_kvb.shape[0])
    wvT_shape = (N_HEADS * V_DIM, w_kvb.shape[0])
    n = ada_w.shape[1]
    n_gates = w_inT.shape[0] - W_IN_GATES
    steps = n_gates // W_IN_BLK
    tn = n // steps
    g0 = W_IN_GATES // W_IN_BLK
    blk = lambda r, f: pl.BlockSpec((r, d), f)
    return pl.pallas_call(
        _adaln_body,
        grid=(steps,),
        in_specs=[pl.BlockSpec((rows, d), lambda j: (0, 0)),
                  pl.BlockSpec((d, tn), lambda j: (0, j)),
                  pl.BlockSpec((1, tn), lambda j: (0, j)),
                  blk(FOURIER_W, lambda j: (0, 0)),
                  blk(Q_LORA, lambda j: (FOURIER_W // Q_LORA, 0)),
                  blk(2 * KV_LORA, lambda j: ((FOURIER_W + Q_LORA) // (2 * KV_LORA), 0)),
                  blk(W_IN_BLK, lambda j: (g0 + j, 0)),
                  blk(W_IN_BLK, lambda j: (g0 + 1 + j, 0)),
                  const(w_qb.shape), const(w_kvb.shape)],
        out_specs=[pl.BlockSpec((rows, tn), lambda j: (0, j)),
                   const((d, FOURIER_W)), const((d, Q_LORA)), const((d, 2 * KV_LORA)),
                   pl.BlockSpec((1, d, W_IN_BLK), lambda j: (j, 0, 0)),
                   const(wqbT_shape), const(wkT_shape), const(wvT_shape)],
        out_shape=[jax.ShapeDtypeStruct((rows, n), F32),
                   jax.ShapeDtypeStruct((d, FOURIER_W), BF16),
                   jax.ShapeDtypeStruct((d, Q_LORA), BF16),
                   jax.ShapeDtypeStruct((d, 2 * KV_LORA), BF16),
                   jax.ShapeDtypeStruct((steps, d, W_IN_BLK), BF16),
                   jax.ShapeDtypeStruct(wqbT_shape, BF16),
                   jax.ShapeDtypeStruct(wkT_shape, BF16),
                   jax.ShapeDtypeStruct(wvT_shape, BF16)],
        compiler_params=_params("arbitrary"),
        name="adaln",
    )(cc, ada_w, ada_b, w_inT, w_inT, w_inT, w_inT, w_inT, w_qb, w_kvb)


def _mod_rows(mod_ref, row):
    m = mod_ref[pl.ds(row, 1), :]
    d = m.shape[1] // 6
    return jnp.concatenate([m[:, k * d:(k + 1) * d] for k in range(6)], axis=0)


def _norm_mod(x, m, g):
    ms = jnp.sum(x * x, axis=-1, keepdims=True) * (1.0 / x.shape[-1])
    return (x * lax.rsqrt(ms + EPS)) * (g * (1.0 + m[1:2])) + m[0:1]


def _kv_latents(kvp, kvg_ref):
    kvn = _rms_lanes(kvp[:, :KV_LORA], KV_LORA) * kvg_ref[...]
    return kvn.T.astype(BF16), kvp[:, KV_LORA:].T[0:QK_ROPE]


def _lanes(ref, t):
    return jnp.concatenate([ref[...]] * (t // LANES), axis=1)


def _kv_heads(knT, vT, krT, kgn_ref, kgr_ref, cos_ref, sin_ref, k_ref, vT_ref, tok):
    t = knT.shape[1]
    sr = jnp.sum(krT * krT, axis=0, keepdims=True)
    krg = krT * _lanes(kgr_ref, t)
    kr_rot = krg * cos_ref[:, tok] + _rot_half_rows(krg) * sin_ref[:, tok]
    kgn = _lanes(kgn_ref, t)
    ones = jnp.ones((V_ROWS - V_DIM, t), F32)
    zpad = jnp.zeros((HEAD_PAD - HEAD_QK, t), F32)
    for h in range(N_HEADS):
        kn = knT[h * QK_NOPE:(h + 1) * QK_NOPE]
        ss = jnp.sum(kn * kn, axis=0, keepdims=True) + sr
        r = lax.rsqrt(ss * (1.0 / HEAD_QK) + EPS)
        kT = jnp.concatenate([kn * kgn * r, kr_rot * r, zpad], axis=0)
        k_ref[0, h, tok, :] = kT.T.astype(k_ref.dtype)
        vh = vT[h * V_DIM:(h + 1) * V_DIM]
        vT_ref[0, h, :, tok] = jnp.concatenate([vh, ones], axis=0).astype(vT_ref.dtype)


def _q_heads(qT, qgain_ref, cos_ref, sin_ref, qT_ref, tok):
    qgain, cos, sin = _lanes(qgain_ref, qT.shape[1]), cos_ref[:, tok], sin_ref[:, tok]
    for h in range(N_HEADS):
        qh = qT[h * HEAD_PAD:(h + 1) * HEAD_PAD]
        r = lax.rsqrt(jnp.sum(qh * qh, axis=0, keepdims=True) * (1.0 / HEAD_QK) + EPS)
        qh = qh * qgain * r
        qr = qh[QK_NOPE:HEAD_QK]
        qr = qr * cos + _rot_half_rows(qr) * sin
        qT_ref[0, h, :, tok] = jnp.concatenate([qh[:QK_NOPE], qr, qh[HEAD_QK:]],
                                               axis=0).astype(qT_ref.dtype)


def _ctx_body(x_ref, mod_ref, g1_ref, wkv_ref, kvg_ref, wkbT_ref, wvT_ref, kgn_ref, kgr_ref,
              cos_ref, sin_ref, k_ref, vT_ref):
    ctx_row = pl.num_programs(0)
    hb = _norm_mod(x_ref[0], _mod_rows(mod_ref, ctx_row), g1_ref[...]).astype(BF16)
    kvnT, krT = _kv_latents(_dot(hb, wkv_ref[...]), kvg_ref)
    _kv_heads(_dot(wkbT_ref[...], kvnT), _dot(wvT_ref[...], kvnT), krT,
              kgn_ref, kgr_ref, cos_ref, sin_ref, k_ref, vT_ref, slice(None))


def _proj_body(x_ref, mod_ref, g1_ref, wf_ref, cs_ref, wg_ref, bg_ref,
               wq_ref, qg_ref, wqbT_ref, qgain_ref,
               wkv_ref, kvg_ref, wkbT_ref, wvT_ref, kgn_ref, kgr_ref, cos_ref, sin_ref,
               z_ref, g_ref, qT_ref, k_ref, vT_ref):
    part = x_ref.shape[1] // PROJ_PARTS
    m = _mod_rows(mod_ref, pl.program_id(0))
    for p in range(PROJ_PARTS):
        tok = slice(p * part, (p + 1) * part)
        hb = _norm_mod(x_ref[0, tok, :], m, g1_ref[...]).astype(BF16)
        qp = _dot(hb, wq_ref[...])
        kvp = _dot(hb, wkv_ref[...])
        f = _dot(hb, wf_ref[...]).astype(BF16)

        qnT = (_rms_lanes(qp, Q_LORA) * qg_ref[...]).T.astype(BF16)
        kvnT, krT = _kv_latents(kvp, kvg_ref)

        def gate_chunk(c):
            cols = slice(c * GATE_CHUNK, (c + 1) * GATE_CHUNK)
            g_ref[0, tok, cols] = jax.nn.sigmoid(
                _dot(hb, wg_ref[c]) + bg_ref[:, cols]).astype(g_ref.dtype)

        gate_chunk(0)

        qT = _dot(wqbT_ref[...], qnT)
        knT = _dot(wkbT_ref[...], kvnT)
        vT = _dot(wvT_ref[...], kvnT)

        _q_heads(qT, qgain_ref, cos_ref, sin_ref, qT_ref, tok)
        _kv_heads(knT, vT, krT, kgn_ref, kgr_ref, cos_ref, sin_ref, k_ref, vT_ref, tok)
        for c in range(1, g_ref.shape[2] // GATE_CHUNK):
            gate_chunk(c)

        for g in range(N_GROUPS):
            cols = slice(g * GROUP_W, (g + 1) * GROUP_W)
            z = _dot(f[:, cols], cs_ref[...])
            z_ref[0, 0, tok, cols] = z[:, :GROUP_W].astype(z_ref.dtype)
            z_ref[0, 1, tok, cols] = z[:, GROUP_W:].astype(z_ref.dtype)


def _full(shape):
    nd = len(shape)
    return pl.BlockSpec(shape, lambda *_: (0,) * nd, pipeline_mode=pl.Buffered(1))


def _ctx_kv(ctx, mod, g1, wkv, kvg, wkbT, wvT, kgn, kgr, cosT, sinT):
    b, t, d = ctx.shape
    return pl.pallas_call(
        _ctx_body,
        grid=(b,),
        in_specs=[pl.BlockSpec((1, t, d), lambda i: (i, 0, 0)),
                  _full(mod.shape),
                  _full(g1.shape), _full(wkv.shape), _full(kvg.shape), _full(wkbT.shape),
                  _full(wvT.shape), _full(kgn.shape), _full(kgr.shape),
                  _full(cosT.shape), _full(sinT.shape)],
        out_specs=[pl.BlockSpec((1, N_HEADS, t, HEAD_PAD), lambda i: (i, 0, 0, 0)),
                   pl.BlockSpec((1, N_HEADS, V_ROWS, t), lambda i: (i, 0, 0, 0))],
        out_shape=[jax.ShapeDtypeStruct((b, N_HEADS, t, HEAD_PAD), BF16),
                   jax.ShapeDtypeStruct((b, N_HEADS, V_ROWS, t), BF16)],
        compiler_params=_params("arbitrary"),
        name="ctx_kv",
    )(ctx, mod, g1, wkv, kvg, wkbT, wvT, kgn, kgr, cosT, sinT)


def _proj(x, mod, g1, wf, cs, wg, bg, wq, qg, wqbT, qgain, wkv, kvg, wkbT, wvT, kgn, kgr,
          cosT, sinT, ts):
    b, s, d = x.shape
    assert wg.shape == (2 * d // GATE_CHUNK, d, GATE_CHUNK)
    tok = lambda shape: pl.BlockSpec(shape, lambda i, j: (0, j))
    return pl.pallas_call(
        _proj_body,
        grid=(b, s // ts),
        in_specs=[pl.BlockSpec((1, ts, d), lambda i, j: (i, j, 0)),
                  _full(mod.shape),
                  _full(g1.shape), _full(wf.shape), _full(cs.shape), _full(wg.shape),
                  _full(bg.shape), _full(wq.shape), _full(qg.shape), _full(wqbT.shape),
                  _full(qgain.shape),
                  _full(wkv.shape), _full(kvg.shape), _full(wkbT.shape), _full(wvT.shape),
                  _full(kgn.shape), _full(kgr.shape), tok((QK_ROPE, ts)), tok((QK_ROPE, ts))],
        out_specs=[pl.BlockSpec((1, 2, ts, FOURIER_W), lambda i, j: (i, 0, j, 0)),
                   pl.BlockSpec((1, ts, 2 * d), lambda i, j: (i, j, 0)),
                   pl.BlockSpec((1, N_HEADS, HEAD_PAD, ts), lambda i, j: (i, 0, 0, j)),
                   pl.BlockSpec((1, N_HEADS, ts, HEAD_PAD), lambda i, j: (i, 0, j, 0)),
                   pl.BlockSpec((1, N_HEADS, V_ROWS, ts), lambda i, j: (i, 0, 0, j))],
        out_shape=[jax.ShapeDtypeStruct((b, 2, s, FOURIER_W), BF16),
                   jax.ShapeDtypeStruct((b, s, 2 * d), BF16),
                   jax.ShapeDtypeStruct((b, N_HEADS, HEAD_PAD, s), BF16),
                   jax.ShapeDtypeStruct((b, N_HEADS, s, HEAD_PAD), BF16),
                   jax.ShapeDtypeStruct((b, N_HEADS, V_ROWS, s), BF16)],
        compiler_params=_params("arbitrary", "arbitrary"),
        name="proj",
    )(x, mod, g1, wf, cs, wg, bg, wq, qg, wqbT, qgain, wkv, kvg, wkbT, wvT, kgn, kgr,
      cosT, sinT)


def _dft_body(g1_ref, g2_ref, gb_ref, z_ref, o_ref, a_scr):
    step_rows = DFT_BLK * DFT_STEP
    nstep = RADIX // step_rows
    j = pl.program_id(1)
    r8 = DFT_RADIX_A
    slab = (2, r8, DFT_ABLK, FOURIER_W)

    @pl.when(j < nstep)
    def _():
        for t in range(step_rows // DFT_ABLK):
            rows = slice(t * DFT_ABLK, (t + 1) * DFT_ABLK)
            off = pl.multiple_of(j * step_rows + t * DFT_ABLK, DFT_ABLK)
            t1 = []
            for a2 in range(r8):
                z = z_ref[0, :, :, a2, rows, :].reshape(2 * r8 * DFT_ABLK, FOURIER_W)
                t1.append(_dot(g1_ref[...], z).reshape(slab))
            for k2a in range(r8):
                x = jnp.concatenate([t1[a2][r, k2a] for r in range(2) for a2 in range(r8)], axis=0)
                a = _dot(g2_ref[k2a], x.astype(BF16)).reshape(slab)
                for r in range(2):
                    for k2b in range(r8):
                        a_scr[r, k2a + r8 * k2b, pl.ds(off, DFT_ABLK), :] = a[r, k2b]

    @pl.when(j >= nstep)
    def _():
        ys = []
        for t in range(DFT_STEP):
            blk = (j - nstep) * DFT_STEP + t
            off = pl.multiple_of(blk * DFT_BLK, DFT_BLK)
            a = a_scr[:, pl.ds(off, DFT_BLK), :, :]
            a = a.reshape(2 * DFT_BLK * RADIX, FOURIER_W).astype(BF16)
            y = _dot(gb_ref[blk], a)
            ys.append(y.reshape(RADIX, DFT_BLK, FOURIER_W))
        o_ref[0] = jnp.concatenate(ys, axis=1).astype(o_ref.dtype)


def _dft(g1, g2, gb, z6):
    b = z6.shape[0]
    nstep = RADIX // (DFT_BLK * DFT_STEP)
    rows = DFT_BLK * DFT_STEP
    r8 = DFT_RADIX_A
    assert rows % DFT_ABLK == 0 and z6.shape[2:4] == (r8, r8)
    return pl.pallas_call(
        _dft_body,
        grid=(b, 2 * nstep),
        in_specs=[_full(g1.shape), _full(g2.shape), _full(gb.shape),
                  pl.BlockSpec((1, 2, r8, r8, rows, FOURIER_W),
                               lambda i, j: (i, 0, 0, 0, jnp.minimum(j, nstep - 1), 0))],
        out_specs=pl.BlockSpec((1, RADIX, rows, FOURIER_W),
                               lambda i, j: (i, 0, jnp.maximum(j - nstep, 0), 0)),
        out_shape=jax.ShapeDtypeStruct((b, RADIX, RADIX, FOURIER_W), BF16),
        scratch_shapes=[pltpu.VMEM((2, RADIX, RADIX, FOURIER_W), F32)],
        compiler_params=_params("arbitrary", "arbitrary"),
        name="dft",
    )(g1, g2, gb, z6)


def _attn_body(qT_ref, k_ref, vT_ref, kc_ref, vcT_ref, *rest, tq, tkb, n_cast):
    for src, dst in zip(rest[:n_cast], rest[n_cast + 1:]):
        dst[...] = src[...].astype(dst.dtype).reshape(dst.shape)
    o_ref = rest[n_cast]
    nkb = k_ref.shape[2] // tkb
    nblk = nkb + 1

    def colmax(s):
        r = s.shape[0] // 8
        return jnp.max(jnp.max(s.reshape(8, r, s.shape[1]), axis=0), axis=0, keepdims=True)

    def update(carry, s, vT_blk):
        m, acc = carry
        m_new = jnp.maximum(m, colmax(s))
        p = jnp.exp2(s - m_new).astype(BF16)
        acc = jnp.exp2(m - m_new) * acc + _dot(vT_blk, p)
        return m_new, acc

    nq = qT_ref.shape[3] // tq

    def k_blk(i):
        return k_ref[0, 0, i * tkb:(i + 1) * tkb, :] if i < nkb else kc_ref[0, 0]

    def vT_blk(i):
        return vT_ref[0, 0, :, i * tkb:(i + 1) * tkb] if i < nkb else vcT_ref[0, 0]

    def q_cols(qi):
        return qT_ref[0, 0, :, qi * tq:(qi + 1) * tq]

    pending = [_dot(k_blk(i), q_cols(0)) for i in range(ATTN_AHEAD)]
    for qi in range(nq):
        qT = q_cols(qi)
        carry = (jnp.full((1, tq), -jnp.inf, F32), jnp.zeros((V_ROWS, tq), F32))
        for i in range(nblk):
            j = i + ATTN_AHEAD
            if j < nblk:
                pending.append(_dot(k_blk(j), qT))
            elif qi + 1 < nq:
                pending.append(_dot(k_blk(j - nblk), q_cols(qi + 1)))
            carry = update(carry, pending.pop(0), vT_blk(i))
        _, acc = carry
        o_ref[0, :, qi * tq:(qi + 1) * tq] = (
            acc[:V_DIM] / acc[V_DIM:V_DIM + 1]).astype(o_ref.dtype)


def _attention(qT, k, vT, kc, vcT, weights, tq, tkb):
    b, h, _, s = qT.shape
    sc = kc.shape[2]
    steps = b * h
    w_in_specs, w_out_specs, w_out_shapes = [], [], []
    for w, axis, chunk in weights:
        blk = tuple(n // steps if a == axis else n for a, n in enumerate(w.shape))
        spec = pl.BlockSpec(blk, (lambda i, j: (i * h + j, 0)) if axis == 0
                            else (lambda i, j: (0, i * h + j)))
        w_in_specs.append(spec)
        if chunk is None:
            w_out_specs.append(spec)
            w_out_shapes.append(jax.ShapeDtypeStruct(w.shape, BF16))
        else:
            per = chunk // blk[1]
            w_out_specs.append(pl.BlockSpec(
                (1,) + blk, lambda i, j, per=per: ((i * h + j) // per, 0, (i * h + j) % per)))
            w_out_shapes.append(jax.ShapeDtypeStruct((w.shape[1] // chunk, w.shape[0], chunk), BF16))
    outs = pl.pallas_call(
        functools.partial(_attn_body, tq=tq, tkb=tkb, n_cast=len(weights)),
        grid=(b, h),
        in_specs=[pl.BlockSpec((1, 1, HEAD_PAD, s), lambda i, j: (i, j, 0, 0)),
                  pl.BlockSpec((1, 1, s, HEAD_PAD), lambda i, j: (i, j, 0, 0)),
                  pl.BlockSpec((1, 1, V_ROWS, s), lambda i, j: (i, j, 0, 0)),
                  pl.BlockSpec((1, 1, sc, HEAD_PAD), lambda i, j: (i, j, 0, 0)),
                  pl.BlockSpec((1, 1, V_ROWS, sc), lambda i, j: (i, j, 0, 0))] + w_in_specs,
        out_specs=[pl.BlockSpec((1, V_DIM, s), lambda i, j: (i, j, 0))] + w_out_specs,
        out_shape=[jax.ShapeDtypeStruct((b, h * V_DIM, s), BF16)] + w_out_shapes,
        compiler_params=_params("arbitrary", "arbitrary"),
        name="attention",
    )(qT, k, vT, kc, vcT, *[w for w, _, _ in weights])
    return outs[0], outs[1:]


def _post_body(fm_ref, aT_ref, g_ref, x_ref, mod_ref, wfo_ref, wao_ref, wout_ref,
               g2_ref, wup_ref, wdn_ref, o_ref):
    d = x_ref.shape[-1]
    m = _mod_rows(mod_ref, pl.program_id(0))
    part = x_ref.shape[1] // POST_PARTS
    toks = [slice(p * part, (p + 1) * part) for p in range(POST_PARTS)]
    x1s, hbs = [], []
    for tok in toks:
        yf = _dot(fm_ref[0, tok, :], wfo_ref[...])
        ya = lax.dot_general(aT_ref[0, :, tok], wao_ref[...], (((0,), (0,)), ((), ())),
                             preferred_element_type=F32)
        g = g_ref[0, tok, :]
        y = g[:, :d].astype(F32) * yf + g[:, d:].astype(F32) * ya
        x1 = x_ref[0, tok, :] + m[2:3] * _dot(y.astype(BF16), wout_ref[...])
        x1s.append(x1)
        hbs.append(_norm_mod(x1, m[3:5], g2_ref[...]).astype(BF16))
    for tok, x1, hb in zip(toks, x1s, hbs):
        acc = jnp.zeros(x1.shape, F32)
        for c in range(wup_ref.shape[0]):
            u = jnp.maximum(_dot(hb, wup_ref[c]), 0.0)
            acc = acc + _dot((u * u).astype(BF16), wdn_ref[c])
        o_ref[0, tok, :] = x1 + m[5:6] * acc


def _post(fm, aT, gates, x, mod, wfo, wao, wout, g2, wup, wdn, ts):
    b, s, d = x.shape
    return pl.pallas_call(
        _post_body,
        grid=(b, s // ts),
        in_specs=[pl.BlockSpec((1, ts, FOURIER_W), lambda i, j: (i, j, 0)),
                  pl.BlockSpec((1, aT.shape[1], ts), lambda i, j: (i, 0, j)),
                  pl.BlockSpec((1, ts, 2 * d), lambda i, j: (i, j, 0)),
                  pl.BlockSpec((1, ts, d), lambda i, j: (i, j, 0)),
                  _full(mod.shape),
                  _full(wfo.shape), _full(wao.shape), _full(wout.shape),
                  _full(g2.shape), _full(wup.shape), _full(wdn.shape)],
        out_specs=pl.BlockSpec((1, ts, d), lambda i, j: (i, j, 0)),
        out_shape=jax.ShapeDtypeStruct((b, s, d), F32),
        compiler_params=_params("arbitrary", "arbitrary"),
        name="post",
    )(fm, aT, gates, x, mod, wfo, wao, wout, g2, wup, wdn)


@functools.lru_cache(maxsize=None)
def _dft_tables():
    j = np.arange(GROUP_W)
    ang = 2.0 * np.pi * np.outer(j, j) / GROUP_W
    cs = np.concatenate([np.cos(ang), -np.sin(ang)], axis=1) / math.sqrt(GROUP_W)

    r8 = DFT_RADIX_A
    d = np.arange(r8)
    eye = np.eye(DFT_ABLK)

    def real_block(theta):
        c, s = np.cos(theta), np.sin(theta)
        return np.kron(np.block([[c, s], [-s, c]]) / math.sqrt(r8), eye)

    g1 = real_block(2.0 * np.pi * np.outer(d, d) / r8)
    g2 = np.stack([real_block(2.0 * np.pi * (k2a * d[None, :] / RADIX + np.outer(d, d) / r8))
                   for k2a in range(r8)])

    n = RADIX * RADIX
    k1 = np.arange(RADIX)[:, None, None]
    k2 = np.arange(RADIX)[None, :, None]
    bb = np.arange(RADIX)[None, None, :]
    th = 2.0 * np.pi * ((bb * (RADIX * k1 + k2)) % n) / n
    gc = np.cos(th) / math.sqrt(RADIX)
    gs = np.sin(th) / math.sqrt(RADIX)
    nblk = RADIX // DFT_BLK
    gtab = np.zeros((nblk, RADIX, DFT_BLK, 2, DFT_BLK, RADIX), np.float32)
    for jb in range(nblk):
        for q in range(DFT_BLK):
            gtab[jb, :, q, 0, q, :] = gc[:, jb * DFT_BLK + q, :]
            gtab[jb, :, q, 1, q, :] = gs[:, jb * DFT_BLK + q, :]
    gtab = gtab.reshape(nblk, RADIX * DFT_BLK, 2 * DFT_BLK * RADIX)
    return cs.astype(np.float32), g1.astype(np.float32), g2.astype(np.float32), gtab


@functools.lru_cache(maxsize=None)
def _rope_tables_t(n):
    rows = n // GRID_W
    row = np.repeat(np.arange(rows, dtype=np.float64), GRID_W)
    col = np.tile(np.arange(GRID_W, dtype=np.float64), rows)
    n_freq = QK_ROPE // 4
    freqs = ROPE_THETA ** (-np.arange(n_freq, dtype=np.float64) / n_freq)
    ang_r = row[:, None] * freqs[None, :]
    ang_c = col[:, None] * freqs[None, :]
    ang = np.concatenate([ang_r, ang_r, ang_c, ang_c], axis=-1)
    return np.cos(ang).T.astype(np.float32), np.sin(ang).T.astype(np.float32)


def kernel(x, c, ctx, c_ctx, ada_w, ada_b, norm1_g, norm2_g, w_in, b_gate, w_fourier, q_norm_g, w_qb,
           kv_norm_g, w_kvb, q_gain, k_gain, w_mla_o, w_out, w_up, w_down):
    bsz, s, d = x.shape
    sc = ctx.shape[1]
    assert ada_w.shape[0] == 1 and s == RADIX * RADIX and d == D_MODEL

    cc = jnp.zeros((8, d), F32).at[:bsz].set(c).at[bsz].set(c_ctx)
    mod, wf, wq, wkv, wg, wqbT, wkbT, wvT = _adaln(
        cc, ada_w[0], ada_b[0][None, :], w_in[0].T, w_qb[0], w_kvb[0])

    g1 = norm1_g[0][None, :]
    g2 = norm2_g[0][None, :]
    qg = q_norm_g[0][None, :]
    kvg = kv_norm_g[0][None, :]
    bg = b_gate[0][None, :]
    qscale = (HEAD_QK ** -0.5) * math.log2(math.e)
    qgain = jnp.broadcast_to(jnp.pad(q_gain[0] * qscale, (0, HEAD_PAD - HEAD_QK))[:, None], (HEAD_PAD, LANES))
    kgn = jnp.broadcast_to(k_gain[0][:QK_NOPE, None], (QK_NOPE, LANES))
    kgr = jnp.broadcast_to(k_gain[0][QK_NOPE:, None], (QK_ROPE, LANES))
    cosT, sinT = (jnp.asarray(t) for t in _rope_tables_t(s))
    cos1 = jnp.asarray(np.ones((QK_ROPE, sc), np.float32))
    sin0 = jnp.asarray(np.zeros((QK_ROPE, sc), np.float32))

    cs, ga1, ga2, gtab = (jnp.asarray(t).astype(BF16) for t in _dft_tables())

    kc, vcT = _ctx_kv(ctx, mod, g1, wkv, kvg, wkbT, wvT, kgn, kgr, cos1, sin0)

    z, gates, qT, k, vT = _proj(x, mod, g1, wf, cs, wg, bg, wq, qg, wqbT, qgain, wkv, kvg, wkbT, wvT,
                                kgn, kgr, cosT, sinT, PROJ_TILE)

    r8 = DFT_RADIX_A
    fm = _dft(ga1, ga2, gtab, z.reshape(bsz, 2, r8, r8, RADIX, FOURIER_W)).reshape(bsz, s, FOURIER_W)

    aT, (wfo, wao, wout, wup, wdn) = _attention(
        qT, k, vT, kc, vcT,
        [(w_fourier[0], 0, None), (w_mla_o[0], 0, None), (w_out[0], 0, None),
         (w_up[0], 1, FF_CHUNK), (w_down[0], 0, None)],
        ATTN_TQ, ATTN_TK)
    wdn = wdn.reshape(-1, FF_CHUNK, d)

    return _post(fm, aT, gates, x, mod, wfo, wao, wout, g2, wup, wdn, POST_TILE)
```

```python
import functools
import math

import numpy as np
import jax
import jax.numpy as jnp
from jax import lax
from jax.experimental import pallas as pl
from jax.experimental.pallas import tpu as pltpu

F32 = jnp.float32
BF16 = jnp.bfloat16

D_MODEL = 1024
GRID_W = 64
N_GROUPS = 4
GROUP_W = 128
FOURIER_W = N_GROUPS * GROUP_W
N_HEADS = 8
QK_NOPE = 64
QK_ROPE = 32
HEAD_QK = QK_NOPE + QK_ROPE
LANES = 128
SUBLANES = 8
HEAD_PAD = LANES
V_DIM = 64
V_ROWS = V_DIM + 16
Q_LORA = 256
KV_LORA = 128
W_IN_GATES = FOURIER_W + Q_LORA + KV_LORA + QK_ROPE
ROPE_THETA = 10000.0
EPS = 1e-6
RADIX = 64
DFT_RADIX_A = 8
DFT_ABLK = 2 * SUBLANES
DFT_BLK = SUBLANES
DFT_STEP = 2
POST_TILE = 512
POST_PARTS = 2
PROJ_TILE = 1024
PROJ_PARTS = 4
W_IN_BLK = 256
GATE_CHUNK = W_IN_BLK
FF_CHUNK = 1024
ATTN_TQ = 512
ATTN_TK = 256
ATTN_AHEAD = 2

V7X_VMEM_LIMIT = 56 * 1024 * 1024


def _params(*sem):
    return pltpu.CompilerParams(dimension_semantics=sem, vmem_limit_bytes=V7X_VMEM_LIMIT)


def _dot(a, b):
    return jnp.dot(a, b, preferred_element_type=F32)


def _rms_lanes(x, n):
    return x * lax.rsqrt(jnp.sum(x * x, axis=-1, keepdims=True) * (1.0 / n) + EPS)


def _rot_half_rows(x):
    return jnp.concatenate([-x[8:16], x[0:8], -x[24:32], x[16:24]], axis=0)


def _adaln_body(c_ref, w_ref, b_ref, tf_ref, tq_ref, tkv_ref, tg_ref, wqb_ref, wkvb_ref,
                o_ref, wf_ref, wq_ref, wkv_ref, wg_ref, wqbT_ref, wkbT_ref, wvT_ref):
    c = c_ref[...]
    s = c * jax.nn.sigmoid(c)
    o_ref[...] = _dot(s, w_ref[...]) + b_ref[...]

    @pl.when(pl.program_id(0) == 0)
    def _():
        wf_ref[...] = tf_ref[...].T.astype(BF16)
        wq_ref[...] = tq_ref[...].T.astype(BF16)
        wkv_ref[...] = tkv_ref[...].T.astype(BF16)
        wqbT_ref[...] = wqb_ref[...].T.astype(BF16)
        wkvT = wkvb_ref[...].T
        hw = QK_NOPE + V_DIM
        wkbT_ref[...] = jnp.concatenate(
            [wkvT[h * hw:h * hw + QK_NOPE] for h in range(N_HEADS)], axis=0).astype(BF16)
        wvT_ref[...] = jnp.concatenate(
            [wkvT[h * hw + QK_NOPE:(h + 1) * hw] for h in range(N_HEADS)], axis=0).astype(BF16)

    wg_ref[0] = tg_ref[...].T.astype(BF16)


def _adaln(cc, ada_w, ada_b, w_inT, w_qb, w_kvb):
    rows, d = cc.shape
    const = lambda shape: pl.BlockSpec(shape, lambda j: (0, 0))
    wqbT_shape = (N_HEADS * HEAD_QK, w_qb.shape[0])
    wkT_shape = (N_HEADS * QK_NOPE, w_kvb.shape[0])
    wvT_shape = (N_HEADS * V_DIM, w_kvb.shape[0])
    n = ada_w.shape[1]
    n_gates = w_inT.shape[0] - W_IN_GATES
    steps = n_gates // W_IN_BLK
    tn = n // steps
    blk = lambda r, f: pl.BlockSpec((r, d), f)
    return pl.pallas_call(
        _adaln_body,
        grid=(steps,),
        in_specs=[pl.BlockSpec((rows, d), lambda j: (0, 0)),
                  pl.BlockSpec((d, tn), lambda j: (0, j)),
                  pl.BlockSpec((1, tn), lambda j: (0, j)),
                  blk(FOURIER_W, lambda j: (0, 0)),
                  blk(Q_LORA, lambda j: (FOURIER_W // Q_LORA, 0)),
                  blk(2 * KV_LORA, lambda j: ((FOURIER_W + Q_LORA) // (2 * KV_LORA), 0)),
                  pl.BlockSpec((pl.Element(W_IN_BLK), pl.Element(d)),
                               lambda j: (pl.multiple_of(W_IN_GATES + j * W_IN_BLK, SUBLANES), 0)),
                  const(w_qb.shape), const(w_kvb.shape)],
        out_specs=[pl.BlockSpec((rows, tn), lambda j: (0, j)),
                   const((d, FOURIER_W)), const((d, Q_LORA)), const((d, 2 * KV_LORA)),
                   pl.BlockSpec((1, d, W_IN_BLK), lambda j: (j, 0, 0)),
                   const(wqbT_shape), const(wkT_shape), const(wvT_shape)],
        out_shape=[jax.ShapeDtypeStruct((rows, n), F32),
                   jax.ShapeDtypeStruct((d, FOURIER_W), BF16),
                   jax.ShapeDtypeStruct((d, Q_LORA), BF16),
                   jax.ShapeDtypeStruct((d, 2 * KV_LORA), BF16),
                   jax.ShapeDtypeStruct((steps, d, W_IN_BLK), BF16),
                   jax.ShapeDtypeStruct(wqbT_shape, BF16),
                   jax.ShapeDtypeStruct(wkT_shape, BF16),
                   jax.ShapeDtypeStruct(wvT_shape, BF16)],
        compiler_params=_params("arbitrary"),
        name="adaln",
    )(cc, ada_w, ada_b, w_inT, w_inT, w_inT, w_inT, w_qb, w_kvb)


def _mod_rows(mod_ref, row):
    m = mod_ref[pl.ds(row, 1), :]
    d = m.shape[1] // 6
    return jnp.concatenate([m[:, k * d:(k + 1) * d] for k in range(6)], axis=0)


def _norm_mod(x, m, g):
    ms = jnp.sum(x * x, axis=-1, keepdims=True) * (1.0 / x.shape[-1])
    return (x * lax.rsqrt(ms + EPS)) * (g * (1.0 + m[1:2])) + m[0:1]


def _kv_latents(kvp, kvg_ref):
    kvn = _rms_lanes(kvp[:, :KV_LORA], KV_LORA) * kvg_ref[...]
    return kvn.T.astype(BF16), kvp[:, KV_LORA:].T[0:QK_ROPE]


def _lanes(ref, t):
    return jnp.concatenate([ref[...]] * (t // LANES), axis=1)


def _kv_heads(knT, vT, krT, kgn_ref, kgr_ref, cos_ref, sin_ref, k_ref, vT_ref, tok):
    t = knT.shape[1]
    sr = jnp.sum(krT * krT, axis=0, keepdims=True)
    krg = krT * _lanes(kgr_ref, t)
    kr_rot = krg * cos_ref[:, tok] + _rot_half_rows(krg) * sin_ref[:, tok]
    kgn = _lanes(kgn_ref, t)
    ones = jnp.ones((V_ROWS - V_DIM, t), F32)
    zpad = jnp.zeros((HEAD_PAD - HEAD_QK, t), F32)
    for h in range(N_HEADS):
        kn = knT[h * QK_NOPE:(h + 1) * QK_NOPE]
        ss = jnp.sum(kn * kn, axis=0, keepdims=True) + sr
        r = lax.rsqrt(ss * (1.0 / HEAD_QK) + EPS)
        kT = jnp.concatenate([kn * kgn * r, kr_rot * r, zpad], axis=0)
        k_ref[0, h, tok, :] = kT.T.astype(k_ref.dtype)
        vh = vT[h * V_DIM:(h + 1) * V_DIM]
        vT_ref[0, h, :, tok] = jnp.concatenate([vh, ones], axis=0).astype(vT_ref.dtype)


def _q_heads(qT, qgain_ref, cos_ref, sin_ref, qT_ref, tok):
    t = qT.shape[1]
    qgain, cos, sin = _lanes(qgain_ref, t), cos_ref[:, tok], sin_ref[:, tok]
    zpad = jnp.zeros((HEAD_PAD - HEAD_QK, t), F32)
    for h in range(N_HEADS):
        qh = qT[h * HEAD_QK:(h + 1) * HEAD_QK]
        r = lax.rsqrt(jnp.sum(qh * qh, axis=0, keepdims=True) * (1.0 / HEAD_QK) + EPS)
        qh = qh * qgain * r
        qr = qh[QK_NOPE:]
        qr = qr * cos + _rot_half_rows(qr) * sin
        qT_ref[0, h, :, tok] = jnp.concatenate([qh[:QK_NOPE], qr, zpad],
                                               axis=0).astype(qT_ref.dtype)


def _ctx_body(x_ref, mod_ref, g1_ref, wkv_ref, kvg_ref, wkbT_ref, wvT_ref, kgn_ref, kgr_ref,
              cos_ref, sin_ref, k_ref, vT_ref):
    ctx_row = pl.num_programs(0)
    hb = _norm_mod(x_ref[0], _mod_rows(mod_ref, ctx_row), g1_ref[...]).astype(BF16)
    kvnT, krT = _kv_latents(_dot(hb, wkv_ref[...]), kvg_ref)
    _kv_heads(_dot(wkbT_ref[...], kvnT), _dot(wvT_ref[...], kvnT), krT,
              kgn_ref, kgr_ref, cos_ref, sin_ref, k_ref, vT_ref, slice(None))


def _proj_body(x_ref, mod_ref, g1_ref, wf_ref, cs_ref, wg_ref, bg_ref,
               wq_ref, qg_ref, wqbT_ref, qgain_ref,
               wkv_ref, kvg_ref, wkbT_ref, wvT_ref, kgn_ref, kgr_ref, cos_ref, sin_ref,
               z_ref, g_ref, qT_ref, k_ref, vT_ref):
    part = x_ref.shape[1] // PROJ_PARTS
    m = _mod_rows(mod_ref, pl.program_id(0))
    for p in range(PROJ_PARTS):
        tok = slice(p * part, (p + 1) * part)
        hb = _norm_mod(x_ref[0, tok, :], m, g1_ref[...]).astype(BF16)
        qp = _dot(hb, wq_ref[...])
        kvp = _dot(hb, wkv_ref[...])
        f = _dot(hb, wf_ref[...]).astype(BF16)

        qnT = (_rms_lanes(qp, Q_LORA) * qg_ref[...]).T.astype(BF16)
        kvnT, krT = _kv_latents(kvp, kvg_ref)

        def gate_chunk(c):
            cols = slice(c * GATE_CHUNK, (c + 1) * GATE_CHUNK)
            g_ref[0, tok, cols] = jax.nn.sigmoid(
                _dot(hb, wg_ref[c]) + bg_ref[:, cols]).astype(g_ref.dtype)

        gate_chunk(0)

        qT = _dot(wqbT_ref[...], qnT)
        knT = _dot(wkbT_ref[...], kvnT)
        vT = _dot(wvT_ref[...], kvnT)

        _q_heads(qT, qgain_ref, cos_ref, sin_ref, qT_ref, tok)
        _kv_heads(knT, vT, krT, kgn_ref, kgr_ref, cos_ref, sin_ref, k_ref, vT_ref, tok)
        for c in range(1, g_ref.shape[2] // GATE_CHUNK):
            gate_chunk(c)

        for g in range(N_GROUPS):
            cols = slice(g * GROUP_W, (g + 1) * GROUP_W)
            z = _dot(f[:, cols], cs_ref[...])
            z_ref[0, 0, tok, cols] = z[:, :GROUP_W].astype(z_ref.dtype)
            z_ref[0, 1, tok, cols] = z[:, GROUP_W:].astype(z_ref.dtype)


def _full(shape):
    nd = len(shape)
    return pl.BlockSpec(shape, lambda *_: (0,) * nd, pipeline_mode=pl.Buffered(1))


def _ctx_kv(ctx, mod, g1, wkv, kvg, wkbT, wvT, kgn, kgr, cosT, sinT):
    b, t, d = ctx.shape
    return pl.pallas_call(
        _ctx_body,
        grid=(b,),
        in_specs=[pl.BlockSpec((1, t, d), lambda i: (i, 0, 0)),
                  _full(mod.shape),
                  _full(g1.shape), _full(wkv.shape), _full(kvg.shape), _full(wkbT.shape),
                  _full(wvT.shape), _full(kgn.shape), _full(kgr.shape),
                  _full(cosT.shape), _full(sinT.shape)],
        out_specs=[pl.BlockSpec((1, N_HEADS, t, HEAD_PAD), lambda i: (i, 0, 0, 0)),
                   pl.BlockSpec((1, N_HEADS, V_ROWS, t), lambda i: (i, 0, 0, 0))],
        out_shape=[jax.ShapeDtypeStruct((b, N_HEADS, t, HEAD_PAD), BF16),
                   jax.ShapeDtypeStruct((b, N_HEADS, V_ROWS, t), BF16)],
        compiler_params=_params("arbitrary"),
        name="ctx_kv",
    )(ctx, mod, g1, wkv, kvg, wkbT, wvT, kgn, kgr, cosT, sinT)


def _proj(x, mod, g1, wf, cs, wg, bg, wq, qg, wqbT, qgain, wkv, kvg, wkbT, wvT, kgn, kgr,
          cosT, sinT, ts):
    b, s, d = x.shape
    assert wg.shape == (2 * d // GATE_CHUNK, d, GATE_CHUNK)
    tok = lambda shape: pl.BlockSpec(shape, lambda i, j: (0, j))
    return pl.pallas_call(
        _proj_body,
        grid=(b, s // ts),
        in_specs=[pl.BlockSpec((1, ts, d), lambda i, j: (i, j, 0)),
                  _full(mod.shape),
                  _full(g1.shape), _full(wf.shape), _full(cs.shape), _full(wg.shape),
                  _full(bg.shape), _full(wq.shape), _full(qg.shape), _full(wqbT.shape),
                  _full(qgain.shape),
                  _full(wkv.shape), _full(kvg.shape), _full(wkbT.shape), _full(wvT.shape),
                  _full(kgn.shape), _full(kgr.shape), tok((QK_ROPE, ts)), tok((QK_ROPE, ts))],
        out_specs=[pl.BlockSpec((1, 2, ts, FOURIER_W), lambda i, j: (i, 0, j, 0)),
                   pl.BlockSpec((1, ts, 2 * d), lambda i, j: (i, j, 0)),
                   pl.BlockSpec((1, N_HEADS, HEAD_PAD, ts), lambda i, j: (i, 0, 0, j)),
                   pl.BlockSpec((1, N_HEADS, ts, HEAD_PAD), lambda i, j: (i, 0, j, 0)),
                   pl.BlockSpec((1, N_HEADS, V_ROWS, ts), lambda i, j: (i, 0, 0, j))],
        out_shape=[jax.ShapeDtypeStruct((b, 2, s, FOURIER_W), BF16),
                   jax.ShapeDtypeStruct((b, s, 2 * d), BF16),
                   jax.ShapeDtypeStruct((b, N_HEADS, HEAD_PAD, s), BF16),
                   jax.ShapeDtypeStruct((b, N_HEADS, s, HEAD_PAD), BF16),
                   jax.ShapeDtypeStruct((b, N_HEADS, V_ROWS, s), BF16)],
        compiler_params=_params("arbitrary", "arbitrary"),
        name="proj",
    )(x, mod, g1, wf, cs, wg, bg, wq, qg, wqbT, qgain, wkv, kvg, wkbT, wvT, kgn, kgr,
      cosT, sinT)


def _dft_body(g1_ref, g2_ref, gb_ref, z_ref, o_ref, a_scr):
    step_rows = DFT_BLK * DFT_STEP
    nstep = RADIX // step_rows
    j = pl.program_id(1)
    r8 = DFT_RADIX_A
    slab = (2, r8, DFT_ABLK, FOURIER_W)

    @pl.when(j < nstep)
    def _():
        for t in range(step_rows // DFT_ABLK):
            rows = slice(t * DFT_ABLK, (t + 1) * DFT_ABLK)
            off = pl.multiple_of(j * step_rows + t * DFT_ABLK, DFT_ABLK)
            t1 = []
            for a2 in range(r8):
                z = z_ref[0, :, :, a2, rows, :].reshape(2 * r8 * DFT_ABLK, FOURIER_W)
                t1.append(_dot(g1_ref[...], z).reshape(slab))
            for k2a in range(r8):
                x = jnp.concatenate([t1[a2][r, k2a] for r in range(2) for a2 in range(r8)], axis=0)
                a = _dot(g2_ref[k2a], x.astype(BF16)).reshape(slab)
                for r in range(2):
                    for k2b in range(r8):
                        a_scr[r, k2a + r8 * k2b, pl.ds(off, DFT_ABLK), :] = a[r, k2b]

    @pl.when(j >= nstep)
    def _():
        ys = []
        for t in range(DFT_STEP):
            blk = (j - nstep) * DFT_STEP + t
            off = pl.multiple_of(blk * DFT_BLK, DFT_BLK)
            a = a_scr[:, pl.ds(off, DFT_BLK), :, :]
            a = a.reshape(2 * DFT_BLK * RADIX, FOURIER_W).astype(BF16)
            y = _dot(gb_ref[blk], a)
            ys.append(y.reshape(RADIX, DFT_BLK, FOURIER_W))
        o_ref[0] = jnp.concatenate(ys, axis=1).astype(o_ref.dtype)


def _dft(g1, g2, gb, z6):
    b = z6.shape[0]
    nstep = RADIX // (DFT_BLK * DFT_STEP)
    rows = DFT_BLK * DFT_STEP
    r8 = DFT_RADIX_A
    assert rows % DFT_ABLK == 0 and z6.shape[2:4] == (r8, r8)
    return pl.pallas_call(
        _dft_body,
        grid=(b, 2 * nstep),
        in_specs=[_full(g1.shape), _full(g2.shape), _full(gb.shape),
                  pl.BlockSpec((1, 2, r8, r8, rows, FOURIER_W),
                               lambda i, j: (i, 0, 0, 0, jnp.minimum(j, nstep - 1), 0))],
        out_specs=pl.BlockSpec((1, RADIX, rows, FOURIER_W),
                               lambda i, j: (i, 0, jnp.maximum(j - nstep, 0), 0)),
        out_shape=jax.ShapeDtypeStruct((b, RADIX, RADIX, FOURIER_W), BF16),
        scratch_shapes=[pltpu.VMEM((2, RADIX, RADIX, FOURIER_W), F32)],
        compiler_params=_params("arbitrary", "arbitrary"),
        name="dft",
    )(g1, g2, gb, z6)


def _attn_body(qT_ref, k_ref, vT_ref, kc_ref, vcT_ref, *rest, tq, tkb, n_cast):
    for src, dst in zip(rest[:n_cast], rest[n_cast + 1:]):
        dst[...] = src[...].astype(dst.dtype).reshape(dst.shape)
    o_ref = rest[n_cast]
    nkb = k_ref.shape[2] // tkb
    nblk = nkb + 1

    def colmax(s):
        r = s.shape[0] // SUBLANES
        return jnp.max(jnp.max(s.reshape(SUBLANES, r, s.shape[1]), axis=0), axis=0, keepdims=True)

    def update(carry, s, vT_blk):
        m, acc = carry
        m_new = jnp.maximum(m, colmax(s))
        p = jnp.exp2(s - m_new).astype(BF16)
        acc = jnp.exp2(m - m_new) * acc + _dot(vT_blk, p)
        return m_new, acc

    nq = qT_ref.shape[3] // tq

    def k_blk(i):
        return k_ref[0, 0, i * tkb:(i + 1) * tkb, :] if i < nkb else kc_ref[0, 0]

    def vT_blk(i):
        return vT_ref[0, 0, :, i * tkb:(i + 1) * tkb] if i < nkb else vcT_ref[0, 0]

    def q_cols(qi):
        return qT_ref[0, 0, :, qi * tq:(qi + 1) * tq]

    pending = [_dot(k_blk(i), q_cols(0)) for i in range(ATTN_AHEAD)]
    for qi in range(nq):
        qT = q_cols(qi)
        carry = (jnp.full((1, tq), -jnp.inf, F32), jnp.zeros((V_ROWS, tq), F32))
        for i in range(nblk):
            j = i + ATTN_AHEAD
            if j < nblk:
                pending.append(_dot(k_blk(j), qT))
            elif qi + 1 < nq:
                pending.append(_dot(k_blk(j - nblk), q_cols(qi + 1)))
            carry = update(carry, pending.pop(0), vT_blk(i))
        _, acc = carry
        o_ref[0, :, qi * tq:(qi + 1) * tq] = (
            acc[:V_DIM] / acc[V_DIM:V_DIM + 1]).astype(o_ref.dtype)


def _attention(qT, k, vT, kc, vcT, weights, tq, tkb):
    b, h, _, s = qT.shape
    sc = kc.shape[2]
    steps = b * h
    w_in_specs, w_out_specs, w_out_shapes = [], [], []
    for w, axis, chunk in weights:
        blk = tuple(n // steps if a == axis else n for a, n in enumerate(w.shape))
        spec = pl.BlockSpec(blk, (lambda i, j: (i * h + j, 0)) if axis == 0
                            else (lambda i, j: (0, i * h + j)))
        w_in_specs.append(spec)
        if chunk is None:
            w_out_specs.append(spec)
            w_out_shapes.append(jax.ShapeDtypeStruct(w.shape, BF16))
        else:
            per = chunk // blk[1]
            w_out_specs.append(pl.BlockSpec(
                (1,) + blk, lambda i, j, per=per: ((i * h + j) // per, 0, (i * h + j) % per)))
            w_out_shapes.append(jax.ShapeDtypeStruct((w.shape[1] // chunk, w.shape[0], chunk), BF16))
    outs = pl.pallas_call(
        functools.partial(_attn_body, tq=tq, tkb=tkb, n_cast=len(weights)),
        grid=(b, h),
        in_specs=[pl.BlockSpec((1, 1, HEAD_PAD, s), lambda i, j: (i, j, 0, 0)),
                  pl.BlockSpec((1, 1, s, HEAD_PAD), lambda i, j: (i, j, 0, 0)),
                  pl.BlockSpec((1, 1, V_ROWS, s), lambda i, j: (i, j, 0, 0)),
                  pl.BlockSpec((1, 1, sc, HEAD_PAD), lambda i, j: (i, j, 0, 0)),
                  pl.BlockSpec((1, 1, V_ROWS, sc), lambda i, j: (i, j, 0, 0))] + w_in_specs,
        out_specs=[pl.BlockSpec((1, V_DIM, s), lambda i, j: (i, j, 0))] + w_out_specs,
        out_shape=[jax.ShapeDtypeStruct((b, h * V_DIM, s), BF16)] + w_out_shapes,
        compiler_params=_params("arbitrary", "arbitrary"),
        name="attention",
    )(qT, k, vT, kc, vcT, *[w for w, _, _ in weights])
    return outs[0], outs[1:]


def _post_body(fm_ref, aT_ref, g_ref, x_ref, mod_ref, wfo_ref, wao_ref, wout_ref,
               g2_ref, wup_ref, wdn_ref, o_ref):
    d = x_ref.shape[-1]
    m = _mod_rows(mod_ref, pl.program_id(0))
    part = x_ref.shape[1] // POST_PARTS
    toks = [slice(p * part, (p + 1) * part) for p in range(POST_PARTS)]
    x1s, hbs = [], []
    for tok in toks:
        yf = _dot(fm_ref[0, tok, :], wfo_ref[...])
        ya = lax.dot_general(aT_ref[0, :, tok], wao_ref[...], (((0,), (0,)), ((), ())),
                             preferred_element_type=F32)
        g = g_ref[0, tok, :]
        y = g[:, :d].astype(F32) * yf + g[:, d:].astype(F32) * ya
        x1 = x_ref[0, tok, :] + m[2:3] * _dot(y.astype(BF16), wout_ref[...])
        x1s.append(x1)
        hbs.append(_norm_mod(x1, m[3:5], g2_ref[...]).astype(BF16))
    for tok, x1, hb in zip(toks, x1s, hbs):
        acc = jnp.zeros(x1.shape, F32)
        for c in range(wup_ref.shape[0]):
            u = jnp.maximum(_dot(hb, wup_ref[c]), 0.0)
            acc = acc + _dot((u * u).astype(BF16), wdn_ref[c])
        o_ref[0, tok, :] = x1 + m[5:6] * acc


def _post(fm, aT, gates, x, mod, wfo, wao, wout, g2, wup, wdn, ts):
    b, s, d = x.shape
    return pl.pallas_call(
        _post_body,
        grid=(b, s // ts),
        in_specs=[pl.BlockSpec((1, ts, FOURIER_W), lambda i, j: (i, j, 0)),
                  pl.BlockSpec((1, aT.shape[1], ts), lambda i, j: (i, 0, j)),
                  pl.BlockSpec((1, ts, 2 * d), lambda i, j: (i, j, 0)),
                  pl.BlockSpec((1, ts, d), lambda i, j: (i, j, 0)),
                  _full(mod.shape),
                  _full(wfo.shape), _full(wao.shape), _full(wout.shape),
                  _full(g2.shape), _full(wup.shape), _full(wdn.shape)],
        out_specs=pl.BlockSpec((1, ts, d), lambda i, j: (i, j, 0)),
        out_shape=jax.ShapeDtypeStruct((b, s, d), F32),
        compiler_params=_params("arbitrary", "arbitrary"),
        name="post",
    )(fm, aT, gates, x, mod, wfo, wao, wout, g2, wup, wdn)


@functools.lru_cache(maxsize=None)
def _dft_tables():
    j = np.arange(GROUP_W)
    ang = 2.0 * np.pi * np.outer(j, j) / GROUP_W
    cs = np.concatenate([np.cos(ang), -np.sin(ang)], axis=1) / math.sqrt(GROUP_W)

    r8 = DFT_RADIX_A
    d = np.arange(r8)
    eye = np.eye(DFT_ABLK)

    def real_block(theta):
        c, s = np.cos(theta), np.sin(theta)
        return np.kron(np.block([[c, s], [-s, c]]) / math.sqrt(r8), eye)

    g1 = real_block(2.0 * np.pi * np.outer(d, d) / r8)
    g2 = np.stack([real_block(2.0 * np.pi * (k2a * d[None, :] / RADIX + np.outer(d, d) / r8))
                   for k2a in range(r8)])

    n = RADIX * RADIX
    k1 = np.arange(RADIX)[:, None, None]
    k2 = np.arange(RADIX)[None, :, None]
    bb = np.arange(RADIX)[None, None, :]
    th = 2.0 * np.pi * ((bb * (RADIX * k1 + k2)) % n) / n
    gc = np.cos(th) / math.sqrt(RADIX)
    gs = np.sin(th) / math.sqrt(RADIX)
    nblk = RADIX // DFT_BLK
    gtab = np.zeros((nblk, RADIX, DFT_BLK, 2, DFT_BLK, RADIX), np.float32)
    for jb in range(nblk):
        for q in range(DFT_BLK):
            gtab[jb, :, q, 0, q, :] = gc[:, jb * DFT_BLK + q, :]
            gtab[jb, :, q, 1, q, :] = gs[:, jb * DFT_BLK + q, :]
    gtab = gtab.reshape(nblk, RADIX * DFT_BLK, 2 * DFT_BLK * RADIX)
    return cs.astype(np.float32), g1.astype(np.float32), g2.astype(np.float32), gtab


@functools.lru_cache(maxsize=None)
def _rope_tables_t(n):
    rows = n // GRID_W
    row = np.repeat(np.arange(rows, dtype=np.float64), GRID_W)
    col = np.tile(np.arange(GRID_W, dtype=np.float64), rows)
    n_freq = QK_ROPE // 4
    freqs = ROPE_THETA ** (-np.arange(n_freq, dtype=np.float64) / n_freq)
    ang_r = row[:, None] * freqs[None, :]
    ang_c = col[:, None] * freqs[None, :]
    ang = np.concatenate([ang_r, ang_r, ang_c, ang_c], axis=-1)
    return np.cos(ang).T.astype(np.float32), np.sin(ang).T.astype(np.float32)


def kernel(x, c, ctx, c_ctx, ada_w, ada_b, norm1_g, norm2_g, w_in, b_gate, w_fourier, q_norm_g, w_qb,
           kv_norm_g, w_kvb, q_gain, k_gain, w_mla_o, w_out, w_up, w_down):
    bsz, s, d = x.shape
    sc = ctx.shape[1]
    assert ada_w.shape[0] == 1 and s == RADIX * RADIX and d == D_MODEL

    cc = jnp.zeros((8, d), F32).at[:bsz].set(c).at[bsz].set(c_ctx)
    mod, wf, wq, wkv, wg, wqbT, wkbT, wvT = _adaln(
        cc, ada_w[0], ada_b[0][None, :], w_in[0].T, w_qb[0], w_kvb[0])

    g1 = norm1_g[0][None, :]
    g2 = norm2_g[0][None, :]
    qg = q_norm_g[0][None, :]
    kvg = kv_norm_g[0][None, :]
    bg = b_gate[0][None, :]
    qscale = (HEAD_QK ** -0.5) * math.log2(math.e)
    qgain = jnp.broadcast_to((q_gain[0] * qscale)[:, None], (HEAD_QK, LANES))
    kgn = jnp.broadcast_to(k_gain[0][:QK_NOPE, None], (QK_NOPE, LANES))
    kgr = jnp.broadcast_to(k_gain[0][QK_NOPE:, None], (QK_ROPE, LANES))
    cosT, sinT = (jnp.asarray(t) for t in _rope_tables_t(s))
    cos1 = jnp.asarray(np.ones((QK_ROPE, sc), np.float32))
    sin0 = jnp.asarray(np.zeros((QK_ROPE, sc), np.float32))

    cs, ga1, ga2, gtab = (jnp.asarray(t).astype(BF16) for t in _dft_tables())

    kc, vcT = _ctx_kv(ctx, mod, g1, wkv, kvg, wkbT, wvT, kgn, kgr, cos1, sin0)

    z, gates, qT, k, vT = _proj(x, mod, g1, wf, cs, wg, bg, wq, qg, wqbT, qgain, wkv, kvg, wkbT, wvT,
                                kgn, kgr, cosT, sinT, PROJ_TILE)

    r8 = DFT_RADIX_A
    fm = _dft(ga1, ga2, gtab, z.reshape(bsz, 2, r8, r8, RADIX, FOURIER_W)).reshape(bsz, s, FOURIER_W)

    aT, (wfo, wao, wout, wup, wdn) = _attention(
        qT, k, vT, kc, vcT,
        [(w_fourier[0], 0, None), (w_mla_o[0], 0, None), (w_out[0], 0, None),
         (w_up[0], 1, FF_CHUNK), (w_down[0], 0, None)],
        ATTN_TQ, ATTN_TK)
    wdn = wdn.reshape(-1, FF_CHUNK, d)

    return _post(fm, aT, gates, x, mod, wfo, wao, wout, g2, wup, wdn, POST_TILE)
```

```python
import functools
import math

import numpy as np
import jax
import jax.numpy as jnp
from jax import lax
from jax.experimental import pallas as pl
from jax.experimental.pallas import tpu as pltpu

F32 = jnp.float32
BF16 = jnp.bfloat16

D_MODEL = 1024
GRID_W = 64
N_GROUPS = 4
GROUP_W = 128
FOURIER_W = N_GROUPS * GROUP_W
N_HEADS = 8
QK_NOPE = 64
QK_ROPE = 32
HEAD_QK = QK_NOPE + QK_ROPE
LANES = 128
SUBLANES = 8
HEAD_PAD = LANES
V_DIM = 64
V_ROWS = V_DIM + 16
Q_LORA = 256
KV_LORA = 128
W_IN_GATES = FOURIER_W + Q_LORA + KV_LORA + QK_ROPE
ROPE_THETA = 10000.0
EPS = 1e-6
RADIX = 64
DFT_RADIX_A = 8
DFT_ABLK = 2 * SUBLANES
DFT_BLK = SUBLANES
DFT_STEP = 4
POST_TILE = 512
POST_PARTS = 2
PROJ_TILE = 1024
PROJ_PARTS = 4
W_IN_BLK = 256
GATE_CHUNK = W_IN_BLK
FF_CHUNK = 1024
ATTN_TQ = 512
ATTN_TK = 256
ATTN_AHEAD = 2

V7X_VMEM_LIMIT = 56 * 1024 * 1024


def _params(*sem):
    return pltpu.CompilerParams(dimension_semantics=sem, vmem_limit_bytes=V7X_VMEM_LIMIT)


def _dot(a, b):
    return jnp.dot(a, b, preferred_element_type=F32)


def _rms_lanes(x, n):
    return x * lax.rsqrt(jnp.sum(x * x, axis=-1, keepdims=True) * (1.0 / n) + EPS)


def _rot_half_rows(x):
    return jnp.concatenate([-x[8:16], x[0:8], -x[24:32], x[16:24]], axis=0)


def _adaln_body(c_ref, w_ref, b_ref, tf_ref, tq_ref, tkv_ref, tg_ref, wqb_ref, wkvb_ref,
                o_ref, wf_ref, wq_ref, wkv_ref, wg_ref, wqbT_ref, wkbT_ref, wvT_ref):
    c = c_ref[...]
    s = c * jax.nn.sigmoid(c)
    o_ref[...] = _dot(s, w_ref[...]) + b_ref[...]

    @pl.when(pl.program_id(0) == 0)
    def _():
        wf_ref[...] = tf_ref[...].T.astype(BF16)
        wq_ref[...] = tq_ref[...].T.astype(BF16)
        wkv_ref[...] = tkv_ref[...].T.astype(BF16)
        wqbT_ref[...] = wqb_ref[...].T.astype(BF16)
        wkvT = wkvb_ref[...].T
        hw = QK_NOPE + V_DIM
        wkbT_ref[...] = jnp.concatenate(
            [wkvT[h * hw:h * hw + QK_NOPE] for h in range(N_HEADS)], axis=0).astype(BF16)
        wvT_ref[...] = jnp.concatenate(
            [wkvT[h * hw + QK_NOPE:(h + 1) * hw] for h in range(N_HEADS)], axis=0).astype(BF16)

    wg_ref[0] = tg_ref[...].T.astype(BF16)


def _adaln(cc, ada_w, ada_b, w_inT, w_qb, w_kvb):
    rows, d = cc.shape
    const = lambda shape: pl.BlockSpec(shape, lambda j: (0, 0))
    wqbT_shape = (N_HEADS * HEAD_QK, w_qb.shape[0])
    wkT_shape = (N_HEADS * QK_NOPE, w_kvb.shape[0])
    wvT_shape = (N_HEADS * V_DIM, w_kvb.shape[0])
    n = ada_w.shape[1]
    n_gates = w_inT.shape[0] - W_IN_GATES
    steps = n_gates // W_IN_BLK
    tn = n // steps
    blk = lambda r, f: pl.BlockSpec((r, d), f)
    return pl.pallas_call(
        _adaln_body,
        grid=(steps,),
        in_specs=[pl.BlockSpec((rows, d), lambda j: (0, 0)),
                  pl.BlockSpec((d, tn), lambda j: (0, j)),
                  pl.BlockSpec((1, tn), lambda j: (0, j)),
                  blk(FOURIER_W, lambda j: (0, 0)),
                  blk(Q_LORA, lambda j: (FOURIER_W // Q_LORA, 0)),
                  blk(2 * KV_LORA, lambda j: ((FOURIER_W + Q_LORA) // (2 * KV_LORA), 0)),
                  pl.BlockSpec((pl.Element(W_IN_BLK), pl.Element(d)),
                               lambda j: (pl.multiple_of(W_IN_GATES + j * W_IN_BLK, SUBLANES), 0)),
                  const(w_qb.shape), const(w_kvb.shape)],
        out_specs=[pl.BlockSpec((rows, tn), lambda j: (0, j)),
                   const((d, FOURIER_W)), const((d, Q_LORA)), const((d, 2 * KV_LORA)),
                   pl.BlockSpec((1, d, W_IN_BLK), lambda j: (j, 0, 0)),
                   const(wqbT_shape), const(wkT_shape), const(wvT_shape)],
        out_shape=[jax.ShapeDtypeStruct((rows, n), F32),
                   jax.ShapeDtypeStruct((d, FOURIER_W), BF16),
                   jax.ShapeDtypeStruct((d, Q_LORA), BF16),
                   jax.ShapeDtypeStruct((d, 2 * KV_LORA), BF16),
                   jax.ShapeDtypeStruct((steps, d, W_IN_BLK), BF16),
                   jax.ShapeDtypeStruct(wqbT_shape, BF16),
                   jax.ShapeDtypeStruct(wkT_shape, BF16),
                   jax.ShapeDtypeStruct(wvT_shape, BF16)],
        compiler_params=_params("arbitrary"),
        name="adaln",
    )(cc, ada_w, ada_b, w_inT, w_inT, w_inT, w_inT, w_qb, w_kvb)


def _mod_rows(mod_ref, row):
    m = mod_ref[pl.ds(row, 1), :]
    d = m.shape[1] // 6
    return jnp.concatenate([m[:, k * d:(k + 1) * d] for k in range(6)], axis=0)


def _norm_mod(x, m, g):
    ms = jnp.sum(x * x, axis=-1, keepdims=True) * (1.0 / x.shape[-1])
    return (x * lax.rsqrt(ms + EPS)) * (g * (1.0 + m[1:2])) + m[0:1]


def _kv_latents(kvp, kvg_ref):
    kvn = _rms_lanes(kvp[:, :KV_LORA], KV_LORA) * kvg_ref[...]
    return kvn.T.astype(BF16), kvp[:, KV_LORA:].T[0:QK_ROPE]


def _lanes(ref, t):
    return jnp.concatenate([ref[...]] * (t // LANES), axis=1)


def _kv_heads(knT, vT, krT, kgn_ref, kgr_ref, cos_ref, sin_ref, k_ref, vT_ref, tok):
    t = knT.shape[1]
    sr = jnp.sum(krT * krT, axis=0, keepdims=True)
    krg = krT * _lanes(kgr_ref, t)
    kr_rot = krg * cos_ref[:, tok] + _rot_half_rows(krg) * sin_ref[:, tok]
    kgn = _lanes(kgn_ref, t)
    ones = jnp.ones((V_ROWS - V_DIM, t), F32)
    zpad = jnp.zeros((HEAD_PAD - HEAD_QK, t), F32)
    for h in range(N_HEADS):
        kn = knT[h * QK_NOPE:(h + 1) * QK_NOPE]
        ss = jnp.sum(kn * kn, axis=0, keepdims=True) + sr
        r = lax.rsqrt(ss * (1.0 / HEAD_QK) + EPS)
        kT = jnp.concatenate([kn * kgn * r, kr_rot * r, zpad], axis=0)
        k_ref[0, h, tok, :] = kT.T.astype(k_ref.dtype)
        vh = vT[h * V_DIM:(h + 1) * V_DIM]
        vT_ref[0, h, :, tok] = jnp.concatenate([vh, ones], axis=0).astype(vT_ref.dtype)


def _q_heads(qT, qgain_ref, cos_ref, sin_ref, qT_ref, tok):
    t = qT.shape[1]
    qgain, cos, sin = _lanes(qgain_ref, t), cos_ref[:, tok], sin_ref[:, tok]
    zpad = jnp.zeros((HEAD_PAD - HEAD_QK, t), F32)
    for h in range(N_HEADS):
        qh = qT[h * HEAD_QK:(h + 1) * HEAD_QK]
        r = lax.rsqrt(jnp.sum(qh * qh, axis=0, keepdims=True) * (1.0 / HEAD_QK) + EPS)
        qh = qh * qgain * r
        qr = qh[QK_NOPE:]
        qr = qr * cos + _rot_half_rows(qr) * sin
        qT_ref[0, h, :, tok] = jnp.concatenate([qh[:QK_NOPE], qr, zpad],
                                               axis=0).astype(qT_ref.dtype)


def _ctx_body(x_ref, mod_ref, g1_ref, wkv_ref, kvg_ref, wkbT_ref, wvT_ref, kgn_ref, kgr_ref,
              cos_ref, sin_ref, k_ref, vT_ref):
    ctx_row = pl.num_programs(0)
    hb = _norm_mod(x_ref[0], _mod_rows(mod_ref, ctx_row), g1_ref[...]).astype(BF16)
    kvnT, krT = _kv_latents(_dot(hb, wkv_ref[...]), kvg_ref)
    _kv_heads(_dot(wkbT_ref[...], kvnT), _dot(wvT_ref[...], kvnT), krT,
              kgn_ref, kgr_ref, cos_ref, sin_ref, k_ref, vT_ref, slice(None))


def _proj_body(x_ref, mod_ref, g1_ref, wf_ref, cs_ref, wg_ref, bg_ref,
               wq_ref, qg_ref, wqbT_ref, qgain_ref,
               wkv_ref, kvg_ref, wkbT_ref, wvT_ref, kgn_ref, kgr_ref, cos_ref, sin_ref,
               z_ref, g_ref, qT_ref, k_ref, vT_ref):
    part = x_ref.shape[1] // PROJ_PARTS
    m = _mod_rows(mod_ref, pl.program_id(0))
    for p in range(PROJ_PARTS):
        tok = slice(p * part, (p + 1) * part)
        hb = _norm_mod(x_ref[0, tok, :], m, g1_ref[...]).astype(BF16)
        qp = _dot(hb, wq_ref[...])
        kvp = _dot(hb, wkv_ref[...])
        f = _dot(hb, wf_ref[...]).astype(BF16)

        qnT = (_rms_lanes(qp, Q_LORA) * qg_ref[...]).T.astype(BF16)
        kvnT, krT = _kv_latents(kvp, kvg_ref)

        def gate_chunk(c):
            cols = slice(c * GATE_CHUNK, (c + 1) * GATE_CHUNK)
            g_ref[0, tok, cols] = jax.nn.sigmoid(
                _dot(hb, wg_ref[c]) + bg_ref[:, cols]).astype(g_ref.dtype)

        gate_chunk(0)

        qT = _dot(wqbT_ref[...], qnT)
        knT = _dot(wkbT_ref[...], kvnT)
        vT = _dot(wvT_ref[...], kvnT)

        _q_heads(qT, qgain_ref, cos_ref, sin_ref, qT_ref, tok)
        _kv_heads(knT, vT, krT, kgn_ref, kgr_ref, cos_ref, sin_ref, k_ref, vT_ref, tok)
        for c in range(1, g_ref.shape[2] // GATE_CHUNK):
            gate_chunk(c)

        for g in range(N_GROUPS):
            cols = slice(g * GROUP_W, (g + 1) * GROUP_W)
            z = _dot(f[:, cols], cs_ref[...])
            z_ref[0, 0, tok, cols] = z[:, :GROUP_W].astype(z_ref.dtype)
            z_ref[0, 1, tok, cols] = z[:, GROUP_W:].astype(z_ref.dtype)


def _full(shape):
    nd = len(shape)
    return pl.BlockSpec(shape, lambda *_: (0,) * nd, pipeline_mode=pl.Buffered(1))


def _ctx_kv(ctx, mod, g1, wkv, kvg, wkbT, wvT, kgn, kgr, cosT, sinT):
    b, t, d = ctx.shape
    return pl.pallas_call(
        _ctx_body,
        grid=(b,),
        in_specs=[pl.BlockSpec((1, t, d), lambda i: (i, 0, 0)),
                  _full(mod.shape),
                  _full(g1.shape), _full(wkv.shape), _full(kvg.shape), _full(wkbT.shape),
                  _full(wvT.shape), _full(kgn.shape), _full(kgr.shape),
                  _full(cosT.shape), _full(sinT.shape)],
        out_specs=[pl.BlockSpec((1, N_HEADS, t, HEAD_PAD), lambda i: (i, 0, 0, 0)),
                   pl.BlockSpec((1, N_HEADS, V_ROWS, t), lambda i: (i, 0, 0, 0))],
        out_shape=[jax.ShapeDtypeStruct((b, N_HEADS, t, HEAD_PAD), BF16),
                   jax.ShapeDtypeStruct((b, N_HEADS, V_ROWS, t), BF16)],
        compiler_params=_params("arbitrary"),
        name="ctx_kv",
    )(ctx, mod, g1, wkv, kvg, wkbT, wvT, kgn, kgr, cosT, sinT)


def _proj(x, mod, g1, wf, cs, wg, bg, wq, qg, wqbT, qgain, wkv, kvg, wkbT, wvT, kgn, kgr,
          cosT, sinT, ts):
    b, s, d = x.shape
    assert wg.shape == (2 * d // GATE_CHUNK, d, GATE_CHUNK)
    tok = lambda shape: pl.BlockSpec(shape, lambda i, j: (0, j))
    return pl.pallas_call(
        _proj_body,
        grid=(b, s // ts),
        in_specs=[pl.BlockSpec((1, ts, d), lambda i, j: (i, j, 0)),
                  _full(mod.shape),
                  _full(g1.shape), _full(wf.shape), _full(cs.shape), _full(wg.shape),
                  _full(bg.shape), _full(wq.shape), _full(qg.shape), _full(wqbT.shape),
                  _full(qgain.shape),
                  _full(wkv.shape), _full(kvg.shape), _full(wkbT.shape), _full(wvT.shape),
                  _full(kgn.shape), _full(kgr.shape), tok((QK_ROPE, ts)), tok((QK_ROPE, ts))],
        out_specs=[pl.BlockSpec((1, 2, ts, FOURIER_W), lambda i, j: (i, 0, j, 0)),
                   pl.BlockSpec((1, ts, 2 * d), lambda i, j: (i, j, 0)),
                   pl.BlockSpec((1, N_HEADS, HEAD_PAD, ts), lambda i, j: (i, 0, 0, j)),
                   pl.BlockSpec((1, N_HEADS, ts, HEAD_PAD), lambda i, j: (i, 0, j, 0)),
                   pl.BlockSpec((1, N_HEADS, V_ROWS, ts), lambda i, j: (i, 0, 0, j))],
        out_shape=[jax.ShapeDtypeStruct((b, 2, s, FOURIER_W), BF16),
                   jax.ShapeDtypeStruct((b, s, 2 * d), BF16),
                   jax.ShapeDtypeStruct((b, N_HEADS, HEAD_PAD, s), BF16),
                   jax.ShapeDtypeStruct((b, N_HEADS, s, HEAD_PAD), BF16),
                   jax.ShapeDtypeStruct((b, N_HEADS, V_ROWS, s), BF16)],
        compiler_params=_params("arbitrary", "arbitrary"),
        name="proj",
    )(x, mod, g1, wf, cs, wg, bg, wq, qg, wqbT, qgain, wkv, kvg, wkbT, wvT, kgn, kgr,
      cosT, sinT)


def _dft_body(g1_ref, g2_ref, gb_ref, z_ref, o_ref, a_scr):
    step_rows = DFT_BLK * DFT_STEP
    nstep = RADIX // step_rows
    j = pl.program_id(1)
    r8 = DFT_RADIX_A
    slab = (2, r8, DFT_ABLK, FOURIER_W)

    @pl.when(j < nstep)
    def _():
        for t in range(step_rows // DFT_ABLK):
            rows = slice(t * DFT_ABLK, (t + 1) * DFT_ABLK)
            off = pl.multiple_of(j * step_rows + t * DFT_ABLK, DFT_ABLK)
            t1 = []
            for a2 in range(r8):
                z = z_ref[0, :, :, a2, rows, :].reshape(2 * r8 * DFT_ABLK, FOURIER_W)
                t1.append(_dot(g1_ref[...], z).reshape(slab))
            for k2a in range(r8):
                x = jnp.concatenate([t1[a2][r, k2a] for r in range(2) for a2 in range(r8)], axis=0)
                a = _dot(g2_ref[k2a], x.astype(BF16)).reshape(slab)
                for r in range(2):
                    for k2b in range(r8):
                        a_scr[r, k2a + r8 * k2b, pl.ds(off, DFT_ABLK), :] = a[r, k2b]

    @pl.when(j >= nstep)
    def _():
        ys = []
        for t in range(DFT_STEP):
            blk = (j - nstep) * DFT_STEP + t
            off = pl.multiple_of(blk * DFT_BLK, DFT_BLK)
            a = a_scr[:, pl.ds(off, DFT_BLK), :, :]
            a = a.reshape(2 * DFT_BLK * RADIX, FOURIER_W).astype(BF16)
            y = _dot(gb_ref[blk], a)
            ys.append(y.reshape(RADIX, DFT_BLK, FOURIER_W))
        o_ref[0] = jnp.concatenate(ys, axis=1).astype(o_ref.dtype)


def _dft(g1, g2, gb, z6):
    b = z6.shape[0]
    nstep = RADIX // (DFT_BLK * DFT_STEP)
    rows = DFT_BLK * DFT_STEP
    r8 = DFT_RADIX_A
    assert rows % DFT_ABLK == 0 and z6.shape[2:4] == (r8, r8)
    return pl.pallas_call(
        _dft_body,
        grid=(b, 2 * nstep),
        in_specs=[_full(g1.shape), _full(g2.shape), _full(gb.shape),
                  pl.BlockSpec((1, 2, r8, r8, rows, FOURIER_W),
                               lambda i, j: (i, 0, 0, 0, jnp.minimum(j, nstep - 1), 0))],
        out_specs=pl.BlockSpec((1, RADIX, rows, FOURIER_W),
                               lambda i, j: (i, 0, jnp.maximum(j - nstep, 0), 0)),
        out_shape=jax.ShapeDtypeStruct((b, RADIX, RADIX, FOURIER_W), BF16),
        scratch_shapes=[pltpu.VMEM((2, RADIX, RADIX, FOURIER_W), F32)],
        compiler_params=_params("arbitrary", "arbitrary"),
        name="dft",
    )(g1, g2, gb, z6)


def _attn_body(qT_ref, k_ref, vT_ref, kc_ref, vcT_ref, *rest, tq, tkb, n_cast):
    for src, dst in zip(rest[:n_cast], rest[n_cast + 1:]):
        dst[...] = src[...].astype(dst.dtype).reshape(dst.shape)
    o_ref = rest[n_cast]
    nkb = k_ref.shape[2] // tkb
    nblk = nkb + 1

    def colmax(s):
        r = s.shape[0] // SUBLANES
        return jnp.max(jnp.max(s.reshape(SUBLANES, r, s.shape[1]), axis=0), axis=0, keepdims=True)

    def update(carry, s, vT_blk):
        m, acc = carry
        m_new = jnp.maximum(m, colmax(s))
        p = jnp.exp2(s - m_new).astype(BF16)
        acc = jnp.exp2(m - m_new) * acc + _dot(vT_blk, p)
        return m_new, acc

    nq = qT_ref.shape[3] // tq

    def k_blk(i):
        return k_ref[0, 0, i * tkb:(i + 1) * tkb, :] if i < nkb else kc_ref[0, 0]

    def vT_blk(i):
        return vT_ref[0, 0, :, i * tkb:(i + 1) * tkb] if i < nkb else vcT_ref[0, 0]

    def q_cols(qi):
        return qT_ref[0, 0, :, qi * tq:(qi + 1) * tq]

    pending = [_dot(k_blk(i), q_cols(0)) for i in range(ATTN_AHEAD)]
    for qi in range(nq):
        qT = q_cols(qi)
        carry = (jnp.full((1, tq), -jnp.inf, F32), jnp.zeros((V_ROWS, tq), F32))
        for i in range(nblk):
            j = i + ATTN_AHEAD
            if j < nblk:
                pending.append(_dot(k_blk(j), qT))
            elif qi + 1 < nq:
                pending.append(_dot(k_blk(j - nblk), q_cols(qi + 1)))
            carry = update(carry, pending.pop(0), vT_blk(i))
        _, acc = carry
        o_ref[0, :, qi * tq:(qi + 1) * tq] = (
            acc[:V_DIM] / acc[V_DIM:V_DIM + 1]).astype(o_ref.dtype)


def _attention(qT, k, vT, kc, vcT, weights, tq, tkb):
    b, h, _, s = qT.shape
    sc = kc.shape[2]
    steps = b * h
    w_in_specs, w_out_specs, w_out_shapes = [], [], []
    for w, axis, chunk in weights:
        blk = tuple(n // steps if a == axis else n for a, n in enumerate(w.shape))
        spec = pl.BlockSpec(blk, (lambda i, j: (i * h + j, 0)) if axis == 0
                            else (lambda i, j: (0, i * h + j)))
        w_in_specs.append(spec)
        if chunk is None:
            w_out_specs.append(spec)
            w_out_shapes.append(jax.ShapeDtypeStruct(w.shape, BF16))
        else:
            per = chunk // blk[1]
            w_out_specs.append(pl.BlockSpec(
                (1,) + blk, lambda i, j, per=per: ((i * h + j) // per, 0, (i * h + j) % per)))
            w_out_shapes.append(jax.ShapeDtypeStruct((w.shape[1] // chunk, w.shape[0], chunk), BF16))
    outs = pl.pallas_call(
        functools.partial(_attn_body, tq=tq, tkb=tkb, n_cast=len(weights)),
        grid=(b, h),
        in_specs=[pl.BlockSpec((1, 1, HEAD_PAD, s), lambda i, j: (i, j, 0, 0)),
                  pl.BlockSpec((1, 1, s, HEAD_PAD), lambda i, j: (i, j, 0, 0)),
                  pl.BlockSpec((1, 1, V_ROWS, s), lambda i, j: (i, j, 0, 0)),
                  pl.BlockSpec((1, 1, sc, HEAD_PAD), lambda i, j: (i, j, 0, 0)),
                  pl.BlockSpec((1, 1, V_ROWS, sc), lambda i, j: (i, j, 0, 0))] + w_in_specs,
        out_specs=[pl.BlockSpec((1, V_DIM, s), lambda i, j: (i, j, 0))] + w_out_specs,
        out_shape=[jax.ShapeDtypeStruct((b, h * V_DIM, s), BF16)] + w_out_shapes,
        compiler_params=_params("arbitrary", "arbitrary"),
        name="attention",
    )(qT, k, vT, kc, vcT, *[w for w, _, _ in weights])
    return outs[0], outs[1:]


def _post_body(fm_ref, aT_ref, g_ref, x_ref, mod_ref, wfo_ref, wao_ref, wout_ref,
               g2_ref, wup_ref, wdn_ref, o_ref):
    d = x_ref.shape[-1]
    m = _mod_rows(mod_ref, pl.program_id(0))
    part = x_ref.shape[1] // POST_PARTS
    toks = [slice(p * part, (p + 1) * part) for p in range(POST_PARTS)]
    x1s, hbs = [], []
    for tok in toks:
        yf = _dot(fm_ref[0, tok, :], wfo_ref[...])
        ya = lax.dot_general(aT_ref[0, :, tok], wao_ref[...], (((0,), (0,)), ((), ())),
                             preferred_element_type=F32)
        g = g_ref[0, tok, :]
        y = g[:, :d].astype(F32) * yf + g[:, d:].astype(F32) * ya
        x1 = x_ref[0, tok, :] + m[2:3] * _dot(y.astype(BF16), wout_ref[...])
        x1s.append(x1)
        hbs.append(_norm_mod(x1, m[3:5], g2_ref[...]).astype(BF16))
    for tok, x1, hb in zip(toks, x1s, hbs):
        acc = jnp.zeros(x1.shape, F32)
        for c in range(wup_ref.shape[0]):
            u = jnp.maximum(_dot(hb, wup_ref[c]), 0.0)
            acc = acc + _dot((u * u).astype(BF16), wdn_ref[c])
        o_ref[0, tok, :] = x1 + m[5:6] * acc


def _post(fm, aT, gates, x, mod, wfo, wao, wout, g2, wup, wdn, ts):
    b, s, d = x.shape
    return pl.pallas_call(
        _post_body,
        grid=(b, s // ts),
        in_specs=[pl.BlockSpec((1, ts, FOURIER_W), lambda i, j: (i, j, 0)),
                  pl.BlockSpec((1, aT.shape[1], ts), lambda i, j: (i, 0, j)),
                  pl.BlockSpec((1, ts, 2 * d), lambda i, j: (i, j, 0)),
                  pl.BlockSpec((1, ts, d), lambda i, j: (i, j, 0)),
                  _full(mod.shape),
                  _full(wfo.shape), _full(wao.shape), _full(wout.shape),
                  _full(g2.shape), _full(wup.shape), _full(wdn.shape)],
        out_specs=pl.BlockSpec((1, ts, d), lambda i, j: (i, j, 0)),
        out_shape=jax.ShapeDtypeStruct((b, s, d), F32),
        compiler_params=_params("arbitrary", "arbitrary"),
        name="post",
    )(fm, aT, gates, x, mod, wfo, wao, wout, g2, wup, wdn)


@functools.lru_cache(maxsize=None)
def _dft_tables():
    j = np.arange(GROUP_W)
    ang = 2.0 * np.pi * np.outer(j, j) / GROUP_W
    cs = np.concatenate([np.cos(ang), -np.sin(ang)], axis=1) / math.sqrt(GROUP_W)

    r8 = DFT_RADIX_A
    d = np.arange(r8)
    eye = np.eye(DFT_ABLK)

    def real_block(theta):
        c, s = np.cos(theta), np.sin(theta)
        return np.kron(np.block([[c, s], [-s, c]]) / math.sqrt(r8), eye)

    g1 = real_block(2.0 * np.pi * np.outer(d, d) / r8)
    g2 = np.stack([real_block(2.0 * np.pi * (k2a * d[None, :] / RADIX + np.outer(d, d) / r8))
                   for k2a in range(r8)])

    n = RADIX * RADIX
    k1 = np.arange(RADIX)[:, None, None]
    k2 = np.arange(RADIX)[None, :, None]
    bb = np.arange(RADIX)[None, None, :]
    th = 2.0 * np.pi * ((bb * (RADIX * k1 + k2)) % n) / n
    gc = np.cos(th) / math.sqrt(RADIX)
    gs = np.sin(th) / math.sqrt(RADIX)
    nblk = RADIX // DFT_BLK
    gtab = np.zeros((nblk, RADIX, DFT_BLK, 2, DFT_BLK, RADIX), np.float32)
    for jb in range(nblk):
        for q in range(DFT_BLK):
            gtab[jb, :, q, 0, q, :] = gc[:, jb * DFT_BLK + q, :]
            gtab[jb, :, q, 1, q, :] = gs[:, jb * DFT_BLK + q, :]
    gtab = gtab.reshape(nblk, RADIX * DFT_BLK, 2 * DFT_BLK * RADIX)
    return cs.astype(np.float32), g1.astype(np.float32), g2.astype(np.float32), gtab


@functools.lru_cache(maxsize=None)
def _rope_tables_t(n):
    rows = n // GRID_W
    row = np.repeat(np.arange(rows, dtype=np.float64), GRID_W)
    col = np.tile(np.arange(GRID_W, dtype=np.float64), rows)
    n_freq = QK_ROPE // 4
    freqs = ROPE_THETA ** (-np.arange(n_freq, dtype=np.float64) / n_freq)
    ang_r = row[:, None] * freqs[None, :]
    ang_c = col[:, None] * freqs[None, :]
    ang = np.concatenate([ang_r, ang_r, ang_c, ang_c], axis=-1)
    return np.cos(ang).T.astype(np.float32), np.sin(ang).T.astype(np.float32)


def kernel(x, c, ctx, c_ctx, ada_w, ada_b, norm1_g, norm2_g, w_in, b_gate, w_fourier, q_norm_g, w_qb,
           kv_norm_g, w_kvb, q_gain, k_gain, w_mla_o, w_out, w_up, w_down):
    bsz, s, d = x.shape
    sc = ctx.shape[1]
    assert ada_w.shape[0] == 1 and s == RADIX * RADIX and d == D_MODEL

    cc = jnp.zeros((8, d), F32).at[:bsz].set(c).at[bsz].set(c_ctx)
    mod, wf, wq, wkv, wg, wqbT, wkbT, wvT = _adaln(
        cc, ada_w[0], ada_b[0][None, :], w_in[0].T, w_qb[0], w_kvb[0])

    g1 = norm1_g[0][None, :]
    g2 = norm2_g[0][None, :]
    qg = q_norm_g[0][None, :]
    kvg = kv_norm_g[0][None, :]
    bg = b_gate[0][None, :]
    qscale = (HEAD_QK ** -0.5) * math.log2(math.e)
    qgain = jnp.broadcast_to((q_gain[0] * qscale)[:, None], (HEAD_QK, LANES))
    kgn = jnp.broadcast_to(k_gain[0][:QK_NOPE, None], (QK_NOPE, LANES))
    kgr = jnp.broadcast_to(k_gain[0][QK_NOPE:, None], (QK_ROPE, LANES))
    cosT, sinT = (jnp.asarray(t) for t in _rope_tables_t(s))
    cos1 = jnp.asarray(np.ones((QK_ROPE, sc), np.float32))
    sin0 = jnp.asarray(np.zeros((QK_ROPE, sc), np.float32))

    cs, ga1, ga2, gtab = (jnp.asarray(t).astype(BF16) for t in _dft_tables())

    kc, vcT = _ctx_kv(ctx, mod, g1, wkv, kvg, wkbT, wvT, kgn, kgr, cos1, sin0)

    z, gates, qT, k, vT = _proj(x, mod, g1, wf, cs, wg, bg, wq, qg, wqbT, qgain, wkv, kvg, wkbT, wvT,
                                kgn, kgr, cosT, sinT, PROJ_TILE)

    r8 = DFT_RADIX_A
    fm = _dft(ga1, ga2, gtab, z.reshape(bsz, 2, r8, r8, RADIX, FOURIER_W)).reshape(bsz, s, FOURIER_W)

    aT, (wfo, wao, wout, wup, wdn) = _attention(
        qT, k, vT, kc, vcT,
        [(w_fourier[0], 0, None), (w_mla_o[0], 0, None), (w_out[0], 0, None),
         (w_up[0], 1, FF_CHUNK), (w_down[0], 0, None)],
        ATTN_TQ, ATTN_TK)
    wdn = wdn.reshape(-1, FF_CHUNK, d)

    return _post(fm, aT, gates, x, mod, wfo, wao, wout, g2, wup, wdn, POST_TILE)
```

```python
import functools
import math

import numpy as np
import jax
import jax.numpy as jnp
from jax import lax
from jax.experimental import pallas as pl
from jax.experimental.pallas import tpu as pltpu

F32 = jnp.float32
BF16 = jnp.bfloat16

D_MODEL = 1024
GRID_W = 64
N_GROUPS = 4
GROUP_W = 128
FOURIER_W = N_GROUPS * GROUP_W
N_HEADS = 8
QK_NOPE = 64
QK_ROPE = 32
HEAD_QK = QK_NOPE + QK_ROPE
LANES = 128
SUBLANES = 8
HEAD_PAD = LANES
V_DIM = 64
V_ROWS = V_DIM + 2 * SUBLANES
Q_LORA = 256
KV_LORA = 128
W_IN_GATES = FOURIER_W + Q_LORA + KV_LORA + QK_ROPE
ROPE_THETA = 10000.0
EPS = 1e-6
RADIX = 64
DFT_RADIX_A = 8
DFT_ABLK = 2 * SUBLANES
DFT_BLK = SUBLANES
DFT_STEP = 2
POST_TILE = 512
POST_PARTS = 2
PROJ_TILE = 1024
PROJ_PARTS = 4
W_IN_BLK = 256
GATE_CHUNK = W_IN_BLK
FF_CHUNK = 1024
ATTN_TQ = 512
ATTN_TK = 256
ATTN_AHEAD = 2

V7X_VMEM_LIMIT = 56 * 1024 * 1024


def _params(*sem):
    return pltpu.CompilerParams(dimension_semantics=sem, vmem_limit_bytes=V7X_VMEM_LIMIT)


def _dot(a, b):
    return jnp.dot(a, b, preferred_element_type=F32)


def _rms_lanes(x, n):
    return x * lax.rsqrt(jnp.sum(x * x, axis=-1, keepdims=True) * (1.0 / n) + EPS)


def _rot_half_rows(x):
    return jnp.concatenate([-x[8:16], x[0:8], -x[24:32], x[16:24]], axis=0)


def _adaln_body(c_ref, w_ref, b_ref, tf_ref, tq_ref, tkv_ref, tg_ref, wqb_ref, wkvb_ref,
                o_ref, wf_ref, wq_ref, wkv_ref, wg_ref, wqbT_ref, wkbT_ref, wvT_ref):
    c = c_ref[...]
    s = c * jax.nn.sigmoid(c)
    o_ref[...] = _dot(s, w_ref[...]) + b_ref[...]

    @pl.when(pl.program_id(0) == 0)
    def _():
        wf_ref[...] = tf_ref[...].T.astype(BF16)
        wq_ref[...] = tq_ref[...].T.astype(BF16)
        wkv_ref[...] = tkv_ref[...].T.astype(BF16)
        wqbT_ref[...] = wqb_ref[...].T.astype(BF16)
        wkvT = wkvb_ref[...].T
        hw = QK_NOPE + V_DIM
        wkbT_ref[...] = jnp.concatenate(
            [wkvT[h * hw:h * hw + QK_NOPE] for h in range(N_HEADS)], axis=0).astype(BF16)
        wvT_ref[...] = jnp.concatenate(
            [wkvT[h * hw + QK_NOPE:(h + 1) * hw] for h in range(N_HEADS)], axis=0).astype(BF16)

    wg_ref[0] = tg_ref[...].T.astype(BF16)


def _adaln(cc, ada_w, ada_b, w_inT, w_qb, w_kvb):
    rows, d = cc.shape
    const = lambda shape: pl.BlockSpec(shape, lambda j: (0, 0))
    wqbT_shape = (N_HEADS * HEAD_QK, w_qb.shape[0])
    wkT_shape = (N_HEADS * QK_NOPE, w_kvb.shape[0])
    wvT_shape = (N_HEADS * V_DIM, w_kvb.shape[0])
    n = ada_w.shape[1]
    n_gates = w_inT.shape[0] - W_IN_GATES
    steps = n_gates // W_IN_BLK
    tn = n // steps
    blk = lambda r, f: pl.BlockSpec((r, d), f)
    return pl.pallas_call(
        _adaln_body,
        grid=(steps,),
        in_specs=[pl.BlockSpec((rows, d), lambda j: (0, 0)),
                  pl.BlockSpec((d, tn), lambda j: (0, j)),
                  pl.BlockSpec((1, tn), lambda j: (0, j)),
                  blk(FOURIER_W, lambda j: (0, 0)),
                  blk(Q_LORA, lambda j: (FOURIER_W // Q_LORA, 0)),
                  blk(2 * KV_LORA, lambda j: ((FOURIER_W + Q_LORA) // (2 * KV_LORA), 0)),
                  pl.BlockSpec((pl.Element(W_IN_BLK), pl.Element(d)),
                               lambda j: (pl.multiple_of(W_IN_GATES + j * W_IN_BLK, SUBLANES), 0)),
                  const(w_qb.shape), const(w_kvb.shape)],
        out_specs=[pl.BlockSpec((rows, tn), lambda j: (0, j)),
                   const((d, FOURIER_W)), const((d, Q_LORA)), const((d, 2 * KV_LORA)),
                   pl.BlockSpec((1, d, W_IN_BLK), lambda j: (j, 0, 0)),
                   const(wqbT_shape), const(wkT_shape), const(wvT_shape)],
        out_shape=[jax.ShapeDtypeStruct((rows, n), F32),
                   jax.ShapeDtypeStruct((d, FOURIER_W), BF16),
                   jax.ShapeDtypeStruct((d, Q_LORA), BF16),
                   jax.ShapeDtypeStruct((d, 2 * KV_LORA), BF16),
                   jax.ShapeDtypeStruct((steps, d, W_IN_BLK), BF16),
                   jax.ShapeDtypeStruct(wqbT_shape, BF16),
                   jax.ShapeDtypeStruct(wkT_shape, BF16),
                   jax.ShapeDtypeStruct(wvT_shape, BF16)],
        compiler_params=_params("arbitrary"),
        name="adaln",
    )(cc, ada_w, ada_b, w_inT, w_inT, w_inT, w_inT, w_qb, w_kvb)


def _mod_rows(mod_ref, row):
    m = mod_ref[pl.ds(row, 1), :]
    d = m.shape[1] // 6
    return jnp.concatenate([m[:, k * d:(k + 1) * d] for k in range(6)], axis=0)


def _norm_mod(x, m, g):
    ms = jnp.sum(x * x, axis=-1, keepdims=True) * (1.0 / x.shape[-1])
    return (x * lax.rsqrt(ms + EPS)) * (g * (1.0 + m[1:2])) + m[0:1]


def _kv_latents(kvp, kvg_ref):
    kvn = _rms_lanes(kvp[:, :KV_LORA], KV_LORA) * kvg_ref[...]
    return kvn.T.astype(BF16), kvp[:, KV_LORA:].T[0:QK_ROPE]


def _lanes(ref, t):
    return jnp.concatenate([ref[...]] * (t // LANES), axis=1)


def _kv_heads(knT, vT, krT, kgn_ref, kgr_ref, cos_ref, sin_ref, k_ref, vT_ref, tok):
    t = knT.shape[1]
    sr = jnp.sum(krT * krT, axis=0, keepdims=True)
    krg = krT * _lanes(kgr_ref, t)
    kr_rot = krg * cos_ref[:, tok] + _rot_half_rows(krg) * sin_ref[:, tok]
    kgn = _lanes(kgn_ref, t)
    ones = jnp.ones((V_ROWS - V_DIM, t), F32)
    zpad = jnp.zeros((HEAD_PAD - HEAD_QK, t), F32)
    for h in range(N_HEADS):
        kn = knT[h * QK_NOPE:(h + 1) * QK_NOPE]
        ss = jnp.sum(kn * kn, axis=0, keepdims=True) + sr
        r = lax.rsqrt(ss * (1.0 / HEAD_QK) + EPS)
        kT = jnp.concatenate([kn * kgn * r, kr_rot * r, zpad], axis=0)
        k_ref[0, h, tok, :] = kT.T.astype(k_ref.dtype)
        vh = vT[h * V_DIM:(h + 1) * V_DIM]
        vT_ref[0, h, :, tok] = jnp.concatenate([vh, ones], axis=0).astype(vT_ref.dtype)


def _q_heads(qT, qgain_ref, cos_ref, sin_ref, qT_ref, tok):
    t = qT.shape[1]
    qgain, cos, sin = _lanes(qgain_ref, t), cos_ref[:, tok], sin_ref[:, tok]
    zpad = jnp.zeros((HEAD_PAD - HEAD_QK, t), F32)
    for h in range(N_HEADS):
        qh = qT[h * HEAD_QK:(h + 1) * HEAD_QK]
        r = lax.rsqrt(jnp.sum(qh * qh, axis=0, keepdims=True) * (1.0 / HEAD_QK) + EPS)
        qh = qh * qgain * r
        qr = qh[QK_NOPE:]
        qr = qr * cos + _rot_half_rows(qr) * sin
        qT_ref[0, h, :, tok] = jnp.concatenate([qh[:QK_NOPE], qr, zpad],
                                               axis=0).astype(qT_ref.dtype)


def _ctx_body(x_ref, mod_ref, g1_ref, wkv_ref, kvg_ref, wkbT_ref, wvT_ref, kgn_ref, kgr_ref,
              cos_ref, sin_ref, k_ref, vT_ref):
    ctx_row = pl.num_programs(0)
    hb = _norm_mod(x_ref[0], _mod_rows(mod_ref, ctx_row), g1_ref[...]).astype(BF16)
    kvnT, krT = _kv_latents(_dot(hb, wkv_ref[...]), kvg_ref)
    _kv_heads(_dot(wkbT_ref[...], kvnT), _dot(wvT_ref[...], kvnT), krT,
              kgn_ref, kgr_ref, cos_ref, sin_ref, k_ref, vT_ref, slice(None))


def _proj_body(x_ref, mod_ref, g1_ref, wf_ref, cs_ref, wg_ref, bg_ref,
               wq_ref, qg_ref, wqbT_ref, qgain_ref,
               wkv_ref, kvg_ref, wkbT_ref, wvT_ref, kgn_ref, kgr_ref, cos_ref, sin_ref,
               z_ref, g_ref, qT_ref, k_ref, vT_ref):
    part = x_ref.shape[1] // PROJ_PARTS
    m = _mod_rows(mod_ref, pl.program_id(0))
    for p in range(PROJ_PARTS):
        tok = slice(p * part, (p + 1) * part)
        hb = _norm_mod(x_ref[0, tok, :], m, g1_ref[...]).astype(BF16)
        qp = _dot(hb, wq_ref[...])
        kvp = _dot(hb, wkv_ref[...])
        f = _dot(hb, wf_ref[...]).astype(BF16)

        qnT = (_rms_lanes(qp, Q_LORA) * qg_ref[...]).T.astype(BF16)
        kvnT, krT = _kv_latents(kvp, kvg_ref)

        def gate_chunk(c):
            cols = slice(c * GATE_CHUNK, (c + 1) * GATE_CHUNK)
            g_ref[0, tok, cols] = jax.nn.sigmoid(
                _dot(hb, wg_ref[c]) + bg_ref[:, cols]).astype(g_ref.dtype)

        gate_chunk(0)

        qT = _dot(wqbT_ref[...], qnT)
        knT = _dot(wkbT_ref[...], kvnT)
        vT = _dot(wvT_ref[...], kvnT)

        _q_heads(qT, qgain_ref, cos_ref, sin_ref, qT_ref, tok)
        _kv_heads(knT, vT, krT, kgn_ref, kgr_ref, cos_ref, sin_ref, k_ref, vT_ref, tok)
        for c in range(1, g_ref.shape[2] // GATE_CHUNK):
            gate_chunk(c)

        for g in range(N_GROUPS):
            cols = slice(g * GROUP_W, (g + 1) * GROUP_W)
            z = _dot(f[:, cols], cs_ref[...])
            z_ref[0, 0, tok, cols] = z[:, :GROUP_W].astype(z_ref.dtype)
            z_ref[0, 1, tok, cols] = z[:, GROUP_W:].astype(z_ref.dtype)


def _full(shape):
    nd = len(shape)
    return pl.BlockSpec(shape, lambda *_: (0,) * nd, pipeline_mode=pl.Buffered(1))


def _ctx_kv(ctx, mod, g1, wkv, kvg, wkbT, wvT, kgn, kgr, cosT, sinT):
    b, t, d = ctx.shape
    return pl.pallas_call(
        _ctx_body,
        grid=(b,),
        in_specs=[pl.BlockSpec((1, t, d), lambda i: (i, 0, 0)),
                  _full(mod.shape),
                  _full(g1.shape), _full(wkv.shape), _full(kvg.shape), _full(wkbT.shape),
                  _full(wvT.shape), _full(kgn.shape), _full(kgr.shape),
                  _full(cosT.shape), _full(sinT.shape)],
        out_specs=[pl.BlockSpec((1, N_HEADS, t, HEAD_PAD), lambda i: (i, 0, 0, 0)),
                   pl.BlockSpec((1, N_HEADS, V_ROWS, t), lambda i: (i, 0, 0, 0))],
        out_shape=[jax.ShapeDtypeStruct((b, N_HEADS, t, HEAD_PAD), BF16),
                   jax.ShapeDtypeStruct((b, N_HEADS, V_ROWS, t), BF16)],
        compiler_params=_params("arbitrary"),
        name="ctx_kv",
    )(ctx, mod, g1, wkv, kvg, wkbT, wvT, kgn, kgr, cosT, sinT)


def _proj(x, mod, g1, wf, cs, wg, bg, wq, qg, wqbT, qgain, wkv, kvg, wkbT, wvT, kgn, kgr,
          cosT, sinT, ts):
    b, s, d = x.shape
    assert wg.shape == (2 * d // GATE_CHUNK, d, GATE_CHUNK)
    tok = lambda shape: pl.BlockSpec(shape, lambda i, j: (0, j))
    return pl.pallas_call(
        _proj_body,
        grid=(b, s // ts),
        in_specs=[pl.BlockSpec((1, ts, d), lambda i, j: (i, j, 0)),
                  _full(mod.shape),
                  _full(g1.shape), _full(wf.shape), _full(cs.shape), _full(wg.shape),
                  _full(bg.shape), _full(wq.shape), _full(qg.shape), _full(wqbT.shape),
                  _full(qgain.shape),
                  _full(wkv.shape), _full(kvg.shape), _full(wkbT.shape), _full(wvT.shape),
                  _full(kgn.shape), _full(kgr.shape), tok((QK_ROPE, ts)), tok((QK_ROPE, ts))],
        out_specs=[pl.BlockSpec((1, 2, ts, FOURIER_W), lambda i, j: (i, 0, j, 0)),
                   pl.BlockSpec((1, ts, 2 * d), lambda i, j: (i, j, 0)),
                   pl.BlockSpec((1, N_HEADS, HEAD_PAD, ts), lambda i, j: (i, 0, 0, j)),
                   pl.BlockSpec((1, N_HEADS, ts, HEAD_PAD), lambda i, j: (i, 0, j, 0)),
                   pl.BlockSpec((1, N_HEADS, V_ROWS, ts), lambda i, j: (i, 0, 0, j))],
        out_shape=[jax.ShapeDtypeStruct((b, 2, s, FOURIER_W), BF16),
                   jax.ShapeDtypeStruct((b, s, 2 * d), BF16),
                   jax.ShapeDtypeStruct((b, N_HEADS, HEAD_PAD, s), BF16),
                   jax.ShapeDtypeStruct((b, N_HEADS, s, HEAD_PAD), BF16),
                   jax.ShapeDtypeStruct((b, N_HEADS, V_ROWS, s), BF16)],
        compiler_params=_params("arbitrary", "arbitrary"),
        name="proj",
    )(x, mod, g1, wf, cs, wg, bg, wq, qg, wqbT, qgain, wkv, kvg, wkbT, wvT, kgn, kgr,
      cosT, sinT)


def _dft_body(g1_ref, g2_ref, gb_ref, z_ref, o_ref, a_scr):
    step_rows = DFT_BLK * DFT_STEP
    nstep = RADIX // step_rows
    j = pl.program_id(1)
    r8 = DFT_RADIX_A
    slab = (2, r8, DFT_ABLK, FOURIER_W)

    @pl.when(j < nstep)
    def _():
        for t in range(step_rows // DFT_ABLK):
            rows = slice(t * DFT_ABLK, (t + 1) * DFT_ABLK)
            off = pl.multiple_of(j * step_rows + t * DFT_ABLK, DFT_ABLK)
            t1 = []
            for a2 in range(r8):
                z = z_ref[0, :, :, a2, rows, :].reshape(2 * r8 * DFT_ABLK, FOURIER_W)
                t1.append(_dot(g1_ref[...], z).reshape(slab))
            for k2a in range(r8):
                x = jnp.concatenate([t1[a2][r, k2a] for r in range(2) for a2 in range(r8)], axis=0)
                a = _dot(g2_ref[k2a], x.astype(BF16)).reshape(slab)
                for r in range(2):
                    for k2b in range(r8):
                        a_scr[r, k2a + r8 * k2b, pl.ds(off, DFT_ABLK), :] = a[r, k2b]

    @pl.when(j >= nstep)
    def _():
        ys = []
        for t in range(DFT_STEP):
            blk = (j - nstep) * DFT_STEP + t
            off = pl.multiple_of(blk * DFT_BLK, DFT_BLK)
            a = a_scr[:, pl.ds(off, DFT_BLK), :, :]
            a = a.reshape(2 * DFT_BLK * RADIX, FOURIER_W).astype(BF16)
            y = _dot(gb_ref[blk], a)
            ys.append(y.reshape(RADIX, DFT_BLK, FOURIER_W))
        o_ref[0] = jnp.concatenate(ys, axis=1).astype(o_ref.dtype)


def _dft(g1, g2, gb, z6):
    b = z6.shape[0]
    nstep = RADIX // (DFT_BLK * DFT_STEP)
    rows = DFT_BLK * DFT_STEP
    r8 = DFT_RADIX_A
    assert rows % DFT_ABLK == 0 and z6.shape[2:4] == (r8, r8)
    return pl.pallas_call(
        _dft_body,
        grid=(b, 2 * nstep),
        in_specs=[_full(g1.shape), _full(g2.shape), _full(gb.shape),
                  pl.BlockSpec((1, 2, r8, r8, rows, FOURIER_W),
                               lambda i, j: (i, 0, 0, 0, jnp.minimum(j, nstep - 1), 0))],
        out_specs=pl.BlockSpec((1, RADIX, rows, FOURIER_W),
                               lambda i, j: (i, 0, jnp.maximum(j - nstep, 0), 0)),
        out_shape=jax.ShapeDtypeStruct((b, RADIX, RADIX, FOURIER_W), BF16),
        scratch_shapes=[pltpu.VMEM((2, RADIX, RADIX, FOURIER_W), F32)],
        compiler_params=_params("arbitrary", "arbitrary"),
        name="dft",
    )(g1, g2, gb, z6)


def _attn_body(qT_ref, k_ref, vT_ref, kc_ref, vcT_ref, *rest, tq, tkb, n_cast):
    for src, dst in zip(rest[:n_cast], rest[n_cast + 1:]):
        dst[...] = src[...].astype(dst.dtype).reshape(dst.shape)
    o_ref = rest[n_cast]
    nkb = k_ref.shape[2] // tkb
    nblk = nkb + 1

    def colmax(s):
        n = 4 * SUBLANES
        return jnp.max(jnp.max(s.reshape(n, s.shape[0] // n, s.shape[1]), axis=0), axis=0, keepdims=True)

    def update(carry, s, vT_blk):
        m, acc = carry
        m_new = jnp.maximum(m, colmax(s))
        p = jnp.exp2(s - m_new).astype(BF16)
        acc = jnp.exp2(m - m_new) * acc + _dot(vT_blk, p)
        return m_new, acc

    nq = qT_ref.shape[3] // tq

    def k_blk(i):
        return k_ref[0, 0, i * tkb:(i + 1) * tkb, :] if i < nkb else kc_ref[0, 0]

    def vT_blk(i):
        return vT_ref[0, 0, :, i * tkb:(i + 1) * tkb] if i < nkb else vcT_ref[0, 0]

    def q_cols(qi):
        return qT_ref[0, 0, :, qi * tq:(qi + 1) * tq]

    pending = [_dot(k_blk(i), q_cols(0)) for i in range(ATTN_AHEAD)]
    for qi in range(nq):
        qT = q_cols(qi)
        carry = (jnp.full((1, tq), -jnp.inf, F32), jnp.zeros((V_ROWS, tq), F32))
        for i in range(nblk):
            j = i + ATTN_AHEAD
            if j < nblk:
                pending.append(_dot(k_blk(j), qT))
            elif qi + 1 < nq:
                pending.append(_dot(k_blk(j - nblk), q_cols(qi + 1)))
            carry = update(carry, pending.pop(0), vT_blk(i))
        _, acc = carry
        o_ref[0, :, qi * tq:(qi + 1) * tq] = (
            acc[:V_DIM] / acc[V_DIM:V_DIM + 1]).astype(o_ref.dtype)


def _attention(qT, k, vT, kc, vcT, weights, tq, tkb):
    b, h, _, s = qT.shape
    sc = kc.shape[2]
    steps = b * h
    w_in_specs, w_out_specs, w_out_shapes = [], [], []
    for w, axis, chunk in weights:
        blk = tuple(n // steps if a == axis else n for a, n in enumerate(w.shape))
        spec = pl.BlockSpec(blk, (lambda i, j: (i * h + j, 0)) if axis == 0
                            else (lambda i, j: (0, i * h + j)))
        w_in_specs.append(spec)
        if chunk is None:
            w_out_specs.append(spec)
            w_out_shapes.append(jax.ShapeDtypeStruct(w.shape, BF16))
        else:
            per = chunk // blk[1]
            w_out_specs.append(pl.BlockSpec(
                (1,) + blk, lambda i, j, per=per: ((i * h + j) // per, 0, (i * h + j) % per)))
            w_out_shapes.append(jax.ShapeDtypeStruct((w.shape[1] // chunk, w.shape[0], chunk), BF16))
    outs = pl.pallas_call(
        functools.partial(_attn_body, tq=tq, tkb=tkb, n_cast=len(weights)),
        grid=(b, h),
        in_specs=[pl.BlockSpec((1, 1, HEAD_PAD, s), lambda i, j: (i, j, 0, 0)),
                  pl.BlockSpec((1, 1, s, HEAD_PAD), lambda i, j: (i, j, 0, 0)),
                  pl.BlockSpec((1, 1, V_ROWS, s), lambda i, j: (i, j, 0, 0)),
                  pl.BlockSpec((1, 1, sc, HEAD_PAD), lambda i, j: (i, j, 0, 0)),
                  pl.BlockSpec((1, 1, V_ROWS, sc), lambda i, j: (i, j, 0, 0))] + w_in_specs,
        out_specs=[pl.BlockSpec((1, V_DIM, s), lambda i, j: (i, j, 0))] + w_out_specs,
        out_shape=[jax.ShapeDtypeStruct((b, h * V_DIM, s), BF16)] + w_out_shapes,
        compiler_params=_params("arbitrary", "arbitrary"),
        name="attention",
    )(qT, k, vT, kc, vcT, *[w for w, _, _ in weights])
    return outs[0], outs[1:]


def _post_body(fm_ref, aT_ref, g_ref, x_ref, mod_ref, wfo_ref, wao_ref, wout_ref,
               g2_ref, wup_ref, wdn_ref, o_ref):
    d = x_ref.shape[-1]
    m = _mod_rows(mod_ref, pl.program_id(0))
    part = x_ref.shape[1] // POST_PARTS
    toks = [slice(p * part, (p + 1) * part) for p in range(POST_PARTS)]
    x1s, hbs = [], []
    for tok in toks:
        yf = _dot(fm_ref[0, tok, :], wfo_ref[...])
        ya = lax.dot_general(aT_ref[0, :, tok], wao_ref[...], (((0,), (0,)), ((), ())),
                             preferred_element_type=F32)
        g = g_ref[0, tok, :]
        y = g[:, :d].astype(F32) * yf + g[:, d:].astype(F32) * ya
        x1 = x_ref[0, tok, :] + m[2:3] * _dot(y.astype(BF16), wout_ref[...])
        x1s.append(x1)
        hbs.append(_norm_mod(x1, m[3:5], g2_ref[...]).astype(BF16))
    for tok, x1, hb in zip(toks, x1s, hbs):
        acc = jnp.zeros(x1.shape, F32)
        for c in range(wup_ref.shape[0]):
            u = jnp.maximum(_dot(hb, wup_ref[c]), 0.0)
            acc = acc + _dot((u * u).astype(BF16), wdn_ref[c])
        o_ref[0, tok, :] = x1 + m[5:6] * acc


def _post(fm, aT, gates, x, mod, wfo, wao, wout, g2, wup, wdn, ts):
    b, s, d = x.shape
    return pl.pallas_call(
        _post_body,
        grid=(b, s // ts),
        in_specs=[pl.BlockSpec((1, ts, FOURIER_W), lambda i, j: (i, j, 0)),
                  pl.BlockSpec((1, aT.shape[1], ts), lambda i, j: (i, 0, j)),
                  pl.BlockSpec((1, ts, 2 * d), lambda i, j: (i, j, 0)),
                  pl.BlockSpec((1, ts, d), lambda i, j: (i, j, 0)),
                  _full(mod.shape),
                  _full(wfo.shape), _full(wao.shape), _full(wout.shape),
                  _full(g2.shape), _full(wup.shape), _full(wdn.shape)],
        out_specs=pl.BlockSpec((1, ts, d), lambda i, j: (i, j, 0)),
        out_shape=jax.ShapeDtypeStruct((b, s, d), F32),
        compiler_params=_params("arbitrary", "arbitrary"),
        name="post",
    )(fm, aT, gates, x, mod, wfo, wao, wout, g2, wup, wdn)


@functools.lru_cache(maxsize=None)
def _dft_tables():
    j = np.arange(GROUP_W)
    ang = 2.0 * np.pi * np.outer(j, j) / GROUP_W
    cs = np.concatenate([np.cos(ang), -np.sin(ang)], axis=1) / math.sqrt(GROUP_W)

    r8 = DFT_RADIX_A
    d = np.arange(r8)
    eye = np.eye(DFT_ABLK)

    def real_block(theta):
        c, s = np.cos(theta), np.sin(theta)
        return np.kron(np.block([[c, s], [-s, c]]) / math.sqrt(r8), eye)

    g1 = real_block(2.0 * np.pi * np.outer(d, d) / r8)
    g2 = np.stack([real_block(2.0 * np.pi * (k2a * d[None, :] / RADIX + np.outer(d, d) / r8))
                   for k2a in range(r8)])

    n = RADIX * RADIX
    k1 = np.arange(RADIX)[:, None, None]
    k2 = np.arange(RADIX)[None, :, None]
    bb = np.arange(RADIX)[None, None, :]
    th = 2.0 * np.pi * ((bb * (RADIX * k1 + k2)) % n) / n
    gc = np.cos(th) / math.sqrt(RADIX)
    gs = np.sin(th) / math.sqrt(RADIX)
    nblk = RADIX // DFT_BLK
    gtab = np.zeros((nblk, RADIX, DFT_BLK, 2, DFT_BLK, RADIX), np.float32)
    for jb in range(nblk):
        for q in range(DFT_BLK):
            gtab[jb, :, q, 0, q, :] = gc[:, jb * DFT_BLK + q, :]
            gtab[jb, :, q, 1, q, :] = gs[:, jb * DFT_BLK + q, :]
    gtab = gtab.reshape(nblk, RADIX * DFT_BLK, 2 * DFT_BLK * RADIX)
    return cs.astype(np.float32), g1.astype(np.float32), g2.astype(np.float32), gtab


@functools.lru_cache(maxsize=None)
def _rope_tables_t(n):
    rows = n // GRID_W
    row = np.repeat(np.arange(rows, dtype=np.float64), GRID_W)
    col = np.tile(np.arange(GRID_W, dtype=np.float64), rows)
    n_freq = QK_ROPE // 4
    freqs = ROPE_THETA ** (-np.arange(n_freq, dtype=np.float64) / n_freq)
    ang_r = row[:, None] * freqs[None, :]
    ang_c = col[:, None] * freqs[None, :]
    ang = np.concatenate([ang_r, ang_r, ang_c, ang_c], axis=-1)
    return np.cos(ang).T.astype(np.float32), np.sin(ang).T.astype(np.float32)


def kernel(x, c, ctx, c_ctx, ada_w, ada_b, norm1_g, norm2_g, w_in, b_gate, w_fourier, q_norm_g, w_qb,
           kv_norm_g, w_kvb, q_gain, k_gain, w_mla_o, w_out, w_up, w_down):
    bsz, s, d = x.shape
    sc = ctx.shape[1]
    assert ada_w.shape[0] == 1 and s == RADIX * RADIX and d == D_MODEL

    cc = jnp.zeros((8, d), F32).at[:bsz].set(c).at[bsz].set(c_ctx)
    mod, wf, wq, wkv, wg, wqbT, wkbT, wvT = _adaln(
        cc, ada_w[0], ada_b[0][None, :], w_in[0].T, w_qb[0], w_kvb[0])

    g1 = norm1_g[0][None, :]
    g2 = norm2_g[0][None, :]
    qg = q_norm_g[0][None, :]
    kvg = kv_norm_g[0][None, :]
    bg = b_gate[0][None, :]
    qscale = (HEAD_QK ** -0.5) * math.log2(math.e)
    qgain = jnp.broadcast_to((q_gain[0] * qscale)[:, None], (HEAD_QK, LANES))
    kgn = jnp.broadcast_to(k_gain[0][:QK_NOPE, None], (QK_NOPE, LANES))
    kgr = jnp.broadcast_to(k_gain[0][QK_NOPE:, None], (QK_ROPE, LANES))
    cosT, sinT = (jnp.asarray(t) for t in _rope_tables_t(s))
    cos1 = jnp.asarray(np.ones((QK_ROPE, sc), np.float32))
    sin0 = jnp.asarray(np.zeros((QK_ROPE, sc), np.float32))

    cs, ga1, ga2, gtab = (jnp.asarray(t).astype(BF16) for t in _dft_tables())

    kc, vcT = _ctx_kv(ctx, mod, g1, wkv, kvg, wkbT, wvT, kgn, kgr, cos1, sin0)

    z, gates, qT, k, vT = _proj(x, mod, g1, wf, cs, wg, bg, wq, qg, wqbT, qgain, wkv, kvg, wkbT, wvT,
                                kgn, kgr, cosT, sinT, PROJ_TILE)

    r8 = DFT_RADIX_A
    fm = _dft(ga1, ga2, gtab, z.reshape(bsz, 2, r8, r8, RADIX, FOURIER_W)).reshape(bsz, s, FOURIER_W)

    aT, (wfo, wao, wout, wup, wdn) = _attention(
        qT, k, vT, kc, vcT,
        [(w_fourier[0], 0, None), (w_mla_o[0], 0, None), (w_out[0], 0, None),
         (w_up[0], 1, FF_CHUNK), (w_down[0], 0, None)],
        ATTN_TQ, ATTN_TK)
    wdn = wdn.reshape(-1, FF_CHUNK, d)

    return _post(fm, aT, gates, x, mod, wfo, wao, wout, g2, wup, wdn, POST_TILE)
```
